```python
import math
import numpy as np
import jax, jax.numpy as jnp
from jax import lax

D_MODEL = 2048
BATCH = 2
SEQ = 4096
DEPTH = 4
DEC_BATCH = 8
DEC_SEQ = 8
PAST_LEN = 16384
PAGE_SIZE = 128

N_MIXERS = 3
N_LAYERS_A = (DEPTH + 2) // 3
N_LAYERS_B = (DEPTH + 1) // 3
N_LAYERS_C = DEPTH // 3

D_RNN = (5 * D_MODEL) // 4
N_RNN_BLOCKS = 16
RNN_BLOCK = D_RNN // N_RNN_BLOCKS
CONV_W = 4
LRU_C = 8.0

HEAD_DIM = 128
N_HEADS = D_MODEL // HEAD_DIM
N_KV_B = 4
N_IDX_HEADS = 16
IDX_DIM = 64
IDX_TOPK = 256
IDX_SCALE = (IDX_DIM * N_IDX_HEADS) ** -0.5
DSA_PROJ = N_HEADS * HEAD_DIM + 2 * N_KV_B * HEAD_DIM + N_IDX_HEADS * IDX_DIM + IDX_DIM + N_IDX_HEADS
DSA_ROWS = 128
MOBA_BLOCK = 256
MOBA_TOPK = 3
MOBA_ROWS = 32

N_BUCKETS = 32
MAX_DISTANCE = 128

N_GROUPS = 4
EXPERTS_PER_GROUP = 4
N_EXPERTS = N_GROUPS * EXPERTS_PER_GROUP
TOPK_IN_GROUP = 2
D_EXPERT = D_MODEL // 4

EPS = 1e-6

kernel_name = 'hybrid_rglru_dsa_moba_hmoe_step'


def _rmsnorm(x, g):
    xf = x.astype(jnp.float32)
    xf = xf * lax.rsqrt(jnp.mean(xf * xf, axis=-1, keepdims=True) + EPS)
    return (xf * g.astype(jnp.float32)).astype(x.dtype)


def _ada(c, w, b):
    mod = jax.nn.silu(c) @ w + b
    return [m[:, None, :] for m in jnp.split(mod, 6, axis=-1)]


def _modulate(h, shift, scale):
    return h * (1 + scale) + shift


def _block_size(T, target):
    b = max(1, min(T, target))
    while T % b:
        b -= 1
    return b


def _t5_bucket(dist):
    n = jnp.maximum(dist, 0)
    max_exact = N_BUCKETS // 2
    nf = jnp.maximum(n, 1).astype(jnp.float32)
    large = max_exact + (jnp.log(nf / max_exact) / math.log(MAX_DISTANCE / max_exact) * (N_BUCKETS - max_exact)).astype(jnp.int32)
    return jnp.where(n < max_exact, n, jnp.minimum(large, N_BUCKETS - 1))


def _gather_rows(pos, new_rows, pool, page_table, past_len, head=None):
    Bq, T = new_rows.shape[:2]
    bidx = jnp.arange(Bq, dtype=jnp.int32).reshape((Bq,) + (1,) * (pos.ndim - 1))
    new_i = jnp.clip(pos - past_len, 0, T - 1)
    from_new = new_rows[bidx, new_i] if head is None else new_rows[bidx, new_i, head]
    if pool is None:
        return from_new
    p = jnp.clip(pos, 0, past_len - 1)
    phys = page_table[bidx, p // PAGE_SIZE]
    off = p % PAGE_SIZE
    from_pool = pool[phys, off] if head is None else pool[phys, off, head]
    is_past = (pos < past_len).reshape(pos.shape + (1,) * (from_new.ndim - pos.ndim))
    return jnp.where(is_past, from_pool, from_new)


def _linear_combine(left, right):
    a1, b1 = left
    a2, b2 = right
    return a1 * a2, a2 * b1 + b2


def _rglru_mixer(h, conv_state, h0, w_in, conv_w, conv_b, w_r, b_r, w_i, b_i, lam, w_out):
    Bq, T, _ = h.shape
    u = h @ w_in
    gate_branch, xb = u[..., :D_RNN], u[..., D_RNN:]
    xpad = jnp.concatenate([conv_state.astype(xb.dtype), xb], axis=1)
    xc = conv_b + xpad[:, 0:T] * conv_w[0]
    for j in range(1, CONV_W):
        xc = xc + xpad[:, j:j + T] * conv_w[j]
    new_conv = xpad[:, xpad.shape[1] - (CONV_W - 1):]
    xblk = xc.reshape(Bq, T, N_RNN_BLOCKS, RNN_BLOCK)
    r = jax.nn.sigmoid(jnp.einsum('btnc,ncd->btnd', xblk, w_r).reshape(Bq, T, D_RNN) + b_r)
    i = jax.nn.sigmoid(jnp.einsum('btnc,ncd->btnd', xblk, w_i).reshape(Bq, T, D_RNN) + b_i)
    log_a = LRU_C * r.astype(jnp.float32) * jax.nn.log_sigmoid(lam.astype(jnp.float32))
    a = jnp.exp(log_a)
    b = jnp.sqrt(jnp.maximum(-jnp.expm1(2.0 * log_a), 0.0)) * (i * xc).astype(jnp.float32)
    b = b.at[:, 0].add(a[:, 0] * h0.astype(jnp.float32))
    _, hs = lax.associative_scan(_linear_combine, (a, b), axis=1)
    y = jax.nn.gelu(gate_branch) * hs.astype(h.dtype)
    return y @ w_out, new_conv, hs[:, -1].astype(h0.dtype)


def _dsa_mixer(h, w_in, w_out, rel_bias, k_pool, v_pool, kidx_pool, page_table, past_len):
    Bq, T, _ = h.shape
    sizes = (N_HEADS * HEAD_DIM, N_KV_B * HEAD_DIM, N_KV_B * HEAD_DIM, N_IDX_HEADS * IDX_DIM, IDX_DIM, N_IDX_HEADS)
    cuts = np.cumsum(sizes)[:-1].tolist()
    q, k, v, qi, ki, wi = jnp.split(h @ w_in, cuts, axis=-1)
    q = q.reshape(Bq, T, N_HEADS, HEAD_DIM)
    k = k.reshape(Bq, T, N_KV_B, HEAD_DIM)
    v = v.reshape(Bq, T, N_KV_B, HEAD_DIM)
    qi = qi.reshape(Bq, T, N_IDX_HEADS, IDX_DIM)
    if kidx_pool is None:
        k_idx_all = ki
    else:
        k_idx_all = jnp.concatenate([kidx_pool[page_table].reshape(Bq, past_len, IDX_DIM), ki], axis=1)
    kidx_f = k_idx_all.astype(jnp.float32)
    L = k_idx_all.shape[1]
    topk = min(IDX_TOPK, L // 4)
    qb = _block_size(T, max(1, DSA_ROWS // Bq))
    nblk = T // qb
    group = N_HEADS // N_KV_B
    kpos = jnp.arange(L, dtype=jnp.int32)

    def block(args):
        q_b, qi_b, wi_b, qpos = args
        s = jnp.einsum('bqhd,bsd->bqhs', qi_b.astype(jnp.float32), kidx_f)
        score = jnp.einsum('bqh,bqhs->bqs', wi_b.astype(jnp.float32), jax.nn.relu(s)) * IDX_SCALE
        score = jnp.where(kpos[None, None, :] <= qpos[None, :, None], score, -jnp.inf)
        _, sel = lax.top_k(score, topk)
        valid = sel <= qpos[None, :, None]
        ks = _gather_rows(sel, k, k_pool, page_table, past_len)
        vs = _gather_rows(sel, v, v_pool, page_table, past_len)
        qg = q_b.reshape(Bq, qb, N_KV_B, group, HEAD_DIM)
        logits = jnp.einsum('bqgrd,bqkgd->bqgrk', qg, ks).astype(jnp.float32) * HEAD_DIM ** -0.5
        bias = rel_bias[_t5_bucket(qpos[None, :, None] - sel)].astype(jnp.float32)
        bias = bias.reshape(Bq, qb, topk, N_KV_B, group).transpose(0, 1, 3, 4, 2)
        logits = jnp.where(valid[:, :, None, None, :], logits + bias, -jnp.inf)
        p = jax.nn.softmax(logits, axis=-1).astype(vs.dtype)
        return jnp.einsum('bqgrk,bqkgd->bqgrd', p, vs).reshape(Bq, qb, N_HEADS * HEAD_DIM)

    to_blocks = lambda a: a.reshape((Bq, nblk, qb) + a.shape[2:]).swapaxes(0, 1)
    qpos_all = (past_len + jnp.arange(T, dtype=jnp.int32)).reshape(nblk, qb)
    o = lax.map(block, (to_blocks(q), to_blocks(qi), to_blocks(wi), qpos_all))
    o = o.swapaxes(0, 1).reshape(Bq, T, N_HEADS * HEAD_DIM)
    return o @ w_out, k, v, ki


def _page_sums(rows):
    Bq, T = rows.shape[:2]
    P = -(-T // PAGE_SIZE)
    rows = jnp.pad(rows.astype(jnp.float32), ((0, 0), (0, P * PAGE_SIZE - T), (0, 0), (0, 0)))
    return rows.reshape((Bq, P, PAGE_SIZE) + rows.shape[2:]).sum(axis=2)


def _block_means(page_sums):
    Bq, P = page_sums.shape[:2]
    ppb = MOBA_BLOCK // PAGE_SIZE
    nb = -(-P // ppb)
    ps = jnp.pad(page_sums, ((0, 0), (0, nb * ppb - P), (0, 0), (0, 0)))
    return ps.reshape(Bq, nb, ppb, N_HEADS, HEAD_DIM).sum(axis=2) / MOBA_BLOCK


def _moba_mixer(h, w_in, w_out, rel_bias, k_pool, v_pool, page_table, past_len):
    Bq, T, _ = h.shape
    q, k, v = jnp.split(h @ w_in, 3, axis=-1)
    q = q.reshape(Bq, T, N_HEADS, HEAD_DIM)
    k = k.reshape(Bq, T, N_HEADS, HEAD_DIM)
    v = v.reshape(Bq, T, N_HEADS, HEAD_DIM)
    page_sums = _page_sums(k)
    if k_pool is not None:
        pool_sums = jnp.sum(k_pool.astype(jnp.float32), axis=1)
        page_sums = jnp.concatenate([pool_sums[page_table], page_sums], axis=1)
    means = _block_means(page_sums)
    nb = means.shape[1]
    nsel = min(MOBA_TOPK, nb)
    qb = _block_size(T, max(1, MOBA_ROWS // Bq))
    nblk = T // qb
    offs = jnp.arange(MOBA_BLOCK, dtype=jnp.int32)
    hidx = jnp.arange(N_HEADS, dtype=jnp.int32)[None, None, :, None]
    bpos = jnp.arange(nb, dtype=jnp.int32)

    def block(args):
        q_b, qpos = args
        own = qpos // MOBA_BLOCK
        gate = jnp.einsum('bqhd,bnhd->bqhn', q_b.astype(jnp.float32), means)
        gate = jnp.where(bpos[None, None, None, :] < own[None, :, None, None], gate, -jnp.inf)
        _, top = lax.top_k(gate, nsel)
        own_b = jnp.broadcast_to(own[None, :, None, None], (Bq, qb, N_HEADS, 1))
        blocks = jnp.concatenate([top.astype(jnp.int32), own_b], axis=-1)
        slot_ok = jnp.concatenate([top < own_b, jnp.ones(own_b.shape, dtype=bool)], axis=-1)
        pos = (blocks[..., None] * MOBA_BLOCK + offs).reshape(Bq, qb, N_HEADS, (nsel + 1) * MOBA_BLOCK)
        valid = jnp.broadcast_to(slot_ok[..., None], slot_ok.shape + (MOBA_BLOCK,)).reshape(pos.shape)
        valid = valid & (pos <= qpos[None, :, None, None])
        ks = _gather_rows(pos, k, k_pool, page_table, past_len, head=hidx)
        vs = _gather_rows(pos, v, v_pool, page_table, past_len, head=hidx)
        logits = jnp.einsum('bqhd,bqhjd->bqhj', q_b, ks).astype(jnp.float32) * HEAD_DIM ** -0.5
        bias = rel_bias[_t5_bucket(qpos[None, :, None, None] - pos), hidx].astype(jnp.float32)
        logits = jnp.where(valid, logits + bias, -jnp.inf)
        p = jax.nn.softmax(logits, axis=-1).astype(vs.dtype)
        return jnp.einsum('bqhj,bqhjd->bqhd', p, vs).reshape(Bq, qb, N_HEADS * HEAD_DIM)

    q_blocks = q.reshape(Bq, nblk, qb, N_HEADS, HEAD_DIM).swapaxes(0, 1)
    qpos_all = (past_len + jnp.arange(T, dtype=jnp.int32)).reshape(nblk, qb)
    o = lax.map(block, (q_blocks, qpos_all))
    o = o.swapaxes(0, 1).reshape(Bq, T, N_HEADS * HEAD_DIM)
    return o @ w_out, k, v


def _hier_moe(h, w_group, w_expert, w1, w3, w2):
    Bq, T, D = h.shape
    xt = h.reshape(Bq * T, D)
    g_logits = (xt @ w_group).astype(jnp.float32)
    g_sel = jnp.argmax(g_logits, axis=-1)
    g_w = jnp.take_along_axis(jax.nn.softmax(g_logits, axis=-1), g_sel[:, None], axis=1)
    e_logits = (xt @ w_expert).astype(jnp.float32).reshape(-1, N_GROUPS, EXPERTS_PER_GROUP)
    e_in = jnp.take_along_axis(e_logits, g_sel[:, None, None], axis=1)[:, 0]
    top_v, top_i = lax.top_k(e_in, TOPK_IN_GROUP)
    w_sel = jax.nn.softmax(top_v, axis=-1) * g_w
    expert_id = g_sel[:, None] * EXPERTS_PER_GROUP + top_i
    gates = jnp.sum(jax.nn.one_hot(expert_id, N_EXPERTS, dtype=jnp.float32) * w_sel[..., None], axis=1)
    hidden = jax.nn.silu(jnp.einsum('nd,edf->nef', xt, w1)) * jnp.einsum('nd,edf->nef', xt, w3)
    out = jnp.einsum('nef,efd->nd', hidden * gates[:, :, None].astype(hidden.dtype), w2)
    return out.reshape(Bq, T, D)


def setup_inputs(seed: int = 0) -> dict:
    key = jax.random.key(seed)
    ks = jax.random.split(key, 36)
    f32 = jnp.float32
    d = D_MODEL
    n_pages = PAST_LEN // PAGE_SIZE
    n_phys = (5 * DEC_BATCH * n_pages + 3) // 4

    def nrm(k, shape, scale):
        return jax.random.normal(k, shape, f32) * scale

    page_table = jax.random.permutation(ks[9], n_phys)[: DEC_BATCH * n_pages].reshape(DEC_BATCH, n_pages).astype(jnp.int32)
    u = jax.random.uniform(ks[25], (N_LAYERS_A, D_RNN), f32, 0.9, 0.999)
    s = u ** (1.0 / LRU_C)
    lru_lambda = jnp.log(s) - jnp.log1p(-s)
    return {
        'x_prompt': nrm(ks[0], (BATCH, SEQ, d), 1.0),
        'x_sample': nrm(ks[1], (DEC_BATCH, DEC_SEQ, d), 1.0),
        'state_lru_h': nrm(ks[2], (N_LAYERS_A, DEC_BATCH, D_RNN), 0.5),
        'state_lru_conv': nrm(ks[3], (N_LAYERS_A, DEC_BATCH, CONV_W - 1, D_RNN), 1.0),
        'cache_dsa_k': nrm(ks[4], (N_LAYERS_B, n_phys, PAGE_SIZE, N_KV_B, HEAD_DIM), 1.0),
        'cache_dsa_v': nrm(ks[5], (N_LAYERS_B, n_phys, PAGE_SIZE, N_KV_B, HEAD_DIM), 1.0),
        'cache_dsa_kidx': nrm(ks[6], (N_LAYERS_B, n_phys, PAGE_SIZE, IDX_DIM), 1.0),
        'cache_moba_k': nrm(ks[7], (N_LAYERS_C, n_phys, PAGE_SIZE, N_HEADS, HEAD_DIM), 1.0),
        'cache_moba_v': nrm(ks[8], (N_LAYERS_C, n_phys, PAGE_SIZE, N_HEADS, HEAD_DIM), 1.0),
        'page_table': page_table,
        'c_prompt': nrm(ks[10], (BATCH, d), 1.0),
        'c_sample': nrm(ks[11], (DEC_BATCH, d), 1.0),
        'rel_bias': nrm(ks[12], (N_BUCKETS, N_HEADS), 0.5),
        'norm_mix_g': 1.0 + nrm(ks[13], (DEPTH, d), 0.02),
        'norm_ffn_g': 1.0 + nrm(ks[14], (DEPTH, d), 0.02),
        'final_norm_g': 1.0 + nrm(ks[15], (d,), 0.02),
        'ada_w': nrm(ks[16], (DEPTH, d, 6 * d), 0.5 * d ** -0.5),
        'ada_b': nrm(ks[17], (DEPTH, 6 * d), 0.02),
        'lru_w_in': nrm(ks[18], (N_LAYERS_A, d, 2 * D_RNN), d ** -0.5),
        'lru_conv_w': nrm(ks[19], (N_LAYERS_A, CONV_W, D_RNN), CONV_W ** -0.5),
        'lru_conv_b': nrm(ks[20], (N_LAYERS_A, D_RNN), 0.02),
        'lru_w_rgate': nrm(ks[21], (N_LAYERS_A, N_RNN_BLOCKS, RNN_BLOCK, RNN_BLOCK), RNN_BLOCK ** -0.5),
        'lru_b_rgate': nrm(ks[22], (N_LAYERS_A, D_RNN), 0.1),
        'lru_w_igate': nrm(ks[23], (N_LAYERS_A, N_RNN_BLOCKS, RNN_BLOCK, RNN_BLOCK), RNN_BLOCK ** -0.5),
        'lru_b_igate': nrm(ks[24], (N_LAYERS_A, D_RNN), 0.1),
        'lru_lambda': lru_lambda,
        'lru_w_out': nrm(ks[26], (N_LAYERS_A, D_RNN, d), D_RNN ** -0.5),
        'dsa_w_in': nrm(ks[27], (N_LAYERS_B, d, DSA_PROJ), d ** -0.5),
        'dsa_w_out': nrm(ks[28], (N_LAYERS_B, N_HEADS * HEAD_DIM, d), (N_HEADS * HEAD_DIM) ** -0.5),
        'moba_w_in': nrm(ks[29], (N_LAYERS_C, d, 3 * N_HEADS * HEAD_DIM), d ** -0.5),
        'moba_w_out': nrm(ks[30], (N_LAYERS_C, N_HEADS * HEAD_DIM, d), (N_HEADS * HEAD_DIM) ** -0.5),
        'moe_w_group': nrm(ks[31], (DEPTH, d, N_GROUPS), d ** -0.5),
        'moe_w_expert': nrm(ks[32], (DEPTH, d, N_EXPERTS), d ** -0.5),
        'moe_w1': nrm(ks[33], (DEPTH, N_EXPERTS, d, D_EXPERT), d ** -0.5),
        'moe_w3': nrm(ks[34], (DEPTH, N_EXPERTS, d, D_EXPERT), d ** -0.5),
        'moe_w2': nrm(ks[35], (DEPTH, N_EXPERTS, D_EXPERT, d), D_EXPERT ** -0.5),
    }


def reference(x_prompt, x_sample, state_lru_h, state_lru_conv, cache_dsa_k, cache_dsa_v, cache_dsa_kidx,
              cache_moba_k, cache_moba_v, page_table, c_prompt, c_sample, rel_bias, norm_mix_g, norm_ffn_g,
              final_norm_g, ada_w, ada_b, lru_w_in, lru_conv_w, lru_conv_b, lru_w_rgate, lru_b_rgate,
              lru_w_igate, lru_b_igate, lru_lambda, lru_w_out, dsa_w_in, dsa_w_out, moba_w_in, moba_w_out,
              moe_w_group, moe_w_expert, moe_w1, moe_w3, moe_w2):
    xp, xs = x_prompt, x_sample
    lru_h_p, lru_c_p, lru_h_s, lru_c_s = [], [], [], []
    dsa_k_p, dsa_v_p, dsa_i_p, dsa_k_s, dsa_v_s, dsa_i_s = [], [], [], [], [], []
    moba_k_p, moba_v_p, moba_k_s, moba_v_s = [], [], [], []
    for i in range(DEPTH):
        kind, j = i % N_MIXERS, i // N_MIXERS
        shp1, scp1, gtp1, shp2, scp2, gtp2 = _ada(c_prompt, ada_w[i], ada_b[i])
        shs1, scs1, gts1, shs2, scs2, gts2 = _ada(c_sample, ada_w[i], ada_b[i])
        hp = _modulate(_rmsnorm(xp, norm_mix_g[i]), shp1, scp1)
        hs = _modulate(_rmsnorm(xs, norm_mix_g[i]), shs1, scs1)
        if kind == 0:
            lru_params = (lru_w_in[j], lru_conv_w[j], lru_conv_b[j], lru_w_rgate[j], lru_b_rgate[j],
                          lru_w_igate[j], lru_b_igate[j], lru_lambda[j], lru_w_out[j])
            conv0 = jnp.zeros((xp.shape[0], CONV_W - 1, D_RNN), xp.dtype)
            h0 = jnp.zeros((xp.shape[0], D_RNN), xp.dtype)
            mp, cp, hp_state = _rglru_mixer(hp, conv0, h0, *lru_params)
            ms, cs, hs_state = _rglru_mixer(hs, state_lru_conv[j], state_lru_h[j], *lru_params)
            lru_h_p.append(hp_state)
            lru_c_p.append(cp)
            lru_h_s.append(hs_state)
            lru_c_s.append(cs)
        elif kind == 1:
            mp, kp, vp, ip = _dsa_mixer(hp, dsa_w_in[j], dsa_w_out[j], rel_bias, None, None, None, None, 0)
            ms, k_s, v_s, i_s = _dsa_mixer(hs, dsa_w_in[j], dsa_w_out[j], rel_bias, cache_dsa_k[j],
                                          cache_dsa_v[j], cache_dsa_kidx[j], page_table, PAST_LEN)
            dsa_k_p.append(kp)
            dsa_v_p.append(vp)
            dsa_i_p.append(ip)
            dsa_k_s.append(k_s)
            dsa_v_s.append(v_s)
            dsa_i_s.append(i_s)
        else:
            mp, kp, vp = _moba_mixer(hp, moba_w_in[j], moba_w_out[j], rel_bias, None, None, None, 0)
            ms, k_s, v_s = _moba_mixer(hs, moba_w_in[j], moba_w_out[j], rel_bias, cache_moba_k[j],
                                      cache_moba_v[j], page_table, PAST_LEN)
            moba_k_p.append(kp)
            moba_v_p.append(vp)
            moba_k_s.append(k_s)
            moba_v_s.append(v_s)
        xp = xp + gtp1 * mp
        xs = xs + gts1 * ms
        moe_params = (moe_w_group[i], moe_w_expert[i], moe_w1[i], moe_w3[i], moe_w2[i])
        xp = xp + gtp2 * _hier_moe(_modulate(_rmsnorm(xp, norm_ffn_g[i]), shp2, scp2), *moe_params)
        xs = xs + gts2 * _hier_moe(_modulate(_rmsnorm(xs, norm_ffn_g[i]), shs2, scs2), *moe_params)
    y_prompt = _rmsnorm(xp, final_norm_g)
    y_sample = _rmsnorm(xs, final_norm_g)
    return (y_prompt, y_sample,
            jnp.stack(lru_h_p), jnp.stack(lru_c_p), jnp.stack(lru_h_s), jnp.stack(lru_c_s),
            jnp.stack(dsa_k_p), jnp.stack(dsa_v_p), jnp.stack(dsa_i_p),
            jnp.stack(dsa_k_s), jnp.stack(dsa_v_s), jnp.stack(dsa_i_s),
            jnp.stack(moba_k_p), jnp.stack(moba_v_p), jnp.stack(moba_k_s), jnp.stack(moba_v_s))
```

```python
import functools
import math

import numpy as np
import jax
import jax.numpy as jnp
from jax import lax
from jax.experimental import pallas as pl
from jax.experimental.pallas import tpu as pltpu

F32 = jnp.float32
BF16 = jnp.bfloat16

LANES = 128
SUBLANES = 8
VMEM_LIMIT_BYTES = 56 * 1024 * 1024

HEAD_DIM = 128
N_HEADS = 16
N_KV_B = 4
N_IDX_HEADS = 16
IDX_DIM = 64
IDX_TOPK = 256
IDX_SCALE = (IDX_DIM * N_IDX_HEADS) ** -0.5
PAGE_SIZE = 128
MOBA_BLOCK = 256
MOBA_TOPK = 3
N_BUCKETS = 32
MAX_DISTANCE = 128
N_RNN_BLOCKS = 16
CONV_W = 4
LRU_C = 8.0
N_GROUPS = 4
EXPERTS_PER_GROUP = 4
N_EXPERTS = N_GROUPS * EXPERTS_PER_GROUP
EPS = 1e-6
ATTN_SCALE = HEAD_DIM ** -0.5
NEG_INF = float("-inf")


def _params(*sem):
    return pltpu.CompilerParams(dimension_semantics=sem, vmem_limit_bytes=VMEM_LIMIT_BYTES)


def _bdot(a, b):
    return jnp.dot(a.astype(BF16), b.astype(BF16), preferred_element_type=F32)


def _bdot_nt(a, b):
    return lax.dot_general(a.astype(BF16), b.astype(BF16), (((1,), (1,)), ((), ())),
                           preferred_element_type=F32)


def _ada_kernel(c_ref, w_ref, b_ref, o_ref):
    c = c_ref[...]
    o_ref[...] = _bdot(c * jax.nn.sigmoid(c), w_ref[...]) + b_ref[...]


def _ada(c, ada_w, ada_b):
    depth, d, n = ada_w.shape
    rows = c.shape[0]
    tn = next(w for w in (1024, 512, 256, LANES) if n % w == 0)
    return pl.pallas_call(
        _ada_kernel,
        grid=(depth, n // tn),
        in_specs=[
            pl.BlockSpec((rows, d), lambda l, j: (0, 0)),
            pl.BlockSpec((None, d, tn), lambda l, j: (l, 0, j)),
            pl.BlockSpec((None, 1, tn), lambda l, j: (l, 0, j)),
        ],
        out_specs=pl.BlockSpec((None, rows, tn), lambda l, j: (l, 0, j)),
        out_shape=jax.ShapeDtypeStruct((depth, rows, n), F32),
        compiler_params=_params("arbitrary", "arbitrary"),
        name="ada_mod",
    )(c, ada_w, ada_b.reshape(depth, 1, n))


def _norm_mod(x, g, shift, scale):
    ms = jnp.mean(x * x, axis=-1, keepdims=True)
    h = x * lax.rsqrt(ms + EPS) * g
    return h * (1.0 + scale) + shift


def _norm_mm_kernel(x_ref, g_ref, sh_ref, sc_ref, w_ref, *refs, emit_h, highest):
    if emit_h:
        o_ref, ho_ref, h_ref = refs
    else:
        o_ref, h_ref = refs
        ho_ref = None

    @pl.when(pl.program_id(2) == 0)
    def _():
        h = _norm_mod(x_ref[...], g_ref[...], sh_ref[...], sc_ref[...])
        h_ref[...] = h.astype(h_ref.dtype)
        if ho_ref is not None:
            ho_ref[...] = h.astype(ho_ref.dtype)

    if highest:
        o_ref[...] = jnp.dot(h_ref[...], w_ref[...], precision=lax.Precision.HIGHEST,
                             preferred_element_type=F32)
    else:
        o_ref[...] = _bdot(h_ref[...], w_ref[...])


def _norm_matmul(x, g, shift, scale, w, *, tm, tn, emit_h=False, highest=False):
    bsz, t, d = x.shape
    n = w.shape[1]
    r = shift.shape[1]
    assert t % tm == 0 and n % tn == 0 and (r == 1 or r == tm == t)
    out_shape = [jax.ShapeDtypeStruct((bsz, t, n), F32)]
    out_specs = [pl.BlockSpec((None, tm, tn), lambda b, i, j: (b, i, j))]
    if emit_h:
        out_shape.append(jax.ShapeDtypeStruct((bsz, t, d), BF16))
        out_specs.append(pl.BlockSpec((None, tm, d), lambda b, i, j: (b, i, 0)))
    mod_map = (lambda b, i, j: (b, 0, 0)) if r == 1 else (lambda b, i, j: (b, i, 0))
    res = pl.pallas_call(
        functools.partial(_norm_mm_kernel, emit_h=emit_h, highest=highest),
        grid=(bsz, t // tm, n // tn),
        in_specs=[
            pl.BlockSpec((None, tm, d), lambda b, i, j: (b, i, 0)),
            pl.BlockSpec((1, d), lambda b, i, j: (0, 0)),
            pl.BlockSpec((None, r, d), mod_map),
            pl.BlockSpec((None, r, d), mod_map),
            pl.BlockSpec((d, tn), lambda b, i, j: (0, j)),
        ],
        out_specs=out_specs,
        out_shape=out_shape,
        scratch_shapes=[pltpu.VMEM((tm, d), F32 if highest else BF16)],
        compiler_params=_params("arbitrary", "arbitrary", "arbitrary"),
        name="norm_matmul",
    )(x, g.reshape(1, d), shift, scale, w)
    return res if emit_h else res[0]


def _mm_res_kernel(a_ref, w_ref, r_ref, g_ref, o_ref):
    o_ref[...] = r_ref[...] + g_ref[...] * _bdot(a_ref[...], w_ref[...])


def _matmul_residual(a, w, res, gate, *, tm, tn):
    bsz, t, k = a.shape
    n = w.shape[1]
    r = gate.shape[1]
    assert t % tm == 0 and n % tn == 0 and (r == 1 or r == tm == t)
    gate_map = (lambda b, i, j: (b, 0, j)) if r == 1 else (lambda b, i, j: (b, i, j))
    return pl.pallas_call(
        _mm_res_kernel,
        grid=(bsz, t // tm, n // tn),
        in_specs=[
            pl.BlockSpec((None, tm, k), lambda b, i, j: (b, i, 0)),
            pl.BlockSpec((k, tn), lambda b, i, j: (0, j)),
            pl.BlockSpec((None, tm, tn), lambda b, i, j: (b, i, j)),
            pl.BlockSpec((None, r, tn), gate_map),
        ],
        out_specs=pl.BlockSpec((None, tm, tn), lambda b, i, j: (b, i, j)),
        out_shape=jax.ShapeDtypeStruct((bsz, t, n), F32),
        compiler_params=_params("arbitrary", "arbitrary", "arbitrary"),
        name="matmul_residual",
    )(a, w, res, gate)


def _rmsnorm_kernel(x_ref, g_ref, o_ref):
    x = x_ref[...]
    ms = jnp.mean(x * x, axis=-1, keepdims=True)
    o_ref[...] = x * lax.rsqrt(ms + EPS) * g_ref[...]


def _rmsnorm(x, g, *, tm):
    bsz, t, d = x.shape
    return pl.pallas_call(
        _rmsnorm_kernel,
        grid=(bsz, t // tm),
        in_specs=[pl.BlockSpec((None, tm, d), lambda b, i: (b, i, 0)),
                  pl.BlockSpec((1, d), lambda b, i: (0, 0))],
        out_specs=pl.BlockSpec((None, tm, d), lambda b, i: (b, i, 0)),
        out_shape=jax.ShapeDtypeStruct((bsz, t, d), F32),
        compiler_params=_params("arbitrary", "arbitrary"),
        name="final_rmsnorm",
    )(x, g.reshape(1, d))


GATE_CHUNK = 640


def _log_sigmoid(x):
    return jnp.minimum(x, 0.0) - jnp.log1p(jnp.exp(-jnp.abs(x)))


def _lru_kernel(gb_ref, xb_ref, cs_ref, h0_ref, cw_ref, cb_ref, wr_ref, br_ref, wi_ref, bi_ref, lam_ref,
                y_ref, hl_ref, nc_ref, ext_ref, a_ref, b_ref, hs_ref, carry_ref, *, tt):
    c_all = xb_ref.shape[-1]

    @pl.when(pl.program_id(1) == 0)
    def _():
        ext_ref[0:SUBLANES, :] = cs_ref[...]
        carry_ref[...] = h0_ref[...]

    xb = xb_ref[...]
    ext_ref[SUBLANES:SUBLANES + tt, :] = xb
    cw = cw_ref[...]
    xc = cb_ref[...] + xb * cw[CONV_W - 1:CONV_W, :]
    for j in range(1, CONV_W):
        xc = xc + ext_ref[pl.ds(SUBLANES - j, tt), :] * cw[CONV_W - 1 - j:CONV_W - j, :]
    tail = ext_ref[tt:tt + SUBLANES, :]
    ext_ref[0:SUBLANES, :] = tail
    nc_ref[...] = tail

    xcb = xc.astype(BF16)
    r_parts, i_parts = [], []
    for c in range(c_all // GATE_CHUNK):
        sl = slice(c * GATE_CHUNK, (c + 1) * GATE_CHUNK)
        r_parts.append(jnp.dot(xcb[:, sl], wr_ref[c], preferred_element_type=F32))
        i_parts.append(jnp.dot(xcb[:, sl], wi_ref[c], preferred_element_type=F32))
    r = jax.nn.sigmoid(jnp.concatenate(r_parts, axis=1) + br_ref[...])
    ig = jax.nn.sigmoid(jnp.concatenate(i_parts, axis=1) + bi_ref[...])
    log_a = LRU_C * r * _log_sigmoid(lam_ref[...])
    th = jnp.tanh(log_a)
    one_minus_a2 = -2.0 * th / (1.0 - th)
    a_ref[...] = jnp.exp(log_a)
    b_ref[...] = jnp.sqrt(jnp.maximum(one_minus_a2, 0.0)) * (ig * xc)

    row = lax.broadcasted_iota(jnp.int32, (SUBLANES, GATE_CHUNK), 0)
    for c in range(c_all // GATE_CHUNK):
        sl = slice(c * GATE_CHUNK, (c + 1) * GATE_CHUNK)

        def body(grp, carry, sl=sl):
            r8 = pl.multiple_of(grp * SUBLANES, SUBLANES)
            av = a_ref[pl.ds(r8, SUBLANES), sl]
            bv = b_ref[pl.ds(r8, SUBLANES), sl]
            for s in (1, 2, 4):
                m = row >= s
                a_sh = jnp.where(m, pltpu.roll(av, s, axis=0), 1.0)
                b_sh = jnp.where(m, pltpu.roll(bv, s, axis=0), 0.0)
                bv = av * b_sh + bv
                av = av * a_sh
            h = av * carry + bv
            hs_ref[pl.ds(r8, SUBLANES), sl] = h
            return h[SUBLANES - 1:SUBLANES, :]

        carry_ref[:, sl] = lax.fori_loop(0, tt // SUBLANES, body, carry_ref[:, sl])

    hl_ref[...] = carry_ref[...]
    y_ref[...] = (jax.nn.gelu(gb_ref[...]) * hs_ref[...]).astype(y_ref.dtype)


def _lru_core(u, conv_state, h0, conv_w, conv_b, w_r, b_r, w_i, b_i, lam, *, tt):
    bsz, t, c2 = u.shape
    c = c2 // 2
    blk = c // N_RNN_BLOCKS
    per = GATE_CHUNK // blk
    nchunk = c // GATE_CHUNK

    def block_diag(w):
        wc = w.reshape(nchunk, per, blk, blk).astype(BF16)
        eye = jnp.eye(per, dtype=BF16)
        return jnp.einsum("cpij,pq->cpiqj", wc, eye).reshape(nchunk, GATE_CHUNK, GATE_CHUNK)

    cs_pad = jnp.pad(conv_state, ((0, 0), (SUBLANES - (CONV_W - 1), 0), (0, 0)))
    row = lambda v: v.reshape(1, c)
    full = lambda shp: pl.BlockSpec(shp, lambda b, i: (0,) * len(shp))
    y, hl, nc = pl.pallas_call(
        functools.partial(_lru_kernel, tt=tt),
        grid=(bsz, t // tt),
        in_specs=[
            pl.BlockSpec((None, tt, c), lambda b, i: (b, i, 0)),
            pl.BlockSpec((None, tt, c), lambda b, i: (b, i, 1)),
            pl.BlockSpec((None, SUBLANES, c), lambda b, i: (b, 0, 0)),
            pl.BlockSpec((None, 1, c), lambda b, i: (b, 0, 0)),
            full((CONV_W, c)), full((1, c)),
            full((nchunk, GATE_CHUNK, GATE_CHUNK)), full((1, c)),
            full((nchunk, GATE_CHUNK, GATE_CHUNK)), full((1, c)),
            full((1, c)),
        ],
        out_specs=[
            pl.BlockSpec((None, tt, c), lambda b, i: (b, i, 0)),
            pl.BlockSpec((None, 1, c), lambda b, i: (b, 0, 0)),
            pl.BlockSpec((None, SUBLANES, c), lambda b, i: (b, 0, 0)),
        ],
        out_shape=[
            jax.ShapeDtypeStruct((bsz, t, c), BF16),
            jax.ShapeDtypeStruct((bsz, 1, c), F32),
            jax.ShapeDtypeStruct((bsz, SUBLANES, c), F32),
        ],
        scratch_shapes=[
            pltpu.VMEM((tt + SUBLANES, c), F32),
            pltpu.VMEM((tt, c), F32),
            pltpu.VMEM((tt, c), F32),
            pltpu.VMEM((tt, c), F32),
            pltpu.VMEM((1, c), F32),
        ],
        compiler_params=_params("arbitrary", "arbitrary"),
        name="lru_core",
    )(u, u, cs_pad, h0.reshape(bsz, 1, c), conv_w, row(conv_b), block_diag(w_r), row(b_r),
      block_diag(w_i), row(b_i), row(lam))
    return y, hl.reshape(bsz, c), nc[:, SUBLANES - (CONV_W - 1):, :]


def _route_gates(lg):
    col = lax.broadcasted_iota(jnp.int32, lg.shape, 1)
    big = jnp.int32(LANES)
    is_g = col < N_GROUPS
    gl = jnp.where(is_g, lg, NEG_INF)
    gmax = jnp.max(gl, axis=1, keepdims=True)
    g_sel = jnp.min(jnp.where(gl == gmax, col, big), axis=1, keepdims=True)
    g_w = 1.0 / jnp.sum(jnp.exp(gl - gmax), axis=1, keepdims=True)
    eid = col - N_GROUPS
    in_grp = (eid >= 0) & (eid < N_EXPERTS) & ((eid // EXPERTS_PER_GROUP) == g_sel)
    e_in = jnp.where(in_grp, lg, NEG_INF)
    top1 = jnp.max(e_in, axis=1, keepdims=True)
    idx1 = jnp.min(jnp.where((e_in == top1) & in_grp, col, big), axis=1, keepdims=True)
    rest = in_grp & (col != idx1)
    e2 = jnp.where(rest, lg, NEG_INF)
    top2 = jnp.max(e2, axis=1, keepdims=True)
    idx2 = jnp.min(jnp.where((e2 == top2) & rest, col, big), axis=1, keepdims=True)
    z = jnp.exp(top2 - top1)
    w_first = g_w / (1.0 + z)
    w_second = g_w * z / (1.0 + z)
    return jnp.where(col == idx1, w_first, 0.0) + jnp.where(col == idx2, w_second, 0.0)


def _moe_dense_kernel(h_ref, lg_ref, w1_ref, w3_ref, w2_ref, x_ref, g_ref, o_ref, gates_ref, acc_ref):
    e = pl.program_id(2)

    @pl.when(e == 0)
    def _():
        gates_ref[...] = _route_gates(lg_ref[...])
        acc_ref[...] = jnp.zeros_like(acc_ref)

    gates = gates_ref[...]
    col = lax.broadcasted_iota(jnp.int32, gates.shape, 1)
    ge = jnp.sum(jnp.where(col == e + N_GROUPS, gates, 0.0), axis=1, keepdims=True)
    h = h_ref[...]
    a1 = jnp.dot(h, w1_ref[...], preferred_element_type=F32)
    a3 = jnp.dot(h, w3_ref[...], preferred_element_type=F32)
    hidden = (a1 * jax.nn.sigmoid(a1)) * a3 * ge
    acc_ref[...] += jnp.dot(hidden.astype(BF16), w2_ref[...], preferred_element_type=F32)

    @pl.when(e == pl.num_programs(2) - 1)
    def _():
        o_ref[...] = x_ref[...] + g_ref[...] * acc_ref[...]


def _moe_dense(h, logits, w1, w3, w2, x, gate, *, tm):
    bsz, t, d = x.shape
    ne, _, f = w1.shape
    r = gate.shape[1]
    gate_map = (lambda b, i, e: (b, 0, 0)) if r == 1 else (lambda b, i, e: (b, i, 0))
    tok = lambda b, i, e: (b, i, 0)
    return pl.pallas_call(
        _moe_dense_kernel,
        grid=(bsz, t // tm, ne),
        in_specs=[
            pl.BlockSpec((None, tm, d), tok),
            pl.BlockSpec((None, tm, LANES), tok),
            pl.BlockSpec((None, d, f), lambda b, i, e: (e, 0, 0)),
            pl.BlockSpec((None, d, f), lambda b, i, e: (e, 0, 0)),
            pl.BlockSpec((None, f, d), lambda b, i, e: (e, 0, 0)),
            pl.BlockSpec((None, tm, d), tok),
            pl.BlockSpec((None, r, d), gate_map),
        ],
        out_specs=pl.BlockSpec((None, tm, d), tok),
        out_shape=jax.ShapeDtypeStruct((bsz, t, d), F32),
        scratch_shapes=[pltpu.VMEM((tm, LANES), F32), pltpu.VMEM((tm, d), F32)],
        compiler_params=_params("arbitrary", "arbitrary", "arbitrary"),
        name="moe_dense",
    )(h, logits, w1, w3, w2, x, gate)


MASKED = -1e30


def _t5_bucket_table(tq):
    qi = np.arange(tq, dtype=np.int32)[:, None]
    col = np.arange(2 * tq, dtype=np.int32)[None, :]
    n = np.maximum(qi + tq - col, 0)
    max_exact = N_BUCKETS // 2
    nf = np.maximum(n, 1).astype(np.float32)
    large = max_exact + (np.log(nf / np.float32(max_exact)) / np.float32(math.log(MAX_DISTANCE / max_exact))
                         * np.float32(N_BUCKETS - max_exact)).astype(np.int32)
    table = np.where(n < max_exact, n, np.minimum(large, N_BUCKETS - 1)).astype(np.int32)
    assert table[0, 0] == N_BUCKETS - 1
    return table


def _bias_kernel(rb_ref, bk_ref, o_ref):
    h = pl.program_id(0)
    bk = bk_ref[...]
    acc = jnp.zeros(bk.shape, F32)
    for k in range(N_BUCKETS):
        acc = jnp.where(bk == k, rb_ref[k, h], acc)
    o_ref[...] = acc - rb_ref[N_BUCKETS - 1, h]


def _near_bias(rel_bias, tq):
    return pl.pallas_call(
        _bias_kernel,
        grid=(N_HEADS,),
        in_specs=[pl.BlockSpec(memory_space=pltpu.SMEM),
                  pl.BlockSpec((tq, 2 * tq), lambda h: (0, 0))],
        out_specs=pl.BlockSpec((None, tq, 2 * tq), lambda h: (h, 0, 0)),
        out_shape=jax.ShapeDtypeStruct((N_HEADS, tq, 2 * tq), F32),
        name="near_bias",
    )(rel_bias, jnp.asarray(_t5_bucket_table(tq)))


_NEG_INF_KEY = int(np.int32(np.array(-np.inf, np.float32).view(np.int32)) ^ np.int32(0x7FFFFFFF))


def _sort_key(x):
    k = pltpu.bitcast(x, jnp.int32)
    return k ^ ((k >> 31) & jnp.int32(0x7FFFFFFF))


def _kth_largest_key(count_ge, kth):
    zero = jnp.int32(0)
    ans = jnp.where(count_ge(zero) >= kth, zero, jnp.int32(-2 ** 31))

    def body(it, ans):
        cand = ans | (jnp.int32(1) << (30 - it))
        return jnp.where(count_ge(cand) >= kth, cand, ans)

    return lax.fori_loop(0, 31, body, ans)


def _dsa_select_kernel(qi_ref, kw_ref, kall_ref, m_ref, key_ref, *, tq, tk, topk):
    i = pl.program_id(1)
    n_kt = (i * tq + tq + tk - 1) // tk
    qpos = i * tq + lax.broadcasted_iota(jnp.int32, (tq, tk), 0)
    wi = kw_ref[:, IDX_DIM:IDX_DIM + N_IDX_HEADS]
    qi = qi_ref[...].astype(BF16)
    key_ref[...] = jnp.full(key_ref.shape, _NEG_INF_KEY, jnp.int32)

    def score_tile(kt, carry):
        k0 = pl.multiple_of(kt * tk, tk)
        kid = kall_ref[pl.ds(k0, tk), 0:IDX_DIM].astype(BF16)
        acc = jnp.zeros((tq, tk), F32)
        for h in range(N_IDX_HEADS):
            s = _bdot_nt(qi[:, h * IDX_DIM:(h + 1) * IDX_DIM], kid)
            acc = acc + wi[:, h:h + 1] * jnp.maximum(s, 0.0)
        kpos = k0 + lax.broadcasted_iota(jnp.int32, (tq, tk), 1)
        sc = jnp.where(kpos <= qpos, acc * IDX_SCALE, NEG_INF)
        key_ref[:, pl.ds(k0, tk)] = _sort_key(sc)
        return carry

    lax.fori_loop(0, n_kt, score_tile, 0)

    def count_ge(cand):
        def body(kt, acc):
            k0 = pl.multiple_of(kt * tk, tk)
            ge = (key_ref[:, pl.ds(k0, tk)] >= cand).astype(jnp.int32)
            for j in range(tk // LANES):
                acc = acc + ge[:, j * LANES:(j + 1) * LANES]
            return acc
        acc = lax.fori_loop(0, n_kt, body, jnp.zeros((tq, LANES), jnp.int32))
        return jnp.sum(acc, axis=1, keepdims=True)

    thr = _kth_largest_key(count_ge, topk)
    keys = key_ref[...]
    m_ref[...] = ((keys >= thr) & (keys > _NEG_INF_KEY)).astype(m_ref.dtype)


def _dsa_select(proj, *, tq, tk, topk):
    bsz, t, _ = proj.shape
    qi_w = N_IDX_HEADS * IDX_DIM
    qi_blk = (N_HEADS * HEAD_DIM + 2 * N_KV_B * HEAD_DIM) // qi_w
    kw_blk = (N_HEADS * HEAD_DIM + 2 * N_KV_B * HEAD_DIM + qi_w) // LANES
    return pl.pallas_call(
        functools.partial(_dsa_select_kernel, tq=tq, tk=tk, topk=topk),
        grid=(bsz, t // tq),
        in_specs=[
            pl.BlockSpec((None, tq, qi_w), lambda b, i: (b, i, qi_blk)),
            pl.BlockSpec((None, tq, LANES), lambda b, i: (b, i, kw_blk)),
            pl.BlockSpec((None, t, LANES), lambda b, i: (b, 0, kw_blk)),
        ],
        out_specs=pl.BlockSpec((None, tq, t), lambda b, i: (b, i, 0)),
        out_shape=jax.ShapeDtypeStruct((bsz, t, t), BF16),
        scratch_shapes=[pltpu.VMEM((tq, t), jnp.int32)],
        compiler_params=_params("arbitrary", "arbitrary"),
        name="dsa_select",
    )(proj, proj, proj)


def _attn_kernel(q_ref, k_ref, v_ref, m_ref, tb_ref, o_ref, m_sc, l_sc, acc_sc, *, tq, rep, block_mask):
    i = pl.program_id(2)
    q = jnp.concatenate([q_ref[:, r * HEAD_DIM:(r + 1) * HEAD_DIM] for r in range(rep)], axis=0).astype(BF16)
    m_sc[...] = jnp.full(m_sc.shape, MASKED, F32)
    l_sc[...] = jnp.zeros_like(l_sc)
    acc_sc[...] = jnp.zeros_like(acc_sc)

    def key_mask(j):
        if block_mask:
            lane = lax.broadcasted_iota(jnp.int32, (tq, LANES), 1)
            vis = jnp.sum(jnp.where(lane == j, m_ref[...].astype(F32), 0.0), axis=1, keepdims=True)
            return jnp.broadcast_to(vis > 0.0, (tq, tq))
        return m_ref[:, pl.ds(pl.multiple_of(j * tq, tq), tq)] > 0

    def step(j, bias, causal):
        k0 = pl.multiple_of(j * tq, tq)
        s = _bdot_nt(q, k_ref[pl.ds(k0, tq), :]) * ATTN_SCALE
        if bias is not None:
            s = s + bias
        ok = key_mask(j)
        if causal:
            ok = ok & (lax.broadcasted_iota(jnp.int32, (tq, tq), 0) >= lax.broadcasted_iota(jnp.int32, (tq, tq), 1))
        s = jnp.where(jnp.concatenate([ok] * rep, axis=0), s, MASKED)
        m_old = m_sc[...]
        m_new = jnp.maximum(m_old, jnp.max(s, axis=1, keepdims=True))
        alpha = jnp.exp(m_old - m_new)
        p = jnp.exp(s - m_new)
        l_sc[...] = alpha * l_sc[...] + jnp.sum(p, axis=1, keepdims=True)
        acc_sc[...] = alpha * acc_sc[...] + jnp.dot(p.astype(BF16), v_ref[pl.ds(k0, tq), :],
                                                   preferred_element_type=F32)
        m_sc[...] = m_new

    def far(j, carry):
        step(j, None, False)
        return carry

    lax.fori_loop(0, jnp.maximum(i - 1, 0), far, 0)

    def near_bias(lo):
        return jnp.concatenate([tb_ref[r, :, lo:lo + tq] for r in range(rep)], axis=0)

    @pl.when(i >= 1)
    def _():
        step(i - 1, near_bias(0), False)

    step(i, near_bias(tq), True)
    out = acc_sc[...] / l_sc[...]
    o_ref[...] = jnp.concatenate([out[r * tq:(r + 1) * tq, :] for r in range(rep)], axis=1).astype(o_ref.dtype)


def _attention(q_src, k, v, mask, near_bias, *, tq, rep, block_mask):
    bsz, t, _ = q_src.shape
    g = N_HEADS // rep
    if block_mask:
        assert tq == MOBA_BLOCK and rep == 1
        mask_spec = pl.BlockSpec((None, None, tq, LANES), lambda b, h, i: (b, h, i, 0))
    else:
        mask_spec = pl.BlockSpec((None, tq, t), lambda b, h, i: (b, i, 0))
    rows = rep * tq
    return pl.pallas_call(
        functools.partial(_attn_kernel, tq=tq, rep=rep, block_mask=block_mask),
        grid=(bsz, g, t // tq),
        in_specs=[
            pl.BlockSpec((None, tq, rep * HEAD_DIM), lambda b, h, i: (b, i, h)),
            pl.BlockSpec((None, t, HEAD_DIM), lambda b, h, i: (b, 0, h)),
            pl.BlockSpec((None, t, HEAD_DIM), lambda b, h, i: (b, 0, h)),
            mask_spec,
            pl.BlockSpec((rep, tq, 2 * tq), lambda b, h, i: (h, 0, 0)),
        ],
        out_specs=pl.BlockSpec((None, tq, rep * HEAD_DIM), lambda b, h, i: (b, i, h)),
        out_shape=jax.ShapeDtypeStruct((bsz, t, N_HEADS * HEAD_DIM), BF16),
        scratch_shapes=[pltpu.VMEM((rows, 1), F32), pltpu.VMEM((rows, 1), F32), pltpu.VMEM((rows, HEAD_DIM), F32)],
        compiler_params=_params("arbitrary", "arbitrary", "arbitrary"),
        name="masked_attention",
    )(q_src, k, v, mask, near_bias)


def _block_means_kernel(tbl_ref, p0_ref, p1_ref, o_ref):
    del tbl_ref
    total = jnp.sum(p0_ref[...], axis=0, keepdims=True) + jnp.sum(p1_ref[...], axis=0, keepdims=True)
    o_ref[...] = total / MOBA_BLOCK


def _block_means(pool, table):
    _, _, w = pool.shape
    bsz, npg = table.shape
    nb = npg // 2
    out = pl.pallas_call(
        _block_means_kernel,
        grid_spec=pltpu.PrefetchScalarGridSpec(
            num_scalar_prefetch=1,
            grid=(bsz, nb),
            in_specs=[
                pl.BlockSpec((None, PAGE_SIZE, w), lambda b, n, tbl: (tbl[b, 2 * n], 0, 0)),
                pl.BlockSpec((None, PAGE_SIZE, w), lambda b, n, tbl: (tbl[b, 2 * n + 1], 0, 0)),
            ],
            out_specs=pl.BlockSpec((None, None, 1, w), lambda b, n, tbl: (b, n, 0, 0)),
        ),
        out_shape=jax.ShapeDtypeStruct((bsz, nb, 1, w), F32),
        compiler_params=_params("arbitrary", "arbitrary"),
        name="block_means",
    )(table, pool, pool)
    return out.reshape(bsz, nb, w)


def _moba_select_kernel(q_ref, mean_ref, o_ref, *, tq, past_len):
    i = pl.program_id(1)
    own = (past_len + i * tq + lax.broadcasted_iota(jnp.int32, (tq, LANES), 0)) // MOBA_BLOCK
    blk = lax.broadcasted_iota(jnp.int32, (tq, LANES), 1)
    past = blk < own
    for h in range(N_HEADS):
        sl = slice(h * HEAD_DIM, (h + 1) * HEAD_DIM)
        gate = lax.dot_general(q_ref[:, sl], mean_ref[:, sl], (((1,), (1,)), ((), ())),
                               precision=lax.Precision.HIGHEST, preferred_element_type=F32)
        g = jnp.where(past, gate, NEG_INF)
        kth = jnp.max(g, axis=1, keepdims=True)
        for _ in range(MOBA_TOPK - 1):
            kth = jnp.max(jnp.where(g < kth, g, NEG_INF), axis=1, keepdims=True)
        o_ref[h] = ((past & (g >= kth)) | (blk == own)).astype(o_ref.dtype)


def _moba_select(q_src, means, *, tq, past_len):
    bsz, t, _ = q_src.shape
    w = N_HEADS * HEAD_DIM
    return pl.pallas_call(
        functools.partial(_moba_select_kernel, tq=tq, past_len=past_len),
        grid=(bsz, t // tq),
        in_specs=[
            pl.BlockSpec((None, tq, w), lambda b, i: (b, i, 0)),
            pl.BlockSpec((None, LANES, w), lambda b, i: (b, 0, 0)),
        ],
        out_specs=pl.BlockSpec((None, N_HEADS, tq, LANES), lambda b, i: (b, 0, i, 0)),
        out_shape=jax.ShapeDtypeStruct((bsz, N_HEADS, t, LANES), BF16),
        compiler_params=_params("arbitrary", "arbitrary"),
        name="moba_select",
    )(q_src, means)


def _dsa_select_paged_kernel(tbl_ref, qi_ref, wi_ref, *rest, pp, topk, past_len, t_new):
    del tbl_ref
    page_refs = rest[:pp]
    knew_ref, m_ref, key_ref = rest[pp:]
    p = pl.program_id(1)
    qi = qi_ref[...].astype(BF16)
    wi = wi_ref[...]

    def score(kpage):
        s = _bdot_nt(qi, kpage)
        w = wi * jnp.maximum(s, 0.0)
        return jnp.sum(w.reshape(N_IDX_HEADS, t_new, PAGE_SIZE), axis=0) * IDX_SCALE

    for k in range(pp):
        off = pl.multiple_of((p * pp + k) * PAGE_SIZE, PAGE_SIZE)
        key_ref[:, pl.ds(off, PAGE_SIZE)] = _sort_key(score(page_refs[k][...]))

    @pl.when(p == pl.num_programs(1) - 1)
    def _():
        row = lax.broadcasted_iota(jnp.int32, (t_new, PAGE_SIZE), 0)
        col = lax.broadcasted_iota(jnp.int32, (t_new, PAGE_SIZE), 1)
        sc = jnp.where(col <= row, score(knew_ref[...]), NEG_INF)
        key_ref[:, past_len:past_len + PAGE_SIZE] = _sort_key(sc)

        def count_ge(cand):
            return jnp.sum((key_ref[...] >= cand).astype(jnp.int32), axis=1, keepdims=True)

        thr = _kth_largest_key(count_ge, topk)
        keys = key_ref[...]
        m_ref[...] = ((keys >= thr) & (keys > _NEG_INF_KEY)).astype(m_ref.dtype)


def _dsa_select_paged(qi_rows, wi_rows, kidx_pool, page_table, kidx_new, *, pp, topk):
    bsz, rows, _ = qi_rows.shape
    t_new = rows // N_IDX_HEADS
    n_pages = page_table.shape[1]
    past_len = n_pages * PAGE_SIZE
    lpad = past_len + PAGE_SIZE
    page_spec = lambda k: pl.BlockSpec((None, PAGE_SIZE, IDX_DIM), lambda b, p, tbl: (tbl[b, p * pp + k], 0, 0))
    return pl.pallas_call(
        functools.partial(_dsa_select_paged_kernel, pp=pp, topk=topk, past_len=past_len, t_new=t_new),
        grid_spec=pltpu.PrefetchScalarGridSpec(
            num_scalar_prefetch=1,
            grid=(bsz, n_pages // pp),
            in_specs=[
                pl.BlockSpec((None, rows, IDX_DIM), lambda b, p, tbl: (b, 0, 0)),
                pl.BlockSpec((None, rows, 1), lambda b, p, tbl: (b, 0, 0)),
                *[page_spec(k) for k in range(pp)],
                pl.BlockSpec((None, PAGE_SIZE, IDX_DIM), lambda b, p, tbl: (b, 0, 0)),
            ],
            out_specs=pl.BlockSpec((None, t_new, lpad), lambda b, p, tbl: (b, 0, 0)),
            scratch_shapes=[pltpu.VMEM((t_new, lpad), jnp.int32)],
        ),
        out_shape=jax.ShapeDtypeStruct((bsz, t_new, lpad), BF16),
        compiler_params=_params("arbitrary", "arbitrary"),
        name="dsa_select_paged",
    )(page_table, qi_rows, wi_rows, *([kidx_pool] * pp), kidx_new)


def _col_vector(row_vec):
    n = row_vec.shape[1]
    eye = lax.broadcasted_iota(jnp.int32, (n, n), 0) == lax.broadcasted_iota(jnp.int32, (n, n), 1)
    return jnp.sum(jnp.where(eye, row_vec, 0.0), axis=1, keepdims=True)


def _paged_attn_kernel(tbl_ref, qbd_ref, *rest, pp):
    del tbl_ref
    k_refs, v_refs = rest[:pp], rest[pp:2 * pp]
    mt_ref, mnew_ref, knew_ref, vnew_ref, bias_ref, o_ref, m_sc, l_sc, acc_sc = rest[2 * pp:]
    p = pl.program_id(1)
    last = pl.num_programs(1) - 1

    @pl.when(p == 0)
    def _():
        m_sc[...] = jnp.full(m_sc.shape, MASKED, F32)
        l_sc[...] = jnp.zeros_like(l_sc)
        acc_sc[...] = jnp.zeros_like(acc_sc)

    qbd = qbd_ref[...]

    def page(kp, vp, mask, bias):
        s = jnp.dot(kp.astype(BF16), qbd, preferred_element_type=F32) * ATTN_SCALE
        if bias is not None:
            s = s + bias
        s = jnp.where(mask > 0, s, MASKED)
        m_old = m_sc[...]
        m_new = jnp.maximum(m_old, jnp.max(s, axis=0, keepdims=True))
        alpha = jnp.exp(m_old - m_new)
        pt = jnp.exp(s - m_new)
        l_sc[...] = alpha * l_sc[...] + jnp.sum(pt, axis=0, keepdims=True)
        pv = lax.dot_general(pt.astype(BF16), vp.astype(BF16), (((0,), (0,)), ((), ())),
                             preferred_element_type=F32)
        acc_sc[...] = acc_sc[...] * _col_vector(alpha) + pv
        m_sc[...] = m_new

    is_last = (p == last).astype(F32)
    for k in range(pp):
        bias = bias_ref[0] * is_last if k == pp - 1 else None
        page(k_refs[k][...], v_refs[k][...], mt_ref[k * PAGE_SIZE:(k + 1) * PAGE_SIZE, :], bias)

    @pl.when(p == last)
    def _():
        page(knew_ref[...], vnew_ref[...], mnew_ref[...], bias_ref[1])
        o_ref[...] = acc_sc[...] / _col_vector(l_sc[...])


def _paged_attention(qbd, k_pool, v_pool, page_table, mask_t, k_new, v_new, bias_t, *, pp):
    bsz, w, _ = qbd.shape
    n_pages = page_table.shape[1]
    assert n_pages % pp == 0
    page_spec = lambda k: pl.BlockSpec((None, PAGE_SIZE, w), lambda b, p, tbl: (tbl[b, p * pp + k], 0, 0))
    per_b = lambda shp: pl.BlockSpec((None,) + shp, lambda b, p, tbl: (b, 0, 0))
    return pl.pallas_call(
        functools.partial(_paged_attn_kernel, pp=pp),
        grid_spec=pltpu.PrefetchScalarGridSpec(
            num_scalar_prefetch=1,
            grid=(bsz, n_pages // pp),
            in_specs=[
                per_b((w, LANES)),
                *[page_spec(k) for k in range(pp)],
                *[page_spec(k) for k in range(pp)],
                pl.BlockSpec((None, pp * PAGE_SIZE, LANES), lambda b, p, tbl: (b, p, 0)),
                pl.BlockSpec((None, PAGE_SIZE, LANES), lambda b, p, tbl: (b, n_pages, 0)),
                per_b((PAGE_SIZE, w)),
                per_b((PAGE_SIZE, w)),
                pl.BlockSpec((2, PAGE_SIZE, LANES), lambda b, p, tbl: (0, 0, 0)),
            ],
            out_specs=per_b((LANES, w)),
            scratch_shapes=[pltpu.VMEM((1, LANES), F32), pltpu.VMEM((1, LANES), F32), pltpu.VMEM((LANES, w), F32)],
        ),
        out_shape=jax.ShapeDtypeStruct((bsz, LANES, w), F32),
        compiler_params=_params("arbitrary", "arbitrary"),
        name="paged_attention",
    )(page_table, qbd, *([k_pool] * pp), *([v_pool] * pp), mask_t, mask_t, k_new, v_new, bias_t)


def _block_diag_queries(q, n_kv):
    bsz, t, _ = q.shape
    q4 = q.reshape(bsz, t, N_HEADS, HEAD_DIM)
    onehot = (jnp.arange(N_HEADS)[:, None] // (N_HEADS // n_kv) == jnp.arange(n_kv)[None, :]).astype(q.dtype)
    qbd = q4[:, :, :, None, :] * onehot[None, None, :, :, None]
    return qbd.transpose(0, 3, 4, 2, 1).reshape(bsz, n_kv * HEAD_DIM, N_HEADS * t).astype(BF16)


def _own_head_lanes(out, n_kv, t):
    bsz = out.shape[0]
    out5 = out.reshape(bsz, N_HEADS, t, n_kv, HEAD_DIM)
    heads = jnp.arange(N_HEADS)
    picked = out5[:, heads, :, heads // (N_HEADS // n_kv), :]
    return picked.transpose(1, 2, 0, 3).reshape(bsz, t, N_HEADS * HEAD_DIM)


def _decode_bias(near_bias, tq, t):
    tail = near_bias[:, :t, tq - PAGE_SIZE:tq + PAGE_SIZE]
    return tail.transpose(2, 0, 1).reshape(2, PAGE_SIZE, N_HEADS * t)


def _pad_rows(a, rows):
    return jnp.pad(a, ((0, 0), (0, rows - a.shape[1]), (0, 0)))


def _new_token_mask(t):
    key = jnp.arange(PAGE_SIZE)[:, None]
    tok = jnp.tile(jnp.arange(t), N_HEADS)[None, :]
    return (key <= tok).astype(BF16)


ATTN_TQ = 256
SELECT_TQ = 128
SELECT_TK = 512
Q_W = N_HEADS * HEAD_DIM
KV_W = N_KV_B * HEAD_DIM
DSA_PROJ = Q_W + 2 * KV_W + N_IDX_HEADS * IDX_DIM + IDX_DIM + N_IDX_HEADS
DSA_PROJ_PAD = -(-DSA_PROJ // 512) * 512


def _dsa_split(proj):
    k = proj[..., Q_W:Q_W + KV_W]
    v = proj[..., Q_W + KV_W:Q_W + 2 * KV_W]
    o = Q_W + 2 * KV_W + N_IDX_HEADS * IDX_DIM
    return k, v, proj[..., o:o + IDX_DIM]


def _dsa_prompt(proj, near_bias):
    bsz, t, _ = proj.shape
    k, v, ki = _dsa_split(proj)
    mask = _dsa_select(proj, tq=min(SELECT_TQ, t), tk=min(SELECT_TK, t), topk=min(IDX_TOPK, t // 4))
    o = _attention(proj, k.astype(BF16), v.astype(BF16), mask, near_bias, tq=ATTN_TQ, rep=N_HEADS // N_KV_B,
                   block_mask=False)
    return o, k.reshape(bsz, t, N_KV_B, HEAD_DIM), v.reshape(bsz, t, N_KV_B, HEAD_DIM), ki


def _dsa_sample(proj, k_pool, v_pool, kidx_pool, page_table, near_bias):
    bsz, t, _ = proj.shape
    n_phys = k_pool.shape[0]
    past_len = page_table.shape[1] * PAGE_SIZE
    k, v, ki = _dsa_split(proj)
    o_qi = Q_W + 2 * KV_W
    qi = proj[..., o_qi:o_qi + N_IDX_HEADS * IDX_DIM].reshape(bsz, t, N_IDX_HEADS, IDX_DIM)
    wi = proj[..., o_qi + N_IDX_HEADS * IDX_DIM + IDX_DIM:DSA_PROJ]
    qi_rows = qi.transpose(0, 2, 1, 3).reshape(bsz, N_IDX_HEADS * t, IDX_DIM)
    wi_rows = wi.transpose(0, 2, 1).reshape(bsz, N_IDX_HEADS * t, 1)
    sel = _dsa_select_paged(qi_rows, wi_rows, kidx_pool, page_table, _pad_rows(ki, PAGE_SIZE),
                            pp=8, topk=min(IDX_TOPK, (past_len + t) // 4))
    mask_t = jnp.tile(sel.transpose(0, 2, 1), (1, 1, N_HEADS))
    out = _paged_attention(
        _block_diag_queries(proj[..., :Q_W], N_KV_B),
        k_pool.reshape(n_phys, PAGE_SIZE, KV_W), v_pool.reshape(n_phys, PAGE_SIZE, KV_W), page_table,
        mask_t, _pad_rows(k, PAGE_SIZE), _pad_rows(v, PAGE_SIZE), _decode_bias(near_bias, ATTN_TQ, t), pp=4)
    o = _own_head_lanes(out, N_KV_B, t).astype(BF16)
    return o, k.reshape(bsz, t, N_KV_B, HEAD_DIM), v.reshape(bsz, t, N_KV_B, HEAD_DIM), ki


def _moba_prompt(proj, near_bias):
    bsz, t, _ = proj.shape
    k = proj[..., Q_W:2 * Q_W]
    v = proj[..., 2 * Q_W:3 * Q_W]
    pages = jnp.arange(bsz * (t // PAGE_SIZE), dtype=jnp.int32).reshape(bsz, t // PAGE_SIZE)
    means = _block_means(k.reshape(bsz * (t // PAGE_SIZE), PAGE_SIZE, Q_W), pages)
    sel = _moba_select(proj, _pad_rows(means, LANES), tq=ATTN_TQ, past_len=0)
    o = _attention(proj, k.astype(BF16), v.astype(BF16), sel, near_bias, tq=ATTN_TQ, rep=1, block_mask=True)
    return o, k.reshape(bsz, t, N_HEADS, HEAD_DIM), v.reshape(bsz, t, N_HEADS, HEAD_DIM)


def _moba_sample(proj, k_pool, v_pool, page_table, near_bias):
    bsz, t, _ = proj.shape
    n_phys = k_pool.shape[0]
    n_pages = page_table.shape[1]
    past_len = n_pages * PAGE_SIZE
    k = proj[..., Q_W:2 * Q_W]
    v = proj[..., 2 * Q_W:3 * Q_W]
    k_pages = k_pool.reshape(n_phys, PAGE_SIZE, Q_W)
    means = _block_means(k_pages, page_table)
    sel = _moba_select(proj, _pad_rows(means, LANES), tq=t, past_len=past_len)
    past = jnp.repeat(sel[..., :past_len // MOBA_BLOCK], MOBA_BLOCK, axis=-1)
    past_t = past.transpose(0, 3, 1, 2).reshape(bsz, past_len, N_HEADS * t)
    new_t = jnp.broadcast_to(_new_token_mask(t), (bsz, PAGE_SIZE, N_HEADS * t))
    out = _paged_attention(
        _block_diag_queries(proj[..., :Q_W], N_HEADS),
        k_pages, v_pool.reshape(n_phys, PAGE_SIZE, Q_W), page_table,
        jnp.concatenate([past_t, new_t], axis=1), _pad_rows(k, PAGE_SIZE), _pad_rows(v, PAGE_SIZE),
        _decode_bias(near_bias, ATTN_TQ, t), pp=4)
    o = _own_head_lanes(out, N_HEADS, t).astype(BF16)
    return o, k.reshape(bsz, t, N_HEADS, HEAD_DIM), v.reshape(bsz, t, N_HEADS, HEAD_DIM)


N_MIXERS = 3
PROMPT_TM = 1024
MATMUL_TN = 512
LRU_TT = 256
MOE_TM = 512
ROUTE_W = LANES


def kernel(x_prompt, x_sample, state_lru_h, state_lru_conv, cache_dsa_k, cache_dsa_v, cache_dsa_kidx,
           cache_moba_k, cache_moba_v, page_table, c_prompt, c_sample, rel_bias, norm_mix_g, norm_ffn_g,
           final_norm_g, ada_w, ada_b, lru_w_in, lru_conv_w, lru_conv_b, lru_w_rgate, lru_b_rgate,
           lru_w_igate, lru_b_igate, lru_lambda, lru_w_out, dsa_w_in, dsa_w_out, moba_w_in, moba_w_out,
           moe_w_group, moe_w_expert, moe_w1, moe_w3, moe_w2):
    bp, tp, d = x_prompt.shape
    bs, ts, _ = x_sample.shape
    depth = ada_w.shape[0]
    ns = bs * ts
    c_rnn = lru_w_out.shape[1]

    c_all = jnp.concatenate([c_prompt, c_sample], axis=0)
    c_rows = -(-c_all.shape[0] // SUBLANES) * SUBLANES
    mods = _ada(jnp.pad(c_all, ((0, c_rows - c_all.shape[0]), (0, 0))), ada_w, ada_b)
    mods = mods.reshape(depth, c_rows, 6, d)
    near_bias = _near_bias(rel_bias, ATTN_TQ)

    xp = x_prompt
    xs = x_sample.reshape(1, ns, d)
    outs = {n: [] for n in ("lru_h_p", "lru_c_p", "lru_h_s", "lru_c_s", "dsa_k_p", "dsa_v_p", "dsa_i_p",
                            "dsa_k_s", "dsa_v_s", "dsa_i_s", "moba_k_p", "moba_v_p", "moba_k_s", "moba_v_s")}
    for i in range(depth):
        kind, j = i % N_MIXERS, i // N_MIXERS
        mod_p = [mods[i, :bp, m][:, None, :] for m in range(6)]
        mod_s = [jnp.repeat(mods[i, bp:bp + bs, m], ts, axis=0)[None] for m in range(6)]
        shp1, scp1, gtp1, shp2, scp2, gtp2 = mod_p
        shs1, scs1, gts1, shs2, scs2, gts2 = mod_s
        mix_in = lambda w, tn=MATMUL_TN: (
            _norm_matmul(xp, norm_mix_g[i], shp1, scp1, w, tm=min(PROMPT_TM, tp), tn=tn),
            _norm_matmul(xs, norm_mix_g[i], shs1, scs1, w, tm=ns, tn=tn))
        if kind == 0:
            up, us = mix_in(lru_w_in[j])
            lru_params = (lru_conv_w[j], lru_conv_b[j], lru_w_rgate[j], lru_b_rgate[j], lru_w_igate[j],
                          lru_b_igate[j], lru_lambda[j])
            yp, hp_state, cp = _lru_core(up, jnp.zeros((bp, CONV_W - 1, c_rnn), F32), jnp.zeros((bp, c_rnn), F32),
                                         *lru_params, tt=LRU_TT)
            ys, hs_state, cs = _lru_core(us.reshape(bs, ts, 2 * c_rnn), state_lru_conv[j], state_lru_h[j],
                                         *lru_params, tt=ts)
            ys = ys.reshape(1, ns, c_rnn)
            w_out = lru_w_out[j]
            outs["lru_h_p"].append(hp_state)
            outs["lru_c_p"].append(cp)
            outs["lru_h_s"].append(hs_state)
            outs["lru_c_s"].append(cs)
        elif kind == 1:
            w_in = jnp.pad(dsa_w_in[j], ((0, 0), (0, DSA_PROJ_PAD - DSA_PROJ)))
            pp_, ps_ = mix_in(w_in)
            yp, kp, vp, ip = _dsa_prompt(pp_, near_bias)
            ys, k_s, v_s, i_s = _dsa_sample(ps_.reshape(bs, ts, DSA_PROJ_PAD), cache_dsa_k[j], cache_dsa_v[j],
                                            cache_dsa_kidx[j], page_table, near_bias)
            ys = ys.reshape(1, ns, Q_W)
            w_out = dsa_w_out[j]
            for n, val in zip(("dsa_k_p", "dsa_v_p", "dsa_i_p", "dsa_k_s", "dsa_v_s", "dsa_i_s"),
                              (kp, vp, ip, k_s, v_s, i_s)):
                outs[n].append(val)
        else:
            pp_, ps_ = mix_in(moba_w_in[j])
            yp, kp, vp = _moba_prompt(pp_, near_bias)
            ys, k_s, v_s = _moba_sample(ps_.reshape(bs, ts, 3 * Q_W), cache_moba_k[j], cache_moba_v[j],
                                        page_table, near_bias)
            ys = ys.reshape(1, ns, Q_W)
            w_out = moba_w_out[j]
            for n, val in zip(("moba_k_p", "moba_v_p", "moba_k_s", "moba_v_s"), (kp, vp, k_s, v_s)):
                outs[n].append(val)
        xp = _matmul_residual(yp, w_out, xp, gtp1, tm=min(PROMPT_TM, tp), tn=min(MATMUL_TN, d))
        xs = _matmul_residual(ys, w_out, xs, gts1, tm=ns, tn=min(MATMUL_TN, d))

        w_route = jnp.pad(jnp.concatenate([moe_w_group[i], moe_w_expert[i]], axis=1),
                          ((0, 0), (0, ROUTE_W - N_GROUPS - N_EXPERTS)))
        w1, w3, w2 = moe_w1[i].astype(BF16), moe_w3[i].astype(BF16), moe_w2[i].astype(BF16)
        lg_p, h_p = _norm_matmul(xp, norm_ffn_g[i], shp2, scp2, w_route, tm=min(PROMPT_TM, tp), tn=ROUTE_W,
                                 emit_h=True, highest=True)
        xp = _moe_dense(h_p, lg_p, w1, w3, w2, xp, gtp2, tm=min(MOE_TM, tp))
        lg_s, h_s = _norm_matmul(xs, norm_ffn_g[i], shs2, scs2, w_route, tm=ns, tn=ROUTE_W,
                                 emit_h=True, highest=True)
        xs = _moe_dense(h_s, lg_s, w1, w3, w2, xs, gts2, tm=ns)

    y_prompt = _rmsnorm(xp, final_norm_g, tm=min(MOE_TM, tp))
    y_sample = _rmsnorm(xs, final_norm_g, tm=ns).reshape(bs, ts, d)
    stack = lambda n: jnp.stack(outs[n])
    return (y_prompt, y_sample, stack("lru_h_p"), stack("lru_c_p"), stack("lru_h_s"), stack("lru_c_s"),
            stack("dsa_k_p"), stack("dsa_v_p"), stack("dsa_i_p"), stack("dsa_k_s"), stack("dsa_v_s"),
            stack("dsa_i_s"), stack("moba_k_p"), stack("moba_v_p"), stack("moba_k_s"), stack("moba_v_s"))
```

```python
import functools
import math

import numpy as np
import jax
import jax.numpy as jnp
from jax import lax
from jax.experimental import pallas as pl
from jax.experimental.pallas import tpu as pltpu

F32 = jnp.float32
BF16 = jnp.bfloat16

LANES = 128
SUBLANES = 8
VMEM_LIMIT_BYTES = 56 * 1024 * 1024

HEAD_DIM = 128
N_HEADS = 16
N_KV_B = 4
N_IDX_HEADS = 16
IDX_DIM = 64
IDX_TOPK = 256
IDX_SCALE = (IDX_DIM * N_IDX_HEADS) ** -0.5
PAGE_SIZE = 128
MOBA_BLOCK = 256
MOBA_TOPK = 3
N_BUCKETS = 32
MAX_DISTANCE = 128
N_RNN_BLOCKS = 16
CONV_W = 4
LRU_C = 8.0
N_GROUPS = 4
EXPERTS_PER_GROUP = 4
N_EXPERTS = N_GROUPS * EXPERTS_PER_GROUP
EPS = 1e-6
ATTN_SCALE = HEAD_DIM ** -0.5
NEG_INF = float("-inf")


def _params(*sem):
    return pltpu.CompilerParams(dimension_semantics=sem, vmem_limit_bytes=VMEM_LIMIT_BYTES)


def _bdot(a, b):
    return jnp.dot(a.astype(BF16), b.astype(BF16), preferred_element_type=F32)


def _bdot_nt(a, b):
    return lax.dot_general(a.astype(BF16), b.astype(BF16), (((1,), (1,)), ((), ())),
                           preferred_element_type=F32)


def _ada_kernel(c_ref, w_ref, b_ref, o_ref):
    c = c_ref[...]
    o_ref[...] = _bdot(c * jax.nn.sigmoid(c), w_ref[...]) + b_ref[...]


def _ada(c, ada_w, ada_b):
    depth, d, n = ada_w.shape
    rows = c.shape[0]
    tn = next(w for w in (1024, 512, 256, LANES) if n % w == 0)
    return pl.pallas_call(
        _ada_kernel,
        grid=(depth, n // tn),
        in_specs=[
            pl.BlockSpec((rows, d), lambda l, j: (0, 0)),
            pl.BlockSpec((None, d, tn), lambda l, j: (l, 0, j)),
            pl.BlockSpec((None, 1, tn), lambda l, j: (l, 0, j)),
        ],
        out_specs=pl.BlockSpec((None, rows, tn), lambda l, j: (l, 0, j)),
        out_shape=jax.ShapeDtypeStruct((depth, rows, n), F32),
        compiler_params=_params("arbitrary", "arbitrary"),
        name="ada_mod",
    )(c, ada_w, ada_b.reshape(depth, 1, n))


def _norm_mod(x, g, shift, scale):
    ms = jnp.mean(x * x, axis=-1, keepdims=True)
    h = x * lax.rsqrt(ms + EPS) * g
    return h * (1.0 + scale) + shift


def _norm_mm_kernel(x_ref, g_ref, sh_ref, sc_ref, w_ref, *refs, emit_h, highest):
    if emit_h:
        o_ref, ho_ref, h_ref = refs
    else:
        o_ref, h_ref = refs
        ho_ref = None

    @pl.when(pl.program_id(2) == 0)
    def _():
        h = _norm_mod(x_ref[...], g_ref[...], sh_ref[...], sc_ref[...])
        h_ref[...] = h.astype(h_ref.dtype)
        if ho_ref is not None:
            ho_ref[...] = h.astype(ho_ref.dtype)

    if highest:
        o_ref[...] = jnp.dot(h_ref[...], w_ref[...], precision=lax.Precision.HIGHEST,
                             preferred_element_type=F32)
    else:
        o_ref[...] = _bdot(h_ref[...], w_ref[...])


def _norm_matmul(x, g, shift, scale, w, *, tm, tn, emit_h=False, highest=False, h_dtype=BF16):
    bsz, t, d = x.shape
    n = w.shape[1]
    r = shift.shape[1]
    assert t % tm == 0 and n % tn == 0 and (r == 1 or r == tm == t)
    out_shape = [jax.ShapeDtypeStruct((bsz, t, n), F32)]
    out_specs = [pl.BlockSpec((None, tm, tn), lambda b, i, j: (b, i, j))]
    if emit_h:
        out_shape.append(jax.ShapeDtypeStruct((bsz, t, d), h_dtype))
        out_specs.append(pl.BlockSpec((None, tm, d), lambda b, i, j: (b, i, 0)))
    mod_map = (lambda b, i, j: (b, 0, 0)) if r == 1 else (lambda b, i, j: (b, i, 0))
    res = pl.pallas_call(
        functools.partial(_norm_mm_kernel, emit_h=emit_h, highest=highest),
        grid=(bsz, t // tm, n // tn),
        in_specs=[
            pl.BlockSpec((None, tm, d), lambda b, i, j: (b, i, 0)),
            pl.BlockSpec((1, d), lambda b, i, j: (0, 0)),
            pl.BlockSpec((None, r, d), mod_map),
            pl.BlockSpec((None, r, d), mod_map),
            pl.BlockSpec((d, tn), lambda b, i, j: (0, j)),
        ],
        out_specs=out_specs,
        out_shape=out_shape,
        scratch_shapes=[pltpu.VMEM((tm, d), F32 if highest else BF16)],
        compiler_params=_params("arbitrary", "arbitrary", "arbitrary"),
        name="norm_matmul",
    )(x, g.reshape(1, d), shift, scale, w)
    return res if emit_h else res[0]


def _mm_res_kernel(a_ref, w_ref, r_ref, g_ref, o_ref):
    o_ref[...] = r_ref[...] + g_ref[...] * _bdot(a_ref[...], w_ref[...])


def _matmul_residual(a, w, res, gate, *, tm, tn):
    bsz, t, k = a.shape
    n = w.shape[1]
    r = gate.shape[1]
    assert t % tm == 0 and n % tn == 0 and (r == 1 or r == tm == t)
    gate_map = (lambda b, i, j: (b, 0, j)) if r == 1 else (lambda b, i, j: (b, i, j))
    return pl.pallas_call(
        _mm_res_kernel,
        grid=(bsz, t // tm, n // tn),
        in_specs=[
            pl.BlockSpec((None, tm, k), lambda b, i, j: (b, i, 0)),
            pl.BlockSpec((k, tn), lambda b, i, j: (0, j)),
            pl.BlockSpec((None, tm, tn), lambda b, i, j: (b, i, j)),
            pl.BlockSpec((None, r, tn), gate_map),
        ],
        out_specs=pl.BlockSpec((None, tm, tn), lambda b, i, j: (b, i, j)),
        out_shape=jax.ShapeDtypeStruct((bsz, t, n), F32),
        compiler_params=_params("arbitrary", "arbitrary", "arbitrary"),
        name="matmul_residual",
    )(a, w, res, gate)


def _rmsnorm_kernel(x_ref, g_ref, o_ref):
    x = x_ref[...]
    ms = jnp.mean(x * x, axis=-1, keepdims=True)
    o_ref[...] = x * lax.rsqrt(ms + EPS) * g_ref[...]


def _rmsnorm(x, g, *, tm):
    bsz, t, d = x.shape
    return pl.pallas_call(
        _rmsnorm_kernel,
        grid=(bsz, t // tm),
        in_specs=[pl.BlockSpec((None, tm, d), lambda b, i: (b, i, 0)),
                  pl.BlockSpec((1, d), lambda b, i: (0, 0))],
        out_specs=pl.BlockSpec((None, tm, d), lambda b, i: (b, i, 0)),
        out_shape=jax.ShapeDtypeStruct((bsz, t, d), F32),
        compiler_params=_params("arbitrary", "arbitrary"),
        name="final_rmsnorm",
    )(x, g.reshape(1, d))


GATE_CHUNK = 640


def _log_sigmoid(x):
    return jnp.minimum(x, 0.0) - jnp.log1p(jnp.exp(-jnp.abs(x)))


def _lru_kernel(gb_ref, xb_ref, cs_ref, h0_ref, cw_ref, cb_ref, wr_ref, br_ref, wi_ref, bi_ref, lam_ref,
                y_ref, hl_ref, nc_ref, ext_ref, a_ref, b_ref, hs_ref, carry_ref, *, tt):
    c_all = xb_ref.shape[-1]

    @pl.when(pl.program_id(1) == 0)
    def _():
        ext_ref[0:SUBLANES, :] = cs_ref[...]
        carry_ref[...] = h0_ref[...]

    xb = xb_ref[...]
    ext_ref[SUBLANES:SUBLANES + tt, :] = xb
    cw = cw_ref[...]
    xc = cb_ref[...] + xb * cw[CONV_W - 1:CONV_W, :]
    for j in range(1, CONV_W):
        xc = xc + ext_ref[pl.ds(SUBLANES - j, tt), :] * cw[CONV_W - 1 - j:CONV_W - j, :]
    tail = ext_ref[tt:tt + SUBLANES, :]
    ext_ref[0:SUBLANES, :] = tail
    nc_ref[...] = tail

    xcb = xc.astype(BF16)
    r_parts, i_parts = [], []
    for c in range(c_all // GATE_CHUNK):
        sl = slice(c * GATE_CHUNK, (c + 1) * GATE_CHUNK)
        r_parts.append(jnp.dot(xcb[:, sl], wr_ref[c], preferred_element_type=F32))
        i_parts.append(jnp.dot(xcb[:, sl], wi_ref[c], preferred_element_type=F32))
    r = jax.nn.sigmoid(jnp.concatenate(r_parts, axis=1) + br_ref[...])
    ig = jax.nn.sigmoid(jnp.concatenate(i_parts, axis=1) + bi_ref[...])
    log_a = LRU_C * r * _log_sigmoid(lam_ref[...])
    th = jnp.tanh(log_a)
    one_minus_a2 = -2.0 * th / (1.0 - th)
    a_ref[...] = jnp.exp(log_a)
    b_ref[...] = jnp.sqrt(jnp.maximum(one_minus_a2, 0.0)) * (ig * xc)

    row = lax.broadcasted_iota(jnp.int32, (SUBLANES, GATE_CHUNK), 0)
    for c in range(c_all // GATE_CHUNK):
        sl = slice(c * GATE_CHUNK, (c + 1) * GATE_CHUNK)

        def body(grp, carry, sl=sl):
            r8 = pl.multiple_of(grp * SUBLANES, SUBLANES)
            av = a_ref[pl.ds(r8, SUBLANES), sl]
            bv = b_ref[pl.ds(r8, SUBLANES), sl]
            for s in (1, 2, 4):
                m = row >= s
                a_sh = jnp.where(m, pltpu.roll(av, s, axis=0), 1.0)
                b_sh = jnp.where(m, pltpu.roll(bv, s, axis=0), 0.0)
                bv = av * b_sh + bv
                av = av * a_sh
            h = av * carry + bv
            hs_ref[pl.ds(r8, SUBLANES), sl] = h
            return h[SUBLANES - 1:SUBLANES, :]

        carry_ref[:, sl] = lax.fori_loop(0, tt // SUBLANES, body, carry_ref[:, sl])

    hl_ref[...] = carry_ref[...]
    y_ref[...] = (jax.nn.gelu(gb_ref[...]) * hs_ref[...]).astype(y_ref.dtype)


def _lru_core(u, conv_state, h0, conv_w, conv_b, w_r, b_r, w_i, b_i, lam, *, tt):
    bsz, t, c2 = u.shape
    c = c2 // 2
    blk = c // N_RNN_BLOCKS
    per = GATE_CHUNK // blk
    nchunk = c // GATE_CHUNK

    def block_diag(w):
        wc = w.reshape(nchunk, per, blk, blk).astype(BF16)
        eye = jnp.eye(per, dtype=BF16)
        return jnp.einsum("cpij,pq->cpiqj", wc, eye).reshape(nchunk, GATE_CHUNK, GATE_CHUNK)

    cs_pad = jnp.pad(conv_state, ((0, 0), (SUBLANES - (CONV_W - 1), 0), (0, 0)))
    row = lambda v: v.reshape(1, c)
    full = lambda shp: pl.BlockSpec(shp, lambda b, i: (0,) * len(shp))
    y, hl, nc = pl.pallas_call(
        functools.partial(_lru_kernel, tt=tt),
        grid=(bsz, t // tt),
        in_specs=[
            pl.BlockSpec((None, tt, c), lambda b, i: (b, i, 0)),
            pl.BlockSpec((None, tt, c), lambda b, i: (b, i, 1)),
            pl.BlockSpec((None, SUBLANES, c), lambda b, i: (b, 0, 0)),
            pl.BlockSpec((None, 1, c), lambda b, i: (b, 0, 0)),
            full((CONV_W, c)), full((1, c)),
            full((nchunk, GATE_CHUNK, GATE_CHUNK)), full((1, c)),
            full((nchunk, GATE_CHUNK, GATE_CHUNK)), full((1, c)),
            full((1, c)),
        ],
        out_specs=[
            pl.BlockSpec((None, tt, c), lambda b, i: (b, i, 0)),
            pl.BlockSpec((None, 1, c), lambda b, i: (b, 0, 0)),
            pl.BlockSpec((None, SUBLANES, c), lambda b, i: (b, 0, 0)),
        ],
        out_shape=[
            jax.ShapeDtypeStruct((bsz, t, c), BF16),
            jax.ShapeDtypeStruct((bsz, 1, c), F32),
            jax.ShapeDtypeStruct((bsz, SUBLANES, c), F32),
        ],
        scratch_shapes=[
            pltpu.VMEM((tt + SUBLANES, c), F32),
            pltpu.VMEM((tt, c), F32),
            pltpu.VMEM((tt, c), F32),
            pltpu.VMEM((tt, c), F32),
            pltpu.VMEM((1, c), F32),
        ],
        compiler_params=_params("arbitrary", "arbitrary"),
        name="lru_core",
    )(u, u, cs_pad, h0.reshape(bsz, 1, c), conv_w, row(conv_b), block_diag(w_r), row(b_r),
      block_diag(w_i), row(b_i), row(lam))
    return y, hl.reshape(bsz, c), nc[:, SUBLANES - (CONV_W - 1):, :]


def _route_gates(lg):
    return _route(lg)[0]


def _route(lg):
    col = lax.broadcasted_iota(jnp.int32, lg.shape, 1)
    big = jnp.int32(LANES)
    is_g = col < N_GROUPS
    gl = jnp.where(is_g, lg, NEG_INF)
    gmax = jnp.max(gl, axis=1, keepdims=True)
    g_sel = jnp.min(jnp.where(gl == gmax, col, big), axis=1, keepdims=True)
    g_w = 1.0 / jnp.sum(jnp.exp(gl - gmax), axis=1, keepdims=True)
    eid = col - N_GROUPS
    in_grp = (eid >= 0) & (eid < N_EXPERTS) & ((eid // EXPERTS_PER_GROUP) == g_sel)
    e_in = jnp.where(in_grp, lg, NEG_INF)
    top1 = jnp.max(e_in, axis=1, keepdims=True)
    idx1 = jnp.min(jnp.where((e_in == top1) & in_grp, col, big), axis=1, keepdims=True)
    rest = in_grp & (col != idx1)
    e2 = jnp.where(rest, lg, NEG_INF)
    top2 = jnp.max(e2, axis=1, keepdims=True)
    idx2 = jnp.min(jnp.where((e2 == top2) & rest, col, big), axis=1, keepdims=True)
    z = jnp.exp(top2 - top1)
    w_first = g_w / (1.0 + z)
    w_second = g_w * z / (1.0 + z)
    return jnp.where(col == idx1, w_first, 0.0) + jnp.where(col == idx2, w_second, 0.0), g_sel


def _moe_dense_kernel(h_ref, lg_ref, w1_ref, w3_ref, w2_ref, x_ref, g_ref, o_ref, gates_ref, acc_ref):
    e = pl.program_id(2)

    @pl.when(e == 0)
    def _():
        gates_ref[...] = _route_gates(lg_ref[...])
        acc_ref[...] = jnp.zeros_like(acc_ref)

    gates = gates_ref[...]
    col = lax.broadcasted_iota(jnp.int32, gates.shape, 1)
    ge = jnp.sum(jnp.where(col == e + N_GROUPS, gates, 0.0), axis=1, keepdims=True)
    h = h_ref[...]
    a1 = jnp.dot(h, w1_ref[...], preferred_element_type=F32)
    a3 = jnp.dot(h, w3_ref[...], preferred_element_type=F32)
    hidden = (a1 * jax.nn.sigmoid(a1)) * a3 * ge
    acc_ref[...] += jnp.dot(hidden.astype(BF16), w2_ref[...], preferred_element_type=F32)

    @pl.when(e == pl.num_programs(2) - 1)
    def _():
        o_ref[...] = x_ref[...] + g_ref[...] * acc_ref[...]


def _moe_dense(h, logits, w1, w3, w2, x, gate, *, tm):
    bsz, t, d = x.shape
    ne, _, f = w1.shape
    r = gate.shape[1]
    gate_map = (lambda b, i, e: (b, 0, 0)) if r == 1 else (lambda b, i, e: (b, i, 0))
    tok = lambda b, i, e: (b, i, 0)
    return pl.pallas_call(
        _moe_dense_kernel,
        grid=(bsz, t // tm, ne),
        in_specs=[
            pl.BlockSpec((None, tm, d), tok),
            pl.BlockSpec((None, tm, LANES), tok),
            pl.BlockSpec((None, d, f), lambda b, i, e: (e, 0, 0)),
            pl.BlockSpec((None, d, f), lambda b, i, e: (e, 0, 0)),
            pl.BlockSpec((None, f, d), lambda b, i, e: (e, 0, 0)),
            pl.BlockSpec((None, tm, d), tok),
            pl.BlockSpec((None, r, d), gate_map),
        ],
        out_specs=pl.BlockSpec((None, tm, d), tok),
        out_shape=jax.ShapeDtypeStruct((bsz, t, d), F32),
        scratch_shapes=[pltpu.VMEM((tm, LANES), F32), pltpu.VMEM((tm, d), F32)],
        compiler_params=_params("arbitrary", "arbitrary", "arbitrary"),
        name="moe_dense",
    )(h, logits, w1, w3, w2, x, gate)


INFO_GROUP_LANE = 0
INFO_RANK_LANE = 1


def _route_kernel(lg_ref, info_ref, cnt_ref, carry_ref):
    @pl.when(pl.program_id(0) == 0)
    def _():
        carry_ref[...] = jnp.zeros_like(carry_ref)

    gates, g_sel = _route(lg_ref[...])
    tm = gates.shape[0]
    col = lax.broadcasted_iota(jnp.int32, gates.shape, 1)
    onehot = (col == g_sel).astype(F32)
    before = (lax.broadcasted_iota(jnp.int32, (tm, tm), 0) > lax.broadcasted_iota(jnp.int32, (tm, tm), 1))
    earlier = jnp.dot(before.astype(BF16), onehot.astype(BF16), preferred_element_type=F32) + carry_ref[...]
    rank = jnp.sum(jnp.where(col == g_sel, earlier, 0.0), axis=1, keepdims=True)
    carry_ref[...] += jnp.sum(onehot, axis=0, keepdims=True)
    info_ref[...] = (gates + jnp.where(col == INFO_GROUP_LANE, g_sel.astype(F32), 0.0)
                     + jnp.where(col == INFO_RANK_LANE, rank, 0.0))
    cnt_ref[...] = carry_ref[...]


def _route_tokens(logits, *, tm):
    n = logits.shape[0]
    assert n % tm == 0
    return pl.pallas_call(
        _route_kernel,
        grid=(n // tm,),
        in_specs=[pl.BlockSpec((tm, LANES), lambda i: (i, 0))],
        out_specs=[pl.BlockSpec((tm, LANES), lambda i: (i, 0)), pl.BlockSpec((1, LANES), lambda i: (0, 0))],
        out_shape=[jax.ShapeDtypeStruct((n, LANES), F32), jax.ShapeDtypeStruct((1, LANES), F32)],
        scratch_shapes=[pltpu.VMEM((1, LANES), F32)],
        compiler_params=_params("arbitrary"),
        name="moe_route",
    )(logits)


def _group_layout(info, counts, tm):
    n = info.shape[0]
    g_sel = info[:, INFO_GROUP_LANE].astype(jnp.int32)
    rank = info[:, INFO_RANK_LANE].astype(jnp.int32)
    cnt = counts[0, :N_GROUPS].astype(jnp.int32)
    padded = -(-cnt // tm) * tm
    ends = jnp.cumsum(padded)
    starts = ends - padded
    n_slots = n + N_GROUPS * tm
    src = jnp.zeros((n_slots,), jnp.int32).at[starts[g_sel] + rank].set(jnp.arange(n, dtype=jnp.int32))
    tile_start = jnp.arange(n_slots // tm, dtype=jnp.int32) * tm
    tile_group = jnp.minimum(jnp.sum((tile_start[:, None] >= ends[None, :]).astype(jnp.int32), axis=1), N_GROUPS - 1)
    occupied = jnp.clip((starts + cnt)[tile_group] - tile_start, 0, tm)
    occupied = jnp.where(tile_start < ends[-1], occupied, 0)
    return src, tile_group, occupied, (ends[-1] // tm).reshape(1)


def _wait_row_copies(src, dst, sem, rows):
    def wait(n):
        def body(c, carry):
            pltpu.make_async_copy(src.at[pl.ds(0, n), :], dst.at[pl.ds(0, n), :], sem).wait()
            return carry
        return body
    lax.fori_loop(0, rows // SUBLANES, wait(SUBLANES), 0)
    lax.fori_loop(0, rows % SUBLANES, wait(1), 0)


def _moe_group_kernel(src_ref, tg_ref, occ_ref, used_ref, h_hbm, info_hbm, w1_ref, w3_ref, w2_ref, out_hbm,
                      xbuf, xb, gbuf, acc, sem_x, sem_g, sem_o, *, tm):
    i = pl.program_id(0)
    e = pl.program_id(1)
    base = i * tm
    live = i < used_ref[0]
    rows = occ_ref[i]

    @pl.when((i == 0) & (e == 0))
    def _():
        xbuf[...] = jnp.zeros_like(xbuf)
        gbuf[...] = jnp.zeros_like(gbuf)

    @pl.when(live & (e == 0))
    def _():
        def issue(r, carry):
            tok = src_ref[base + r]
            pltpu.make_async_copy(h_hbm.at[pl.ds(tok, 1), :], xbuf.at[pl.ds(r, 1), :], sem_x).start()
            pltpu.make_async_copy(info_hbm.at[pl.ds(tok, 1), :], gbuf.at[pl.ds(r, 1), :], sem_g).start()
            return carry
        lax.fori_loop(0, rows, issue, 0)
        _wait_row_copies(h_hbm, xbuf, sem_x, rows)
        _wait_row_copies(info_hbm, gbuf, sem_g, rows)
        xb[...] = xbuf[...].astype(BF16)
        acc[...] = jnp.zeros_like(acc)

    @pl.when(live)
    def _():
        lane = lax.broadcasted_iota(jnp.int32, (tm, LANES), 1)
        gate_lane = N_GROUPS + EXPERTS_PER_GROUP * tg_ref[i] + e
        ge = jnp.sum(jnp.where(lane == gate_lane, gbuf[...], 0.0), axis=1, keepdims=True)
        x = xb[...]
        a1 = jnp.dot(x, w1_ref[...], preferred_element_type=F32)
        a3 = jnp.dot(x, w3_ref[...], preferred_element_type=F32)
        hidden = (a1 * jax.nn.sigmoid(a1)) * a3 * ge
        acc[...] += jnp.dot(hidden.astype(BF16), w2_ref[...], preferred_element_type=F32)

    @pl.when(live & (e == EXPERTS_PER_GROUP - 1))
    def _():
        def issue(r, carry):
            tok = src_ref[base + r]
            pltpu.make_async_copy(acc.at[pl.ds(r, 1), :], out_hbm.at[pl.ds(tok, 1), :], sem_o).start()
            return carry
        lax.fori_loop(0, rows, issue, 0)
        _wait_row_copies(acc, out_hbm, sem_o, rows)


def _moe_grouped(h, info, src, tile_group, occupied, n_used, w1, w3, w2, *, tm):
    n, d = h.shape
    f = w1.shape[2]
    n_slots = src.shape[0]

    def expert(i, e, src_, tg, occ, used):
        e_eff = jnp.where(i < used[0], e, EXPERTS_PER_GROUP - 1)
        return (tg[i] * EXPERTS_PER_GROUP + e_eff, 0, 0)

    return pl.pallas_call(
        functools.partial(_moe_group_kernel, tm=tm),
        grid_spec=pltpu.PrefetchScalarGridSpec(
            num_scalar_prefetch=4,
            grid=(n_slots // tm, EXPERTS_PER_GROUP),
            in_specs=[
                pl.BlockSpec(memory_space=pl.ANY),
                pl.BlockSpec(memory_space=pl.ANY),
                pl.BlockSpec((None, d, f), expert),
                pl.BlockSpec((None, d, f), expert),
                pl.BlockSpec((None, f, d), expert),
            ],
            out_specs=pl.BlockSpec(memory_space=pl.ANY),
            scratch_shapes=[
                pltpu.VMEM((tm, d), F32), pltpu.VMEM((tm, d), BF16), pltpu.VMEM((tm, LANES), F32),
                pltpu.VMEM((tm, d), F32),
                pltpu.SemaphoreType.DMA(()), pltpu.SemaphoreType.DMA(()), pltpu.SemaphoreType.DMA(()),
            ],
        ),
        out_shape=jax.ShapeDtypeStruct((n, d), F32),
        compiler_params=_params("arbitrary", "arbitrary"),
        name="moe_grouped",
    )(src, tile_group, occupied, n_used, h, info, w1, w3, w2)


def _gated_add_kernel(x_ref, y_ref, g_ref, o_ref):
    o_ref[...] = x_ref[...] + g_ref[...] * y_ref[...]


def _gated_add(x, y, gate, *, tm):
    bsz, t, d = x.shape
    per_b = t // tm
    return pl.pallas_call(
        _gated_add_kernel,
        grid=(bsz, per_b),
        in_specs=[
            pl.BlockSpec((None, tm, d), lambda b, i: (b, i, 0)),
            pl.BlockSpec((tm, d), lambda b, i: (b * per_b + i, 0)),
            pl.BlockSpec((None, 1, d), lambda b, i: (b, 0, 0)),
        ],
        out_specs=pl.BlockSpec((None, tm, d), lambda b, i: (b, i, 0)),
        out_shape=jax.ShapeDtypeStruct((bsz, t, d), F32),
        compiler_params=_params("arbitrary", "arbitrary"),
        name="gated_add",
    )(x, y, gate)


MASKED = -1e30


def _t5_bucket_table(tq):
    qi = np.arange(tq, dtype=np.int32)[:, None]
    col = np.arange(2 * tq, dtype=np.int32)[None, :]
    n = np.maximum(qi + tq - col, 0)
    max_exact = N_BUCKETS // 2
    nf = np.maximum(n, 1).astype(np.float32)
    large = max_exact + (np.log(nf / np.float32(max_exact)) / np.float32(math.log(MAX_DISTANCE / max_exact))
                         * np.float32(N_BUCKETS - max_exact)).astype(np.int32)
    table = np.where(n < max_exact, n, np.minimum(large, N_BUCKETS - 1)).astype(np.int32)
    assert table[0, 0] == N_BUCKETS - 1
    return table


def _bias_kernel(rb_ref, bk_ref, o_ref):
    h = pl.program_id(0)
    bk = bk_ref[...]
    acc = jnp.zeros(bk.shape, F32)
    for k in range(N_BUCKETS):
        acc = jnp.where(bk == k, rb_ref[k, h], acc)
    o_ref[...] = acc - rb_ref[N_BUCKETS - 1, h]


def _near_bias(rel_bias, tq):
    return pl.pallas_call(
        _bias_kernel,
        grid=(N_HEADS,),
        in_specs=[pl.BlockSpec(memory_space=pltpu.SMEM),
                  pl.BlockSpec((tq, 2 * tq), lambda h: (0, 0))],
        out_specs=pl.BlockSpec((None, tq, 2 * tq), lambda h: (h, 0, 0)),
        out_shape=jax.ShapeDtypeStruct((N_HEADS, tq, 2 * tq), F32),
        name="near_bias",
    )(rel_bias, jnp.asarray(_t5_bucket_table(tq)))


_NEG_INF_KEY = int(np.int32(np.array(-np.inf, np.float32).view(np.int32)) ^ np.int32(0x7FFFFFFF))


def _sort_key(x):
    k = pltpu.bitcast(x, jnp.int32)
    return k ^ ((k >> 31) & jnp.int32(0x7FFFFFFF))


def _kth_largest_key(count_ge, kth):
    zero = jnp.int32(0)
    ans = jnp.where(count_ge(zero) >= kth, zero, jnp.int32(-2 ** 31))

    def body(it, ans):
        cand = ans | (jnp.int32(1) << (30 - it))
        return jnp.where(count_ge(cand) >= kth, cand, ans)

    return lax.fori_loop(0, 31, body, ans)


def _dsa_select_kernel(qi_ref, kw_ref, kall_ref, m_ref, key_ref, *, tq, tk, topk):
    i = pl.program_id(1)
    n_kt = (i * tq + tq + tk - 1) // tk
    qpos = i * tq + lax.broadcasted_iota(jnp.int32, (tq, tk), 0)
    wi = kw_ref[:, IDX_DIM:IDX_DIM + N_IDX_HEADS]
    qi = qi_ref[...].astype(BF16)
    key_ref[...] = jnp.full(key_ref.shape, _NEG_INF_KEY, jnp.int32)

    def score_tile(kt, carry):
        k0 = pl.multiple_of(kt * tk, tk)
        kid = kall_ref[pl.ds(k0, tk), 0:IDX_DIM].astype(BF16)
        acc = jnp.zeros((tq, tk), F32)
        for h in range(N_IDX_HEADS):
            s = _bdot_nt(qi[:, h * IDX_DIM:(h + 1) * IDX_DIM], kid)
            acc = acc + wi[:, h:h + 1] * jnp.maximum(s, 0.0)
        kpos = k0 + lax.broadcasted_iota(jnp.int32, (tq, tk), 1)
        sc = jnp.where(kpos <= qpos, acc * IDX_SCALE, NEG_INF)
        key_ref[:, pl.ds(k0, tk)] = _sort_key(sc)
        return carry

    lax.fori_loop(0, n_kt, score_tile, 0)

    def count_ge(cand):
        def body(kt, acc):
            k0 = pl.multiple_of(kt * tk, tk)
            ge = (key_ref[:, pl.ds(k0, tk)] >= cand).astype(jnp.int32)
            for j in range(tk // LANES):
                acc = acc + ge[:, j * LANES:(j + 1) * LANES]
            return acc
        acc = lax.fori_loop(0, n_kt, body, jnp.zeros((tq, LANES), jnp.int32))
        return jnp.sum(acc, axis=1, keepdims=True)

    thr = _kth_largest_key(count_ge, topk)
    keys = key_ref[...]
    m_ref[...] = ((keys >= thr) & (keys > _NEG_INF_KEY)).astype(m_ref.dtype)


def _dsa_select(proj, *, tq, tk, topk):
    bsz, t, _ = proj.shape
    qi_w = N_IDX_HEADS * IDX_DIM
    qi_blk = (N_HEADS * HEAD_DIM + 2 * N_KV_B * HEAD_DIM) // qi_w
    kw_blk = (N_HEADS * HEAD_DIM + 2 * N_KV_B * HEAD_DIM + qi_w) // LANES
    return pl.pallas_call(
        functools.partial(_dsa_select_kernel, tq=tq, tk=tk, topk=topk),
        grid=(bsz, t // tq),
        in_specs=[
            pl.BlockSpec((None, tq, qi_w), lambda b, i: (b, i, qi_blk)),
            pl.BlockSpec((None, tq, LANES), lambda b, i: (b, i, kw_blk)),
            pl.BlockSpec((None, t, LANES), lambda b, i: (b, 0, kw_blk)),
        ],
        out_specs=pl.BlockSpec((None, tq, t), lambda b, i: (b, i, 0)),
        out_shape=jax.ShapeDtypeStruct((bsz, t, t), BF16),
        scratch_shapes=[pltpu.VMEM((tq, t), jnp.int32)],
        compiler_params=_params("arbitrary", "arbitrary"),
        name="dsa_select",
    )(proj, proj, proj)


def _attn_kernel(q_ref, k_ref, v_ref, m_ref, tb_ref, o_ref, m_sc, l_sc, acc_sc, *, tq, rep, shared_kv, block_mask):
    i = pl.program_id(2)
    m_sc[...] = jnp.full(m_sc.shape, MASKED, F32)
    l_sc[...] = jnp.zeros_like(l_sc)
    acc_sc[...] = jnp.zeros_like(acc_sc)
    lane_tiles = tq // LANES

    def key_mask(j, r, causal):
        if block_mask:
            lane = lax.broadcasted_iota(jnp.int32, (tq, LANES), 1)
            vis = jnp.sum(jnp.where(lane == j, m_ref[r].astype(F32), 0.0), axis=1, keepdims=True)
            ok = jnp.broadcast_to(vis > 0.0, (tq, tq))
        else:
            ok = m_ref[:, pl.ds(pl.multiple_of(j * tq, tq), tq)] > 0
        if causal:
            ok = ok & (lax.broadcasted_iota(jnp.int32, (tq, tq), 0) >= lax.broadcasted_iota(jnp.int32, (tq, tq), 1))
        return ok

    def step(j, bias_lo, causal):
        k0 = pl.multiple_of(j * tq, tq)
        ok_shared = None if block_mask else key_mask(j, 0, causal)
        for r in range(rep):
            kv = slice(0, HEAD_DIM) if shared_kv else slice(r * HEAD_DIM, (r + 1) * HEAD_DIM)
            kt = k_ref[pl.ds(k0, tq), kv]
            vt = v_ref[pl.ds(k0, tq), kv]
            ok = key_mask(j, r, causal) if block_mask else ok_shared
            q = q_ref[:, r * HEAD_DIM:(r + 1) * HEAD_DIM].astype(BF16)
            s = _bdot_nt(q, kt) * ATTN_SCALE
            if bias_lo is not None:
                s = s + tb_ref[r, :, bias_lo:bias_lo + tq]
            s = jnp.where(ok, s, MASKED)
            m_old = m_sc[r]
            m_new = jnp.maximum(m_old, jnp.max(s, axis=1, keepdims=True))
            alpha = jnp.exp(m_old - m_new)
            p = jnp.exp(s - jnp.concatenate([m_new] * lane_tiles, axis=1))
            l_sc[r] = alpha * l_sc[r] + jnp.sum(p, axis=1, keepdims=True)
            acc_sc[r] = alpha * acc_sc[r] + jnp.dot(p.astype(BF16), vt, preferred_element_type=F32)
            m_sc[r] = m_new

    def far(j, carry):
        step(j, None, False)
        return carry

    lax.fori_loop(0, jnp.maximum(i - 1, 0), far, 0)

    @pl.when(i >= 1)
    def _():
        step(i - 1, 0, False)

    step(i, tq, True)
    o_ref[...] = jnp.concatenate([acc_sc[r] / l_sc[r] for r in range(rep)], axis=1).astype(o_ref.dtype)


def _attention(q_src, k, v, mask, near_bias, *, tq, rep, shared_kv, block_mask):
    bsz, t, _ = q_src.shape
    g = N_HEADS // rep
    kv_w = HEAD_DIM if shared_kv else rep * HEAD_DIM
    assert HEAD_DIM == LANES and k.shape[-1] == g * kv_w
    if block_mask:
        assert tq == MOBA_BLOCK
        mask_spec = pl.BlockSpec((None, rep, tq, LANES), lambda b, h, i: (b, h, i, 0))
    else:
        mask_spec = pl.BlockSpec((None, tq, t), lambda b, h, i: (b, i, 0))
    stat = pltpu.VMEM((rep, tq, LANES), F32)
    return pl.pallas_call(
        functools.partial(_attn_kernel, tq=tq, rep=rep, shared_kv=shared_kv, block_mask=block_mask),
        grid=(bsz, g, t // tq),
        in_specs=[
            pl.BlockSpec((None, tq, rep * HEAD_DIM), lambda b, h, i: (b, i, h)),
            pl.BlockSpec((None, t, kv_w), lambda b, h, i: (b, 0, h)),
            pl.BlockSpec((None, t, kv_w), lambda b, h, i: (b, 0, h)),
            mask_spec,
            pl.BlockSpec((rep, tq, 2 * tq), lambda b, h, i: (h, 0, 0)),
        ],
        out_specs=pl.BlockSpec((None, tq, rep * HEAD_DIM), lambda b, h, i: (b, i, h)),
        out_shape=jax.ShapeDtypeStruct((bsz, t, N_HEADS * HEAD_DIM), BF16),
        scratch_shapes=[stat, stat, stat],
        compiler_params=_params("arbitrary", "arbitrary", "arbitrary"),
        name="masked_attention",
    )(q_src, k, v, mask, near_bias)


def _block_means_kernel(tbl_ref, p0_ref, p1_ref, o_ref):
    del tbl_ref
    o_ref[...] = (jnp.sum(p0_ref[...], axis=0) + jnp.sum(p1_ref[...], axis=0)) / MOBA_BLOCK


def _block_means(pool, layer, table):
    _, _, _, h, dh = pool.shape
    bsz, npg = table.shape
    nb = npg // 2
    page = lambda k: pl.BlockSpec((None, None, PAGE_SIZE, h, dh), lambda b, n, tbl: (layer, tbl[b, 2 * n + k], 0, 0, 0))
    out = pl.pallas_call(
        _block_means_kernel,
        grid_spec=pltpu.PrefetchScalarGridSpec(
            num_scalar_prefetch=1,
            grid=(bsz, nb),
            in_specs=[page(0), page(1)],
            out_specs=pl.BlockSpec((None, None, h, dh), lambda b, n, tbl: (b, n, 0, 0)),
        ),
        out_shape=jax.ShapeDtypeStruct((bsz, nb, h, dh), F32),
        compiler_params=_params("arbitrary", "arbitrary"),
        name="block_means",
    )(table, pool, pool)
    return out.reshape(bsz, nb, h * dh)


def _moba_select_kernel(q_ref, mean_ref, o_ref, *, tq, past_len):
    i = pl.program_id(1)
    own = (past_len + i * tq + lax.broadcasted_iota(jnp.int32, (tq, LANES), 0)) // MOBA_BLOCK
    blk = lax.broadcasted_iota(jnp.int32, (tq, LANES), 1)
    past = blk < own
    for h in range(N_HEADS):
        sl = slice(h * HEAD_DIM, (h + 1) * HEAD_DIM)
        gate = lax.dot_general(q_ref[:, sl], mean_ref[:, sl], (((1,), (1,)), ((), ())),
                               precision=lax.Precision.HIGHEST, preferred_element_type=F32)
        g = jnp.where(past, gate, NEG_INF)
        kth = jnp.max(g, axis=1, keepdims=True)
        for _ in range(MOBA_TOPK - 1):
            kth = jnp.max(jnp.where(g < kth, g, NEG_INF), axis=1, keepdims=True)
        o_ref[h] = ((past & (g >= kth)) | (blk == own)).astype(o_ref.dtype)


def _moba_select(q_src, means, *, tq, past_len):
    bsz, t, _ = q_src.shape
    w = N_HEADS * HEAD_DIM
    return pl.pallas_call(
        functools.partial(_moba_select_kernel, tq=tq, past_len=past_len),
        grid=(bsz, t // tq),
        in_specs=[
            pl.BlockSpec((None, tq, w), lambda b, i: (b, i, 0)),
            pl.BlockSpec((None, LANES, w), lambda b, i: (b, 0, 0)),
        ],
        out_specs=pl.BlockSpec((None, N_HEADS, tq, LANES), lambda b, i: (b, 0, i, 0)),
        out_shape=jax.ShapeDtypeStruct((bsz, N_HEADS, t, LANES), BF16),
        compiler_params=_params("arbitrary", "arbitrary"),
        name="moba_select",
    )(q_src, means)


def _dsa_select_paged_kernel(tbl_ref, qi_ref, wi_ref, *rest, pp, topk, past_len, t_new):
    del tbl_ref
    page_refs = rest[:pp]
    knew_ref, m_ref, key_ref = rest[pp:]
    p = pl.program_id(1)
    qi = qi_ref[...].astype(BF16)
    wi = wi_ref[...]

    def score(kpage):
        s = _bdot_nt(qi, kpage)
        w = wi * jnp.maximum(s, 0.0)
        return jnp.sum(w.reshape(N_IDX_HEADS, t_new, PAGE_SIZE), axis=0) * IDX_SCALE

    for k in range(pp):
        off = pl.multiple_of((p * pp + k) * PAGE_SIZE, PAGE_SIZE)
        key_ref[:, pl.ds(off, PAGE_SIZE)] = _sort_key(score(page_refs[k][...]))

    @pl.when(p == pl.num_programs(1) - 1)
    def _():
        row = lax.broadcasted_iota(jnp.int32, (t_new, PAGE_SIZE), 0)
        col = lax.broadcasted_iota(jnp.int32, (t_new, PAGE_SIZE), 1)
        sc = jnp.where(col <= row, score(knew_ref[...]), NEG_INF)
        key_ref[:, past_len:past_len + PAGE_SIZE] = _sort_key(sc)

        def count_ge(cand):
            return jnp.sum((key_ref[...] >= cand).astype(jnp.int32), axis=1, keepdims=True)

        thr = _kth_largest_key(count_ge, topk)
        keys = key_ref[...]
        m_ref[...] = ((keys >= thr) & (keys > _NEG_INF_KEY)).astype(m_ref.dtype)


def _dsa_select_paged(qi_rows, wi_rows, kidx_pool, layer, page_table, kidx_new, *, pp, topk):
    bsz, rows, _ = qi_rows.shape
    t_new = rows // N_IDX_HEADS
    n_pages = page_table.shape[1]
    assert n_pages % pp == 0
    past_len = n_pages * PAGE_SIZE
    lpad = past_len + PAGE_SIZE
    page_spec = lambda k: pl.BlockSpec((None, None, PAGE_SIZE, IDX_DIM),
                                       lambda b, p, tbl: (layer, tbl[b, p * pp + k], 0, 0))
    return pl.pallas_call(
        functools.partial(_dsa_select_paged_kernel, pp=pp, topk=topk, past_len=past_len, t_new=t_new),
        grid_spec=pltpu.PrefetchScalarGridSpec(
            num_scalar_prefetch=1,
            grid=(bsz, n_pages // pp),
            in_specs=[
                pl.BlockSpec((None, rows, IDX_DIM), lambda b, p, tbl: (b, 0, 0)),
                pl.BlockSpec((None, rows, 1), lambda b, p, tbl: (b, 0, 0)),
                *[page_spec(k) for k in range(pp)],
                pl.BlockSpec((None, PAGE_SIZE, IDX_DIM), lambda b, p, tbl: (b, 0, 0)),
            ],
            out_specs=pl.BlockSpec((None, t_new, lpad), lambda b, p, tbl: (b, 0, 0)),
            scratch_shapes=[pltpu.VMEM((t_new, lpad), jnp.int32)],
        ),
        out_shape=jax.ShapeDtypeStruct((bsz, t_new, lpad), BF16),
        compiler_params=_params("arbitrary", "arbitrary"),
        name="dsa_select_paged",
    )(page_table, qi_rows, wi_rows, *([kidx_pool] * pp), kidx_new)


def _col_vector(row_vec):
    n = row_vec.shape[1]
    eye = lax.broadcasted_iota(jnp.int32, (n, n), 0) == lax.broadcasted_iota(jnp.int32, (n, n), 1)
    return jnp.sum(jnp.where(eye, row_vec, 0.0), axis=1, keepdims=True)


def _paged_attn_kernel(tbl_ref, qbd_ref, *rest, pp):
    del tbl_ref
    k_refs, v_refs = rest[:pp], rest[pp:2 * pp]
    mt_ref, mnew_ref, knew_ref, vnew_ref, bias_ref, o_ref, m_sc, l_sc, acc_sc = rest[2 * pp:]
    p = pl.program_id(1)
    last = pl.num_programs(1) - 1

    @pl.when(p == 0)
    def _():
        m_sc[...] = jnp.full(m_sc.shape, MASKED, F32)
        l_sc[...] = jnp.zeros_like(l_sc)
        acc_sc[...] = jnp.zeros_like(acc_sc)

    n_kv = qbd_ref.shape[0] // HEAD_DIM

    def heads_on_lanes(ref):
        return jnp.concatenate([ref[pl.ds(g, PAGE_SIZE, stride=n_kv), :] for g in range(n_kv)], axis=1)

    def page(kp, vp, mask, bias):
        s = jnp.dot(kp.astype(BF16), qbd_ref[...], preferred_element_type=F32) * ATTN_SCALE
        if bias is not None:
            s = s + bias
        s = jnp.where(mask > 0, s, MASKED)
        m_old = m_sc[...]
        m_new = jnp.maximum(m_old, jnp.max(s, axis=0, keepdims=True))
        alpha = jnp.exp(m_old - m_new)
        pt = jnp.exp(s - m_new)
        l_sc[...] = alpha * l_sc[...] + jnp.sum(pt, axis=0, keepdims=True)
        pv = lax.dot_general(pt.astype(BF16), vp.astype(BF16), (((0,), (0,)), ((), ())),
                             preferred_element_type=F32)
        acc_sc[...] = acc_sc[...] * _col_vector(alpha) + pv
        m_sc[...] = m_new

    is_last = (p == last).astype(F32)
    for k in range(pp):
        bias = bias_ref[0] * is_last if k == pp - 1 else None
        page(heads_on_lanes(k_refs[k]), heads_on_lanes(v_refs[k]), mt_ref[k * PAGE_SIZE:(k + 1) * PAGE_SIZE, :], bias)

    @pl.when(p == last)
    def _():
        page(knew_ref[...], vnew_ref[...], mnew_ref[...], bias_ref[1])
        o_ref[...] = acc_sc[...] / _col_vector(l_sc[...])


def _paged_attention(qbd, k_pool, v_pool, layer, page_table, mask_t, k_new, v_new, bias_t, *, pp):
    bsz, w, _ = qbd.shape
    n_pages = page_table.shape[1]
    assert n_pages % pp == 0
    n_layers, n_phys, _, n_kv, _ = k_pool.shape
    assert n_kv * HEAD_DIM == w
    as_rows = lambda pool: pool.reshape(n_layers, n_phys, PAGE_SIZE * n_kv, HEAD_DIM)
    k_pool, v_pool = as_rows(k_pool), as_rows(v_pool)
    page_spec = lambda k: pl.BlockSpec((None, None, PAGE_SIZE * n_kv, HEAD_DIM),
                                       lambda b, p, tbl: (layer, tbl[b, p * pp + k], 0, 0))
    per_b = lambda shp: pl.BlockSpec((None,) + shp, lambda b, p, tbl: (b, 0, 0))
    return pl.pallas_call(
        functools.partial(_paged_attn_kernel, pp=pp),
        grid_spec=pltpu.PrefetchScalarGridSpec(
            num_scalar_prefetch=1,
            grid=(bsz, n_pages // pp),
            in_specs=[
                per_b((w, LANES)),
                *[page_spec(k) for k in range(pp)],
                *[page_spec(k) for k in range(pp)],
                pl.BlockSpec((None, pp * PAGE_SIZE, LANES), lambda b, p, tbl: (b, p, 0)),
                pl.BlockSpec((None, PAGE_SIZE, LANES), lambda b, p, tbl: (b, n_pages, 0)),
                per_b((PAGE_SIZE, w)),
                per_b((PAGE_SIZE, w)),
                pl.BlockSpec((2, PAGE_SIZE, LANES), lambda b, p, tbl: (0, 0, 0)),
            ],
            out_specs=per_b((LANES, w)),
            scratch_shapes=[pltpu.VMEM((1, LANES), F32), pltpu.VMEM((1, LANES), F32), pltpu.VMEM((LANES, w), F32)],
        ),
        out_shape=jax.ShapeDtypeStruct((bsz, LANES, w), F32),
        compiler_params=_params("arbitrary", "arbitrary"),
        name="paged_attention",
    )(page_table, qbd, *([k_pool] * pp), *([v_pool] * pp), mask_t, mask_t, k_new, v_new, bias_t)


def _block_diag_queries(q, n_kv):
    bsz, t, _ = q.shape
    q4 = q.reshape(bsz, t, N_HEADS, HEAD_DIM)
    onehot = (jnp.arange(N_HEADS)[:, None] // (N_HEADS // n_kv) == jnp.arange(n_kv)[None, :]).astype(q.dtype)
    qbd = q4[:, :, :, None, :] * onehot[None, None, :, :, None]
    return qbd.transpose(0, 3, 4, 2, 1).reshape(bsz, n_kv * HEAD_DIM, N_HEADS * t).astype(BF16)


def _own_head_lanes(out, n_kv, t):
    bsz = out.shape[0]
    out5 = out.reshape(bsz, N_HEADS, t, n_kv, HEAD_DIM)
    heads = jnp.arange(N_HEADS)
    picked = out5[:, heads, :, heads // (N_HEADS // n_kv), :]
    return picked.transpose(1, 2, 0, 3).reshape(bsz, t, N_HEADS * HEAD_DIM)


def _decode_bias(near_bias, tq, t):
    tail = near_bias[:, :t, tq - PAGE_SIZE:tq + PAGE_SIZE]
    return tail.transpose(2, 0, 1).reshape(2, PAGE_SIZE, N_HEADS * t)


def _pad_rows(a, rows):
    return jnp.pad(a, ((0, 0), (0, rows - a.shape[1]), (0, 0)))


def _new_token_mask(t):
    key = jnp.arange(PAGE_SIZE)[:, None]
    tok = jnp.tile(jnp.arange(t), N_HEADS)[None, :]
    return (key <= tok).astype(BF16)


ATTN_TQ = 256
SELECT_TQ = 128
SELECT_TK = 512
MOBA_HEADS_PER_STEP = 4
Q_W = N_HEADS * HEAD_DIM
KV_W = N_KV_B * HEAD_DIM
DSA_PROJ = Q_W + 2 * KV_W + N_IDX_HEADS * IDX_DIM + IDX_DIM + N_IDX_HEADS
DSA_PROJ_PAD = -(-DSA_PROJ // 512) * 512


def _dsa_split(proj):
    k = proj[..., Q_W:Q_W + KV_W]
    v = proj[..., Q_W + KV_W:Q_W + 2 * KV_W]
    o = Q_W + 2 * KV_W + N_IDX_HEADS * IDX_DIM
    return k, v, proj[..., o:o + IDX_DIM]


def _dsa_prompt(proj, near_bias):
    bsz, t, _ = proj.shape
    k, v, ki = _dsa_split(proj)
    mask = _dsa_select(proj, tq=min(SELECT_TQ, t), tk=min(SELECT_TK, t), topk=min(IDX_TOPK, t // 4))
    o = _attention(proj, k.astype(BF16), v.astype(BF16), mask, near_bias, tq=ATTN_TQ, rep=N_HEADS // N_KV_B,
                   shared_kv=True, block_mask=False)
    return o, k.reshape(bsz, t, N_KV_B, HEAD_DIM), v.reshape(bsz, t, N_KV_B, HEAD_DIM), ki


def _dsa_sample(proj, k_pool, v_pool, kidx_pool, layer, page_table, near_bias):
    bsz, t, _ = proj.shape
    past_len = page_table.shape[1] * PAGE_SIZE
    k, v, ki = _dsa_split(proj)
    o_qi = Q_W + 2 * KV_W
    qi = proj[..., o_qi:o_qi + N_IDX_HEADS * IDX_DIM].reshape(bsz, t, N_IDX_HEADS, IDX_DIM)
    wi = proj[..., o_qi + N_IDX_HEADS * IDX_DIM + IDX_DIM:DSA_PROJ]
    qi_rows = qi.transpose(0, 2, 1, 3).reshape(bsz, N_IDX_HEADS * t, IDX_DIM)
    wi_rows = wi.transpose(0, 2, 1).reshape(bsz, N_IDX_HEADS * t, 1)
    sel = _dsa_select_paged(qi_rows, wi_rows, kidx_pool, layer, page_table, _pad_rows(ki, PAGE_SIZE),
                            pp=8, topk=min(IDX_TOPK, (past_len + t) // 4))
    mask_t = jnp.tile(sel.transpose(0, 2, 1), (1, 1, N_HEADS))
    out = _paged_attention(
        _block_diag_queries(proj[..., :Q_W], N_KV_B), k_pool, v_pool, layer, page_table,
        mask_t, _pad_rows(k, PAGE_SIZE), _pad_rows(v, PAGE_SIZE), _decode_bias(near_bias, ATTN_TQ, t), pp=8)
    o = _own_head_lanes(out, N_KV_B, t).astype(BF16)
    return o, k.reshape(bsz, t, N_KV_B, HEAD_DIM), v.reshape(bsz, t, N_KV_B, HEAD_DIM), ki


def _moba_prompt(proj, near_bias):
    bsz, t, _ = proj.shape
    k = proj[..., Q_W:2 * Q_W]
    v = proj[..., 2 * Q_W:3 * Q_W]
    k_heads = k.reshape(bsz, t, N_HEADS, HEAD_DIM)
    n_pages = t // PAGE_SIZE
    pages = jnp.arange(bsz * n_pages, dtype=jnp.int32).reshape(bsz, n_pages)
    means = _block_means(k_heads.reshape(1, bsz * n_pages, PAGE_SIZE, N_HEADS, HEAD_DIM), 0, pages)
    sel = _moba_select(proj, _pad_rows(means, LANES), tq=ATTN_TQ, past_len=0)
    o = _attention(proj, k.astype(BF16), v.astype(BF16), sel, near_bias, tq=ATTN_TQ, rep=MOBA_HEADS_PER_STEP,
                   shared_kv=False, block_mask=True)
    return o, k_heads, v.reshape(bsz, t, N_HEADS, HEAD_DIM)


def _moba_sample(proj, k_pool, v_pool, layer, page_table, near_bias):
    bsz, t, _ = proj.shape
    n_pages = page_table.shape[1]
    past_len = n_pages * PAGE_SIZE
    k = proj[..., Q_W:2 * Q_W]
    v = proj[..., 2 * Q_W:3 * Q_W]
    means = _block_means(k_pool, layer, page_table)
    sel = _moba_select(proj, _pad_rows(means, LANES), tq=t, past_len=past_len)
    past = jnp.repeat(sel[..., :past_len // MOBA_BLOCK], MOBA_BLOCK, axis=-1)
    past_t = past.transpose(0, 3, 1, 2).reshape(bsz, past_len, N_HEADS * t)
    new_t = jnp.broadcast_to(_new_token_mask(t), (bsz, PAGE_SIZE, N_HEADS * t))
    out = _paged_attention(
        _block_diag_queries(proj[..., :Q_W], N_HEADS), k_pool, v_pool, layer, page_table,
        jnp.concatenate([past_t, new_t], axis=1), _pad_rows(k, PAGE_SIZE), _pad_rows(v, PAGE_SIZE),
        _decode_bias(near_bias, ATTN_TQ, t), pp=4)
    o = _own_head_lanes(out, N_HEADS, t).astype(BF16)
    return o, k.reshape(bsz, t, N_HEADS, HEAD_DIM), v.reshape(bsz, t, N_HEADS, HEAD_DIM)


N_MIXERS = 3
PROMPT_TM = 1024
MATMUL_TN = 512
LRU_TT = 256
MOE_TM = 512
ROUTE_W = LANES


def kernel(x_prompt, x_sample, state_lru_h, state_lru_conv, cache_dsa_k, cache_dsa_v, cache_dsa_kidx,
           cache_moba_k, cache_moba_v, page_table, c_prompt, c_sample, rel_bias, norm_mix_g, norm_ffn_g,
           final_norm_g, ada_w, ada_b, lru_w_in, lru_conv_w, lru_conv_b, lru_w_rgate, lru_b_rgate,
           lru_w_igate, lru_b_igate, lru_lambda, lru_w_out, dsa_w_in, dsa_w_out, moba_w_in, moba_w_out,
           moe_w_group, moe_w_expert, moe_w1, moe_w3, moe_w2):
    bp, tp, d = x_prompt.shape
    bs, ts, _ = x_sample.shape
    depth = ada_w.shape[0]
    ns = bs * ts
    c_rnn = lru_w_out.shape[1]

    c_all = jnp.concatenate([c_prompt, c_sample], axis=0)
    c_rows = -(-c_all.shape[0] // SUBLANES) * SUBLANES
    mods = _ada(jnp.pad(c_all, ((0, c_rows - c_all.shape[0]), (0, 0))), ada_w, ada_b)
    mods = mods.reshape(depth, c_rows, 6, d)
    near_bias = _near_bias(rel_bias, ATTN_TQ)

    xp = x_prompt
    xs = x_sample.reshape(1, ns, d)
    outs = {n: [] for n in ("lru_h_p", "lru_c_p", "lru_h_s", "lru_c_s", "dsa_k_p", "dsa_v_p", "dsa_i_p",
                            "dsa_k_s", "dsa_v_s", "dsa_i_s", "moba_k_p", "moba_v_p", "moba_k_s", "moba_v_s")}
    for i in range(depth):
        kind, j = i % N_MIXERS, i // N_MIXERS
        mod_p = [mods[i, :bp, m][:, None, :] for m in range(6)]
        mod_s = [jnp.repeat(mods[i, bp:bp + bs, m], ts, axis=0)[None] for m in range(6)]
        shp1, scp1, gtp1, shp2, scp2, gtp2 = mod_p
        shs1, scs1, gts1, shs2, scs2, gts2 = mod_s
        mix_in = lambda w, tn=MATMUL_TN: (
            _norm_matmul(xp, norm_mix_g[i], shp1, scp1, w, tm=min(PROMPT_TM, tp), tn=tn),
            _norm_matmul(xs, norm_mix_g[i], shs1, scs1, w, tm=ns, tn=tn))
        if kind == 0:
            up, us = mix_in(lru_w_in[j])
            lru_params = (lru_conv_w[j], lru_conv_b[j], lru_w_rgate[j], lru_b_rgate[j], lru_w_igate[j],
                          lru_b_igate[j], lru_lambda[j])
            yp, hp_state, cp = _lru_core(up, jnp.zeros((bp, CONV_W - 1, c_rnn), F32), jnp.zeros((bp, c_rnn), F32),
                                         *lru_params, tt=LRU_TT)
            ys, hs_state, cs = _lru_core(us.reshape(bs, ts, 2 * c_rnn), state_lru_conv[j], state_lru_h[j],
                                         *lru_params, tt=ts)
            ys = ys.reshape(1, ns, c_rnn)
            w_out = lru_w_out[j]
            outs["lru_h_p"].append(hp_state)
            outs["lru_c_p"].append(cp)
            outs["lru_h_s"].append(hs_state)
            outs["lru_c_s"].append(cs)
        elif kind == 1:
            w_in = jnp.pad(dsa_w_in[j], ((0, 0), (0, DSA_PROJ_PAD - DSA_PROJ)))
            pp_, ps_ = mix_in(w_in)
            yp, kp, vp, ip = _dsa_prompt(pp_, near_bias)
            ys, k_s, v_s, i_s = _dsa_sample(ps_.reshape(bs, ts, DSA_PROJ_PAD), cache_dsa_k, cache_dsa_v,
                                            cache_dsa_kidx, j, page_table, near_bias)
            ys = ys.reshape(1, ns, Q_W)
            w_out = dsa_w_out[j]
            for n, val in zip(("dsa_k_p", "dsa_v_p", "dsa_i_p", "dsa_k_s", "dsa_v_s", "dsa_i_s"),
                              (kp, vp, ip, k_s, v_s, i_s)):
                outs[n].append(val)
        else:
            pp_, ps_ = mix_in(moba_w_in[j])
            yp, kp, vp = _moba_prompt(pp_, near_bias)
            ys, k_s, v_s = _moba_sample(ps_.reshape(bs, ts, 3 * Q_W), cache_moba_k, cache_moba_v, j,
                                        page_table, near_bias)
            ys = ys.reshape(1, ns, Q_W)
            w_out = moba_w_out[j]
            for n, val in zip(("moba_k_p", "moba_v_p", "moba_k_s", "moba_v_s"), (kp, vp, k_s, v_s)):
                outs[n].append(val)
        xp = _matmul_residual(yp, w_out, xp, gtp1, tm=min(PROMPT_TM, tp), tn=min(MATMUL_TN, d))
        xs = _matmul_residual(ys, w_out, xs, gts1, tm=ns, tn=min(MATMUL_TN, d))

        w_route = jnp.pad(jnp.concatenate([moe_w_group[i], moe_w_expert[i]], axis=1),
                          ((0, 0), (0, ROUTE_W - N_GROUPS - N_EXPERTS)))
        w1, w3, w2 = moe_w1[i].astype(BF16), moe_w3[i].astype(BF16), moe_w2[i].astype(BF16)
        lg_p, h_p = _norm_matmul(xp, norm_ffn_g[i], shp2, scp2, w_route, tm=min(PROMPT_TM, tp), tn=ROUTE_W,
                                 emit_h=True, highest=True, h_dtype=F32)
        moe_tm = min(MOE_TM, tp)
        info, counts = _route_tokens(lg_p.reshape(bp * tp, ROUTE_W), tm=moe_tm)
        moe_p = _moe_grouped(h_p.reshape(bp * tp, d), info, *_group_layout(info, counts, moe_tm), w1, w3, w2,
                             tm=moe_tm)
        xp = _gated_add(xp, moe_p, gtp2, tm=moe_tm)
        lg_s, h_s = _norm_matmul(xs, norm_ffn_g[i], shs2, scs2, w_route, tm=ns, tn=ROUTE_W,
                                 emit_h=True, highest=True)
        xs = _moe_dense(h_s, lg_s, w1, w3, w2, xs, gts2, tm=ns)

    y_prompt = _rmsnorm(xp, final_norm_g, tm=min(MOE_TM, tp))
    y_sample = _rmsnorm(xs, final_norm_g, tm=ns).reshape(bs, ts, d)
    stack = lambda n: jnp.stack(outs[n])
    return (y_prompt, y_sample, stack("lru_h_p"), stack("lru_c_p"), stack("lru_h_s"), stack("lru_c_s"),
            stack("dsa_k_p"), stack("dsa_v_p"), stack("dsa_i_p"), stack("dsa_k_s"), stack("dsa_v_s"),
            stack("dsa_i_s"), stack("moba_k_p"), stack("moba_v_p"), stack("moba_k_s"), stack("moba_v_s"))
```

```python
import functools
import math

import numpy as np
import jax
import jax.numpy as jnp
from jax import lax
from jax.experimental import pallas as pl
from jax.experimental.pallas import tpu as pltpu

F32 = jnp.float32
BF16 = jnp.bfloat16

LANES = 128
SUBLANES = 8
VMEM_LIMIT_BYTES = 56 * 1024 * 1024

HEAD_DIM = 128
N_HEADS = 16
N_KV_B = 4
N_IDX_HEADS = 16
IDX_DIM = 64
IDX_TOPK = 256
IDX_SCALE = (IDX_DIM * N_IDX_HEADS) ** -0.5
PAGE_SIZE = 128
MOBA_BLOCK = 256
MOBA_TOPK = 3
N_BUCKETS = 32
MAX_DISTANCE = 128
N_RNN_BLOCKS = 16
CONV_W = 4
LRU_C = 8.0
N_GROUPS = 4
EXPERTS_PER_GROUP = 4
N_EXPERTS = N_GROUPS * EXPERTS_PER_GROUP
EPS = 1e-6
ATTN_SCALE = HEAD_DIM ** -0.5
LOG2E = math.log2(math.e)
QK_SCALE_LOG2 = ATTN_SCALE * LOG2E
NEG_INF = float("-inf")


def _params(*sem):
    return pltpu.CompilerParams(dimension_semantics=sem, vmem_limit_bytes=VMEM_LIMIT_BYTES)


def _bdot(a, b):
    return jnp.dot(a.astype(BF16), b.astype(BF16), preferred_element_type=F32)


def _bdot_nt(a, b):
    return lax.dot_general(a.astype(BF16), b.astype(BF16), (((1,), (1,)), ((), ())),
                           preferred_element_type=F32)


def _ada_kernel(c_ref, w_ref, b_ref, o_ref):
    c = c_ref[...]
    o_ref[...] = _bdot(c * jax.nn.sigmoid(c), w_ref[...]) + b_ref[...]


def _ada(c, ada_w, ada_b):
    depth, d, n = ada_w.shape
    rows = c.shape[0]
    tn = next(w for w in (1024, 512, 256, LANES) if n % w == 0)
    return pl.pallas_call(
        _ada_kernel,
        grid=(depth, n // tn),
        in_specs=[
            pl.BlockSpec((rows, d), lambda l, j: (0, 0)),
            pl.BlockSpec((None, d, tn), lambda l, j: (l, 0, j)),
            pl.BlockSpec((None, 1, tn), lambda l, j: (l, 0, j)),
        ],
        out_specs=pl.BlockSpec((None, rows, tn), lambda l, j: (l, 0, j)),
        out_shape=jax.ShapeDtypeStruct((depth, rows, n), F32),
        compiler_params=_params("arbitrary", "arbitrary"),
        name="ada_mod",
    )(c, ada_w, ada_b.reshape(depth, 1, n))


def _norm_mod(x, g, shift, scale):
    ms = jnp.mean(x * x, axis=-1, keepdims=True)
    h = x * lax.rsqrt(ms + EPS) * g
    return h * (1.0 + scale) + shift


def _norm_mm_kernel(x_ref, g_ref, sh_ref, sc_ref, w_ref, *refs, emit_h, highest):
    if emit_h:
        o_ref, ho_ref, h_ref = refs
    else:
        o_ref, h_ref = refs
        ho_ref = None

    @pl.when(pl.program_id(2) == 0)
    def _():
        h = _norm_mod(x_ref[...], g_ref[...], sh_ref[...], sc_ref[...])
        h_ref[...] = h.astype(h_ref.dtype)
        if ho_ref is not None:
            ho_ref[...] = h.astype(ho_ref.dtype)

    if highest:
        o_ref[...] = jnp.dot(h_ref[...], w_ref[...], precision=lax.Precision.HIGHEST,
                             preferred_element_type=F32)
    else:
        o_ref[...] = _bdot(h_ref[...], w_ref[...])


def _norm_matmul(x, g, shift, scale, w, *, tm, tn, emit_h=False, highest=False, h_dtype=BF16):
    bsz, t, d = x.shape
    n = w.shape[1]
    r = shift.shape[1]
    assert t % tm == 0 and n % tn == 0 and (r == 1 or r == tm == t)
    out_shape = [jax.ShapeDtypeStruct((bsz, t, n), F32)]
    out_specs = [pl.BlockSpec((None, tm, tn), lambda b, i, j: (b, i, j))]
    if emit_h:
        out_shape.append(jax.ShapeDtypeStruct((bsz, t, d), h_dtype))
        out_specs.append(pl.BlockSpec((None, tm, d), lambda b, i, j: (b, i, 0)))
    mod_map = (lambda b, i, j: (b, 0, 0)) if r == 1 else (lambda b, i, j: (b, i, 0))
    res = pl.pallas_call(
        functools.partial(_norm_mm_kernel, emit_h=emit_h, highest=highest),
        grid=(bsz, t // tm, n // tn),
        in_specs=[
            pl.BlockSpec((None, tm, d), lambda b, i, j: (b, i, 0)),
            pl.BlockSpec((1, d), lambda b, i, j: (0, 0)),
            pl.BlockSpec((None, r, d), mod_map),
            pl.BlockSpec((None, r, d), mod_map),
            pl.BlockSpec((d, tn), lambda b, i, j: (0, j)),
        ],
        out_specs=out_specs,
        out_shape=out_shape,
        scratch_shapes=[pltpu.VMEM((tm, d), F32 if highest else BF16)],
        compiler_params=_params("arbitrary", "arbitrary", "arbitrary"),
        name="norm_matmul",
    )(x, g.reshape(1, d), shift, scale, w)
    return res if emit_h else res[0]


def _mm_res_kernel(a_ref, w_ref, r_ref, g_ref, o_ref):
    o_ref[...] = r_ref[...] + g_ref[...] * _bdot(a_ref[...], w_ref[...])


def _matmul_residual(a, w, res, gate, *, tm, tn):
    bsz, t, k = a.shape
    n = w.shape[1]
    r = gate.shape[1]
    assert t % tm == 0 and n % tn == 0 and (r == 1 or r == tm == t)
    gate_map = (lambda b, i, j: (b, 0, j)) if r == 1 else (lambda b, i, j: (b, i, j))
    return pl.pallas_call(
        _mm_res_kernel,
        grid=(bsz, t // tm, n // tn),
        in_specs=[
            pl.BlockSpec((None, tm, k), lambda b, i, j: (b, i, 0)),
            pl.BlockSpec((k, tn), lambda b, i, j: (0, j)),
            pl.BlockSpec((None, tm, tn), lambda b, i, j: (b, i, j)),
            pl.BlockSpec((None, r, tn), gate_map),
        ],
        out_specs=pl.BlockSpec((None, tm, tn), lambda b, i, j: (b, i, j)),
        out_shape=jax.ShapeDtypeStruct((bsz, t, n), F32),
        compiler_params=_params("arbitrary", "arbitrary", "arbitrary"),
        name="matmul_residual",
    )(a, w, res, gate)


def _rmsnorm_kernel(x_ref, g_ref, o_ref):
    x = x_ref[...]
    ms = jnp.mean(x * x, axis=-1, keepdims=True)
    o_ref[...] = x * lax.rsqrt(ms + EPS) * g_ref[...]


def _rmsnorm(x, g, *, tm):
    bsz, t, d = x.shape
    return pl.pallas_call(
        _rmsnorm_kernel,
        grid=(bsz, t // tm),
        in_specs=[pl.BlockSpec((None, tm, d), lambda b, i: (b, i, 0)),
                  pl.BlockSpec((1, d), lambda b, i: (0, 0))],
        out_specs=pl.BlockSpec((None, tm, d), lambda b, i: (b, i, 0)),
        out_shape=jax.ShapeDtypeStruct((bsz, t, d), F32),
        compiler_params=_params("arbitrary", "arbitrary"),
        name="final_rmsnorm",
    )(x, g.reshape(1, d))


GATE_CHUNK = 640


def _log_sigmoid(x):
    return jnp.minimum(x, 0.0) - jnp.log1p(jnp.exp(-jnp.abs(x)))


def _lru_kernel(gb_ref, xb_ref, cs_ref, h0_ref, cw_ref, cb_ref, wr_ref, br_ref, wi_ref, bi_ref, lam_ref,
                y_ref, hl_ref, nc_ref, ext_ref, a_ref, b_ref, hs_ref, carry_ref, *, tt):
    c_all = xb_ref.shape[-1]

    @pl.when(pl.program_id(1) == 0)
    def _():
        ext_ref[0:SUBLANES, :] = cs_ref[...]
        carry_ref[...] = h0_ref[...]

    xb = xb_ref[...]
    ext_ref[SUBLANES:SUBLANES + tt, :] = xb
    cw = cw_ref[...]
    xc = cb_ref[...] + xb * cw[CONV_W - 1:CONV_W, :]
    for j in range(1, CONV_W):
        xc = xc + ext_ref[pl.ds(SUBLANES - j, tt), :] * cw[CONV_W - 1 - j:CONV_W - j, :]
    tail = ext_ref[tt:tt + SUBLANES, :]
    ext_ref[0:SUBLANES, :] = tail
    nc_ref[...] = tail

    xcb = xc.astype(BF16)
    r_parts, i_parts = [], []
    for c in range(c_all // GATE_CHUNK):
        sl = slice(c * GATE_CHUNK, (c + 1) * GATE_CHUNK)
        r_parts.append(jnp.dot(xcb[:, sl], wr_ref[c], preferred_element_type=F32))
        i_parts.append(jnp.dot(xcb[:, sl], wi_ref[c], preferred_element_type=F32))
    r = jax.nn.sigmoid(jnp.concatenate(r_parts, axis=1) + br_ref[...])
    ig = jax.nn.sigmoid(jnp.concatenate(i_parts, axis=1) + bi_ref[...])
    log_a = LRU_C * r * _log_sigmoid(lam_ref[...])
    th = jnp.tanh(log_a)
    one_minus_a2 = -2.0 * th / (1.0 - th)
    a_ref[...] = jnp.exp(log_a)
    b_ref[...] = jnp.sqrt(jnp.maximum(one_minus_a2, 0.0)) * (ig * xc)

    row = lax.broadcasted_iota(jnp.int32, (SUBLANES, GATE_CHUNK), 0)
    for c in range(c_all // GATE_CHUNK):
        sl = slice(c * GATE_CHUNK, (c + 1) * GATE_CHUNK)

        def body(grp, carry, sl=sl):
            r8 = pl.multiple_of(grp * SUBLANES, SUBLANES)
            av = a_ref[pl.ds(r8, SUBLANES), sl]
            bv = b_ref[pl.ds(r8, SUBLANES), sl]
            for s in (1, 2, 4):
                m = row >= s
                a_sh = jnp.where(m, pltpu.roll(av, s, axis=0), 1.0)
                b_sh = jnp.where(m, pltpu.roll(bv, s, axis=0), 0.0)
                bv = av * b_sh + bv
                av = av * a_sh
            h = av * carry + bv
            hs_ref[pl.ds(r8, SUBLANES), sl] = h
            return h[SUBLANES - 1:SUBLANES, :]

        carry_ref[:, sl] = lax.fori_loop(0, tt // SUBLANES, body, carry_ref[:, sl])

    hl_ref[...] = carry_ref[...]
    y_ref[...] = (jax.nn.gelu(gb_ref[...]) * hs_ref[...]).astype(y_ref.dtype)


def _lru_core(u, conv_state, h0, conv_w, conv_b, w_r, b_r, w_i, b_i, lam, *, tt):
    bsz, t, c2 = u.shape
    c = c2 // 2
    blk = c // N_RNN_BLOCKS
    per = GATE_CHUNK // blk
    nchunk = c // GATE_CHUNK

    def block_diag(w):
        wc = w.reshape(nchunk, per, blk, blk).astype(BF16)
        eye = jnp.eye(per, dtype=BF16)
        return jnp.einsum("cpij,pq->cpiqj", wc, eye).reshape(nchunk, GATE_CHUNK, GATE_CHUNK)

    cs_pad = jnp.pad(conv_state, ((0, 0), (SUBLANES - (CONV_W - 1), 0), (0, 0)))
    row = lambda v: v.reshape(1, c)
    full = lambda shp: pl.BlockSpec(shp, lambda b, i: (0,) * len(shp))
    y, hl, nc = pl.pallas_call(
        functools.partial(_lru_kernel, tt=tt),
        grid=(bsz, t // tt),
        in_specs=[
            pl.BlockSpec((None, tt, c), lambda b, i: (b, i, 0)),
            pl.BlockSpec((None, tt, c), lambda b, i: (b, i, 1)),
            pl.BlockSpec((None, SUBLANES, c), lambda b, i: (b, 0, 0)),
            pl.BlockSpec((None, 1, c), lambda b, i: (b, 0, 0)),
            full((CONV_W, c)), full((1, c)),
            full((nchunk, GATE_CHUNK, GATE_CHUNK)), full((1, c)),
            full((nchunk, GATE_CHUNK, GATE_CHUNK)), full((1, c)),
            full((1, c)),
        ],
        out_specs=[
            pl.BlockSpec((None, tt, c), lambda b, i: (b, i, 0)),
            pl.BlockSpec((None, 1, c), lambda b, i: (b, 0, 0)),
            pl.BlockSpec((None, SUBLANES, c), lambda b, i: (b, 0, 0)),
        ],
        out_shape=[
            jax.ShapeDtypeStruct((bsz, t, c), BF16),
            jax.ShapeDtypeStruct((bsz, 1, c), F32),
            jax.ShapeDtypeStruct((bsz, SUBLANES, c), F32),
        ],
        scratch_shapes=[
            pltpu.VMEM((tt + SUBLANES, c), F32),
            pltpu.VMEM((tt, c), F32),
            pltpu.VMEM((tt, c), F32),
            pltpu.VMEM((tt, c), F32),
            pltpu.VMEM((1, c), F32),
        ],
        compiler_params=_params("arbitrary", "arbitrary"),
        name="lru_core",
    )(u, u, cs_pad, h0.reshape(bsz, 1, c), conv_w, row(conv_b), block_diag(w_r), row(b_r),
      block_diag(w_i), row(b_i), row(lam))
    return y, hl.reshape(bsz, c), nc[:, SUBLANES - (CONV_W - 1):, :]


def _route_gates(lg):
    return _route(lg)[0]


def _route(lg):
    col = lax.broadcasted_iota(jnp.int32, lg.shape, 1)
    big = jnp.int32(LANES)
    is_g = col < N_GROUPS
    gl = jnp.where(is_g, lg, NEG_INF)
    gmax = jnp.max(gl, axis=1, keepdims=True)
    g_sel = jnp.min(jnp.where(gl == gmax, col, big), axis=1, keepdims=True)
    g_w = 1.0 / jnp.sum(jnp.exp(gl - gmax), axis=1, keepdims=True)
    eid = col - N_GROUPS
    in_grp = (eid >= 0) & (eid < N_EXPERTS) & ((eid // EXPERTS_PER_GROUP) == g_sel)
    e_in = jnp.where(in_grp, lg, NEG_INF)
    top1 = jnp.max(e_in, axis=1, keepdims=True)
    idx1 = jnp.min(jnp.where((e_in == top1) & in_grp, col, big), axis=1, keepdims=True)
    rest = in_grp & (col != idx1)
    e2 = jnp.where(rest, lg, NEG_INF)
    top2 = jnp.max(e2, axis=1, keepdims=True)
    idx2 = jnp.min(jnp.where((e2 == top2) & rest, col, big), axis=1, keepdims=True)
    z = jnp.exp(top2 - top1)
    w_first = g_w / (1.0 + z)
    w_second = g_w * z / (1.0 + z)
    return jnp.where(col == idx1, w_first, 0.0) + jnp.where(col == idx2, w_second, 0.0), g_sel


def _moe_dense_kernel(h_ref, lg_ref, w1_ref, w3_ref, w2_ref, x_ref, g_ref, o_ref, gates_ref, acc_ref):
    e = pl.program_id(2)

    @pl.when(e == 0)
    def _():
        gates_ref[...] = _route_gates(lg_ref[...])
        acc_ref[...] = jnp.zeros_like(acc_ref)

    gates = gates_ref[...]
    col = lax.broadcasted_iota(jnp.int32, gates.shape, 1)
    ge = jnp.sum(jnp.where(col == e + N_GROUPS, gates, 0.0), axis=1, keepdims=True)
    h = h_ref[...]
    a1 = jnp.dot(h, w1_ref[...], preferred_element_type=F32)
    a3 = jnp.dot(h, w3_ref[...], preferred_element_type=F32)
    hidden = (a1 * jax.nn.sigmoid(a1)) * a3 * ge
    acc_ref[...] += jnp.dot(hidden.astype(BF16), w2_ref[...], preferred_element_type=F32)

    @pl.when(e == pl.num_programs(2) - 1)
    def _():
        o_ref[...] = x_ref[...] + g_ref[...] * acc_ref[...]


def _moe_dense(h, logits, w1, w3, w2, x, gate, *, tm):
    bsz, t, d = x.shape
    ne, _, f = w1.shape
    r = gate.shape[1]
    gate_map = (lambda b, i, e: (b, 0, 0)) if r == 1 else (lambda b, i, e: (b, i, 0))
    tok = lambda b, i, e: (b, i, 0)
    return pl.pallas_call(
        _moe_dense_kernel,
        grid=(bsz, t // tm, ne),
        in_specs=[
            pl.BlockSpec((None, tm, d), tok),
            pl.BlockSpec((None, tm, LANES), tok),
            pl.BlockSpec((None, d, f), lambda b, i, e: (e, 0, 0)),
            pl.BlockSpec((None, d, f), lambda b, i, e: (e, 0, 0)),
            pl.BlockSpec((None, f, d), lambda b, i, e: (e, 0, 0)),
            pl.BlockSpec((None, tm, d), tok),
            pl.BlockSpec((None, r, d), gate_map),
        ],
        out_specs=pl.BlockSpec((None, tm, d), tok),
        out_shape=jax.ShapeDtypeStruct((bsz, t, d), F32),
        scratch_shapes=[pltpu.VMEM((tm, LANES), F32), pltpu.VMEM((tm, d), F32)],
        compiler_params=_params("arbitrary", "arbitrary", "arbitrary"),
        name="moe_dense",
    )(h, logits, w1, w3, w2, x, gate)


INFO_GROUP_LANE = 0
INFO_RANK_LANE = 1


def _route_kernel(lg_ref, info_ref, cnt_ref, carry_ref):
    @pl.when(pl.program_id(0) == 0)
    def _():
        carry_ref[...] = jnp.zeros_like(carry_ref)

    gates, g_sel = _route(lg_ref[...])
    tm = gates.shape[0]
    col = lax.broadcasted_iota(jnp.int32, gates.shape, 1)
    onehot = (col == g_sel).astype(F32)
    before = (lax.broadcasted_iota(jnp.int32, (tm, tm), 0) > lax.broadcasted_iota(jnp.int32, (tm, tm), 1))
    earlier = jnp.dot(before.astype(BF16), onehot.astype(BF16), preferred_element_type=F32) + carry_ref[...]
    rank = jnp.sum(jnp.where(col == g_sel, earlier, 0.0), axis=1, keepdims=True)
    carry_ref[...] += jnp.sum(onehot, axis=0, keepdims=True)
    info_ref[...] = (gates + jnp.where(col == INFO_GROUP_LANE, g_sel.astype(F32), 0.0)
                     + jnp.where(col == INFO_RANK_LANE, rank, 0.0))
    cnt_ref[...] = carry_ref[...]


def _route_tokens(logits, *, tm):
    n = logits.shape[0]
    assert n % tm == 0
    return pl.pallas_call(
        _route_kernel,
        grid=(n // tm,),
        in_specs=[pl.BlockSpec((tm, LANES), lambda i: (i, 0))],
        out_specs=[pl.BlockSpec((tm, LANES), lambda i: (i, 0)), pl.BlockSpec((1, LANES), lambda i: (0, 0))],
        out_shape=[jax.ShapeDtypeStruct((n, LANES), F32), jax.ShapeDtypeStruct((1, LANES), F32)],
        scratch_shapes=[pltpu.VMEM((1, LANES), F32)],
        compiler_params=_params("arbitrary"),
        name="moe_route",
    )(logits)


def _group_layout(info, counts, tm):
    n = info.shape[0]
    g_sel = info[:, INFO_GROUP_LANE].astype(jnp.int32)
    rank = info[:, INFO_RANK_LANE].astype(jnp.int32)
    cnt = counts[0, :N_GROUPS].astype(jnp.int32)
    padded = -(-cnt // tm) * tm
    ends = jnp.cumsum(padded)
    slot = (ends - padded)[g_sel] + rank
    n_slots = n + N_GROUPS * tm
    src = jnp.zeros((n_slots,), jnp.int32).at[slot].set(jnp.arange(n, dtype=jnp.int32))
    is_pad = jnp.ones((n_slots,), jnp.bool_).at[slot].set(False)
    pad_row = n + tm + jnp.cumsum(is_pad.astype(jnp.int32)) - 1
    dst = jnp.concatenate([n + jnp.arange(tm, dtype=jnp.int32), jnp.where(is_pad, pad_row, src)])
    tile_start = jnp.arange(n_slots // tm, dtype=jnp.int32) * tm
    tile_group = jnp.minimum(jnp.sum((tile_start[:, None] >= ends[None, :]).astype(jnp.int32), axis=1), N_GROUPS - 1)
    return src, dst, tile_group, (ends[-1] // tm).reshape(1)


def _moe_group_kernel(src_ref, dst_ref, tg_ref, used_ref, h_hbm, info_hbm, w1_ref, w3_ref, w2_ref, zeros_hbm,
                      out_hbm, xbuf, gbuf, xb, gates, acc, obuf, sem_x, sem_g, sem_o, *, tm):
    del zeros_hbm
    i = pl.program_id(0)
    e = pl.program_id(1)
    used = used_ref[0]
    live = i < used
    chunk = tm // EXPERTS_PER_GROUP
    first = e * chunk
    last_step = e == EXPERTS_PER_GROUP - 1

    def gather_row(tile, r):
        tok = src_ref[tile * tm + r]
        pltpu.make_async_copy(h_hbm.at[pl.ds(tok, 1), :], xbuf.at[pl.ds(r, 1), :], sem_x).start()
        pltpu.make_async_copy(info_hbm.at[pl.ds(tok, 1), :], gbuf.at[pl.ds(r, 1), :], sem_g).start()

    def scatter_chunk(tile):
        slot = (tile + 2) % 2
        for k in range(chunk):
            row = dst_ref[(tile + 1) * tm + first + k]
            pltpu.make_async_copy(obuf.at[slot, pl.ds(first + k, 1), :], out_hbm.at[pl.ds(row, 1), :],
                                  sem_o.at[slot]).start()

    def wait_scatter(slot):
        pltpu.make_async_copy(obuf.at[slot], out_hbm.at[pl.ds(0, tm), :], sem_o.at[slot]).wait()

    @pl.when((i == 0) & (e == 0))
    def _():
        obuf[...] = jnp.zeros_like(obuf)

        def body(r, carry):
            gather_row(0, r)
            return carry
        lax.fori_loop(0, tm, body, 0)

    @pl.when((e == 0) & (i <= used))
    def _():
        pltpu.make_async_copy(h_hbm.at[pl.ds(0, tm), :], xbuf, sem_x).wait()
        pltpu.make_async_copy(info_hbm.at[pl.ds(0, tm), :], gbuf, sem_g).wait()

    @pl.when((e == 0) & live)
    def _():
        xb[...] = xbuf[...].astype(BF16)
        gates[...] = gbuf[...]
        acc[...] = jnp.zeros_like(acc)

    @pl.when(live)
    def _():
        lane = lax.broadcasted_iota(jnp.int32, (tm, LANES), 1)
        gate_lane = N_GROUPS + EXPERTS_PER_GROUP * tg_ref[i] + e
        ge = jnp.sum(jnp.where(lane == gate_lane, gates[...], 0.0), axis=1, keepdims=True)
        x = xb[...]
        a1 = jnp.dot(x, w1_ref[...], preferred_element_type=F32)
        a3 = jnp.dot(x, w3_ref[...], preferred_element_type=F32)
        hidden = (a1 * jax.nn.sigmoid(a1)) * a3 * ge
        acc[...] += jnp.dot(hidden.astype(BF16), w2_ref[...], preferred_element_type=F32)
        for k in range(chunk):
            gather_row(i + 1, first + k)
        scatter_chunk(i - 1)

    @pl.when(i == used)
    def _():
        scatter_chunk(i - 1)

    @pl.when(last_step & live & (i >= 1))
    def _():
        wait_scatter(i % 2)

    @pl.when(last_step & live)
    def _():
        obuf[i % 2] = acc[...]

    @pl.when(last_step & (i == used))
    def _():
        wait_scatter(0)
        wait_scatter(1)


def _moe_grouped(h, info, src, dst, tile_group, n_used, w1, w3, w2, *, tm):
    n, d = h.shape
    f = w1.shape[2]
    n_tiles = src.shape[0] // tm
    n_rows = n + (N_GROUPS + 1) * tm
    assert tm % EXPERTS_PER_GROUP == 0 and dst.shape[0] == (n_tiles + 1) * tm

    def expert(i, e, src_, dst_, tg, used):
        e_eff = jnp.where(i < used[0], e, EXPERTS_PER_GROUP - 1)
        return (tg[i] * EXPERTS_PER_GROUP + e_eff, 0, 0)

    return pl.pallas_call(
        functools.partial(_moe_group_kernel, tm=tm),
        grid_spec=pltpu.PrefetchScalarGridSpec(
            num_scalar_prefetch=4,
            grid=(n_tiles, EXPERTS_PER_GROUP),
            in_specs=[
                pl.BlockSpec(memory_space=pl.ANY),
                pl.BlockSpec(memory_space=pl.ANY),
                pl.BlockSpec((None, d, f), expert),
                pl.BlockSpec((None, d, f), expert),
                pl.BlockSpec((None, f, d), expert),
                pl.BlockSpec(memory_space=pl.ANY),
            ],
            out_specs=pl.BlockSpec(memory_space=pl.ANY),
            scratch_shapes=[
                pltpu.VMEM((tm, d), F32), pltpu.VMEM((tm, LANES), F32), pltpu.VMEM((tm, d), BF16),
                pltpu.VMEM((tm, LANES), F32), pltpu.VMEM((tm, d), F32), pltpu.VMEM((2, tm, d), F32),
                pltpu.SemaphoreType.DMA(()), pltpu.SemaphoreType.DMA(()), pltpu.SemaphoreType.DMA((2,)),
            ],
        ),
        out_shape=jax.ShapeDtypeStruct((n_rows, d), F32),
        input_output_aliases={9: 0},
        compiler_params=_params("arbitrary", "arbitrary"),
        name="moe_grouped",
    )(src, dst, tile_group, n_used, h, info, w1, w3, w2, jnp.zeros((n_rows, d), F32))


def _gated_add_kernel(x_ref, y_ref, g_ref, o_ref):
    o_ref[...] = x_ref[...] + g_ref[...] * y_ref[...]


def _gated_add(x, y, gate, *, tm):
    bsz, t, d = x.shape
    per_b = t // tm
    return pl.pallas_call(
        _gated_add_kernel,
        grid=(bsz, per_b),
        in_specs=[
            pl.BlockSpec((None, tm, d), lambda b, i: (b, i, 0)),
            pl.BlockSpec((tm, d), lambda b, i: (b * per_b + i, 0)),
            pl.BlockSpec((None, 1, d), lambda b, i: (b, 0, 0)),
        ],
        out_specs=pl.BlockSpec((None, tm, d), lambda b, i: (b, i, 0)),
        out_shape=jax.ShapeDtypeStruct((bsz, t, d), F32),
        compiler_params=_params("arbitrary", "arbitrary"),
        name="gated_add",
    )(x, y, gate)


MASKED = -1e30


def _t5_bucket_table(tq):
    qi = np.arange(tq, dtype=np.int32)[:, None]
    col = np.arange(2 * tq, dtype=np.int32)[None, :]
    n = np.maximum(qi + tq - col, 0)
    max_exact = N_BUCKETS // 2
    nf = np.maximum(n, 1).astype(np.float32)
    large = max_exact + (np.log(nf / np.float32(max_exact)) / np.float32(math.log(MAX_DISTANCE / max_exact))
                         * np.float32(N_BUCKETS - max_exact)).astype(np.int32)
    table = np.where(n < max_exact, n, np.minimum(large, N_BUCKETS - 1)).astype(np.int32)
    assert table[0, 0] == N_BUCKETS - 1
    return table


def _bias_kernel(rb_ref, bk_ref, o_ref):
    h = pl.program_id(0)
    bk = bk_ref[...]
    acc = jnp.zeros(bk.shape, F32)
    for k in range(N_BUCKETS):
        acc = jnp.where(bk == k, rb_ref[k, h], acc)
    o_ref[...] = (acc - rb_ref[N_BUCKETS - 1, h]) * LOG2E


def _near_bias(rel_bias, tq):
    return pl.pallas_call(
        _bias_kernel,
        grid=(N_HEADS,),
        in_specs=[pl.BlockSpec(memory_space=pltpu.SMEM),
                  pl.BlockSpec((tq, 2 * tq), lambda h: (0, 0))],
        out_specs=pl.BlockSpec((None, tq, 2 * tq), lambda h: (h, 0, 0)),
        out_shape=jax.ShapeDtypeStruct((N_HEADS, tq, 2 * tq), F32),
        name="near_bias",
    )(rel_bias, jnp.asarray(_t5_bucket_table(tq)))


_NEG_INF_KEY = int(np.int32(np.array(-np.inf, np.float32).view(np.int32)) ^ np.int32(0x7FFFFFFF))


def _sort_key(x):
    k = pltpu.bitcast(x, jnp.int32)
    return k ^ ((k >> 31) & jnp.int32(0x7FFFFFFF))


def _kth_largest_key(count_ge, kth):
    zero = jnp.int32(0)
    ans = jnp.where(count_ge(zero) >= kth, zero, jnp.int32(-2 ** 31))

    def body(it, ans):
        cand = ans | (jnp.int32(1) << (30 - it))
        return jnp.where(count_ge(cand) >= kth, cand, ans)

    return lax.fori_loop(0, 31, body, ans)


def _dsa_select_kernel(qi_ref, kw_ref, kall_ref, m_ref, key_ref, *, tq, tk, topk):
    i = pl.program_id(1)
    n_kt = (i * tq + tq + tk - 1) // tk
    qpos = i * tq + lax.broadcasted_iota(jnp.int32, (tq, tk), 0)
    wi = kw_ref[:, IDX_DIM:IDX_DIM + N_IDX_HEADS]
    qi = qi_ref[...].astype(BF16)
    key_ref[...] = jnp.full(key_ref.shape, _NEG_INF_KEY, jnp.int32)

    def score_tile(kt, carry):
        k0 = pl.multiple_of(kt * tk, tk)
        kid = kall_ref[pl.ds(k0, tk), 0:IDX_DIM].astype(BF16)
        acc = jnp.zeros((tq, tk), F32)
        for h in range(N_IDX_HEADS):
            s = _bdot_nt(qi[:, h * IDX_DIM:(h + 1) * IDX_DIM], kid)
            acc = acc + wi[:, h:h + 1] * jnp.maximum(s, 0.0)
        kpos = k0 + lax.broadcasted_iota(jnp.int32, (tq, tk), 1)
        sc = jnp.where(kpos <= qpos, acc * IDX_SCALE, NEG_INF)
        key_ref[:, pl.ds(k0, tk)] = _sort_key(sc)
        return carry

    lax.fori_loop(0, n_kt, score_tile, 0)

    def count_ge(cand):
        def body(kt, acc):
            k0 = pl.multiple_of(kt * tk, tk)
            ge = (key_ref[:, pl.ds(k0, tk)] >= cand).astype(jnp.int32)
            for j in range(tk // LANES):
                acc = acc + ge[:, j * LANES:(j + 1) * LANES]
            return acc
        acc = lax.fori_loop(0, n_kt, body, jnp.zeros((tq, LANES), jnp.int32))
        return jnp.sum(acc, axis=1, keepdims=True)

    thr = _kth_largest_key(count_ge, topk)
    keys = key_ref[...]
    m_ref[...] = ((keys >= thr) & (keys > _NEG_INF_KEY)).astype(m_ref.dtype)


def _dsa_select(proj, *, tq, tk, topk):
    bsz, t, _ = proj.shape
    qi_w = N_IDX_HEADS * IDX_DIM
    qi_blk = (N_HEADS * HEAD_DIM + 2 * N_KV_B * HEAD_DIM) // qi_w
    kw_blk = (N_HEADS * HEAD_DIM + 2 * N_KV_B * HEAD_DIM + qi_w) // LANES
    return pl.pallas_call(
        functools.partial(_dsa_select_kernel, tq=tq, tk=tk, topk=topk),
        grid=(bsz, t // tq),
        in_specs=[
            pl.BlockSpec((None, tq, qi_w), lambda b, i: (b, i, qi_blk)),
            pl.BlockSpec((None, tq, LANES), lambda b, i: (b, i, kw_blk)),
            pl.BlockSpec((None, t, LANES), lambda b, i: (b, 0, kw_blk)),
        ],
        out_specs=pl.BlockSpec((None, tq, t), lambda b, i: (b, i, 0)),
        out_shape=jax.ShapeDtypeStruct((bsz, t, t), BF16),
        scratch_shapes=[pltpu.VMEM((tq, t), jnp.int32)],
        compiler_params=_params("arbitrary", "arbitrary"),
        name="dsa_select",
    )(proj, proj, proj)


def _attn_kernel(q_ref, k_ref, v_ref, m_ref, tb_ref, o_ref, *state, tq, rep, shared_kv, block_mask):
    m_sc, l_sc, acc_sc = state[:rep], state[rep:2 * rep], state[2 * rep:]
    i = pl.program_id(2)
    for r in range(rep):
        m_sc[r][...] = jnp.full((tq, LANES), MASKED, F32)
        l_sc[r][...] = jnp.zeros((tq, LANES), F32)
        acc_sc[r][...] = jnp.zeros((tq, HEAD_DIM), F32)
    lane_tiles = tq // LANES

    def row_reduce(x, op, reduce):
        part = x[:, :LANES]
        for c in range(1, lane_tiles):
            part = op(part, x[:, c * LANES:(c + 1) * LANES])
        return reduce(part, axis=1, keepdims=True)

    def key_mask(j, r, causal):
        if block_mask:
            lane = lax.broadcasted_iota(jnp.int32, (tq, LANES), 1)
            vis = jnp.sum(jnp.where(lane == j, m_ref[r].astype(F32), 0.0), axis=1, keepdims=True)
            ok = jnp.broadcast_to(vis > 0.0, (tq, tq))
        else:
            ok = m_ref[:, pl.ds(pl.multiple_of(j * tq, tq), tq)] > 0
        if causal:
            ok = ok & (lax.broadcasted_iota(jnp.int32, (tq, tq), 0) >= lax.broadcasted_iota(jnp.int32, (tq, tq), 1))
        return ok

    def step(j, bias_lo, causal):
        k0 = pl.multiple_of(j * tq, tq)
        kv = lambda r: slice(0, HEAD_DIM) if shared_kv else slice(r * HEAD_DIM, (r + 1) * HEAD_DIM)
        ok_shared = None if block_mask else key_mask(j, 0, causal)
        scores = []
        for r in range(rep):
            q = (q_ref[:, r * HEAD_DIM:(r + 1) * HEAD_DIM] * QK_SCALE_LOG2).astype(BF16)
            s = _bdot_nt(q, k_ref[pl.ds(k0, tq), kv(r)])
            if bias_lo is not None:
                s = s + tb_ref[r, :, bias_lo:bias_lo + tq]
            scores.append(jnp.where(key_mask(j, r, causal) if block_mask else ok_shared, s, MASKED))
        probs, alphas = [], []
        for r in range(rep):
            m_old = m_sc[r][...]
            m_new = jnp.maximum(m_old, row_reduce(scores[r], jnp.maximum, jnp.max))
            alpha = jnp.exp2(m_old - m_new)
            p = jnp.exp2(scores[r] - jnp.concatenate([m_new] * lane_tiles, axis=1))
            l_sc[r][...] = alpha * l_sc[r][...] + row_reduce(p, jnp.add, jnp.sum)
            m_sc[r][...] = m_new
            probs.append(p.astype(BF16))
            alphas.append(alpha)
        for r in range(rep):
            pv = jnp.dot(probs[r], v_ref[pl.ds(k0, tq), kv(r)], preferred_element_type=F32)
            acc_sc[r][...] = alphas[r] * acc_sc[r][...] + pv

    def far(j, carry):
        step(j, None, False)
        return carry

    lax.fori_loop(0, jnp.maximum(i - 1, 0), far, 0)

    @pl.when(i >= 1)
    def _():
        step(i - 1, 0, False)

    step(i, tq, True)
    o_ref[...] = jnp.concatenate([acc_sc[r][...] / l_sc[r][...] for r in range(rep)], axis=1).astype(o_ref.dtype)


def _attention(q_src, k, v, mask, near_bias, *, tq, rep, shared_kv, block_mask):
    bsz, t, _ = q_src.shape
    g = N_HEADS // rep
    kv_w = HEAD_DIM if shared_kv else rep * HEAD_DIM
    assert HEAD_DIM == LANES and k.shape[-1] == g * kv_w
    if block_mask:
        assert tq == MOBA_BLOCK
        mask_spec = pl.BlockSpec((None, rep, tq, LANES), lambda b, h, i: (b, h, i, 0))
    else:
        mask_spec = pl.BlockSpec((None, tq, t), lambda b, h, i: (b, i, 0))
    stat = pltpu.VMEM((tq, LANES), F32)
    return pl.pallas_call(
        functools.partial(_attn_kernel, tq=tq, rep=rep, shared_kv=shared_kv, block_mask=block_mask),
        grid=(bsz, g, t // tq),
        in_specs=[
            pl.BlockSpec((None, tq, rep * HEAD_DIM), lambda b, h, i: (b, i, h)),
            pl.BlockSpec((None, t, kv_w), lambda b, h, i: (b, 0, h)),
            pl.BlockSpec((None, t, kv_w), lambda b, h, i: (b, 0, h)),
            mask_spec,
            pl.BlockSpec((rep, tq, 2 * tq), lambda b, h, i: (h, 0, 0)),
        ],
        out_specs=pl.BlockSpec((None, tq, rep * HEAD_DIM), lambda b, h, i: (b, i, h)),
        out_shape=jax.ShapeDtypeStruct((bsz, t, N_HEADS * HEAD_DIM), BF16),
        scratch_shapes=[stat] * (3 * rep),
        compiler_params=_params("arbitrary", "arbitrary", "arbitrary"),
        name="masked_attention",
    )(q_src, k, v, mask, near_bias)


PAGES_PER_BLOCK = MOBA_BLOCK // PAGE_SIZE
MEANS_BLOCKS_PER_STEP = 2


def _block_means_kernel(tbl_ref, *refs):
    del tbl_ref
    page_refs, o_ref = refs[:-1], refs[-1]
    for blk in range(MEANS_BLOCKS_PER_STEP):
        pages = page_refs[blk * PAGES_PER_BLOCK:(blk + 1) * PAGES_PER_BLOCK]
        total = jnp.sum(pages[0][...], axis=0)
        for ref in pages[1:]:
            total = total + jnp.sum(ref[...], axis=0)
        o_ref[blk] = total / MOBA_BLOCK


def _block_means(pool, layer, table):
    _, _, _, h, dh = pool.shape
    bsz, npg = table.shape
    nb = npg // PAGES_PER_BLOCK
    per_step = MEANS_BLOCKS_PER_STEP * PAGES_PER_BLOCK
    assert npg % per_step == 0
    page = lambda k: pl.BlockSpec((None, None, PAGE_SIZE, h, dh),
                                  lambda b, n, tbl: (layer, tbl[b, per_step * n + k], 0, 0, 0))
    out = pl.pallas_call(
        _block_means_kernel,
        grid_spec=pltpu.PrefetchScalarGridSpec(
            num_scalar_prefetch=1,
            grid=(bsz, nb // MEANS_BLOCKS_PER_STEP),
            in_specs=[page(k) for k in range(per_step)],
            out_specs=pl.BlockSpec((None, MEANS_BLOCKS_PER_STEP, h, dh), lambda b, n, tbl: (b, n, 0, 0)),
        ),
        out_shape=jax.ShapeDtypeStruct((bsz, nb, h, dh), F32),
        compiler_params=_params("arbitrary", "arbitrary"),
        name="block_means",
    )(table, *([pool] * per_step))
    return out.reshape(bsz, nb, h * dh)


def _moba_select_kernel(q_ref, mean_ref, o_ref, *, tq, past_len):
    i = pl.program_id(1)
    own = (past_len + i * tq + lax.broadcasted_iota(jnp.int32, (tq, LANES), 0)) // MOBA_BLOCK
    blk = lax.broadcasted_iota(jnp.int32, (tq, LANES), 1)
    past = blk < own
    for h in range(N_HEADS):
        sl = slice(h * HEAD_DIM, (h + 1) * HEAD_DIM)
        gate = lax.dot_general(q_ref[:, sl], mean_ref[:, sl], (((1,), (1,)), ((), ())),
                               precision=lax.Precision.HIGHEST, preferred_element_type=F32)
        g = jnp.where(past, gate, NEG_INF)
        kth = jnp.max(g, axis=1, keepdims=True)
        for _ in range(MOBA_TOPK - 1):
            kth = jnp.max(jnp.where(g < kth, g, NEG_INF), axis=1, keepdims=True)
        o_ref[h] = ((past & (g >= kth)) | (blk == own)).astype(o_ref.dtype)


def _moba_select(q_src, means, *, tq, past_len):
    bsz, t, _ = q_src.shape
    w = N_HEADS * HEAD_DIM
    return pl.pallas_call(
        functools.partial(_moba_select_kernel, tq=tq, past_len=past_len),
        grid=(bsz, t // tq),
        in_specs=[
            pl.BlockSpec((None, tq, w), lambda b, i: (b, i, 0)),
            pl.BlockSpec((None, LANES, w), lambda b, i: (b, 0, 0)),
        ],
        out_specs=pl.BlockSpec((None, N_HEADS, tq, LANES), lambda b, i: (b, 0, i, 0)),
        out_shape=jax.ShapeDtypeStruct((bsz, N_HEADS, t, LANES), BF16),
        compiler_params=_params("arbitrary", "arbitrary"),
        name="moba_select",
    )(q_src, means)


def _dsa_select_paged_kernel(tbl_ref, qi_ref, wi_ref, *rest, pp, topk, past_len, t_new):
    del tbl_ref
    page_refs = rest[:pp]
    knew_ref, m_ref, key_ref = rest[pp:]
    p = pl.program_id(1)
    qi = qi_ref[...].astype(BF16)
    wi = wi_ref[...]

    def score(kpage):
        s = _bdot_nt(qi, kpage)
        w = wi * jnp.maximum(s, 0.0)
        return jnp.sum(w.reshape(N_IDX_HEADS, t_new, PAGE_SIZE), axis=0) * IDX_SCALE

    for k in range(pp):
        off = pl.multiple_of((p * pp + k) * PAGE_SIZE, PAGE_SIZE)
        key_ref[:, pl.ds(off, PAGE_SIZE)] = _sort_key(score(page_refs[k][...]))

    @pl.when(p == pl.num_programs(1) - 1)
    def _():
        row = lax.broadcasted_iota(jnp.int32, (t_new, PAGE_SIZE), 0)
        col = lax.broadcasted_iota(jnp.int32, (t_new, PAGE_SIZE), 1)
        sc = jnp.where(col <= row, score(knew_ref[...]), NEG_INF)
        key_ref[:, past_len:past_len + PAGE_SIZE] = _sort_key(sc)

        def count_ge(cand):
            return jnp.sum((key_ref[...] >= cand).astype(jnp.int32), axis=1, keepdims=True)

        thr = _kth_largest_key(count_ge, topk)
        keys = key_ref[...]
        m_ref[...] = ((keys >= thr) & (keys > _NEG_INF_KEY)).astype(m_ref.dtype)


def _dsa_select_paged(qi_rows, wi_rows, kidx_pool, layer, page_table, kidx_new, *, pp, topk):
    bsz, rows, _ = qi_rows.shape
    t_new = rows // N_IDX_HEADS
    n_pages = page_table.shape[1]
    assert n_pages % pp == 0
    past_len = n_pages * PAGE_SIZE
    lpad = past_len + PAGE_SIZE
    page_spec = lambda k: pl.BlockSpec((None, None, PAGE_SIZE, IDX_DIM),
                                       lambda b, p, tbl: (layer, tbl[b, p * pp + k], 0, 0))
    return pl.pallas_call(
        functools.partial(_dsa_select_paged_kernel, pp=pp, topk=topk, past_len=past_len, t_new=t_new),
        grid_spec=pltpu.PrefetchScalarGridSpec(
            num_scalar_prefetch=1,
            grid=(bsz, n_pages // pp),
            in_specs=[
                pl.BlockSpec((None, rows, IDX_DIM), lambda b, p, tbl: (b, 0, 0)),
                pl.BlockSpec((None, rows, 1), lambda b, p, tbl: (b, 0, 0)),
                *[page_spec(k) for k in range(pp)],
                pl.BlockSpec((None, PAGE_SIZE, IDX_DIM), lambda b, p, tbl: (b, 0, 0)),
            ],
            out_specs=pl.BlockSpec((None, t_new, lpad), lambda b, p, tbl: (b, 0, 0)),
            scratch_shapes=[pltpu.VMEM((t_new, lpad), jnp.int32)],
        ),
        out_shape=jax.ShapeDtypeStruct((bsz, t_new, lpad), BF16),
        compiler_params=_params("arbitrary", "arbitrary"),
        name="dsa_select_paged",
    )(page_table, qi_rows, wi_rows, *([kidx_pool] * pp), kidx_new)


def _col_vector(row_vec):
    n = row_vec.shape[1]
    eye = lax.broadcasted_iota(jnp.int32, (n, n), 0) == lax.broadcasted_iota(jnp.int32, (n, n), 1)
    return jnp.sum(jnp.where(eye, row_vec, 0.0), axis=1, keepdims=True)


def _paged_attn_kernel(tbl_ref, qbd_ref, *rest, pp):
    del tbl_ref
    k_refs, v_refs = rest[:pp], rest[pp:2 * pp]
    mt_ref, mnew_ref, knew_ref, vnew_ref, bias_ref, o_ref, m_sc, l_sc, acc_sc = rest[2 * pp:]
    p = pl.program_id(1)
    last = pl.num_programs(1) - 1

    @pl.when(p == 0)
    def _():
        m_sc[...] = jnp.full(m_sc.shape, MASKED, F32)
        l_sc[...] = jnp.zeros_like(l_sc)
        acc_sc[...] = jnp.zeros_like(acc_sc)

    n_kv = qbd_ref.shape[0] // HEAD_DIM

    def heads_on_lanes(ref):
        return jnp.concatenate([ref[pl.ds(g, PAGE_SIZE, stride=n_kv), :] for g in range(n_kv)], axis=1)

    def page(kp, vp, mask, bias):
        s = jnp.dot(kp.astype(BF16), qbd_ref[...], preferred_element_type=F32)
        if bias is not None:
            s = s + bias
        s = jnp.where(mask > 0, s, MASKED)
        m_old = m_sc[...]
        m_new = jnp.maximum(m_old, jnp.max(s, axis=0, keepdims=True))
        alpha = jnp.exp2(m_old - m_new)
        pt = jnp.exp2(s - m_new)
        l_sc[...] = alpha * l_sc[...] + jnp.sum(pt, axis=0, keepdims=True)
        pv = lax.dot_general(pt.astype(BF16), vp.astype(BF16), (((0,), (0,)), ((), ())),
                             preferred_element_type=F32)
        acc_sc[...] = acc_sc[...] * _col_vector(alpha) + pv
        m_sc[...] = m_new

    is_last = (p == last).astype(F32)
    for k in range(pp):
        bias = bias_ref[0] * is_last if k == pp - 1 else None
        page(heads_on_lanes(k_refs[k]), heads_on_lanes(v_refs[k]), mt_ref[k * PAGE_SIZE:(k + 1) * PAGE_SIZE, :], bias)

    @pl.when(p == last)
    def _():
        page(knew_ref[...], vnew_ref[...], mnew_ref[...], bias_ref[1])
        o_ref[...] = acc_sc[...] / _col_vector(l_sc[...])


def _paged_attention(qbd, k_pool, v_pool, layer, page_table, mask_t, k_new, v_new, bias_t, *, pp):
    bsz, w, _ = qbd.shape
    n_pages = page_table.shape[1]
    assert n_pages % pp == 0
    n_layers, n_phys, _, n_kv, _ = k_pool.shape
    assert n_kv * HEAD_DIM == w
    as_rows = lambda pool: pool.reshape(n_layers, n_phys, PAGE_SIZE * n_kv, HEAD_DIM)
    k_pool, v_pool = as_rows(k_pool), as_rows(v_pool)
    page_spec = lambda k: pl.BlockSpec((None, None, PAGE_SIZE * n_kv, HEAD_DIM),
                                       lambda b, p, tbl: (layer, tbl[b, p * pp + k], 0, 0))
    per_b = lambda shp: pl.BlockSpec((None,) + shp, lambda b, p, tbl: (b, 0, 0))
    return pl.pallas_call(
        functools.partial(_paged_attn_kernel, pp=pp),
        grid_spec=pltpu.PrefetchScalarGridSpec(
            num_scalar_prefetch=1,
            grid=(bsz, n_pages // pp),
            in_specs=[
                per_b((w, LANES)),
                *[page_spec(k) for k in range(pp)],
                *[page_spec(k) for k in range(pp)],
                pl.BlockSpec((None, pp * PAGE_SIZE, LANES), lambda b, p, tbl: (b, p, 0)),
                pl.BlockSpec((None, PAGE_SIZE, LANES), lambda b, p, tbl: (b, n_pages, 0)),
                per_b((PAGE_SIZE, w)),
                per_b((PAGE_SIZE, w)),
                pl.BlockSpec((2, PAGE_SIZE, LANES), lambda b, p, tbl: (0, 0, 0)),
            ],
            out_specs=per_b((LANES, w)),
            scratch_shapes=[pltpu.VMEM((1, LANES), F32), pltpu.VMEM((1, LANES), F32), pltpu.VMEM((LANES, w), F32)],
        ),
        out_shape=jax.ShapeDtypeStruct((bsz, LANES, w), F32),
        compiler_params=_params("arbitrary", "arbitrary"),
        name="paged_attention",
    )(page_table, qbd, *([k_pool] * pp), *([v_pool] * pp), mask_t, mask_t, k_new, v_new, bias_t)


def _block_diag_queries(q, n_kv):
    bsz, t, _ = q.shape
    q4 = q.reshape(bsz, t, N_HEADS, HEAD_DIM) * QK_SCALE_LOG2
    onehot = (jnp.arange(N_HEADS)[:, None] // (N_HEADS // n_kv) == jnp.arange(n_kv)[None, :]).astype(q.dtype)
    qbd = q4[:, :, :, None, :] * onehot[None, None, :, :, None]
    return qbd.transpose(0, 3, 4, 2, 1).reshape(bsz, n_kv * HEAD_DIM, N_HEADS * t).astype(BF16)


def _own_head_lanes(out, n_kv, t):
    bsz = out.shape[0]
    out5 = out.reshape(bsz, N_HEADS, t, n_kv, HEAD_DIM)
    heads = jnp.arange(N_HEADS)
    picked = out5[:, heads, :, heads // (N_HEADS // n_kv), :]
    return picked.transpose(1, 2, 0, 3).reshape(bsz, t, N_HEADS * HEAD_DIM)


def _decode_bias(near_bias, tq, t):
    tail = near_bias[:, :t, tq - PAGE_SIZE:tq + PAGE_SIZE]
    return tail.transpose(2, 0, 1).reshape(2, PAGE_SIZE, N_HEADS * t)


def _pad_rows(a, rows):
    return jnp.pad(a, ((0, 0), (0, rows - a.shape[1]), (0, 0)))


def _new_token_mask(t):
    key = jnp.arange(PAGE_SIZE)[:, None]
    tok = jnp.tile(jnp.arange(t), N_HEADS)[None, :]
    return (key <= tok).astype(BF16)


ATTN_TQ = 256
SELECT_TQ = 128
SELECT_TK = 512
MOBA_HEADS_PER_STEP = 4
Q_W = N_HEADS * HEAD_DIM
KV_W = N_KV_B * HEAD_DIM
DSA_PROJ = Q_W + 2 * KV_W + N_IDX_HEADS * IDX_DIM + IDX_DIM + N_IDX_HEADS
DSA_PROJ_PAD = -(-DSA_PROJ // 512) * 512


def _dsa_split(proj):
    k = proj[..., Q_W:Q_W + KV_W]
    v = proj[..., Q_W + KV_W:Q_W + 2 * KV_W]
    o = Q_W + 2 * KV_W + N_IDX_HEADS * IDX_DIM
    return k, v, proj[..., o:o + IDX_DIM]


def _dsa_prompt(proj, near_bias):
    bsz, t, _ = proj.shape
    k, v, ki = _dsa_split(proj)
    mask = _dsa_select(proj, tq=min(SELECT_TQ, t), tk=min(SELECT_TK, t), topk=min(IDX_TOPK, t // 4))
    o = _attention(proj, k.astype(BF16), v.astype(BF16), mask, near_bias, tq=ATTN_TQ, rep=N_HEADS // N_KV_B,
                   shared_kv=True, block_mask=False)
    return o, k.reshape(bsz, t, N_KV_B, HEAD_DIM), v.reshape(bsz, t, N_KV_B, HEAD_DIM), ki


def _dsa_sample(proj, k_pool, v_pool, kidx_pool, layer, page_table, near_bias):
    bsz, t, _ = proj.shape
    past_len = page_table.shape[1] * PAGE_SIZE
    k, v, ki = _dsa_split(proj)
    o_qi = Q_W + 2 * KV_W
    qi = proj[..., o_qi:o_qi + N_IDX_HEADS * IDX_DIM].reshape(bsz, t, N_IDX_HEADS, IDX_DIM)
    wi = proj[..., o_qi + N_IDX_HEADS * IDX_DIM + IDX_DIM:DSA_PROJ]
    qi_rows = qi.transpose(0, 2, 1, 3).reshape(bsz, N_IDX_HEADS * t, IDX_DIM)
    wi_rows = wi.transpose(0, 2, 1).reshape(bsz, N_IDX_HEADS * t, 1)
    sel = _dsa_select_paged(qi_rows, wi_rows, kidx_pool, layer, page_table, _pad_rows(ki, PAGE_SIZE),
                            pp=8, topk=min(IDX_TOPK, (past_len + t) // 4))
    mask_t = jnp.tile(sel.transpose(0, 2, 1), (1, 1, N_HEADS))
    out = _paged_attention(
        _block_diag_queries(proj[..., :Q_W], N_KV_B), k_pool, v_pool, layer, page_table,
        mask_t, _pad_rows(k, PAGE_SIZE), _pad_rows(v, PAGE_SIZE), _decode_bias(near_bias, ATTN_TQ, t), pp=8)
    o = _own_head_lanes(out, N_KV_B, t).astype(BF16)
    return o, k.reshape(bsz, t, N_KV_B, HEAD_DIM), v.reshape(bsz, t, N_KV_B, HEAD_DIM), ki


def _moba_prompt(proj, near_bias):
    bsz, t, _ = proj.shape
    k = proj[..., Q_W:2 * Q_W]
    v = proj[..., 2 * Q_W:3 * Q_W]
    k_heads = k.reshape(bsz, t, N_HEADS, HEAD_DIM)
    n_pages = t // PAGE_SIZE
    pages = jnp.arange(bsz * n_pages, dtype=jnp.int32).reshape(bsz, n_pages)
    means = _block_means(k_heads.reshape(1, bsz * n_pages, PAGE_SIZE, N_HEADS, HEAD_DIM), 0, pages)
    sel = _moba_select(proj, _pad_rows(means, LANES), tq=ATTN_TQ, past_len=0)
    o = _attention(proj, k.astype(BF16), v.astype(BF16), sel, near_bias, tq=ATTN_TQ, rep=MOBA_HEADS_PER_STEP,
                   shared_kv=False, block_mask=True)
    return o, k_heads, v.reshape(bsz, t, N_HEADS, HEAD_DIM)


def _moba_sample(proj, k_pool, v_pool, layer, page_table, near_bias):
    bsz, t, _ = proj.shape
    n_pages = page_table.shape[1]
    past_len = n_pages * PAGE_SIZE
    k = proj[..., Q_W:2 * Q_W]
    v = proj[..., 2 * Q_W:3 * Q_W]
    means = _block_means(k_pool, layer, page_table)
    sel = _moba_select(proj, _pad_rows(means, LANES), tq=t, past_len=past_len)
    past = jnp.repeat(sel[..., :past_len // MOBA_BLOCK], MOBA_BLOCK, axis=-1)
    past_t = past.transpose(0, 3, 1, 2).reshape(bsz, past_len, N_HEADS * t)
    new_t = jnp.broadcast_to(_new_token_mask(t), (bsz, PAGE_SIZE, N_HEADS * t))
    out = _paged_attention(
        _block_diag_queries(proj[..., :Q_W], N_HEADS), k_pool, v_pool, layer, page_table,
        jnp.concatenate([past_t, new_t], axis=1), _pad_rows(k, PAGE_SIZE), _pad_rows(v, PAGE_SIZE),
        _decode_bias(near_bias, ATTN_TQ, t), pp=4)
    o = _own_head_lanes(out, N_HEADS, t).astype(BF16)
    return o, k.reshape(bsz, t, N_HEADS, HEAD_DIM), v.reshape(bsz, t, N_HEADS, HEAD_DIM)


N_MIXERS = 3
PROMPT_TM = 1024
MATMUL_TN = 512
LRU_TT = 256
MOE_TM = 512
ROUTE_W = LANES


def kernel(x_prompt, x_sample, state_lru_h, state_lru_conv, cache_dsa_k, cache_dsa_v, cache_dsa_kidx,
           cache_moba_k, cache_moba_v, page_table, c_prompt, c_sample, rel_bias, norm_mix_g, norm_ffn_g,
           final_norm_g, ada_w, ada_b, lru_w_in, lru_conv_w, lru_conv_b, lru_w_rgate, lru_b_rgate,
           lru_w_igate, lru_b_igate, lru_lambda, lru_w_out, dsa_w_in, dsa_w_out, moba_w_in, moba_w_out,
           moe_w_group, moe_w_expert, moe_w1, moe_w3, moe_w2):
    bp, tp, d = x_prompt.shape
    bs, ts, _ = x_sample.shape
    depth = ada_w.shape[0]
    ns = bs * ts
    c_rnn = lru_w_out.shape[1]

    c_all = jnp.concatenate([c_prompt, c_sample], axis=0)
    c_rows = -(-c_all.shape[0] // SUBLANES) * SUBLANES
    mods = _ada(jnp.pad(c_all, ((0, c_rows - c_all.shape[0]), (0, 0))), ada_w, ada_b)
    mods = mods.reshape(depth, c_rows, 6, d)
    near_bias = _near_bias(rel_bias, ATTN_TQ)

    xp = x_prompt
    xs = x_sample.reshape(1, ns, d)
    outs = {n: [] for n in ("lru_h_p", "lru_c_p", "lru_h_s", "lru_c_s", "dsa_k_p", "dsa_v_p", "dsa_i_p",
                            "dsa_k_s", "dsa_v_s", "dsa_i_s", "moba_k_p", "moba_v_p", "moba_k_s", "moba_v_s")}
    for i in range(depth):
        kind, j = i % N_MIXERS, i // N_MIXERS
        mod_p = [mods[i, :bp, m][:, None, :] for m in range(6)]
        mod_s = [jnp.repeat(mods[i, bp:bp + bs, m], ts, axis=0)[None] for m in range(6)]
        shp1, scp1, gtp1, shp2, scp2, gtp2 = mod_p
        shs1, scs1, gts1, shs2, scs2, gts2 = mod_s
        mix_in = lambda w, tn=MATMUL_TN: (
            _norm_matmul(xp, norm_mix_g[i], shp1, scp1, w, tm=min(PROMPT_TM, tp), tn=tn),
            _norm_matmul(xs, norm_mix_g[i], shs1, scs1, w, tm=ns, tn=tn))
        if kind == 0:
            up, us = mix_in(lru_w_in[j])
            lru_params = (lru_conv_w[j], lru_conv_b[j], lru_w_rgate[j], lru_b_rgate[j], lru_w_igate[j],
                          lru_b_igate[j], lru_lambda[j])
            yp, hp_state, cp = _lru_core(up, jnp.zeros((bp, CONV_W - 1, c_rnn), F32), jnp.zeros((bp, c_rnn), F32),
                                         *lru_params, tt=LRU_TT)
            ys, hs_state, cs = _lru_core(us.reshape(bs, ts, 2 * c_rnn), state_lru_conv[j], state_lru_h[j],
                                         *lru_params, tt=ts)
            ys = ys.reshape(1, ns, c_rnn)
            w_out = lru_w_out[j]
            outs["lru_h_p"].append(hp_state)
            outs["lru_c_p"].append(cp)
            outs["lru_h_s"].append(hs_state)
            outs["lru_c_s"].append(cs)
        elif kind == 1:
            w_in = jnp.pad(dsa_w_in[j], ((0, 0), (0, DSA_PROJ_PAD - DSA_PROJ)))
            pp_, ps_ = mix_in(w_in)
            yp, kp, vp, ip = _dsa_prompt(pp_, near_bias)
            ys, k_s, v_s, i_s = _dsa_sample(ps_.reshape(bs, ts, DSA_PROJ_PAD), cache_dsa_k, cache_dsa_v,
                                            cache_dsa_kidx, j, page_table, near_bias)
            ys = ys.reshape(1, ns, Q_W)
            w_out = dsa_w_out[j]
            for n, val in zip(("dsa_k_p", "dsa_v_p", "dsa_i_p", "dsa_k_s", "dsa_v_s", "dsa_i_s"),
                              (kp, vp, ip, k_s, v_s, i_s)):
                outs[n].append(val)
        else:
            pp_, ps_ = mix_in(moba_w_in[j])
            yp, kp, vp = _moba_prompt(pp_, near_bias)
            ys, k_s, v_s = _moba_sample(ps_.reshape(bs, ts, 3 * Q_W), cache_moba_k, cache_moba_v, j,
                                        page_table, near_bias)
            ys = ys.reshape(1, ns, Q_W)
            w_out = moba_w_out[j]
            for n, val in zip(("moba_k_p", "moba_v_p", "moba_k_s", "moba_v_s"), (kp, vp, k_s, v_s)):
                outs[n].append(val)
        xp = _matmul_residual(yp, w_out, xp, gtp1, tm=min(PROMPT_TM, tp), tn=min(MATMUL_TN, d))
        xs = _matmul_residual(ys, w_out, xs, gts1, tm=ns, tn=min(MATMUL_TN, d))

        w_route = jnp.pad(jnp.concatenate([moe_w_group[i], moe_w_expert[i]], axis=1),
                          ((0, 0), (0, ROUTE_W - N_GROUPS - N_EXPERTS)))
        w1, w3, w2 = moe_w1[i].astype(BF16), moe_w3[i].astype(BF16), moe_w2[i].astype(BF16)
        lg_p, h_p = _norm_matmul(xp, norm_ffn_g[i], shp2, scp2, w_route, tm=min(PROMPT_TM, tp), tn=ROUTE_W,
                                 emit_h=True, highest=True, h_dtype=F32)
        moe_tm = min(MOE_TM, tp)
        info, counts = _route_tokens(lg_p.reshape(bp * tp, ROUTE_W), tm=moe_tm)
        moe_p = _moe_grouped(h_p.reshape(bp * tp, d), info, *_group_layout(info, counts, moe_tm), w1, w3, w2,
                             tm=moe_tm)
        xp = _gated_add(xp, moe_p, gtp2, tm=moe_tm)
        lg_s, h_s = _norm_matmul(xs, norm_ffn_g[i], shs2, scs2, w_route, tm=ns, tn=ROUTE_W,
                                 emit_h=True, highest=True)
        xs = _moe_dense(h_s, lg_s, w1, w3, w2, xs, gts2, tm=ns)

    y_prompt = _rmsnorm(xp, final_norm_g, tm=min(MOE_TM, tp))
    y_sample = _rmsnorm(xs, final_norm_g, tm=ns).reshape(bs, ts, d)
    stack = lambda n: jnp.stack(outs[n])
    return (y_prompt, y_sample, stack("lru_h_p"), stack("lru_c_p"), stack("lru_h_s"), stack("lru_c_s"),
            stack("dsa_k_p"), stack("dsa_v_p"), stack("dsa_i_p"), stack("dsa_k_s"), stack("dsa_v_s"),
            stack("dsa_i_s"), stack("moba_k_p"), stack("moba_v_p"), stack("moba_k_s"), stack("moba_v_s"))
```

```python
import functools
import math

import numpy as np
import jax
import jax.numpy as jnp
from jax import lax
from jax.experimental import pallas as pl
from jax.experimental.pallas import tpu as pltpu

F32 = jnp.float32
BF16 = jnp.bfloat16

LANES = 128
SUBLANES = 8
VMEM_LIMIT_BYTES = 56 * 1024 * 1024

HEAD_DIM = 128
N_HEADS = 16
N_KV_B = 4
N_IDX_HEADS = 16
IDX_DIM = 64
IDX_TOPK = 256
IDX_SCALE = (IDX_DIM * N_IDX_HEADS) ** -0.5
PAGE_SIZE = 128
MOBA_BLOCK = 256
MOBA_TOPK = 3
N_BUCKETS = 32
MAX_DISTANCE = 128
N_RNN_BLOCKS = 16
CONV_W = 4
LRU_C = 8.0
N_GROUPS = 4
EXPERTS_PER_GROUP = 4
N_EXPERTS = N_GROUPS * EXPERTS_PER_GROUP
EPS = 1e-6
ATTN_SCALE = HEAD_DIM ** -0.5
LOG2E = math.log2(math.e)
QK_SCALE_LOG2 = ATTN_SCALE * LOG2E
NEG_INF = float("-inf")


def _params(*sem):
    return pltpu.CompilerParams(dimension_semantics=sem, vmem_limit_bytes=VMEM_LIMIT_BYTES)


def _bdot(a, b):
    return jnp.dot(a.astype(BF16), b.astype(BF16), preferred_element_type=F32)


def _bdot_nt(a, b):
    return lax.dot_general(a.astype(BF16), b.astype(BF16), (((1,), (1,)), ((), ())),
                           preferred_element_type=F32)


def _ada_kernel(c_ref, w_ref, b_ref, o_ref):
    c = c_ref[...]
    o_ref[...] = _bdot(c * jax.nn.sigmoid(c), w_ref[...]) + b_ref[...]


def _ada(c, ada_w, ada_b):
    depth, d, n = ada_w.shape
    rows = c.shape[0]
    tn = next(w for w in (1024, 512, 256, LANES) if n % w == 0)
    return pl.pallas_call(
        _ada_kernel,
        grid=(depth, n // tn),
        in_specs=[
            pl.BlockSpec((rows, d), lambda l, j: (0, 0)),
            pl.BlockSpec((None, d, tn), lambda l, j: (l, 0, j)),
            pl.BlockSpec((None, 1, tn), lambda l, j: (l, 0, j)),
        ],
        out_specs=pl.BlockSpec((None, rows, tn), lambda l, j: (l, 0, j)),
        out_shape=jax.ShapeDtypeStruct((depth, rows, n), F32),
        compiler_params=_params("arbitrary", "arbitrary"),
        name="ada_mod",
    )(c, ada_w, ada_b.reshape(depth, 1, n))


def _norm_mod(x, g, shift, scale):
    ms = jnp.mean(x * x, axis=-1, keepdims=True)
    h = x * lax.rsqrt(ms + EPS) * g
    return h * (1.0 + scale) + shift


def _norm_mm_kernel(x_ref, g_ref, sh_ref, sc_ref, w_ref, *refs, emit_h, highest):
    if emit_h:
        o_ref, ho_ref, h_ref = refs
    else:
        o_ref, h_ref = refs
        ho_ref = None

    @pl.when(pl.program_id(2) == 0)
    def _():
        h = _norm_mod(x_ref[...], g_ref[...], sh_ref[...], sc_ref[...])
        h_ref[...] = h.astype(h_ref.dtype)
        if ho_ref is not None:
            ho_ref[...] = h.astype(ho_ref.dtype)

    if highest:
        o_ref[...] = jnp.dot(h_ref[...], w_ref[...], precision=lax.Precision.HIGHEST,
                             preferred_element_type=F32)
    else:
        o_ref[...] = _bdot(h_ref[...], w_ref[...])


def _norm_matmul(x, g, shift, scale, w, *, tm, tn, emit_h=False, highest=False, h_dtype=BF16):
    bsz, t, d = x.shape
    n = w.shape[1]
    r = shift.shape[1]
    assert t % tm == 0 and n % tn == 0 and (r == 1 or r == tm == t)
    out_shape = [jax.ShapeDtypeStruct((bsz, t, n), F32)]
    out_specs = [pl.BlockSpec((None, tm, tn), lambda b, i, j: (b, i, j))]
    if emit_h:
        out_shape.append(jax.ShapeDtypeStruct((bsz, t, d), h_dtype))
        out_specs.append(pl.BlockSpec((None, tm, d), lambda b, i, j: (b, i, 0)))
    mod_map = (lambda b, i, j: (b, 0, 0)) if r == 1 else (lambda b, i, j: (b, i, 0))
    res = pl.pallas_call(
        functools.partial(_norm_mm_kernel, emit_h=emit_h, highest=highest),
        grid=(bsz, t // tm, n // tn),
        in_specs=[
            pl.BlockSpec((None, tm, d), lambda b, i, j: (b, i, 0)),
            pl.BlockSpec((1, d), lambda b, i, j: (0, 0)),
            pl.BlockSpec((None, r, d), mod_map),
            pl.BlockSpec((None, r, d), mod_map),
            pl.BlockSpec((d, tn), lambda b, i, j: (0, j)),
        ],
        out_specs=out_specs,
        out_shape=out_shape,
        scratch_shapes=[pltpu.VMEM((tm, d), F32 if highest else BF16)],
        compiler_params=_params("arbitrary", "arbitrary", "arbitrary"),
        name="norm_matmul",
    )(x, g.reshape(1, d), shift, scale, w)
    return res if emit_h else res[0]


def _mm_res_kernel(a_ref, w_ref, r_ref, g_ref, o_ref):
    o_ref[...] = r_ref[...] + g_ref[...] * _bdot(a_ref[...], w_ref[...])


def _matmul_residual(a, w, res, gate, *, tm, tn):
    bsz, t, k = a.shape
    n = w.shape[1]
    r = gate.shape[1]
    assert t % tm == 0 and n % tn == 0 and (r == 1 or r == tm == t)
    gate_map = (lambda b, i, j: (b, 0, j)) if r == 1 else (lambda b, i, j: (b, i, j))
    return pl.pallas_call(
        _mm_res_kernel,
        grid=(bsz, t // tm, n // tn),
        in_specs=[
            pl.BlockSpec((None, tm, k), lambda b, i, j: (b, i, 0)),
            pl.BlockSpec((k, tn), lambda b, i, j: (0, j)),
            pl.BlockSpec((None, tm, tn), lambda b, i, j: (b, i, j)),
            pl.BlockSpec((None, r, tn), gate_map),
        ],
        out_specs=pl.BlockSpec((None, tm, tn), lambda b, i, j: (b, i, j)),
        out_shape=jax.ShapeDtypeStruct((bsz, t, n), F32),
        compiler_params=_params("arbitrary", "arbitrary", "arbitrary"),
        name="matmul_residual",
    )(a, w, res, gate)


def _rmsnorm_kernel(x_ref, g_ref, o_ref):
    x = x_ref[...]
    ms = jnp.mean(x * x, axis=-1, keepdims=True)
    o_ref[...] = x * lax.rsqrt(ms + EPS) * g_ref[...]


def _rmsnorm(x, g, *, tm):
    bsz, t, d = x.shape
    return pl.pallas_call(
        _rmsnorm_kernel,
        grid=(bsz, t // tm),
        in_specs=[pl.BlockSpec((None, tm, d), lambda b, i: (b, i, 0)),
                  pl.BlockSpec((1, d), lambda b, i: (0, 0))],
        out_specs=pl.BlockSpec((None, tm, d), lambda b, i: (b, i, 0)),
        out_shape=jax.ShapeDtypeStruct((bsz, t, d), F32),
        compiler_params=_params("arbitrary", "arbitrary"),
        name="final_rmsnorm",
    )(x, g.reshape(1, d))


GATE_CHUNK = 640


def _log_sigmoid(x):
    return jnp.minimum(x, 0.0) - jnp.log1p(jnp.exp(-jnp.abs(x)))


def _lru_kernel(gb_ref, xb_ref, cs_ref, h0_ref, cw_ref, cb_ref, wr_ref, br_ref, wi_ref, bi_ref, lam_ref,
                y_ref, hl_ref, nc_ref, ext_ref, a_ref, b_ref, hs_ref, carry_ref, *, tt):
    c_all = xb_ref.shape[-1]

    @pl.when(pl.program_id(1) == 0)
    def _():
        ext_ref[0:SUBLANES, :] = cs_ref[...]
        carry_ref[...] = h0_ref[...]

    xb = xb_ref[...]
    ext_ref[SUBLANES:SUBLANES + tt, :] = xb
    cw = cw_ref[...]
    xc = cb_ref[...] + xb * cw[CONV_W - 1:CONV_W, :]
    for j in range(1, CONV_W):
        xc = xc + ext_ref[pl.ds(SUBLANES - j, tt), :] * cw[CONV_W - 1 - j:CONV_W - j, :]
    tail = ext_ref[tt:tt + SUBLANES, :]
    ext_ref[0:SUBLANES, :] = tail
    nc_ref[...] = tail

    xcb = xc.astype(BF16)
    r_parts, i_parts = [], []
    for c in range(c_all // GATE_CHUNK):
        sl = slice(c * GATE_CHUNK, (c + 1) * GATE_CHUNK)
        r_parts.append(jnp.dot(xcb[:, sl], wr_ref[c], preferred_element_type=F32))
        i_parts.append(jnp.dot(xcb[:, sl], wi_ref[c], preferred_element_type=F32))
    r = jax.nn.sigmoid(jnp.concatenate(r_parts, axis=1) + br_ref[...])
    ig = jax.nn.sigmoid(jnp.concatenate(i_parts, axis=1) + bi_ref[...])
    log_a = LRU_C * r * _log_sigmoid(lam_ref[...])
    th = jnp.tanh(log_a)
    one_minus_a2 = -2.0 * th / (1.0 - th)
    a_ref[...] = jnp.exp(log_a)
    b_ref[...] = jnp.sqrt(jnp.maximum(one_minus_a2, 0.0)) * (ig * xc)

    row = lax.broadcasted_iota(jnp.int32, (SUBLANES, GATE_CHUNK), 0)
    for c in range(c_all // GATE_CHUNK):
        sl = slice(c * GATE_CHUNK, (c + 1) * GATE_CHUNK)

        def body(grp, carry, sl=sl):
            r8 = pl.multiple_of(grp * SUBLANES, SUBLANES)
            av = a_ref[pl.ds(r8, SUBLANES), sl]
            bv = b_ref[pl.ds(r8, SUBLANES), sl]
            for s in (1, 2, 4):
                m = row >= s
                a_sh = jnp.where(m, pltpu.roll(av, s, axis=0), 1.0)
                b_sh = jnp.where(m, pltpu.roll(bv, s, axis=0), 0.0)
                bv = av * b_sh + bv
                av = av * a_sh
            h = av * carry + bv
            hs_ref[pl.ds(r8, SUBLANES), sl] = h
            return h[SUBLANES - 1:SUBLANES, :]

        carry_ref[:, sl] = lax.fori_loop(0, tt // SUBLANES, body, carry_ref[:, sl])

    hl_ref[...] = carry_ref[...]
    y_ref[...] = (jax.nn.gelu(gb_ref[...]) * hs_ref[...]).astype(y_ref.dtype)


def _lru_core(u, conv_state, h0, conv_w, conv_b, w_r, b_r, w_i, b_i, lam, *, tt):
    bsz, t, c2 = u.shape
    c = c2 // 2
    blk = c // N_RNN_BLOCKS
    per = GATE_CHUNK // blk
    nchunk = c // GATE_CHUNK

    def block_diag(w):
        wc = w.reshape(nchunk, per, blk, blk).astype(BF16)
        eye = jnp.eye(per, dtype=BF16)
        return jnp.einsum("cpij,pq->cpiqj", wc, eye).reshape(nchunk, GATE_CHUNK, GATE_CHUNK)

    cs_pad = jnp.pad(conv_state, ((0, 0), (SUBLANES - (CONV_W - 1), 0), (0, 0)))
    row = lambda v: v.reshape(1, c)
    full = lambda shp: pl.BlockSpec(shp, lambda b, i: (0,) * len(shp))
    y, hl, nc = pl.pallas_call(
        functools.partial(_lru_kernel, tt=tt),
        grid=(bsz, t // tt),
        in_specs=[
            pl.BlockSpec((None, tt, c), lambda b, i: (b, i, 0)),
            pl.BlockSpec((None, tt, c), lambda b, i: (b, i, 1)),
            pl.BlockSpec((None, SUBLANES, c), lambda b, i: (b, 0, 0)),
            pl.BlockSpec((None, 1, c), lambda b, i: (b, 0, 0)),
            full((CONV_W, c)), full((1, c)),
            full((nchunk, GATE_CHUNK, GATE_CHUNK)), full((1, c)),
            full((nchunk, GATE_CHUNK, GATE_CHUNK)), full((1, c)),
            full((1, c)),
        ],
        out_specs=[
            pl.BlockSpec((None, tt, c), lambda b, i: (b, i, 0)),
            pl.BlockSpec((None, 1, c), lambda b, i: (b, 0, 0)),
            pl.BlockSpec((None, SUBLANES, c), lambda b, i: (b, 0, 0)),
        ],
        out_shape=[
            jax.ShapeDtypeStruct((bsz, t, c), BF16),
            jax.ShapeDtypeStruct((bsz, 1, c), F32),
            jax.ShapeDtypeStruct((bsz, SUBLANES, c), F32),
        ],
        scratch_shapes=[
            pltpu.VMEM((tt + SUBLANES, c), F32),
            pltpu.VMEM((tt, c), F32),
            pltpu.VMEM((tt, c), F32),
            pltpu.VMEM((tt, c), F32),
            pltpu.VMEM((1, c), F32),
        ],
        compiler_params=_params("arbitrary", "arbitrary"),
        name="lru_core",
    )(u, u, cs_pad, h0.reshape(bsz, 1, c), conv_w, row(conv_b), block_diag(w_r), row(b_r),
      block_diag(w_i), row(b_i), row(lam))
    return y, hl.reshape(bsz, c), nc[:, SUBLANES - (CONV_W - 1):, :]


def _route_gates(lg):
    return _route(lg)[0]


def _route(lg):
    col = lax.broadcasted_iota(jnp.int32, lg.shape, 1)
    big = jnp.int32(LANES)
    is_g = col < N_GROUPS
    gl = jnp.where(is_g, lg, NEG_INF)
    gmax = jnp.max(gl, axis=1, keepdims=True)
    g_sel = jnp.min(jnp.where(gl == gmax, col, big), axis=1, keepdims=True)
    g_w = 1.0 / jnp.sum(jnp.exp(gl - gmax), axis=1, keepdims=True)
    eid = col - N_GROUPS
    in_grp = (eid >= 0) & (eid < N_EXPERTS) & ((eid // EXPERTS_PER_GROUP) == g_sel)
    e_in = jnp.where(in_grp, lg, NEG_INF)
    top1 = jnp.max(e_in, axis=1, keepdims=True)
    idx1 = jnp.min(jnp.where((e_in == top1) & in_grp, col, big), axis=1, keepdims=True)
    rest = in_grp & (col != idx1)
    e2 = jnp.where(rest, lg, NEG_INF)
    top2 = jnp.max(e2, axis=1, keepdims=True)
    idx2 = jnp.min(jnp.where((e2 == top2) & rest, col, big), axis=1, keepdims=True)
    z = jnp.exp(top2 - top1)
    w_first = g_w / (1.0 + z)
    w_second = g_w * z / (1.0 + z)
    return jnp.where(col == idx1, w_first, 0.0) + jnp.where(col == idx2, w_second, 0.0), g_sel


def _moe_dense_kernel(h_ref, lg_ref, w1_ref, w3_ref, w2_ref, x_ref, g_ref, o_ref, gates_ref, acc_ref):
    e = pl.program_id(2)

    @pl.when(e == 0)
    def _():
        gates_ref[...] = _route_gates(lg_ref[...])
        acc_ref[...] = jnp.zeros_like(acc_ref)

    gates = gates_ref[...]
    col = lax.broadcasted_iota(jnp.int32, gates.shape, 1)
    ge = jnp.sum(jnp.where(col == e + N_GROUPS, gates, 0.0), axis=1, keepdims=True)
    h = h_ref[...]
    a1 = jnp.dot(h, w1_ref[...], preferred_element_type=F32)
    a3 = jnp.dot(h, w3_ref[...], preferred_element_type=F32)
    hidden = (a1 * jax.nn.sigmoid(a1)) * a3 * ge
    acc_ref[...] += jnp.dot(hidden.astype(BF16), w2_ref[...], preferred_element_type=F32)

    @pl.when(e == pl.num_programs(2) - 1)
    def _():
        o_ref[...] = x_ref[...] + g_ref[...] * acc_ref[...]


def _moe_dense(h, logits, w1, w3, w2, x, gate, *, tm):
    bsz, t, d = x.shape
    ne, _, f = w1.shape
    r = gate.shape[1]
    gate_map = (lambda b, i, e: (b, 0, 0)) if r == 1 else (lambda b, i, e: (b, i, 0))
    tok = lambda b, i, e: (b, i, 0)
    return pl.pallas_call(
        _moe_dense_kernel,
        grid=(bsz, t // tm, ne),
        in_specs=[
            pl.BlockSpec((None, tm, d), tok),
            pl.BlockSpec((None, tm, LANES), tok),
            pl.BlockSpec((None, d, f), lambda b, i, e: (e, 0, 0)),
            pl.BlockSpec((None, d, f), lambda b, i, e: (e, 0, 0)),
            pl.BlockSpec((None, f, d), lambda b, i, e: (e, 0, 0)),
            pl.BlockSpec((None, tm, d), tok),
            pl.BlockSpec((None, r, d), gate_map),
        ],
        out_specs=pl.BlockSpec((None, tm, d), tok),
        out_shape=jax.ShapeDtypeStruct((bsz, t, d), F32),
        scratch_shapes=[pltpu.VMEM((tm, LANES), F32), pltpu.VMEM((tm, d), F32)],
        compiler_params=_params("arbitrary", "arbitrary", "arbitrary"),
        name="moe_dense",
    )(h, logits, w1, w3, w2, x, gate)


INFO_GROUP_LANE = 0
INFO_RANK_LANE = 1


def _route_kernel(lg_ref, info_ref, cnt_ref, carry_ref):
    @pl.when(pl.program_id(0) == 0)
    def _():
        carry_ref[...] = jnp.zeros_like(carry_ref)

    gates, g_sel = _route(lg_ref[...])
    tm = gates.shape[0]
    col = lax.broadcasted_iota(jnp.int32, gates.shape, 1)
    onehot = (col == g_sel).astype(F32)
    before = (lax.broadcasted_iota(jnp.int32, (tm, tm), 0) > lax.broadcasted_iota(jnp.int32, (tm, tm), 1))
    earlier = jnp.dot(before.astype(BF16), onehot.astype(BF16), preferred_element_type=F32) + carry_ref[...]
    rank = jnp.sum(jnp.where(col == g_sel, earlier, 0.0), axis=1, keepdims=True)
    carry_ref[...] += jnp.sum(onehot, axis=0, keepdims=True)
    info_ref[...] = (gates + jnp.where(col == INFO_GROUP_LANE, g_sel.astype(F32), 0.0)
                     + jnp.where(col == INFO_RANK_LANE, rank, 0.0))
    cnt_ref[...] = carry_ref[...]


def _route_tokens(logits, *, tm):
    n = logits.shape[0]
    assert n % tm == 0
    return pl.pallas_call(
        _route_kernel,
        grid=(n // tm,),
        in_specs=[pl.BlockSpec((tm, LANES), lambda i: (i, 0))],
        out_specs=[pl.BlockSpec((tm, LANES), lambda i: (i, 0)), pl.BlockSpec((1, LANES), lambda i: (0, 0))],
        out_shape=[jax.ShapeDtypeStruct((n, LANES), F32), jax.ShapeDtypeStruct((1, LANES), F32)],
        scratch_shapes=[pltpu.VMEM((1, LANES), F32)],
        compiler_params=_params("arbitrary"),
        name="moe_route",
    )(logits)


def _group_layout(info, counts, tm):
    n = info.shape[0]
    g_sel = info[:, INFO_GROUP_LANE].astype(jnp.int32)
    rank = info[:, INFO_RANK_LANE].astype(jnp.int32)
    cnt = counts[0, :N_GROUPS].astype(jnp.int32)
    padded = -(-cnt // tm) * tm
    ends = jnp.cumsum(padded)
    starts = ends - padded
    n_slots = n + N_GROUPS * tm
    src = _slot_tokens(starts[g_sel] + rank, n_slots)
    tile_start = jnp.arange(n_slots // tm, dtype=jnp.int32) * tm
    tile_group = jnp.minimum(jnp.sum((tile_start[:, None] >= ends[None, :]).astype(jnp.int32), axis=1), N_GROUPS - 1)
    slot_group = jnp.repeat(tile_group, tm)
    is_pad = jnp.arange(n_slots, dtype=jnp.int32) - starts[slot_group] >= cnt[slot_group]
    pad_row = n + tm + jnp.cumsum(is_pad.astype(jnp.int32)) - 1
    dst = jnp.concatenate([n + jnp.arange(tm, dtype=jnp.int32), jnp.where(is_pad, pad_row, src)])
    return src, dst, tile_group, (ends[-1] // tm).reshape(1)


SLOT_UNROLL = 8


def _slot_tokens_kernel(slot_ref, src_ref):
    n, n_slots = slot_ref.shape[0], src_ref.shape[0]

    def clear(c, carry):
        for k in range(SLOT_UNROLL):
            src_ref[c * SLOT_UNROLL + k] = 0
        return carry

    def place(c, carry):
        for k in range(SLOT_UNROLL):
            t = c * SLOT_UNROLL + k
            src_ref[slot_ref[t]] = t
        return carry

    lax.fori_loop(0, n_slots // SLOT_UNROLL, clear, 0)
    lax.fori_loop(0, n // SLOT_UNROLL, place, 0)


def _slot_tokens(slot, n_slots):
    assert slot.shape[0] % SLOT_UNROLL == 0 and n_slots % SLOT_UNROLL == 0
    return pl.pallas_call(
        _slot_tokens_kernel,
        in_specs=[pl.BlockSpec(memory_space=pltpu.SMEM)],
        out_specs=pl.BlockSpec(memory_space=pltpu.SMEM),
        out_shape=jax.ShapeDtypeStruct((n_slots,), jnp.int32),
        name="moe_slot_tokens",
    )(slot)


def _moe_group_kernel(src_ref, dst_ref, tg_ref, used_ref, h_hbm, info_hbm, w1_ref, w3_ref, w2_ref, zeros_hbm,
                      out_hbm, xbuf, gbuf, xb, gates, acc, obuf, sem_x, sem_g, sem_o, *, tm):
    del zeros_hbm
    i = pl.program_id(0)
    e = pl.program_id(1)
    used = used_ref[0]
    live = i < used
    chunk = tm // EXPERTS_PER_GROUP
    first = e * chunk
    last_step = e == EXPERTS_PER_GROUP - 1

    def gather_row(tile, r):
        tok = src_ref[tile * tm + r]
        pltpu.make_async_copy(h_hbm.at[pl.ds(tok, 1), :], xbuf.at[pl.ds(r, 1), :], sem_x).start()
        pltpu.make_async_copy(info_hbm.at[pl.ds(tok, 1), :], gbuf.at[pl.ds(r, 1), :], sem_g).start()

    def scatter_chunk(tile):
        slot = (tile + 2) % 2
        for k in range(chunk):
            row = dst_ref[(tile + 1) * tm + first + k]
            pltpu.make_async_copy(obuf.at[slot, pl.ds(first + k, 1), :], out_hbm.at[pl.ds(row, 1), :],
                                  sem_o.at[slot]).start()

    def wait_scatter(slot):
        pltpu.make_async_copy(obuf.at[slot], out_hbm.at[pl.ds(0, tm), :], sem_o.at[slot]).wait()

    @pl.when((i == 0) & (e == 0))
    def _():
        obuf[...] = jnp.zeros_like(obuf)

        def body(r, carry):
            gather_row(0, r)
            return carry
        lax.fori_loop(0, tm, body, 0)

    @pl.when((e == 0) & (i <= used))
    def _():
        pltpu.make_async_copy(h_hbm.at[pl.ds(0, tm), :], xbuf, sem_x).wait()
        pltpu.make_async_copy(info_hbm.at[pl.ds(0, tm), :], gbuf, sem_g).wait()

    @pl.when((e == 0) & live)
    def _():
        xb[...] = xbuf[...].astype(BF16)
        gates[...] = gbuf[...]
        acc[...] = jnp.zeros_like(acc)

    @pl.when(live)
    def _():
        lane = lax.broadcasted_iota(jnp.int32, (tm, LANES), 1)
        gate_lane = N_GROUPS + EXPERTS_PER_GROUP * tg_ref[i] + e
        ge = jnp.sum(jnp.where(lane == gate_lane, gates[...], 0.0), axis=1, keepdims=True)
        x = xb[...]
        a1 = jnp.dot(x, w1_ref[...], preferred_element_type=F32)
        a3 = jnp.dot(x, w3_ref[...], preferred_element_type=F32)
        hidden = (a1 * jax.nn.sigmoid(a1)) * a3 * ge
        acc[...] += jnp.dot(hidden.astype(BF16), w2_ref[...], preferred_element_type=F32)
        for k in range(chunk):
            gather_row(i + 1, first + k)
        scatter_chunk(i - 1)

    @pl.when(i == used)
    def _():
        scatter_chunk(i - 1)

    @pl.when(last_step & live & (i >= 1))
    def _():
        wait_scatter(i % 2)

    @pl.when(last_step & live)
    def _():
        obuf[i % 2] = acc[...]

    @pl.when(last_step & (i == used))
    def _():
        wait_scatter(0)
        wait_scatter(1)


def _moe_grouped(h, info, src, dst, tile_group, n_used, w1, w3, w2, *, tm):
    n, d = h.shape
    f = w1.shape[2]
    n_tiles = src.shape[0] // tm
    n_rows = n + (N_GROUPS + 1) * tm
    assert tm % EXPERTS_PER_GROUP == 0 and dst.shape[0] == (n_tiles + 1) * tm

    def expert(i, e, src_, dst_, tg, used):
        e_eff = jnp.where(i < used[0], e, EXPERTS_PER_GROUP - 1)
        return (tg[i] * EXPERTS_PER_GROUP + e_eff, 0, 0)

    return pl.pallas_call(
        functools.partial(_moe_group_kernel, tm=tm),
        grid_spec=pltpu.PrefetchScalarGridSpec(
            num_scalar_prefetch=4,
            grid=(n_tiles, EXPERTS_PER_GROUP),
            in_specs=[
                pl.BlockSpec(memory_space=pl.ANY),
                pl.BlockSpec(memory_space=pl.ANY),
                pl.BlockSpec((None, d, f), expert),
                pl.BlockSpec((None, d, f), expert),
                pl.BlockSpec((None, f, d), expert),
                pl.BlockSpec(memory_space=pl.ANY),
            ],
            out_specs=pl.BlockSpec(memory_space=pl.ANY),
            scratch_shapes=[
                pltpu.VMEM((tm, d), F32), pltpu.VMEM((tm, LANES), F32), pltpu.VMEM((tm, d), BF16),
                pltpu.VMEM((tm, LANES), F32), pltpu.VMEM((tm, d), F32), pltpu.VMEM((2, tm, d), F32),
                pltpu.SemaphoreType.DMA(()), pltpu.SemaphoreType.DMA(()), pltpu.SemaphoreType.DMA((2,)),
            ],
        ),
        out_shape=jax.ShapeDtypeStruct((n_rows, d), F32),
        input_output_aliases={9: 0},
        compiler_params=_params("arbitrary", "arbitrary"),
        name="moe_grouped",
    )(src, dst, tile_group, n_used, h, info, w1, w3, w2, jnp.zeros((n_rows, d), F32))


def _gated_add_kernel(x_ref, y_ref, g_ref, o_ref):
    o_ref[...] = x_ref[...] + g_ref[...] * y_ref[...]


def _gated_add(x, y, gate, *, tm):
    bsz, t, d = x.shape
    per_b = t // tm
    return pl.pallas_call(
        _gated_add_kernel,
        grid=(bsz, per_b),
        in_specs=[
            pl.BlockSpec((None, tm, d), lambda b, i: (b, i, 0)),
            pl.BlockSpec((tm, d), lambda b, i: (b * per_b + i, 0)),
            pl.BlockSpec((None, 1, d), lambda b, i: (b, 0, 0)),
        ],
        out_specs=pl.BlockSpec((None, tm, d), lambda b, i: (b, i, 0)),
        out_shape=jax.ShapeDtypeStruct((bsz, t, d), F32),
        compiler_params=_params("arbitrary", "arbitrary"),
        name="gated_add",
    )(x, y, gate)


MASKED = -1e30


def _t5_bucket_table(tq):
    qi = np.arange(tq, dtype=np.int32)[:, None]
    col = np.arange(2 * tq, dtype=np.int32)[None, :]
    n = np.maximum(qi + tq - col, 0)
    max_exact = N_BUCKETS // 2
    nf = np.maximum(n, 1).astype(np.float32)
    large = max_exact + (np.log(nf / np.float32(max_exact)) / np.float32(math.log(MAX_DISTANCE / max_exact))
                         * np.float32(N_BUCKETS - max_exact)).astype(np.int32)
    table = np.where(n < max_exact, n, np.minimum(large, N_BUCKETS - 1)).astype(np.int32)
    assert table[0, 0] == N_BUCKETS - 1
    return table


def _bias_kernel(rb_ref, bk_ref, o_ref):
    h = pl.program_id(0)
    bk = bk_ref[...]
    acc = jnp.zeros(bk.shape, F32)
    for k in range(N_BUCKETS):
        acc = jnp.where(bk == k, rb_ref[k, h], acc)
    o_ref[...] = (acc - rb_ref[N_BUCKETS - 1, h]) * LOG2E


def _near_bias(rel_bias, tq):
    return pl.pallas_call(
        _bias_kernel,
        grid=(N_HEADS,),
        in_specs=[pl.BlockSpec(memory_space=pltpu.SMEM),
                  pl.BlockSpec((tq, 2 * tq), lambda h: (0, 0))],
        out_specs=pl.BlockSpec((None, tq, 2 * tq), lambda h: (h, 0, 0)),
        out_shape=jax.ShapeDtypeStruct((N_HEADS, tq, 2 * tq), F32),
        name="near_bias",
    )(rel_bias, jnp.asarray(_t5_bucket_table(tq)))


_NEG_INF_KEY = int(np.int32(np.array(-np.inf, np.float32).view(np.int32)) ^ np.int32(0x7FFFFFFF))


def _sort_key(x):
    k = pltpu.bitcast(x, jnp.int32)
    return k ^ ((k >> 31) & jnp.int32(0x7FFFFFFF))


def _kth_largest_key(count_ge, kth):
    zero = jnp.int32(0)
    ans = jnp.where(count_ge(zero) >= kth, zero, jnp.int32(-2 ** 31))

    def body(it, ans):
        cand = ans | (jnp.int32(1) << (30 - it))
        return jnp.where(count_ge(cand) >= kth, cand, ans)

    return lax.fori_loop(0, 31, body, ans)


def _dsa_select_kernel(qi_ref, kw_ref, kall_ref, m_ref, key_ref, *, tq, tk, topk):
    i = pl.program_id(1)
    n_kt = (i * tq + tq + tk - 1) // tk
    qpos = i * tq + lax.broadcasted_iota(jnp.int32, (tq, tk), 0)
    wi = kw_ref[:, IDX_DIM:IDX_DIM + N_IDX_HEADS]
    qi = qi_ref[...].astype(BF16)
    key_ref[...] = jnp.full(key_ref.shape, _NEG_INF_KEY, jnp.int32)

    def score_tile(kt, carry):
        k0 = pl.multiple_of(kt * tk, tk)
        kid = kall_ref[pl.ds(k0, tk), 0:IDX_DIM].astype(BF16)
        acc = jnp.zeros((tq, tk), F32)
        for h in range(N_IDX_HEADS):
            s = _bdot_nt(qi[:, h * IDX_DIM:(h + 1) * IDX_DIM], kid)
            acc = acc + wi[:, h:h + 1] * jnp.maximum(s, 0.0)
        kpos = k0 + lax.broadcasted_iota(jnp.int32, (tq, tk), 1)
        sc = jnp.where(kpos <= qpos, acc * IDX_SCALE, NEG_INF)
        key_ref[:, pl.ds(k0, tk)] = _sort_key(sc)
        return carry

    lax.fori_loop(0, n_kt, score_tile, 0)

    def count_ge(cand):
        def body(kt, acc):
            k0 = pl.multiple_of(kt * tk, tk)
            ge = (key_ref[:, pl.ds(k0, tk)] >= cand).astype(jnp.int32)
            for j in range(tk // LANES):
                acc = acc + ge[:, j * LANES:(j + 1) * LANES]
            return acc
        acc = lax.fori_loop(0, n_kt, body, jnp.zeros((tq, LANES), jnp.int32))
        return jnp.sum(acc, axis=1, keepdims=True)

    thr = _kth_largest_key(count_ge, topk)
    keys = key_ref[...]
    m_ref[...] = ((keys >= thr) & (keys > _NEG_INF_KEY)).astype(m_ref.dtype)


def _dsa_select(proj, *, tq, tk, topk):
    bsz, t, _ = proj.shape
    qi_w = N_IDX_HEADS * IDX_DIM
    qi_blk = (N_HEADS * HEAD_DIM + 2 * N_KV_B * HEAD_DIM) // qi_w
    kw_blk = (N_HEADS * HEAD_DIM + 2 * N_KV_B * HEAD_DIM + qi_w) // LANES
    return pl.pallas_call(
        functools.partial(_dsa_select_kernel, tq=tq, tk=tk, topk=topk),
        grid=(bsz, t // tq),
        in_specs=[
            pl.BlockSpec((None, tq, qi_w), lambda b, i: (b, i, qi_blk)),
            pl.BlockSpec((None, tq, LANES), lambda b, i: (b, i, kw_blk)),
            pl.BlockSpec((None, t, LANES), lambda b, i: (b, 0, kw_blk)),
        ],
        out_specs=pl.BlockSpec((None, tq, t), lambda b, i: (b, i, 0)),
        out_shape=jax.ShapeDtypeStruct((bsz, t, t), BF16),
        scratch_shapes=[pltpu.VMEM((tq, t), jnp.int32)],
        compiler_params=_params("arbitrary", "arbitrary"),
        name="dsa_select",
    )(proj, proj, proj)


def _attn_kernel(q_ref, k_ref, v_ref, m_ref, tb_ref, o_ref, *state, tq, rep, shared_kv, block_mask):
    m_sc, l_sc, acc_sc = state[:rep], state[rep:2 * rep], state[2 * rep:]
    i = pl.program_id(2)
    for r in range(rep):
        m_sc[r][...] = jnp.full((tq, LANES), MASKED, F32)
        l_sc[r][...] = jnp.zeros((tq, LANES), F32)
        acc_sc[r][...] = jnp.zeros((tq, HEAD_DIM), F32)
    lane_tiles = tq // LANES

    def row_reduce(x, op, reduce):
        part = x[:, :LANES]
        for c in range(1, lane_tiles):
            part = op(part, x[:, c * LANES:(c + 1) * LANES])
        return reduce(part, axis=1, keepdims=True)

    def key_mask(j, r, causal):
        if block_mask:
            lane = lax.broadcasted_iota(jnp.int32, (tq, LANES), 1)
            vis = jnp.sum(jnp.where(lane == j, m_ref[r].astype(F32), 0.0), axis=1, keepdims=True)
            ok = jnp.broadcast_to(vis > 0.0, (tq, tq))
        else:
            ok = m_ref[:, pl.ds(pl.multiple_of(j * tq, tq), tq)] > 0
        if causal:
            ok = ok & (lax.broadcasted_iota(jnp.int32, (tq, tq), 0) >= lax.broadcasted_iota(jnp.int32, (tq, tq), 1))
        return ok

    def step(j, bias_lo, causal):
        k0 = pl.multiple_of(j * tq, tq)
        kv = lambda r: slice(0, HEAD_DIM) if shared_kv else slice(r * HEAD_DIM, (r + 1) * HEAD_DIM)
        ok_shared = None if block_mask else key_mask(j, 0, causal)
        scores = []
        for r in range(rep):
            q = (q_ref[:, r * HEAD_DIM:(r + 1) * HEAD_DIM] * QK_SCALE_LOG2).astype(BF16)
            s = _bdot_nt(q, k_ref[pl.ds(k0, tq), kv(r)])
            if bias_lo is not None:
                s = s + tb_ref[r, :, bias_lo:bias_lo + tq]
            scores.append(jnp.where(key_mask(j, r, causal) if block_mask else ok_shared, s, MASKED))
        probs, alphas = [], []
        for r in range(rep):
            m_old = m_sc[r][...]
            m_new = jnp.maximum(m_old, row_reduce(scores[r], jnp.maximum, jnp.max))
            alpha = jnp.exp2(m_old - m_new)
            p = jnp.exp2(scores[r] - jnp.concatenate([m_new] * lane_tiles, axis=1))
            l_sc[r][...] = alpha * l_sc[r][...] + row_reduce(p, jnp.add, jnp.sum)
            m_sc[r][...] = m_new
            probs.append(p.astype(BF16))
            alphas.append(alpha)
        for r in range(rep):
            pv = jnp.dot(probs[r], v_ref[pl.ds(k0, tq), kv(r)].astype(BF16), preferred_element_type=F32)
            acc_sc[r][...] = alphas[r] * acc_sc[r][...] + pv

    def far(j, carry):
        step(j, None, False)
        return carry

    lax.fori_loop(0, jnp.maximum(i - 1, 0), far, 0)

    @pl.when(i >= 1)
    def _():
        step(i - 1, 0, False)

    step(i, tq, True)
    o_ref[...] = jnp.concatenate([acc_sc[r][...] / l_sc[r][...] for r in range(rep)], axis=1).astype(o_ref.dtype)


def _attention(proj, k_col, v_col, mask, near_bias, *, tq, rep, shared_kv, block_mask):
    bsz, t, _ = proj.shape
    g = N_HEADS // rep
    kv_w = HEAD_DIM if shared_kv else rep * HEAD_DIM
    assert HEAD_DIM == LANES and k_col % kv_w == 0 and v_col % kv_w == 0
    k_blk, v_blk = k_col // kv_w, v_col // kv_w
    if block_mask:
        assert tq == MOBA_BLOCK
        mask_spec = pl.BlockSpec((None, rep, tq, LANES), lambda b, h, i: (b, h, i, 0))
    else:
        mask_spec = pl.BlockSpec((None, tq, t), lambda b, h, i: (b, i, 0))
    stat = pltpu.VMEM((tq, LANES), F32)
    return pl.pallas_call(
        functools.partial(_attn_kernel, tq=tq, rep=rep, shared_kv=shared_kv, block_mask=block_mask),
        grid=(bsz, g, t // tq),
        in_specs=[
            pl.BlockSpec((None, tq, rep * HEAD_DIM), lambda b, h, i: (b, i, h)),
            pl.BlockSpec((None, t, kv_w), lambda b, h, i: (b, 0, k_blk + h)),
            pl.BlockSpec((None, t, kv_w), lambda b, h, i: (b, 0, v_blk + h)),
            mask_spec,
            pl.BlockSpec((rep, tq, 2 * tq), lambda b, h, i: (h, 0, 0)),
        ],
        out_specs=pl.BlockSpec((None, tq, rep * HEAD_DIM), lambda b, h, i: (b, i, h)),
        out_shape=jax.ShapeDtypeStruct((bsz, t, N_HEADS * HEAD_DIM), BF16),
        scratch_shapes=[stat] * (3 * rep),
        compiler_params=_params("arbitrary", "arbitrary", "arbitrary"),
        name="masked_attention",
    )(proj, proj, proj, mask, near_bias)


PAGES_PER_BLOCK = MOBA_BLOCK // PAGE_SIZE
MEANS_BLOCKS_PER_STEP = 2


def _block_means_kernel(tbl_ref, *refs):
    del tbl_ref
    page_refs, o_ref = refs[:-1], refs[-1]
    for blk in range(MEANS_BLOCKS_PER_STEP):
        pages = page_refs[blk * PAGES_PER_BLOCK:(blk + 1) * PAGES_PER_BLOCK]
        total = jnp.sum(pages[0][...], axis=0)
        for ref in pages[1:]:
            total = total + jnp.sum(ref[...], axis=0)
        o_ref[blk] = total / MOBA_BLOCK


def _block_means(pool, layer, table):
    _, _, _, h, dh = pool.shape
    bsz, npg = table.shape
    nb = npg // PAGES_PER_BLOCK
    per_step = MEANS_BLOCKS_PER_STEP * PAGES_PER_BLOCK
    assert npg % per_step == 0
    page = lambda k: pl.BlockSpec((None, None, PAGE_SIZE, h, dh),
                                  lambda b, n, tbl: (layer, tbl[b, per_step * n + k], 0, 0, 0))
    out = pl.pallas_call(
        _block_means_kernel,
        grid_spec=pltpu.PrefetchScalarGridSpec(
            num_scalar_prefetch=1,
            grid=(bsz, nb // MEANS_BLOCKS_PER_STEP),
            in_specs=[page(k) for k in range(per_step)],
            out_specs=pl.BlockSpec((None, MEANS_BLOCKS_PER_STEP, h, dh), lambda b, n, tbl: (b, n, 0, 0)),
        ),
        out_shape=jax.ShapeDtypeStruct((bsz, nb, h, dh), F32),
        compiler_params=_params("arbitrary", "arbitrary"),
        name="block_means",
    )(table, *([pool] * per_step))
    return out.reshape(bsz, nb, h * dh)


def _moba_select_kernel(q_ref, mean_ref, o_ref, *, tq, past_len):
    i = pl.program_id(1)
    own = (past_len + i * tq + lax.broadcasted_iota(jnp.int32, (tq, LANES), 0)) // MOBA_BLOCK
    blk = lax.broadcasted_iota(jnp.int32, (tq, LANES), 1)
    past = blk < own
    for h in range(N_HEADS):
        sl = slice(h * HEAD_DIM, (h + 1) * HEAD_DIM)
        gate = lax.dot_general(q_ref[:, sl], mean_ref[:, sl], (((1,), (1,)), ((), ())),
                               precision=lax.Precision.HIGHEST, preferred_element_type=F32)
        g = jnp.where(past, gate, NEG_INF)
        kth = jnp.max(g, axis=1, keepdims=True)
        for _ in range(MOBA_TOPK - 1):
            kth = jnp.max(jnp.where(g < kth, g, NEG_INF), axis=1, keepdims=True)
        o_ref[h] = ((past & (g >= kth)) | (blk == own)).astype(o_ref.dtype)


def _moba_select(q_src, means, *, tq, past_len):
    bsz, t, _ = q_src.shape
    w = N_HEADS * HEAD_DIM
    return pl.pallas_call(
        functools.partial(_moba_select_kernel, tq=tq, past_len=past_len),
        grid=(bsz, t // tq),
        in_specs=[
            pl.BlockSpec((None, tq, w), lambda b, i: (b, i, 0)),
            pl.BlockSpec((None, LANES, w), lambda b, i: (b, 0, 0)),
        ],
        out_specs=pl.BlockSpec((None, N_HEADS, tq, LANES), lambda b, i: (b, 0, i, 0)),
        out_shape=jax.ShapeDtypeStruct((bsz, N_HEADS, t, LANES), BF16),
        compiler_params=_params("arbitrary", "arbitrary"),
        name="moba_select",
    )(q_src, means)


def _dsa_select_paged_kernel(tbl_ref, qi_ref, wi_ref, *rest, pp, topk, past_len, t_new):
    del tbl_ref
    page_refs = rest[:pp]
    knew_ref, m_ref, key_ref = rest[pp:]
    p = pl.program_id(1)
    qi = qi_ref[...].astype(BF16)
    wi = wi_ref[...]

    def score(kpage):
        s = _bdot_nt(qi, kpage)
        w = wi * jnp.maximum(s, 0.0)
        return jnp.sum(w.reshape(N_IDX_HEADS, t_new, PAGE_SIZE), axis=0) * IDX_SCALE

    for k in range(pp):
        off = pl.multiple_of((p * pp + k) * PAGE_SIZE, PAGE_SIZE)
        key_ref[:, pl.ds(off, PAGE_SIZE)] = _sort_key(score(page_refs[k][...]))

    @pl.when(p == pl.num_programs(1) - 1)
    def _():
        row = lax.broadcasted_iota(jnp.int32, (t_new, PAGE_SIZE), 0)
        col = lax.broadcasted_iota(jnp.int32, (t_new, PAGE_SIZE), 1)
        sc = jnp.where(col <= row, score(knew_ref[...]), NEG_INF)
        key_ref[:, past_len:past_len + PAGE_SIZE] = _sort_key(sc)

        def count_ge(cand):
            return jnp.sum((key_ref[...] >= cand).astype(jnp.int32), axis=1, keepdims=True)

        thr = _kth_largest_key(count_ge, topk)
        keys = key_ref[...]
        m_ref[...] = ((keys >= thr) & (keys > _NEG_INF_KEY)).astype(m_ref.dtype)


def _dsa_select_paged(qi_rows, wi_rows, kidx_pool, layer, page_table, kidx_new, *, pp, topk):
    bsz, rows, _ = qi_rows.shape
    t_new = rows // N_IDX_HEADS
    n_pages = page_table.shape[1]
    assert n_pages % pp == 0
    past_len = n_pages * PAGE_SIZE
    lpad = past_len + PAGE_SIZE
    page_spec = lambda k: pl.BlockSpec((None, None, PAGE_SIZE, IDX_DIM),
                                       lambda b, p, tbl: (layer, tbl[b, p * pp + k], 0, 0))
    return pl.pallas_call(
        functools.partial(_dsa_select_paged_kernel, pp=pp, topk=topk, past_len=past_len, t_new=t_new),
        grid_spec=pltpu.PrefetchScalarGridSpec(
            num_scalar_prefetch=1,
            grid=(bsz, n_pages // pp),
            in_specs=[
                pl.BlockSpec((None, rows, IDX_DIM), lambda b, p, tbl: (b, 0, 0)),
                pl.BlockSpec((None, rows, 1), lambda b, p, tbl: (b, 0, 0)),
                *[page_spec(k) for k in range(pp)],
                pl.BlockSpec((None, PAGE_SIZE, IDX_DIM), lambda b, p, tbl: (b, 0, 0)),
            ],
            out_specs=pl.BlockSpec((None, t_new, lpad), lambda b, p, tbl: (b, 0, 0)),
            scratch_shapes=[pltpu.VMEM((t_new, lpad), jnp.int32)],
        ),
        out_shape=jax.ShapeDtypeStruct((bsz, t_new, lpad), BF16),
        compiler_params=_params("arbitrary", "arbitrary"),
        name="dsa_select_paged",
    )(page_table, qi_rows, wi_rows, *([kidx_pool] * pp), kidx_new)


def _col_vector(row_vec):
    n = row_vec.shape[1]
    eye = lax.broadcasted_iota(jnp.int32, (n, n), 0) == lax.broadcasted_iota(jnp.int32, (n, n), 1)
    return jnp.sum(jnp.where(eye, row_vec, 0.0), axis=1, keepdims=True)


def _paged_attn_kernel(tbl_ref, qbd_ref, *rest, pp):
    del tbl_ref
    k_refs, v_refs = rest[:pp], rest[pp:2 * pp]
    mt_ref, mnew_ref, knew_ref, vnew_ref, bias_ref, o_ref, m_sc, l_sc, acc_sc = rest[2 * pp:]
    p = pl.program_id(1)
    last = pl.num_programs(1) - 1

    @pl.when(p == 0)
    def _():
        m_sc[...] = jnp.full(m_sc.shape, MASKED, F32)
        l_sc[...] = jnp.zeros_like(l_sc)
        acc_sc[...] = jnp.zeros_like(acc_sc)

    n_kv = qbd_ref.shape[0] // HEAD_DIM

    def heads_on_lanes(ref):
        return jnp.concatenate([ref[pl.ds(g, PAGE_SIZE, stride=n_kv), :] for g in range(n_kv)], axis=1)

    def pages(kps, vps, masks, biases):
        scores = []
        for kp, mask, bias in zip(kps, masks, biases):
            s = jnp.dot(kp().astype(BF16), qbd_ref[...], preferred_element_type=F32)
            if bias is not None:
                s = s + bias
            scores.append(jnp.where(mask > 0, s, MASKED))
        m_old = m_sc[...]
        m_new = m_old
        for s in scores:
            m_new = jnp.maximum(m_new, jnp.max(s, axis=0, keepdims=True))
        alpha = jnp.exp2(m_old - m_new)
        l_new = alpha * l_sc[...]
        pv = None
        for s, vp in zip(scores, vps):
            pt = jnp.exp2(s - m_new)
            l_new = l_new + jnp.sum(pt, axis=0, keepdims=True)
            part = lax.dot_general(pt.astype(BF16), vp().astype(BF16), (((0,), (0,)), ((), ())),
                                   preferred_element_type=F32)
            pv = part if pv is None else pv + part
        l_sc[...] = l_new
        acc_sc[...] = acc_sc[...] * _col_vector(alpha) + pv
        m_sc[...] = m_new

    is_last = (p == last).astype(F32)
    pages([functools.partial(heads_on_lanes, r) for r in k_refs],
          [functools.partial(heads_on_lanes, r) for r in v_refs],
          [mt_ref[k * PAGE_SIZE:(k + 1) * PAGE_SIZE, :] for k in range(pp)],
          [None] * (pp - 1) + [bias_ref[0] * is_last])

    @pl.when(p == last)
    def _():
        pages([lambda: knew_ref[...]], [lambda: vnew_ref[...]], [mnew_ref[...]], [bias_ref[1]])
        o_ref[...] = acc_sc[...] / _col_vector(l_sc[...])


def _paged_attention(qbd, k_pool, v_pool, layer, page_table, mask_t, k_new, v_new, bias_t, *, pp):
    bsz, w, _ = qbd.shape
    n_pages = page_table.shape[1]
    assert n_pages % pp == 0
    n_layers, n_phys, _, n_kv, _ = k_pool.shape
    assert n_kv * HEAD_DIM == w
    as_rows = lambda pool: pool.reshape(n_layers, n_phys, PAGE_SIZE * n_kv, HEAD_DIM)
    k_pool, v_pool = as_rows(k_pool), as_rows(v_pool)
    page_spec = lambda k: pl.BlockSpec((None, None, PAGE_SIZE * n_kv, HEAD_DIM),
                                       lambda b, p, tbl: (layer, tbl[b, p * pp + k], 0, 0))
    per_b = lambda shp: pl.BlockSpec((None,) + shp, lambda b, p, tbl: (b, 0, 0))
    return pl.pallas_call(
        functools.partial(_paged_attn_kernel, pp=pp),
        grid_spec=pltpu.PrefetchScalarGridSpec(
            num_scalar_prefetch=1,
            grid=(bsz, n_pages // pp),
            in_specs=[
                per_b((w, LANES)),
                *[page_spec(k) for k in range(pp)],
                *[page_spec(k) for k in range(pp)],
                pl.BlockSpec((None, pp * PAGE_SIZE, LANES), lambda b, p, tbl: (b, p, 0)),
                pl.BlockSpec((None, PAGE_SIZE, LANES), lambda b, p, tbl: (b, n_pages, 0)),
                per_b((PAGE_SIZE, w)),
                per_b((PAGE_SIZE, w)),
                pl.BlockSpec((2, PAGE_SIZE, LANES), lambda b, p, tbl: (0, 0, 0)),
            ],
            out_specs=per_b((LANES, w)),
            scratch_shapes=[pltpu.VMEM((1, LANES), F32), pltpu.VMEM((1, LANES), F32), pltpu.VMEM((LANES, w), F32)],
        ),
        out_shape=jax.ShapeDtypeStruct((bsz, LANES, w), F32),
        compiler_params=_params("arbitrary", "arbitrary"),
        name="paged_attention",
    )(page_table, qbd, *([k_pool] * pp), *([v_pool] * pp), mask_t, mask_t, k_new, v_new, bias_t)


def _block_diag_queries(q, n_kv):
    bsz, t, _ = q.shape
    q4 = q.reshape(bsz, t, N_HEADS, HEAD_DIM) * QK_SCALE_LOG2
    onehot = (jnp.arange(N_HEADS)[:, None] // (N_HEADS // n_kv) == jnp.arange(n_kv)[None, :]).astype(q.dtype)
    qbd = q4[:, :, :, None, :] * onehot[None, None, :, :, None]
    return qbd.transpose(0, 3, 4, 2, 1).reshape(bsz, n_kv * HEAD_DIM, N_HEADS * t).astype(BF16)


def _own_head_lanes(out, n_kv, t):
    bsz = out.shape[0]
    out5 = out.reshape(bsz, N_HEADS, t, n_kv, HEAD_DIM)
    heads = jnp.arange(N_HEADS)
    picked = out5[:, heads, :, heads // (N_HEADS // n_kv), :]
    return picked.transpose(1, 2, 0, 3).reshape(bsz, t, N_HEADS * HEAD_DIM)


def _decode_bias(near_bias, tq, t):
    tail = near_bias[:, :t, tq - PAGE_SIZE:tq + PAGE_SIZE]
    return tail.transpose(2, 0, 1).reshape(2, PAGE_SIZE, N_HEADS * t)


def _pad_rows(a, rows):
    return jnp.pad(a, ((0, 0), (0, rows - a.shape[1]), (0, 0)))


def _new_token_mask(t):
    key = jnp.arange(PAGE_SIZE)[:, None]
    tok = jnp.tile(jnp.arange(t), N_HEADS)[None, :]
    return (key <= tok).astype(BF16)


ATTN_TQ = 256
SELECT_TQ = 256
SELECT_TK = 512
MOBA_HEADS_PER_STEP = 4
Q_W = N_HEADS * HEAD_DIM
KV_W = N_KV_B * HEAD_DIM
DSA_PROJ = Q_W + 2 * KV_W + N_IDX_HEADS * IDX_DIM + IDX_DIM + N_IDX_HEADS
DSA_PROJ_PAD = -(-DSA_PROJ // 512) * 512


def _dsa_split(proj):
    k = proj[..., Q_W:Q_W + KV_W]
    v = proj[..., Q_W + KV_W:Q_W + 2 * KV_W]
    o = Q_W + 2 * KV_W + N_IDX_HEADS * IDX_DIM
    return k, v, proj[..., o:o + IDX_DIM]


def _dsa_prompt(proj, near_bias):
    bsz, t, _ = proj.shape
    k, v, ki = _dsa_split(proj)
    mask = _dsa_select(proj, tq=min(SELECT_TQ, t), tk=min(SELECT_TK, t), topk=min(IDX_TOPK, t // 4))
    o = _attention(proj, Q_W, Q_W + KV_W, mask, near_bias, tq=ATTN_TQ, rep=N_HEADS // N_KV_B,
                   shared_kv=True, block_mask=False)
    return o, k.reshape(bsz, t, N_KV_B, HEAD_DIM), v.reshape(bsz, t, N_KV_B, HEAD_DIM), ki


def _dsa_sample(proj, k_pool, v_pool, kidx_pool, layer, page_table, near_bias):
    bsz, t, _ = proj.shape
    past_len = page_table.shape[1] * PAGE_SIZE
    k, v, ki = _dsa_split(proj)
    o_qi = Q_W + 2 * KV_W
    qi = proj[..., o_qi:o_qi + N_IDX_HEADS * IDX_DIM].reshape(bsz, t, N_IDX_HEADS, IDX_DIM)
    wi = proj[..., o_qi + N_IDX_HEADS * IDX_DIM + IDX_DIM:DSA_PROJ]
    qi_rows = qi.transpose(0, 2, 1, 3).reshape(bsz, N_IDX_HEADS * t, IDX_DIM)
    wi_rows = wi.transpose(0, 2, 1).reshape(bsz, N_IDX_HEADS * t, 1)
    sel = _dsa_select_paged(qi_rows, wi_rows, kidx_pool, layer, page_table, _pad_rows(ki, PAGE_SIZE),
                            pp=8, topk=min(IDX_TOPK, (past_len + t) // 4))
    mask_t = jnp.tile(sel.transpose(0, 2, 1), (1, 1, N_HEADS))
    out = _paged_attention(
        _block_diag_queries(proj[..., :Q_W], N_KV_B), k_pool, v_pool, layer, page_table,
        mask_t, _pad_rows(k, PAGE_SIZE), _pad_rows(v, PAGE_SIZE), _decode_bias(near_bias, ATTN_TQ, t), pp=8)
    o = _own_head_lanes(out, N_KV_B, t).astype(BF16)
    return o, k.reshape(bsz, t, N_KV_B, HEAD_DIM), v.reshape(bsz, t, N_KV_B, HEAD_DIM), ki


def _moba_prompt(proj, near_bias):
    bsz, t, _ = proj.shape
    k = proj[..., Q_W:2 * Q_W]
    v = proj[..., 2 * Q_W:3 * Q_W]
    k_heads = k.reshape(bsz, t, N_HEADS, HEAD_DIM)
    n_pages = t // PAGE_SIZE
    pages = jnp.arange(bsz * n_pages, dtype=jnp.int32).reshape(bsz, n_pages)
    means = _block_means(k_heads.reshape(1, bsz * n_pages, PAGE_SIZE, N_HEADS, HEAD_DIM), 0, pages)
    sel = _moba_select(proj, _pad_rows(means, LANES), tq=ATTN_TQ, past_len=0)
    o = _attention(proj, Q_W, 2 * Q_W, sel, near_bias, tq=ATTN_TQ, rep=MOBA_HEADS_PER_STEP,
                   shared_kv=False, block_mask=True)
    return o, k_heads, v.reshape(bsz, t, N_HEADS, HEAD_DIM)


def _moba_sample(proj, k_pool, v_pool, layer, page_table, near_bias):
    bsz, t, _ = proj.shape
    n_pages = page_table.shape[1]
    past_len = n_pages * PAGE_SIZE
    k = proj[..., Q_W:2 * Q_W]
    v = proj[..., 2 * Q_W:3 * Q_W]
    means = _block_means(k_pool, layer, page_table)
    sel = _moba_select(proj, _pad_rows(means, LANES), tq=t, past_len=past_len)
    past = jnp.repeat(sel[..., :past_len // MOBA_BLOCK], MOBA_BLOCK, axis=-1)
    past_t = past.transpose(0, 3, 1, 2).reshape(bsz, past_len, N_HEADS * t)
    new_t = jnp.broadcast_to(_new_token_mask(t), (bsz, PAGE_SIZE, N_HEADS * t))
    out = _paged_attention(
        _block_diag_queries(proj[..., :Q_W], N_HEADS), k_pool, v_pool, layer, page_table,
        jnp.concatenate([past_t, new_t], axis=1), _pad_rows(k, PAGE_SIZE), _pad_rows(v, PAGE_SIZE),
        _decode_bias(near_bias, ATTN_TQ, t), pp=4)
    o = _own_head_lanes(out, N_HEADS, t).astype(BF16)
    return o, k.reshape(bsz, t, N_HEADS, HEAD_DIM), v.reshape(bsz, t, N_HEADS, HEAD_DIM)


N_MIXERS = 3
PROMPT_TM = 1024
MATMUL_TN = 512
LRU_TT = 256
MOE_TM = 512
ROUTE_W = LANES


def kernel(x_prompt, x_sample, state_lru_h, state_lru_conv, cache_dsa_k, cache_dsa_v, cache_dsa_kidx,
           cache_moba_k, cache_moba_v, page_table, c_prompt, c_sample, rel_bias, norm_mix_g, norm_ffn_g,
           final_norm_g, ada_w, ada_b, lru_w_in, lru_conv_w, lru_conv_b, lru_w_rgate, lru_b_rgate,
           lru_w_igate, lru_b_igate, lru_lambda, lru_w_out, dsa_w_in, dsa_w_out, moba_w_in, moba_w_out,
           moe_w_group, moe_w_expert, moe_w1, moe_w3, moe_w2):
    bp, tp, d = x_prompt.shape
    bs, ts, _ = x_sample.shape
    depth = ada_w.shape[0]
    ns = bs * ts
    c_rnn = lru_w_out.shape[1]

    c_all = jnp.concatenate([c_prompt, c_sample], axis=0)
    c_rows = -(-c_all.shape[0] // SUBLANES) * SUBLANES
    mods = _ada(jnp.pad(c_all, ((0, c_rows - c_all.shape[0]), (0, 0))), ada_w, ada_b)
    mods = mods.reshape(depth, c_rows, 6, d)
    near_bias = _near_bias(rel_bias, ATTN_TQ)

    xp = x_prompt
    xs = x_sample.reshape(1, ns, d)
    outs = {n: [] for n in ("lru_h_p", "lru_c_p", "lru_h_s", "lru_c_s", "dsa_k_p", "dsa_v_p", "dsa_i_p",
                            "dsa_k_s", "dsa_v_s", "dsa_i_s", "moba_k_p", "moba_v_p", "moba_k_s", "moba_v_s")}
    for i in range(depth):
        kind, j = i % N_MIXERS, i // N_MIXERS
        mod_p = [mods[i, :bp, m][:, None, :] for m in range(6)]
        mod_s = [jnp.repeat(mods[i, bp:bp + bs, m], ts, axis=0)[None] for m in range(6)]
        shp1, scp1, gtp1, shp2, scp2, gtp2 = mod_p
        shs1, scs1, gts1, shs2, scs2, gts2 = mod_s
        mix_in = lambda w, tn=MATMUL_TN: (
            _norm_matmul(xp, norm_mix_g[i], shp1, scp1, w, tm=min(PROMPT_TM, tp), tn=tn),
            _norm_matmul(xs, norm_mix_g[i], shs1, scs1, w, tm=ns, tn=tn))
        if kind == 0:
            up, us = mix_in(lru_w_in[j])
            lru_params = (lru_conv_w[j], lru_conv_b[j], lru_w_rgate[j], lru_b_rgate[j], lru_w_igate[j],
                          lru_b_igate[j], lru_lambda[j])
            yp, hp_state, cp = _lru_core(up, jnp.zeros((bp, CONV_W - 1, c_rnn), F32), jnp.zeros((bp, c_rnn), F32),
                                         *lru_params, tt=LRU_TT)
            ys, hs_state, cs = _lru_core(us.reshape(bs, ts, 2 * c_rnn), state_lru_conv[j], state_lru_h[j],
                                         *lru_params, tt=ts)
            ys = ys.reshape(1, ns, c_rnn)
            w_out = lru_w_out[j]
            outs["lru_h_p"].append(hp_state)
            outs["lru_c_p"].append(cp)
            outs["lru_h_s"].append(hs_state)
            outs["lru_c_s"].append(cs)
        elif kind == 1:
            w_in = jnp.pad(dsa_w_in[j], ((0, 0), (0, DSA_PROJ_PAD - DSA_PROJ)))
            pp_, ps_ = mix_in(w_in)
            yp, kp, vp, ip = _dsa_prompt(pp_, near_bias)
            ys, k_s, v_s, i_s = _dsa_sample(ps_.reshape(bs, ts, DSA_PROJ_PAD), cache_dsa_k, cache_dsa_v,
                                            cache_dsa_kidx, j, page_table, near_bias)
            ys = ys.reshape(1, ns, Q_W)
            w_out = dsa_w_out[j]
            for n, val in zip(("dsa_k_p", "dsa_v_p", "dsa_i_p", "dsa_k_s", "dsa_v_s", "dsa_i_s"),
                              (kp, vp, ip, k_s, v_s, i_s)):
                outs[n].append(val)
        else:
            pp_, ps_ = mix_in(moba_w_in[j])
            yp, kp, vp = _moba_prompt(pp_, near_bias)
            ys, k_s, v_s = _moba_sample(ps_.reshape(bs, ts, 3 * Q_W), cache_moba_k, cache_moba_v, j,
                                        page_table, near_bias)
            ys = ys.reshape(1, ns, Q_W)
            w_out = moba_w_out[j]
            for n, val in zip(("moba_k_p", "moba_v_p", "moba_k_s", "moba_v_s"), (kp, vp, k_s, v_s)):
                outs[n].append(val)
        xp = _matmul_residual(yp, w_out, xp, gtp1, tm=min(PROMPT_TM, tp), tn=min(MATMUL_TN, d))
        xs = _matmul_residual(ys, w_out, xs, gts1, tm=ns, tn=min(MATMUL_TN, d))

        w_route = jnp.pad(jnp.concatenate([moe_w_group[i], moe_w_expert[i]], axis=1),
                          ((0, 0), (0, ROUTE_W - N_GROUPS - N_EXPERTS)))
        w1, w3, w2 = moe_w1[i].astype(BF16), moe_w3[i].astype(BF16), moe_w2[i].astype(BF16)
        lg_p, h_p = _norm_matmul(xp, norm_ffn_g[i], shp2, scp2, w_route, tm=min(PROMPT_TM, tp), tn=ROUTE_W,
                                 emit_h=True, highest=True, h_dtype=F32)
        moe_tm = min(MOE_TM, tp)
        info, counts = _route_tokens(lg_p.reshape(bp * tp, ROUTE_W), tm=moe_tm)
        moe_p = _moe_grouped(h_p.reshape(bp * tp, d), info, *_group_layout(info, counts, moe_tm), w1, w3, w2,
                             tm=moe_tm)
        xp = _gated_add(xp, moe_p, gtp2, tm=moe_tm)
        lg_s, h_s = _norm_matmul(xs, norm_ffn_g[i], shs2, scs2, w_route, tm=ns, tn=ROUTE_W,
                                 emit_h=True, highest=True)
        xs = _moe_dense(h_s, lg_s, w1, w3, w2, xs, gts2, tm=ns)

    y_prompt = _rmsnorm(xp, final_norm_g, tm=min(MOE_TM, tp))
    y_sample = _rmsnorm(xs, final_norm_g, tm=ns).reshape(bs, ts, d)
    stack = lambda n: jnp.stack(outs[n])
    return (y_prompt, y_sample, stack("lru_h_p"), stack("lru_c_p"), stack("lru_h_s"), stack("lru_c_s"),
            stack("dsa_k_p"), stack("dsa_v_p"), stack("dsa_i_p"), stack("dsa_k_s"), stack("dsa_v_s"),
            stack("dsa_i_s"), stack("moba_k_p"), stack("moba_v_p"), stack("moba_k_s"), stack("moba_v_s"))
```

```python
import functools
import math

import numpy as np
import jax
import jax.numpy as jnp
from jax import lax
from jax.experimental import pallas as pl
from jax.experimental.pallas import tpu as pltpu

F32 = jnp.float32
BF16 = jnp.bfloat16

LANES = 128
SUBLANES = 8
VMEM_LIMIT_BYTES = 56 * 1024 * 1024

HEAD_DIM = 128
N_HEADS = 16
N_KV_B = 4
N_IDX_HEADS = 16
IDX_DIM = 64
IDX_TOPK = 256
IDX_SCALE = (IDX_DIM * N_IDX_HEADS) ** -0.5
PAGE_SIZE = 128
MOBA_BLOCK = 256
MOBA_TOPK = 3
N_BUCKETS = 32
MAX_DISTANCE = 128
N_RNN_BLOCKS = 16
CONV_W = 4
LRU_C = 8.0
N_GROUPS = 4
EXPERTS_PER_GROUP = 4
N_EXPERTS = N_GROUPS * EXPERTS_PER_GROUP
EPS = 1e-6
ATTN_SCALE = HEAD_DIM ** -0.5
LOG2E = math.log2(math.e)
QK_SCALE_LOG2 = ATTN_SCALE * LOG2E
NEG_INF = float("-inf")


def _params(*sem):
    return pltpu.CompilerParams(dimension_semantics=sem, vmem_limit_bytes=VMEM_LIMIT_BYTES)


def _bdot(a, b):
    return jnp.dot(a.astype(BF16), b.astype(BF16), preferred_element_type=F32)


def _bdot_nt(a, b):
    return lax.dot_general(a.astype(BF16), b.astype(BF16), (((1,), (1,)), ((), ())),
                           preferred_element_type=F32)


def _ada_kernel(c_ref, w_ref, b_ref, o_ref):
    c = c_ref[...]
    o_ref[...] = _bdot(c * jax.nn.sigmoid(c), w_ref[...]) + b_ref[...]


def _ada(c, ada_w, ada_b):
    depth, d, n = ada_w.shape
    rows = c.shape[0]
    tn = next(w for w in (1024, 512, 256, LANES) if n % w == 0)
    return pl.pallas_call(
        _ada_kernel,
        grid=(depth, n // tn),
        in_specs=[
            pl.BlockSpec((rows, d), lambda l, j: (0, 0)),
            pl.BlockSpec((None, d, tn), lambda l, j: (l, 0, j)),
            pl.BlockSpec((None, 1, tn), lambda l, j: (l, 0, j)),
        ],
        out_specs=pl.BlockSpec((None, rows, tn), lambda l, j: (l, 0, j)),
        out_shape=jax.ShapeDtypeStruct((depth, rows, n), F32),
        compiler_params=_params("arbitrary", "arbitrary"),
        name="ada_mod",
    )(c, ada_w, ada_b.reshape(depth, 1, n))


def _norm_mod(x, g, shift, scale):
    ms = jnp.mean(x * x, axis=-1, keepdims=True)
    h = x * lax.rsqrt(ms + EPS) * g
    return h * (1.0 + scale) + shift


def _norm_mm_kernel(x_ref, g_ref, sh_ref, sc_ref, w_ref, *refs, emit_h, highest, residual):
    refs = list(refs)
    y_ref, gy_ref = (refs.pop(0), refs.pop(0)) if residual else (None, None)
    o_ref = refs.pop(0)
    ho_ref = refs.pop(0) if emit_h else None
    xo_ref = refs.pop(0) if residual else None
    h_ref, = refs

    @pl.when(pl.program_id(2) == 0)
    def _():
        x = x_ref[...]
        if residual:
            x = x + gy_ref[...] * y_ref[...]
            xo_ref[...] = x
        h = _norm_mod(x, g_ref[...], sh_ref[...], sc_ref[...])
        h_ref[...] = h.astype(h_ref.dtype)
        if ho_ref is not None:
            ho_ref[...] = h.astype(ho_ref.dtype)

    if highest:
        o_ref[...] = jnp.dot(h_ref[...], w_ref[...], precision=lax.Precision.HIGHEST,
                             preferred_element_type=F32)
    else:
        o_ref[...] = _bdot(h_ref[...], w_ref[...])


def _norm_matmul(x, g, shift, scale, w, *, tm, tn, emit_h=False, highest=False, h_dtype=BF16, residual=None):
    bsz, t, d = x.shape
    n = w.shape[1]
    r = shift.shape[1]
    assert t % tm == 0 and n % tn == 0 and (r == 1 or r == tm == t)
    per_b = t // tm
    out_shape = [jax.ShapeDtypeStruct((bsz, t, n), F32)]
    out_specs = [pl.BlockSpec((None, tm, tn), lambda b, i, j: (b, i, j))]
    row_tile = pl.BlockSpec((None, tm, d), lambda b, i, j: (b, i, 0))
    if emit_h:
        out_shape.append(jax.ShapeDtypeStruct((bsz, t, d), h_dtype))
        out_specs.append(row_tile)
    extra_in, extra_specs = [], []
    if residual is not None:
        extra_in = list(residual)
        extra_specs = [pl.BlockSpec((tm, d), lambda b, i, j: (b * per_b + i, 0)),
                       pl.BlockSpec((None, 1, d), lambda b, i, j: (b, 0, 0))]
        out_shape.append(jax.ShapeDtypeStruct((bsz, t, d), F32))
        out_specs.append(row_tile)
    mod_map = (lambda b, i, j: (b, 0, 0)) if r == 1 else (lambda b, i, j: (b, i, 0))
    res = pl.pallas_call(
        functools.partial(_norm_mm_kernel, emit_h=emit_h, highest=highest, residual=residual is not None),
        grid=(bsz, t // tm, n // tn),
        in_specs=[
            row_tile,
            pl.BlockSpec((1, d), lambda b, i, j: (0, 0)),
            pl.BlockSpec((None, r, d), mod_map),
            pl.BlockSpec((None, r, d), mod_map),
            pl.BlockSpec((d, tn), lambda b, i, j: (0, j)),
            *extra_specs,
        ],
        out_specs=out_specs,
        out_shape=out_shape,
        scratch_shapes=[pltpu.VMEM((tm, d), F32 if highest else BF16)],
        compiler_params=_params("arbitrary", "arbitrary", "arbitrary"),
        name="norm_matmul",
    )(x, g.reshape(1, d), shift, scale, w, *extra_in)
    return res if len(res) > 1 else res[0]


def _mm_res_kernel(a_ref, w_ref, r_ref, g_ref, o_ref):
    o_ref[...] = r_ref[...] + g_ref[...] * _bdot(a_ref[...], w_ref[...])


def _matmul_residual(a, w, res, gate, *, tm, tn):
    bsz, t, k = a.shape
    n = w.shape[1]
    r = gate.shape[1]
    assert t % tm == 0 and n % tn == 0 and (r == 1 or r == tm == t)
    gate_map = (lambda b, i, j: (b, 0, j)) if r == 1 else (lambda b, i, j: (b, i, j))
    return pl.pallas_call(
        _mm_res_kernel,
        grid=(bsz, t // tm, n // tn),
        in_specs=[
            pl.BlockSpec((None, tm, k), lambda b, i, j: (b, i, 0)),
            pl.BlockSpec((k, tn), lambda b, i, j: (0, j)),
            pl.BlockSpec((None, tm, tn), lambda b, i, j: (b, i, j)),
            pl.BlockSpec((None, r, tn), gate_map),
        ],
        out_specs=pl.BlockSpec((None, tm, tn), lambda b, i, j: (b, i, j)),
        out_shape=jax.ShapeDtypeStruct((bsz, t, n), F32),
        compiler_params=_params("arbitrary", "arbitrary", "arbitrary"),
        name="matmul_residual",
    )(a, w, res, gate)


def _rmsnorm_kernel(x_ref, g_ref, *refs, residual):
    x = x_ref[...]
    if residual:
        y_ref, gy_ref, o_ref = refs
        x = x + gy_ref[...] * y_ref[...]
    else:
        o_ref, = refs
    ms = jnp.mean(x * x, axis=-1, keepdims=True)
    o_ref[...] = x * lax.rsqrt(ms + EPS) * g_ref[...]


def _rmsnorm(x, g, *, tm, residual=None):
    bsz, t, d = x.shape
    per_b = t // tm
    extra_in, extra_specs = [], []
    if residual is not None:
        extra_in = list(residual)
        extra_specs = [pl.BlockSpec((tm, d), lambda b, i: (b * per_b + i, 0)),
                       pl.BlockSpec((None, 1, d), lambda b, i: (b, 0, 0))]
    return pl.pallas_call(
        functools.partial(_rmsnorm_kernel, residual=residual is not None),
        grid=(bsz, t // tm),
        in_specs=[pl.BlockSpec((None, tm, d), lambda b, i: (b, i, 0)),
                  pl.BlockSpec((1, d), lambda b, i: (0, 0)), *extra_specs],
        out_specs=pl.BlockSpec((None, tm, d), lambda b, i: (b, i, 0)),
        out_shape=jax.ShapeDtypeStruct((bsz, t, d), F32),
        compiler_params=_params("arbitrary", "arbitrary"),
        name="final_rmsnorm",
    )(x, g.reshape(1, d), *extra_in)


GATE_CHUNK = 640


def _log_sigmoid(x):
    return jnp.minimum(x, 0.0) - jnp.log1p(jnp.exp(-jnp.abs(x)))


def _lru_kernel(gb_ref, xb_ref, cs_ref, h0_ref, cw_ref, cb_ref, wr_ref, br_ref, wi_ref, bi_ref, lam_ref,
                y_ref, hl_ref, nc_ref, ext_ref, a_ref, b_ref, hs_ref, carry_ref, *, tt):
    c_all = xb_ref.shape[-1]

    @pl.when(pl.program_id(1) == 0)
    def _():
        ext_ref[0:SUBLANES, :] = cs_ref[...]
        carry_ref[...] = h0_ref[...]

    xb = xb_ref[...]
    ext_ref[SUBLANES:SUBLANES + tt, :] = xb
    cw = cw_ref[...]
    xc = cb_ref[...] + xb * cw[CONV_W - 1:CONV_W, :]
    for j in range(1, CONV_W):
        xc = xc + ext_ref[pl.ds(SUBLANES - j, tt), :] * cw[CONV_W - 1 - j:CONV_W - j, :]
    tail = ext_ref[tt:tt + SUBLANES, :]
    ext_ref[0:SUBLANES, :] = tail
    nc_ref[...] = tail

    xcb = xc.astype(BF16)
    r_parts, i_parts = [], []
    for c in range(c_all // GATE_CHUNK):
        sl = slice(c * GATE_CHUNK, (c + 1) * GATE_CHUNK)
        r_parts.append(jnp.dot(xcb[:, sl], wr_ref[c], preferred_element_type=F32))
        i_parts.append(jnp.dot(xcb[:, sl], wi_ref[c], preferred_element_type=F32))
    r = jax.nn.sigmoid(jnp.concatenate(r_parts, axis=1) + br_ref[...])
    ig = jax.nn.sigmoid(jnp.concatenate(i_parts, axis=1) + bi_ref[...])
    log_a = LRU_C * r * _log_sigmoid(lam_ref[...])
    th = jnp.tanh(log_a)
    one_minus_a2 = -2.0 * th / (1.0 - th)
    a_ref[...] = jnp.exp(log_a)
    b_ref[...] = jnp.sqrt(jnp.maximum(one_minus_a2, 0.0)) * (ig * xc)

    row = lax.broadcasted_iota(jnp.int32, (SUBLANES, GATE_CHUNK), 0)
    for c in range(c_all // GATE_CHUNK):
        sl = slice(c * GATE_CHUNK, (c + 1) * GATE_CHUNK)

        def body(grp, carry, sl=sl):
            r8 = pl.multiple_of(grp * SUBLANES, SUBLANES)
            av = a_ref[pl.ds(r8, SUBLANES), sl]
            bv = b_ref[pl.ds(r8, SUBLANES), sl]
            for s in (1, 2, 4):
                m = row >= s
                a_sh = jnp.where(m, pltpu.roll(av, s, axis=0), 1.0)
                b_sh = jnp.where(m, pltpu.roll(bv, s, axis=0), 0.0)
                bv = av * b_sh + bv
                av = av * a_sh
            h = av * carry + bv
            hs_ref[pl.ds(r8, SUBLANES), sl] = h
            return h[SUBLANES - 1:SUBLANES, :]

        carry_ref[:, sl] = lax.fori_loop(0, tt // SUBLANES, body, carry_ref[:, sl])

    hl_ref[...] = carry_ref[...]
    y_ref[...] = (jax.nn.gelu(gb_ref[...]) * hs_ref[...]).astype(y_ref.dtype)


def _lru_core(u, conv_state, h0, conv_w, conv_b, w_r, b_r, w_i, b_i, lam, *, tt):
    bsz, t, c2 = u.shape
    c = c2 // 2
    blk = c // N_RNN_BLOCKS
    per = GATE_CHUNK // blk
    nchunk = c // GATE_CHUNK

    def block_diag(w):
        wc = w.reshape(nchunk, per, blk, blk).astype(BF16)
        eye = jnp.eye(per, dtype=BF16)
        return jnp.einsum("cpij,pq->cpiqj", wc, eye).reshape(nchunk, GATE_CHUNK, GATE_CHUNK)

    cs_pad = jnp.pad(conv_state, ((0, 0), (SUBLANES - (CONV_W - 1), 0), (0, 0)))
    row = lambda v: v.reshape(1, c)
    full = lambda shp: pl.BlockSpec(shp, lambda b, i: (0,) * len(shp))
    y, hl, nc = pl.pallas_call(
        functools.partial(_lru_kernel, tt=tt),
        grid=(bsz, t // tt),
        in_specs=[
            pl.BlockSpec((None, tt, c), lambda b, i: (b, i, 0)),
            pl.BlockSpec((None, tt, c), lambda b, i: (b, i, 1)),
            pl.BlockSpec((None, SUBLANES, c), lambda b, i: (b, 0, 0)),
            pl.BlockSpec((None, 1, c), lambda b, i: (b, 0, 0)),
            full((CONV_W, c)), full((1, c)),
            full((nchunk, GATE_CHUNK, GATE_CHUNK)), full((1, c)),
            full((nchunk, GATE_CHUNK, GATE_CHUNK)), full((1, c)),
            full((1, c)),
        ],
        out_specs=[
            pl.BlockSpec((None, tt, c), lambda b, i: (b, i, 0)),
            pl.BlockSpec((None, 1, c), lambda b, i: (b, 0, 0)),
            pl.BlockSpec((None, SUBLANES, c), lambda b, i: (b, 0, 0)),
        ],
        out_shape=[
            jax.ShapeDtypeStruct((bsz, t, c), BF16),
            jax.ShapeDtypeStruct((bsz, 1, c), F32),
            jax.ShapeDtypeStruct((bsz, SUBLANES, c), F32),
        ],
        scratch_shapes=[
            pltpu.VMEM((tt + SUBLANES, c), F32),
            pltpu.VMEM((tt, c), F32),
            pltpu.VMEM((tt, c), F32),
            pltpu.VMEM((tt, c), F32),
            pltpu.VMEM((1, c), F32),
        ],
        compiler_params=_params("arbitrary", "arbitrary"),
        name="lru_core",
    )(u, u, cs_pad, h0.reshape(bsz, 1, c), conv_w, row(conv_b), block_diag(w_r), row(b_r),
      block_diag(w_i), row(b_i), row(lam))
    return y, hl.reshape(bsz, c), nc[:, SUBLANES - (CONV_W - 1):, :]


def _route_gates(lg):
    return _route(lg)[0]


def _route(lg):
    col = lax.broadcasted_iota(jnp.int32, lg.shape, 1)
    big = jnp.int32(LANES)
    is_g = col < N_GROUPS
    gl = jnp.where(is_g, lg, NEG_INF)
    gmax = jnp.max(gl, axis=1, keepdims=True)
    g_sel = jnp.min(jnp.where(gl == gmax, col, big), axis=1, keepdims=True)
    g_w = 1.0 / jnp.sum(jnp.exp(gl - gmax), axis=1, keepdims=True)
    eid = col - N_GROUPS
    in_grp = (eid >= 0) & (eid < N_EXPERTS) & ((eid // EXPERTS_PER_GROUP) == g_sel)
    e_in = jnp.where(in_grp, lg, NEG_INF)
    top1 = jnp.max(e_in, axis=1, keepdims=True)
    idx1 = jnp.min(jnp.where((e_in == top1) & in_grp, col, big), axis=1, keepdims=True)
    rest = in_grp & (col != idx1)
    e2 = jnp.where(rest, lg, NEG_INF)
    top2 = jnp.max(e2, axis=1, keepdims=True)
    idx2 = jnp.min(jnp.where((e2 == top2) & rest, col, big), axis=1, keepdims=True)
    z = jnp.exp(top2 - top1)
    w_first = g_w / (1.0 + z)
    w_second = g_w * z / (1.0 + z)
    return jnp.where(col == idx1, w_first, 0.0) + jnp.where(col == idx2, w_second, 0.0), g_sel


def _moe_dense_kernel(h_ref, lg_ref, w1_ref, w3_ref, w2_ref, x_ref, g_ref, o_ref, gates_ref, acc_ref):
    e = pl.program_id(2)

    @pl.when(e == 0)
    def _():
        gates_ref[...] = _route_gates(lg_ref[...])
        acc_ref[...] = jnp.zeros_like(acc_ref)

    gates = gates_ref[...]
    col = lax.broadcasted_iota(jnp.int32, gates.shape, 1)
    ge = jnp.sum(jnp.where(col == e + N_GROUPS, gates, 0.0), axis=1, keepdims=True)
    h = h_ref[...]
    a1 = jnp.dot(h, w1_ref[...], preferred_element_type=F32)
    a3 = jnp.dot(h, w3_ref[...], preferred_element_type=F32)
    hidden = (a1 * jax.nn.sigmoid(a1)) * a3 * ge
    acc_ref[...] += jnp.dot(hidden.astype(BF16), w2_ref[...], preferred_element_type=F32)

    @pl.when(e == pl.num_programs(2) - 1)
    def _():
        o_ref[...] = x_ref[...] + g_ref[...] * acc_ref[...]


def _moe_dense(h, logits, w1, w3, w2, x, gate, *, tm):
    bsz, t, d = x.shape
    ne, _, f = w1.shape
    r = gate.shape[1]
    gate_map = (lambda b, i, e: (b, 0, 0)) if r == 1 else (lambda b, i, e: (b, i, 0))
    tok = lambda b, i, e: (b, i, 0)
    return pl.pallas_call(
        _moe_dense_kernel,
        grid=(bsz, t // tm, ne),
        in_specs=[
            pl.BlockSpec((None, tm, d), tok),
            pl.BlockSpec((None, tm, LANES), tok),
            pl.BlockSpec((None, d, f), lambda b, i, e: (e, 0, 0)),
            pl.BlockSpec((None, d, f), lambda b, i, e: (e, 0, 0)),
            pl.BlockSpec((None, f, d), lambda b, i, e: (e, 0, 0)),
            pl.BlockSpec((None, tm, d), tok),
            pl.BlockSpec((None, r, d), gate_map),
        ],
        out_specs=pl.BlockSpec((None, tm, d), tok),
        out_shape=jax.ShapeDtypeStruct((bsz, t, d), F32),
        scratch_shapes=[pltpu.VMEM((tm, LANES), F32), pltpu.VMEM((tm, d), F32)],
        compiler_params=_params("arbitrary", "arbitrary", "arbitrary"),
        name="moe_dense",
    )(h, logits, w1, w3, w2, x, gate)


INFO_GROUP_LANE = 0
INFO_RANK_LANE = 1


def _route_kernel(lg_ref, info_ref, cnt_ref, carry_ref):
    @pl.when(pl.program_id(0) == 0)
    def _():
        carry_ref[...] = jnp.zeros_like(carry_ref)

    gates, g_sel = _route(lg_ref[...])
    tm = gates.shape[0]
    col = lax.broadcasted_iota(jnp.int32, gates.shape, 1)
    onehot = (col == g_sel).astype(F32)
    before = (lax.broadcasted_iota(jnp.int32, (tm, tm), 0) > lax.broadcasted_iota(jnp.int32, (tm, tm), 1))
    earlier = jnp.dot(before.astype(BF16), onehot.astype(BF16), preferred_element_type=F32) + carry_ref[...]
    rank = jnp.sum(jnp.where(col == g_sel, earlier, 0.0), axis=1, keepdims=True)
    carry_ref[...] += jnp.sum(onehot, axis=0, keepdims=True)
    info_ref[...] = (gates + jnp.where(col == INFO_GROUP_LANE, g_sel.astype(F32), 0.0)
                     + jnp.where(col == INFO_RANK_LANE, rank, 0.0))
    cnt_ref[...] = carry_ref[...]


def _route_tokens(logits, *, tm):
    n = logits.shape[0]
    assert n % tm == 0
    return pl.pallas_call(
        _route_kernel,
        grid=(n // tm,),
        in_specs=[pl.BlockSpec((tm, LANES), lambda i: (i, 0))],
        out_specs=[pl.BlockSpec((tm, LANES), lambda i: (i, 0)), pl.BlockSpec((1, LANES), lambda i: (0, 0))],
        out_shape=[jax.ShapeDtypeStruct((n, LANES), F32), jax.ShapeDtypeStruct((1, LANES), F32)],
        scratch_shapes=[pltpu.VMEM((1, LANES), F32)],
        compiler_params=_params("arbitrary"),
        name="moe_route",
    )(logits)


def _group_layout(info, counts, tm):
    n = info.shape[0]
    g_sel = info[:, INFO_GROUP_LANE].astype(jnp.int32)
    rank = info[:, INFO_RANK_LANE].astype(jnp.int32)
    cnt = counts[0, :N_GROUPS].astype(jnp.int32)
    padded = -(-cnt // tm) * tm
    ends = jnp.cumsum(padded)
    starts = ends - padded
    n_slots = n + N_GROUPS * tm
    src = _slot_tokens(starts[g_sel] + rank, n_slots)
    tile_start = jnp.arange(n_slots // tm, dtype=jnp.int32) * tm
    tile_group = jnp.minimum(jnp.sum((tile_start[:, None] >= ends[None, :]).astype(jnp.int32), axis=1), N_GROUPS - 1)
    slot_group = jnp.repeat(tile_group, tm)
    is_pad = jnp.arange(n_slots, dtype=jnp.int32) - starts[slot_group] >= cnt[slot_group]
    pad_row = n + tm + jnp.cumsum(is_pad.astype(jnp.int32)) - 1
    dst = jnp.concatenate([n + jnp.arange(tm, dtype=jnp.int32), jnp.where(is_pad, pad_row, src)])
    return src, dst, tile_group, (ends[-1] // tm).reshape(1)


SLOT_UNROLL = 8


def _slot_tokens_kernel(slot_ref, src_ref):
    n, n_slots = slot_ref.shape[0], src_ref.shape[0]

    def clear(c, carry):
        for k in range(SLOT_UNROLL):
            src_ref[c * SLOT_UNROLL + k] = 0
        return carry

    def place(c, carry):
        for k in range(SLOT_UNROLL):
            t = c * SLOT_UNROLL + k
            src_ref[slot_ref[t]] = t
        return carry

    lax.fori_loop(0, n_slots // SLOT_UNROLL, clear, 0)
    lax.fori_loop(0, n // SLOT_UNROLL, place, 0)


def _slot_tokens(slot, n_slots):
    assert slot.shape[0] % SLOT_UNROLL == 0 and n_slots % SLOT_UNROLL == 0
    return pl.pallas_call(
        _slot_tokens_kernel,
        in_specs=[pl.BlockSpec(memory_space=pltpu.SMEM)],
        out_specs=pl.BlockSpec(memory_space=pltpu.SMEM),
        out_shape=jax.ShapeDtypeStruct((n_slots,), jnp.int32),
        name="moe_slot_tokens",
    )(slot)


def _moe_group_kernel(src_ref, dst_ref, tg_ref, used_ref, h_hbm, info_hbm, w1_ref, w3_ref, w2_ref, zeros_hbm,
                      out_hbm, xbuf, gbuf, xb, gates, acc, obuf, sem_x, sem_g, sem_o, *, tm):
    del zeros_hbm
    i = pl.program_id(0)
    e = pl.program_id(1)
    used = used_ref[0]
    live = i < used
    chunk = tm // EXPERTS_PER_GROUP
    first = e * chunk
    last_step = e == EXPERTS_PER_GROUP - 1

    def gather_row(tile, r):
        tok = src_ref[tile * tm + r]
        pltpu.make_async_copy(h_hbm.at[pl.ds(tok, 1), :], xbuf.at[pl.ds(r, 1), :], sem_x).start()
        pltpu.make_async_copy(info_hbm.at[pl.ds(tok, 1), :], gbuf.at[pl.ds(r, 1), :], sem_g).start()

    def scatter_chunk(tile):
        slot = (tile + 2) % 2
        for k in range(chunk):
            row = dst_ref[(tile + 1) * tm + first + k]
            pltpu.make_async_copy(obuf.at[slot, pl.ds(first + k, 1), :], out_hbm.at[pl.ds(row, 1), :],
                                  sem_o.at[slot]).start()

    def wait_scatter(slot):
        pltpu.make_async_copy(obuf.at[slot], out_hbm.at[pl.ds(0, tm), :], sem_o.at[slot]).wait()

    @pl.when((i == 0) & (e == 0))
    def _():
        obuf[...] = jnp.zeros_like(obuf)

        def body(r, carry):
            gather_row(0, r)
            return carry
        lax.fori_loop(0, tm, body, 0)

    @pl.when((e == 0) & (i <= used))
    def _():
        pltpu.make_async_copy(h_hbm.at[pl.ds(0, tm), :], xbuf, sem_x).wait()
        pltpu.make_async_copy(info_hbm.at[pl.ds(0, tm), :], gbuf, sem_g).wait()

    @pl.when((e == 0) & live)
    def _():
        xb[...] = xbuf[...].astype(BF16)
        gates[...] = gbuf[...]
        acc[...] = jnp.zeros_like(acc)

    @pl.when(live)
    def _():
        lane = lax.broadcasted_iota(jnp.int32, (tm, LANES), 1)
        gate_lane = N_GROUPS + EXPERTS_PER_GROUP * tg_ref[i] + e
        ge = jnp.sum(jnp.where(lane == gate_lane, gates[...], 0.0), axis=1, keepdims=True)
        x = xb[...]
        a1 = jnp.dot(x, w1_ref[...], preferred_element_type=F32)
        a3 = jnp.dot(x, w3_ref[...], preferred_element_type=F32)
        hidden = (a1 * jax.nn.sigmoid(a1)) * a3 * ge
        acc[...] += jnp.dot(hidden.astype(BF16), w2_ref[...], preferred_element_type=F32)
        for k in range(chunk):
            gather_row(i + 1, first + k)
        scatter_chunk(i - 1)

    @pl.when(i == used)
    def _():
        scatter_chunk(i - 1)

    @pl.when(last_step & live & (i >= 1))
    def _():
        wait_scatter(i % 2)

    @pl.when(last_step & live)
    def _():
        obuf[i % 2] = acc[...]

    @pl.when(last_step & (i == used))
    def _():
        wait_scatter(0)
        wait_scatter(1)


def _moe_grouped(h, info, src, dst, tile_group, n_used, w1, w3, w2, *, tm):
    n, d = h.shape
    f = w1.shape[2]
    n_tiles = src.shape[0] // tm
    n_rows = n + (N_GROUPS + 1) * tm
    assert tm % EXPERTS_PER_GROUP == 0 and dst.shape[0] == (n_tiles + 1) * tm

    def expert(i, e, src_, dst_, tg, used):
        e_eff = jnp.where(i < used[0], e, EXPERTS_PER_GROUP - 1)
        return (tg[i] * EXPERTS_PER_GROUP + e_eff, 0, 0)

    return pl.pallas_call(
        functools.partial(_moe_group_kernel, tm=tm),
        grid_spec=pltpu.PrefetchScalarGridSpec(
            num_scalar_prefetch=4,
            grid=(n_tiles, EXPERTS_PER_GROUP),
            in_specs=[
                pl.BlockSpec(memory_space=pl.ANY),
                pl.BlockSpec(memory_space=pl.ANY),
                pl.BlockSpec((None, d, f), expert),
                pl.BlockSpec((None, d, f), expert),
                pl.BlockSpec((None, f, d), expert),
                pl.BlockSpec(memory_space=pl.ANY),
            ],
            out_specs=pl.BlockSpec(memory_space=pl.ANY),
            scratch_shapes=[
                pltpu.VMEM((tm, d), F32), pltpu.VMEM((tm, LANES), F32), pltpu.VMEM((tm, d), BF16),
                pltpu.VMEM((tm, LANES), F32), pltpu.VMEM((tm, d), F32), pltpu.VMEM((2, tm, d), F32),
                pltpu.SemaphoreType.DMA(()), pltpu.SemaphoreType.DMA(()), pltpu.SemaphoreType.DMA((2,)),
            ],
        ),
        out_shape=jax.ShapeDtypeStruct((n_rows, d), F32),
        input_output_aliases={9: 0},
        compiler_params=_params("arbitrary", "arbitrary"),
        name="moe_grouped",
    )(src, dst, tile_group, n_used, h, info, w1, w3, w2, jnp.zeros((n_rows, d), F32))


MASKED = -1e30


def _t5_bucket_table(tq):
    qi = np.arange(tq, dtype=np.int32)[:, None]
    col = np.arange(2 * tq, dtype=np.int32)[None, :]
    n = np.maximum(qi + tq - col, 0)
    max_exact = N_BUCKETS // 2
    nf = np.maximum(n, 1).astype(np.float32)
    large = max_exact + (np.log(nf / np.float32(max_exact)) / np.float32(math.log(MAX_DISTANCE / max_exact))
                         * np.float32(N_BUCKETS - max_exact)).astype(np.int32)
    table = np.where(n < max_exact, n, np.minimum(large, N_BUCKETS - 1)).astype(np.int32)
    assert table[0, 0] == N_BUCKETS - 1
    return table


def _bias_kernel(rb_ref, bk_ref, o_ref):
    h = pl.program_id(0)
    bk = bk_ref[...]
    acc = jnp.zeros(bk.shape, F32)
    for k in range(N_BUCKETS):
        acc = jnp.where(bk == k, rb_ref[k, h], acc)
    o_ref[...] = (acc - rb_ref[N_BUCKETS - 1, h]) * LOG2E


def _near_bias(rel_bias, tq):
    return pl.pallas_call(
        _bias_kernel,
        grid=(N_HEADS,),
        in_specs=[pl.BlockSpec(memory_space=pltpu.SMEM),
                  pl.BlockSpec((tq, 2 * tq), lambda h: (0, 0))],
        out_specs=pl.BlockSpec((None, tq, 2 * tq), lambda h: (h, 0, 0)),
        out_shape=jax.ShapeDtypeStruct((N_HEADS, tq, 2 * tq), F32),
        name="near_bias",
    )(rel_bias, jnp.asarray(_t5_bucket_table(tq)))


_NEG_INF_KEY = int(np.int32(np.array(-np.inf, np.float32).view(np.int32)) ^ np.int32(0x7FFFFFFF))


def _sort_key(x):
    k = pltpu.bitcast(x, jnp.int32)
    return k ^ ((k >> 31) & jnp.int32(0x7FFFFFFF))


def _kth_largest_key(count_ge, kth):
    zero = jnp.int32(0)
    first = count_ge(zero)
    ans = jnp.where(first >= kth, zero, jnp.int32(-2 ** 31))
    settled = (first == kth).astype(jnp.int32)

    def cond(state):
        it, _, settled = state
        return (it < 31) & (jnp.min(settled) == 0)

    def body(state):
        it, ans, settled = state
        cand = ans | (jnp.int32(1) << (30 - it))
        count = count_ge(cand)
        return it + 1, jnp.where(count >= kth, cand, ans), settled | (count == kth).astype(jnp.int32)

    return lax.while_loop(cond, body, (jnp.int32(0), ans, settled))[1]


def _dsa_select_kernel(qi_ref, kw_ref, kall_ref, m_ref, key_ref, *, tq, tk, topk):
    i = pl.program_id(1)
    n_kt = (i * tq + tq + tk - 1) // tk
    qpos = i * tq + lax.broadcasted_iota(jnp.int32, (tq, tk), 0)
    wi = kw_ref[:, IDX_DIM:IDX_DIM + N_IDX_HEADS]
    qi = qi_ref[...].astype(BF16)
    key_ref[...] = jnp.full(key_ref.shape, _NEG_INF_KEY, jnp.int32)

    def score_tile(kt, carry):
        k0 = pl.multiple_of(kt * tk, tk)
        kid = kall_ref[pl.ds(k0, tk), 0:IDX_DIM].astype(BF16)
        acc = jnp.zeros((tq, tk), F32)
        for h in range(N_IDX_HEADS):
            s = _bdot_nt(qi[:, h * IDX_DIM:(h + 1) * IDX_DIM], kid)
            acc = acc + wi[:, h:h + 1] * jnp.maximum(s, 0.0)
        kpos = k0 + lax.broadcasted_iota(jnp.int32, (tq, tk), 1)
        sc = jnp.where(kpos <= qpos, acc * IDX_SCALE, NEG_INF)
        key_ref[:, pl.ds(k0, tk)] = _sort_key(sc)
        return carry

    lax.fori_loop(0, n_kt, score_tile, 0)

    def count_ge(cand):
        def body(kt, acc):
            k0 = pl.multiple_of(kt * tk, tk)
            ge = (key_ref[:, pl.ds(k0, tk)] >= cand).astype(jnp.int32)
            for j in range(tk // LANES):
                acc = acc + ge[:, j * LANES:(j + 1) * LANES]
            return acc
        acc = lax.fori_loop(0, n_kt, body, jnp.zeros((tq, LANES), jnp.int32))
        return jnp.sum(acc, axis=1, keepdims=True)

    thr = _kth_largest_key(count_ge, topk)
    keys = key_ref[...]
    m_ref[...] = ((keys >= thr) & (keys > _NEG_INF_KEY)).astype(m_ref.dtype)


def _dsa_select(proj, *, tq, tk, topk):
    bsz, t, _ = proj.shape
    qi_w = N_IDX_HEADS * IDX_DIM
    qi_blk = (N_HEADS * HEAD_DIM + 2 * N_KV_B * HEAD_DIM) // qi_w
    kw_blk = (N_HEADS * HEAD_DIM + 2 * N_KV_B * HEAD_DIM + qi_w) // LANES
    return pl.pallas_call(
        functools.partial(_dsa_select_kernel, tq=tq, tk=tk, topk=topk),
        grid=(bsz, t // tq),
        in_specs=[
            pl.BlockSpec((None, tq, qi_w), lambda b, i: (b, i, qi_blk)),
            pl.BlockSpec((None, tq, LANES), lambda b, i: (b, i, kw_blk)),
            pl.BlockSpec((None, t, LANES), lambda b, i: (b, 0, kw_blk)),
        ],
        out_specs=pl.BlockSpec((None, tq, t), lambda b, i: (b, i, 0)),
        out_shape=jax.ShapeDtypeStruct((bsz, t, t), BF16),
        scratch_shapes=[pltpu.VMEM((tq, t), jnp.int32)],
        compiler_params=_params("arbitrary", "arbitrary"),
        name="dsa_select",
    )(proj, proj, proj)


def _attn_kernel(q_ref, k_ref, v_ref, m_ref, tb_ref, o_ref, *state, tq, rep, shared_kv, block_mask):
    m_sc, l_sc, acc_sc = state[:rep], state[rep:2 * rep], state[2 * rep:]
    i = pl.program_id(2)
    for r in range(rep):
        m_sc[r][...] = jnp.full((tq, LANES), MASKED, F32)
        l_sc[r][...] = jnp.zeros((tq, LANES), F32)
        acc_sc[r][...] = jnp.zeros((tq, HEAD_DIM), F32)
    lane_tiles = tq // LANES

    def row_reduce(x, op, reduce):
        part = x[:, :LANES]
        for c in range(1, lane_tiles):
            part = op(part, x[:, c * LANES:(c + 1) * LANES])
        return reduce(part, axis=1, keepdims=True)

    def key_mask(j, r, causal):
        if block_mask:
            lane = lax.broadcasted_iota(jnp.int32, (tq, LANES), 1)
            vis = jnp.sum(jnp.where(lane == j, m_ref[r].astype(F32), 0.0), axis=1, keepdims=True)
            ok = jnp.broadcast_to(vis > 0.0, (tq, tq))
        else:
            ok = m_ref[:, pl.ds(pl.multiple_of(j * tq, tq), tq)] > 0
        if causal:
            ok = ok & (lax.broadcasted_iota(jnp.int32, (tq, tq), 0) >= lax.broadcasted_iota(jnp.int32, (tq, tq), 1))
        return ok

    def step(j, bias_lo, causal):
        k0 = pl.multiple_of(j * tq, tq)
        kv = lambda r: slice(0, HEAD_DIM) if shared_kv else slice(r * HEAD_DIM, (r + 1) * HEAD_DIM)
        ok_shared = None if block_mask else key_mask(j, 0, causal)
        scores = []
        for r in range(rep):
            q = (q_ref[:, r * HEAD_DIM:(r + 1) * HEAD_DIM] * QK_SCALE_LOG2).astype(BF16)
            s = _bdot_nt(q, k_ref[pl.ds(k0, tq), kv(r)])
            if bias_lo is not None:
                s = s + tb_ref[r, :, bias_lo:bias_lo + tq]
            scores.append(jnp.where(key_mask(j, r, causal) if block_mask else ok_shared, s, MASKED))
        probs, alphas = [], []
        for r in range(rep):
            m_old = m_sc[r][...]
            m_new = jnp.maximum(m_old, row_reduce(scores[r], jnp.maximum, jnp.max))
            alpha = jnp.exp2(m_old - m_new)
            p = jnp.exp2(scores[r] - jnp.concatenate([m_new] * lane_tiles, axis=1))
            l_sc[r][...] = alpha * l_sc[r][...] + row_reduce(p, jnp.add, jnp.sum)
            m_sc[r][...] = m_new
            probs.append(p.astype(BF16))
            alphas.append(alpha)
        for r in range(rep):
            pv = jnp.dot(probs[r], v_ref[pl.ds(k0, tq), kv(r)].astype(BF16), preferred_element_type=F32)
            acc_sc[r][...] = alphas[r] * acc_sc[r][...] + pv

    def far(j, carry):
        step(j, None, False)
        return carry

    lax.fori_loop(0, jnp.maximum(i - 1, 0), far, 0)

    @pl.when(i >= 1)
    def _():
        step(i - 1, 0, False)

    step(i, tq, True)
    o_ref[...] = jnp.concatenate([acc_sc[r][...] / l_sc[r][...] for r in range(rep)], axis=1).astype(o_ref.dtype)


def _attention(proj, k_col, v_col, mask, near_bias, *, tq, rep, shared_kv, block_mask):
    bsz, t, _ = proj.shape
    g = N_HEADS // rep
    kv_w = HEAD_DIM if shared_kv else rep * HEAD_DIM
    assert HEAD_DIM == LANES and k_col % kv_w == 0 and v_col % kv_w == 0
    k_blk, v_blk = k_col // kv_w, v_col // kv_w
    if block_mask:
        assert tq == MOBA_BLOCK
        mask_spec = pl.BlockSpec((None, rep, tq, LANES), lambda b, h, i: (b, h, i, 0))
    else:
        mask_spec = pl.BlockSpec((None, tq, t), lambda b, h, i: (b, i, 0))
    stat = pltpu.VMEM((tq, LANES), F32)
    return pl.pallas_call(
        functools.partial(_attn_kernel, tq=tq, rep=rep, shared_kv=shared_kv, block_mask=block_mask),
        grid=(bsz, g, t // tq),
        in_specs=[
            pl.BlockSpec((None, tq, rep * HEAD_DIM), lambda b, h, i: (b, i, h)),
            pl.BlockSpec((None, t, kv_w), lambda b, h, i: (b, 0, k_blk + h)),
            pl.BlockSpec((None, t, kv_w), lambda b, h, i: (b, 0, v_blk + h)),
            mask_spec,
            pl.BlockSpec((rep, tq, 2 * tq), lambda b, h, i: (h, 0, 0)),
        ],
        out_specs=pl.BlockSpec((None, tq, rep * HEAD_DIM), lambda b, h, i: (b, i, h)),
        out_shape=jax.ShapeDtypeStruct((bsz, t, N_HEADS * HEAD_DIM), BF16),
        scratch_shapes=[stat] * (3 * rep),
        compiler_params=_params("arbitrary", "arbitrary", "arbitrary"),
        name="masked_attention",
    )(proj, proj, proj, mask, near_bias)


PAGES_PER_BLOCK = MOBA_BLOCK // PAGE_SIZE
MEANS_BLOCKS_PER_STEP = 2


def _block_means_kernel(tbl_ref, *refs):
    del tbl_ref
    page_refs, o_ref = refs[:-1], refs[-1]
    for blk in range(MEANS_BLOCKS_PER_STEP):
        pages = page_refs[blk * PAGES_PER_BLOCK:(blk + 1) * PAGES_PER_BLOCK]
        total = jnp.sum(pages[0][...], axis=0)
        for ref in pages[1:]:
            total = total + jnp.sum(ref[...], axis=0)
        o_ref[blk] = total / MOBA_BLOCK


def _block_means(pool, layer, table):
    _, _, _, h, dh = pool.shape
    bsz, npg = table.shape
    nb = npg // PAGES_PER_BLOCK
    per_step = MEANS_BLOCKS_PER_STEP * PAGES_PER_BLOCK
    assert npg % per_step == 0
    page = lambda k: pl.BlockSpec((None, None, PAGE_SIZE, h, dh),
                                  lambda b, n, tbl: (layer, tbl[b, per_step * n + k], 0, 0, 0))
    out = pl.pallas_call(
        _block_means_kernel,
        grid_spec=pltpu.PrefetchScalarGridSpec(
            num_scalar_prefetch=1,
            grid=(bsz, nb // MEANS_BLOCKS_PER_STEP),
            in_specs=[page(k) for k in range(per_step)],
            out_specs=pl.BlockSpec((None, MEANS_BLOCKS_PER_STEP, h, dh), lambda b, n, tbl: (b, n, 0, 0)),
        ),
        out_shape=jax.ShapeDtypeStruct((bsz, nb, h, dh), F32),
        compiler_params=_params("arbitrary", "arbitrary"),
        name="block_means",
    )(table, *([pool] * per_step))
    return out.reshape(bsz, nb, h * dh)


def _moba_select_kernel(q_ref, mean_ref, o_ref, *, tq, past_len):
    i = pl.program_id(1)
    own = (past_len + i * tq + lax.broadcasted_iota(jnp.int32, (tq, LANES), 0)) // MOBA_BLOCK
    blk = lax.broadcasted_iota(jnp.int32, (tq, LANES), 1)
    past = blk < own
    for h in range(N_HEADS):
        sl = slice(h * HEAD_DIM, (h + 1) * HEAD_DIM)
        gate = lax.dot_general(q_ref[:, sl], mean_ref[:, sl], (((1,), (1,)), ((), ())),
                               precision=lax.Precision.HIGHEST, preferred_element_type=F32)
        g = jnp.where(past, gate, NEG_INF)
        kth = jnp.max(g, axis=1, keepdims=True)
        for _ in range(MOBA_TOPK - 1):
            kth = jnp.max(jnp.where(g < kth, g, NEG_INF), axis=1, keepdims=True)
        o_ref[h] = ((past & (g >= kth)) | (blk == own)).astype(o_ref.dtype)


def _moba_select(q_src, means, *, tq, past_len):
    bsz, t, _ = q_src.shape
    w = N_HEADS * HEAD_DIM
    return pl.pallas_call(
        functools.partial(_moba_select_kernel, tq=tq, past_len=past_len),
        grid=(bsz, t // tq),
        in_specs=[
            pl.BlockSpec((None, tq, w), lambda b, i: (b, i, 0)),
            pl.BlockSpec((None, LANES, w), lambda b, i: (b, 0, 0)),
        ],
        out_specs=pl.BlockSpec((None, N_HEADS, tq, LANES), lambda b, i: (b, 0, i, 0)),
        out_shape=jax.ShapeDtypeStruct((bsz, N_HEADS, t, LANES), BF16),
        compiler_params=_params("arbitrary", "arbitrary"),
        name="moba_select",
    )(q_src, means)


def _dsa_select_paged_kernel(tbl_ref, qi_ref, wi_ref, *rest, pp, topk, past_len, t_new):
    del tbl_ref
    page_refs = rest[:pp]
    knew_ref, m_ref, key_ref = rest[pp:]
    p = pl.program_id(1)
    qi = qi_ref[...].astype(BF16)
    wi = wi_ref[...]

    def score(kpage):
        s = _bdot_nt(qi, kpage)
        w = wi * jnp.maximum(s, 0.0)
        return jnp.sum(w.reshape(N_IDX_HEADS, t_new, PAGE_SIZE), axis=0) * IDX_SCALE

    for k in range(pp):
        off = pl.multiple_of((p * pp + k) * PAGE_SIZE, PAGE_SIZE)
        key_ref[:, pl.ds(off, PAGE_SIZE)] = _sort_key(score(page_refs[k][...]))

    @pl.when(p == pl.num_programs(1) - 1)
    def _():
        row = lax.broadcasted_iota(jnp.int32, (t_new, PAGE_SIZE), 0)
        col = lax.broadcasted_iota(jnp.int32, (t_new, PAGE_SIZE), 1)
        sc = jnp.where(col <= row, score(knew_ref[...]), NEG_INF)
        key_ref[:, past_len:past_len + PAGE_SIZE] = _sort_key(sc)

        def count_ge(cand):
            return jnp.sum((key_ref[...] >= cand).astype(jnp.int32), axis=1, keepdims=True)

        thr = _kth_largest_key(count_ge, topk)
        keys = key_ref[...]
        m_ref[...] = ((keys >= thr) & (keys > _NEG_INF_KEY)).astype(m_ref.dtype)


def _dsa_select_paged(qi_rows, wi_rows, kidx_pool, layer, page_table, kidx_new, *, pp, topk):
    bsz, rows, _ = qi_rows.shape
    t_new = rows // N_IDX_HEADS
    n_pages = page_table.shape[1]
    assert n_pages % pp == 0
    past_len = n_pages * PAGE_SIZE
    lpad = past_len + PAGE_SIZE
    page_spec = lambda k: pl.BlockSpec((None, None, PAGE_SIZE, IDX_DIM),
                                       lambda b, p, tbl: (layer, tbl[b, p * pp + k], 0, 0))
    return pl.pallas_call(
        functools.partial(_dsa_select_paged_kernel, pp=pp, topk=topk, past_len=past_len, t_new=t_new),
        grid_spec=pltpu.PrefetchScalarGridSpec(
            num_scalar_prefetch=1,
            grid=(bsz, n_pages // pp),
            in_specs=[
                pl.BlockSpec((None, rows, IDX_DIM), lambda b, p, tbl: (b, 0, 0)),
                pl.BlockSpec((None, rows, 1), lambda b, p, tbl: (b, 0, 0)),
                *[page_spec(k) for k in range(pp)],
                pl.BlockSpec((None, PAGE_SIZE, IDX_DIM), lambda b, p, tbl: (b, 0, 0)),
            ],
            out_specs=pl.BlockSpec((None, t_new, lpad), lambda b, p, tbl: (b, 0, 0)),
            scratch_shapes=[pltpu.VMEM((t_new, lpad), jnp.int32)],
        ),
        out_shape=jax.ShapeDtypeStruct((bsz, t_new, lpad), BF16),
        compiler_params=_params("arbitrary", "arbitrary"),
        name="dsa_select_paged",
    )(page_table, qi_rows, wi_rows, *([kidx_pool] * pp), kidx_new)


def _col_vector(row_vec):
    n = row_vec.shape[1]
    eye = lax.broadcasted_iota(jnp.int32, (n, n), 0) == lax.broadcasted_iota(jnp.int32, (n, n), 1)
    return jnp.sum(jnp.where(eye, row_vec, 0.0), axis=1, keepdims=True)


def _paged_attn_kernel(tbl_ref, qbd_ref, *rest, pp):
    del tbl_ref
    k_refs, v_refs = rest[:pp], rest[pp:2 * pp]
    mt_ref, mnew_ref, knew_ref, vnew_ref, bias_ref, o_ref, m_sc, l_sc, acc_sc = rest[2 * pp:]
    p = pl.program_id(1)
    last = pl.num_programs(1) - 1

    @pl.when(p == 0)
    def _():
        m_sc[...] = jnp.full(m_sc.shape, MASKED, F32)
        l_sc[...] = jnp.zeros_like(l_sc)
        acc_sc[...] = jnp.zeros_like(acc_sc)

    n_kv = qbd_ref.shape[0] // HEAD_DIM

    def heads_on_lanes(ref):
        return jnp.concatenate([ref[pl.ds(g, PAGE_SIZE, stride=n_kv), :] for g in range(n_kv)], axis=1)

    def pages(kps, vps, masks, biases):
        scores = []
        for kp, mask, bias in zip(kps, masks, biases):
            s = jnp.dot(kp().astype(BF16), qbd_ref[...], preferred_element_type=F32)
            if bias is not None:
                s = s + bias
            scores.append(jnp.where(mask > 0, s, MASKED))
        m_old = m_sc[...]
        m_new = m_old
        for s in scores:
            m_new = jnp.maximum(m_new, jnp.max(s, axis=0, keepdims=True))
        alpha = jnp.exp2(m_old - m_new)
        l_new = alpha * l_sc[...]
        pv = None
        for s, vp in zip(scores, vps):
            pt = jnp.exp2(s - m_new)
            l_new = l_new + jnp.sum(pt, axis=0, keepdims=True)
            part = lax.dot_general(pt.astype(BF16), vp().astype(BF16), (((0,), (0,)), ((), ())),
                                   preferred_element_type=F32)
            pv = part if pv is None else pv + part
        l_sc[...] = l_new
        acc_sc[...] = acc_sc[...] * _col_vector(alpha) + pv
        m_sc[...] = m_new

    is_last = (p == last).astype(F32)
    pages([functools.partial(heads_on_lanes, r) for r in k_refs],
          [functools.partial(heads_on_lanes, r) for r in v_refs],
          [mt_ref[k * PAGE_SIZE:(k + 1) * PAGE_SIZE, :] for k in range(pp)],
          [None] * (pp - 1) + [bias_ref[0] * is_last])

    @pl.when(p == last)
    def _():
        pages([lambda: knew_ref[...]], [lambda: vnew_ref[...]], [mnew_ref[...]], [bias_ref[1]])
        o_ref[...] = acc_sc[...] / _col_vector(l_sc[...])


def _paged_attention(qbd, k_pool, v_pool, layer, page_table, mask_t, k_new, v_new, bias_t, *, pp):
    bsz, w, _ = qbd.shape
    n_pages = page_table.shape[1]
    assert n_pages % pp == 0
    n_layers, n_phys, _, n_kv, _ = k_pool.shape
    assert n_kv * HEAD_DIM == w
    as_rows = lambda pool: pool.reshape(n_layers, n_phys, PAGE_SIZE * n_kv, HEAD_DIM)
    k_pool, v_pool = as_rows(k_pool), as_rows(v_pool)
    page_spec = lambda k: pl.BlockSpec((None, None, PAGE_SIZE * n_kv, HEAD_DIM),
                                       lambda b, p, tbl: (layer, tbl[b, p * pp + k], 0, 0))
    per_b = lambda shp: pl.BlockSpec((None,) + shp, lambda b, p, tbl: (b, 0, 0))
    return pl.pallas_call(
        functools.partial(_paged_attn_kernel, pp=pp),
        grid_spec=pltpu.PrefetchScalarGridSpec(
            num_scalar_prefetch=1,
            grid=(bsz, n_pages // pp),
            in_specs=[
                per_b((w, LANES)),
                *[page_spec(k) for k in range(pp)],
                *[page_spec(k) for k in range(pp)],
                pl.BlockSpec((None, pp * PAGE_SIZE, LANES), lambda b, p, tbl: (b, p, 0)),
                pl.BlockSpec((None, PAGE_SIZE, LANES), lambda b, p, tbl: (b, n_pages, 0)),
                per_b((PAGE_SIZE, w)),
                per_b((PAGE_SIZE, w)),
                pl.BlockSpec((2, PAGE_SIZE, LANES), lambda b, p, tbl: (0, 0, 0)),
            ],
            out_specs=per_b((LANES, w)),
            scratch_shapes=[pltpu.VMEM((1, LANES), F32), pltpu.VMEM((1, LANES), F32), pltpu.VMEM((LANES, w), F32)],
        ),
        out_shape=jax.ShapeDtypeStruct((bsz, LANES, w), F32),
        compiler_params=_params("arbitrary", "arbitrary"),
        name="paged_attention",
    )(page_table, qbd, *([k_pool] * pp), *([v_pool] * pp), mask_t, mask_t, k_new, v_new, bias_t)


def _block_diag_queries(q, n_kv):
    bsz, t, _ = q.shape
    q4 = q.reshape(bsz, t, N_HEADS, HEAD_DIM) * QK_SCALE_LOG2
    onehot = (jnp.arange(N_HEADS)[:, None] // (N_HEADS // n_kv) == jnp.arange(n_kv)[None, :]).astype(q.dtype)
    qbd = q4[:, :, :, None, :] * onehot[None, None, :, :, None]
    return qbd.transpose(0, 3, 4, 2, 1).reshape(bsz, n_kv * HEAD_DIM, N_HEADS * t).astype(BF16)


def _own_head_lanes(out, n_kv, t):
    bsz = out.shape[0]
    out5 = out.reshape(bsz, N_HEADS, t, n_kv, HEAD_DIM)
    heads = jnp.arange(N_HEADS)
    picked = out5[:, heads, :, heads // (N_HEADS // n_kv), :]
    return picked.transpose(1, 2, 0, 3).reshape(bsz, t, N_HEADS * HEAD_DIM)


def _decode_bias(near_bias, tq, t):
    tail = near_bias[:, :t, tq - PAGE_SIZE:tq + PAGE_SIZE]
    return tail.transpose(2, 0, 1).reshape(2, PAGE_SIZE, N_HEADS * t)


def _pad_rows(a, rows):
    return jnp.pad(a, ((0, 0), (0, rows - a.shape[1]), (0, 0)))


def _new_token_mask(t):
    key = jnp.arange(PAGE_SIZE)[:, None]
    tok = jnp.tile(jnp.arange(t), N_HEADS)[None, :]
    return (key <= tok).astype(BF16)


ATTN_TQ = 256
SELECT_TQ = 256
SELECT_TK = 512
MOBA_HEADS_PER_STEP = 4
Q_W = N_HEADS * HEAD_DIM
KV_W = N_KV_B * HEAD_DIM
DSA_PROJ = Q_W + 2 * KV_W + N_IDX_HEADS * IDX_DIM + IDX_DIM + N_IDX_HEADS
DSA_PROJ_PAD = -(-DSA_PROJ // 512) * 512


def _dsa_split(proj):
    k = proj[..., Q_W:Q_W + KV_W]
    v = proj[..., Q_W + KV_W:Q_W + 2 * KV_W]
    o = Q_W + 2 * KV_W + N_IDX_HEADS * IDX_DIM
    return k, v, proj[..., o:o + IDX_DIM]


def _dsa_prompt(proj, near_bias):
    bsz, t, _ = proj.shape
    k, v, ki = _dsa_split(proj)
    mask = _dsa_select(proj, tq=min(SELECT_TQ, t), tk=min(SELECT_TK, t), topk=min(IDX_TOPK, t // 4))
    o = _attention(proj, Q_W, Q_W + KV_W, mask, near_bias, tq=ATTN_TQ, rep=N_HEADS // N_KV_B,
                   shared_kv=True, block_mask=False)
    return o, k.reshape(bsz, t, N_KV_B, HEAD_DIM), v.reshape(bsz, t, N_KV_B, HEAD_DIM), ki


def _dsa_sample(proj, k_pool, v_pool, kidx_pool, layer, page_table, near_bias):
    bsz, t, _ = proj.shape
    past_len = page_table.shape[1] * PAGE_SIZE
    k, v, ki = _dsa_split(proj)
    o_qi = Q_W + 2 * KV_W
    qi = proj[..., o_qi:o_qi + N_IDX_HEADS * IDX_DIM].reshape(bsz, t, N_IDX_HEADS, IDX_DIM)
    wi = proj[..., o_qi + N_IDX_HEADS * IDX_DIM + IDX_DIM:DSA_PROJ]
    qi_rows = qi.transpose(0, 2, 1, 3).reshape(bsz, N_IDX_HEADS * t, IDX_DIM)
    wi_rows = wi.transpose(0, 2, 1).reshape(bsz, N_IDX_HEADS * t, 1)
    sel = _dsa_select_paged(qi_rows, wi_rows, kidx_pool, layer, page_table, _pad_rows(ki, PAGE_SIZE),
                            pp=8, topk=min(IDX_TOPK, (past_len + t) // 4))
    mask_t = jnp.tile(sel.transpose(0, 2, 1), (1, 1, N_HEADS))
    out = _paged_attention(
        _block_diag_queries(proj[..., :Q_W], N_KV_B), k_pool, v_pool, layer, page_table,
        mask_t, _pad_rows(k, PAGE_SIZE), _pad_rows(v, PAGE_SIZE), _decode_bias(near_bias, ATTN_TQ, t), pp=8)
    o = _own_head_lanes(out, N_KV_B, t).astype(BF16)
    return o, k.reshape(bsz, t, N_KV_B, HEAD_DIM), v.reshape(bsz, t, N_KV_B, HEAD_DIM), ki


def _moba_prompt(proj, near_bias):
    bsz, t, _ = proj.shape
    k = proj[..., Q_W:2 * Q_W]
    v = proj[..., 2 * Q_W:3 * Q_W]
    k_heads = k.reshape(bsz, t, N_HEADS, HEAD_DIM)
    n_pages = t // PAGE_SIZE
    pages = jnp.arange(bsz * n_pages, dtype=jnp.int32).reshape(bsz, n_pages)
    means = _block_means(k_heads.reshape(1, bsz * n_pages, PAGE_SIZE, N_HEADS, HEAD_DIM), 0, pages)
    sel = _moba_select(proj, _pad_rows(means, LANES), tq=ATTN_TQ, past_len=0)
    o = _attention(proj, Q_W, 2 * Q_W, sel, near_bias, tq=ATTN_TQ, rep=MOBA_HEADS_PER_STEP,
                   shared_kv=False, block_mask=True)
    return o, k_heads, v.reshape(bsz, t, N_HEADS, HEAD_DIM)


def _moba_sample(proj, k_pool, v_pool, layer, page_table, near_bias):
    bsz, t, _ = proj.shape
    n_pages = page_table.shape[1]
    past_len = n_pages * PAGE_SIZE
    k = proj[..., Q_W:2 * Q_W]
    v = proj[..., 2 * Q_W:3 * Q_W]
    means = _block_means(k_pool, layer, page_table)
    sel = _moba_select(proj, _pad_rows(means, LANES), tq=t, past_len=past_len)
    past = jnp.repeat(sel[..., :past_len // MOBA_BLOCK], MOBA_BLOCK, axis=-1)
    past_t = past.transpose(0, 3, 1, 2).reshape(bsz, past_len, N_HEADS * t)
    new_t = jnp.broadcast_to(_new_token_mask(t), (bsz, PAGE_SIZE, N_HEADS * t))
    out = _paged_attention(
        _block_diag_queries(proj[..., :Q_W], N_HEADS), k_pool, v_pool, layer, page_table,
        jnp.concatenate([past_t, new_t], axis=1), _pad_rows(k, PAGE_SIZE), _pad_rows(v, PAGE_SIZE),
        _decode_bias(near_bias, ATTN_TQ, t), pp=4)
    o = _own_head_lanes(out, N_HEADS, t).astype(BF16)
    return o, k.reshape(bsz, t, N_HEADS, HEAD_DIM), v.reshape(bsz, t, N_HEADS, HEAD_DIM)


N_MIXERS = 3
PROMPT_TM = 1024
RESIDUAL_TM = 512
MATMUL_TN = 512
LRU_TT = 256
MOE_TM = 512
ROUTE_W = LANES


def kernel(x_prompt, x_sample, state_lru_h, state_lru_conv, cache_dsa_k, cache_dsa_v, cache_dsa_kidx,
           cache_moba_k, cache_moba_v, page_table, c_prompt, c_sample, rel_bias, norm_mix_g, norm_ffn_g,
           final_norm_g, ada_w, ada_b, lru_w_in, lru_conv_w, lru_conv_b, lru_w_rgate, lru_b_rgate,
           lru_w_igate, lru_b_igate, lru_lambda, lru_w_out, dsa_w_in, dsa_w_out, moba_w_in, moba_w_out,
           moe_w_group, moe_w_expert, moe_w1, moe_w3, moe_w2):
    bp, tp, d = x_prompt.shape
    bs, ts, _ = x_sample.shape
    depth = ada_w.shape[0]
    ns = bs * ts
    c_rnn = lru_w_out.shape[1]

    c_all = jnp.concatenate([c_prompt, c_sample], axis=0)
    c_rows = -(-c_all.shape[0] // SUBLANES) * SUBLANES
    mods = _ada(jnp.pad(c_all, ((0, c_rows - c_all.shape[0]), (0, 0))), ada_w, ada_b)
    mods = mods.reshape(depth, c_rows, 6, d)
    near_bias = _near_bias(rel_bias, ATTN_TQ)

    xp = x_prompt
    pending = None
    xs = x_sample.reshape(1, ns, d)
    outs = {n: [] for n in ("lru_h_p", "lru_c_p", "lru_h_s", "lru_c_s", "dsa_k_p", "dsa_v_p", "dsa_i_p",
                            "dsa_k_s", "dsa_v_s", "dsa_i_s", "moba_k_p", "moba_v_p", "moba_k_s", "moba_v_s")}
    for i in range(depth):
        kind, j = i % N_MIXERS, i // N_MIXERS
        mod_p = [mods[i, :bp, m][:, None, :] for m in range(6)]
        mod_s = [jnp.repeat(mods[i, bp:bp + bs, m], ts, axis=0)[None] for m in range(6)]
        shp1, scp1, gtp1, shp2, scp2, gtp2 = mod_p
        shs1, scs1, gts1, shs2, scs2, gts2 = mod_s
        def mix_in(w, xp=xp, pending=pending, tn=MATMUL_TN):
            proj_s = _norm_matmul(xs, norm_mix_g[i], shs1, scs1, w, tm=ns, tn=tn)
            if pending is None:
                return _norm_matmul(xp, norm_mix_g[i], shp1, scp1, w, tm=min(PROMPT_TM, tp), tn=tn), proj_s, xp
            proj_p, xp_new = _norm_matmul(xp, norm_mix_g[i], shp1, scp1, w.astype(BF16), tm=min(RESIDUAL_TM, tp),
                                          tn=tn, residual=pending)
            return proj_p, proj_s, xp_new
        if kind == 0:
            up, us, xp = mix_in(lru_w_in[j])
            lru_params = (lru_conv_w[j], lru_conv_b[j], lru_w_rgate[j], lru_b_rgate[j], lru_w_igate[j],
                          lru_b_igate[j], lru_lambda[j])
            yp, hp_state, cp = _lru_core(up, jnp.zeros((bp, CONV_W - 1, c_rnn), F32), jnp.zeros((bp, c_rnn), F32),
                                         *lru_params, tt=LRU_TT)
            ys, hs_state, cs = _lru_core(us.reshape(bs, ts, 2 * c_rnn), state_lru_conv[j], state_lru_h[j],
                                         *lru_params, tt=ts)
            ys = ys.reshape(1, ns, c_rnn)
            w_out = lru_w_out[j]
            outs["lru_h_p"].append(hp_state)
            outs["lru_c_p"].append(cp)
            outs["lru_h_s"].append(hs_state)
            outs["lru_c_s"].append(cs)
        elif kind == 1:
            w_in = jnp.pad(dsa_w_in[j], ((0, 0), (0, DSA_PROJ_PAD - DSA_PROJ)))
            pp_, ps_, xp = mix_in(w_in)
            yp, kp, vp, ip = _dsa_prompt(pp_, near_bias)
            ys, k_s, v_s, i_s = _dsa_sample(ps_.reshape(bs, ts, DSA_PROJ_PAD), cache_dsa_k, cache_dsa_v,
                                            cache_dsa_kidx, j, page_table, near_bias)
            ys = ys.reshape(1, ns, Q_W)
            w_out = dsa_w_out[j]
            for n, val in zip(("dsa_k_p", "dsa_v_p", "dsa_i_p", "dsa_k_s", "dsa_v_s", "dsa_i_s"),
                              (kp, vp, ip, k_s, v_s, i_s)):
                outs[n].append(val)
        else:
            pp_, ps_, xp = mix_in(moba_w_in[j])
            yp, kp, vp = _moba_prompt(pp_, near_bias)
            ys, k_s, v_s = _moba_sample(ps_.reshape(bs, ts, 3 * Q_W), cache_moba_k, cache_moba_v, j,
                                        page_table, near_bias)
            ys = ys.reshape(1, ns, Q_W)
            w_out = moba_w_out[j]
            for n, val in zip(("moba_k_p", "moba_v_p", "moba_k_s", "moba_v_s"), (kp, vp, k_s, v_s)):
                outs[n].append(val)
        xp = _matmul_residual(yp, w_out, xp, gtp1, tm=min(PROMPT_TM, tp), tn=min(MATMUL_TN, d))
        xs = _matmul_residual(ys, w_out, xs, gts1, tm=ns, tn=min(MATMUL_TN, d))

        w_route = jnp.pad(jnp.concatenate([moe_w_group[i], moe_w_expert[i]], axis=1),
                          ((0, 0), (0, ROUTE_W - N_GROUPS - N_EXPERTS)))
        w1, w3, w2 = moe_w1[i].astype(BF16), moe_w3[i].astype(BF16), moe_w2[i].astype(BF16)
        lg_p, h_p = _norm_matmul(xp, norm_ffn_g[i], shp2, scp2, w_route, tm=min(PROMPT_TM, tp), tn=ROUTE_W,
                                 emit_h=True, highest=True, h_dtype=F32)
        moe_tm = min(MOE_TM, tp)
        info, counts = _route_tokens(lg_p.reshape(bp * tp, ROUTE_W), tm=moe_tm)
        moe_p = _moe_grouped(h_p.reshape(bp * tp, d), info, *_group_layout(info, counts, moe_tm), w1, w3, w2,
                             tm=moe_tm)
        pending = (moe_p, gtp2)
        lg_s, h_s = _norm_matmul(xs, norm_ffn_g[i], shs2, scs2, w_route, tm=ns, tn=ROUTE_W,
                                 emit_h=True, highest=True)
        xs = _moe_dense(h_s, lg_s, w1, w3, w2, xs, gts2, tm=ns)

    y_prompt = _rmsnorm(xp, final_norm_g, tm=min(MOE_TM, tp), residual=pending)
    y_sample = _rmsnorm(xs, final_norm_g, tm=ns).reshape(bs, ts, d)
    stack = lambda n: jnp.stack(outs[n])
    return (y_prompt, y_sample, stack("lru_h_p"), stack("lru_c_p"), stack("lru_h_s"), stack("lru_c_s"),
            stack("dsa_k_p"), stack("dsa_v_p"), stack("dsa_i_p"), stack("dsa_k_s"), stack("dsa_v_s"),
            stack("dsa_i_s"), stack("moba_k_p"), stack("moba_v_p"), stack("moba_k_s"), stack("moba_v_s"))
```

```python
import functools
import math

import numpy as np
import jax
import jax.numpy as jnp
from jax import lax
from jax.experimental import pallas as pl
from jax.experimental.pallas import tpu as pltpu

F32 = jnp.float32
BF16 = jnp.bfloat16

LANES = 128
SUBLANES = 8
VMEM_LIMIT_BYTES = 56 * 1024 * 1024

HEAD_DIM = 128
N_HEADS = 16
N_KV_B = 4
N_IDX_HEADS = 16
IDX_DIM = 64
IDX_TOPK = 256
IDX_SCALE = (IDX_DIM * N_IDX_HEADS) ** -0.5
PAGE_SIZE = 128
MOBA_BLOCK = 256
MOBA_TOPK = 3
N_BUCKETS = 32
MAX_DISTANCE = 128
N_RNN_BLOCKS = 16
CONV_W = 4
LRU_C = 8.0
N_GROUPS = 4
EXPERTS_PER_GROUP = 4
N_EXPERTS = N_GROUPS * EXPERTS_PER_GROUP
EPS = 1e-6
ATTN_SCALE = HEAD_DIM ** -0.5
LOG2E = math.log2(math.e)
QK_SCALE_LOG2 = ATTN_SCALE * LOG2E
NEG_INF = float("-inf")


def _params(*sem):
    return pltpu.CompilerParams(dimension_semantics=sem, vmem_limit_bytes=VMEM_LIMIT_BYTES)


def _bdot(a, b):
    return jnp.dot(a.astype(BF16), b.astype(BF16), preferred_element_type=F32)


def _bdot_nt(a, b):
    return lax.dot_general(a.astype(BF16), b.astype(BF16), (((1,), (1,)), ((), ())),
                           preferred_element_type=F32)


def _ada_kernel(c_ref, w_ref, b_ref, o_ref):
    c = c_ref[...]
    o_ref[...] = _bdot(c * jax.nn.sigmoid(c), w_ref[...]) + b_ref[...]


def _ada(c, ada_w, ada_b):
    depth, d, n = ada_w.shape
    rows = c.shape[0]
    tn = next(w for w in (1024, 512, 256, LANES) if n % w == 0)
    return pl.pallas_call(
        _ada_kernel,
        grid=(depth, n // tn),
        in_specs=[
            pl.BlockSpec((rows, d), lambda l, j: (0, 0)),
            pl.BlockSpec((None, d, tn), lambda l, j: (l, 0, j)),
            pl.BlockSpec((None, 1, tn), lambda l, j: (l, 0, j)),
        ],
        out_specs=pl.BlockSpec((None, rows, tn), lambda l, j: (l, 0, j)),
        out_shape=jax.ShapeDtypeStruct((depth, rows, n), F32),
        compiler_params=_params("arbitrary", "arbitrary"),
        name="ada_mod",
    )(c, ada_w, ada_b.reshape(depth, 1, n))


def _norm_mod(x, g, shift, scale):
    ms = jnp.mean(x * x, axis=-1, keepdims=True)
    h = x * lax.rsqrt(ms + EPS) * g
    return h * (1.0 + scale) + shift


def _norm_mm_kernel(x_ref, g_ref, sh_ref, sc_ref, w_ref, *refs, emit_h, highest):
    if emit_h:
        o_ref, ho_ref, h_ref = refs
    else:
        o_ref, h_ref = refs
        ho_ref = None

    @pl.when(pl.program_id(2) == 0)
    def _():
        h = _norm_mod(x_ref[...], g_ref[...], sh_ref[...], sc_ref[...])
        h_ref[...] = h.astype(h_ref.dtype)
        if ho_ref is not None:
            ho_ref[...] = h.astype(ho_ref.dtype)

    if highest:
        o_ref[...] = jnp.dot(h_ref[...], w_ref[...], precision=lax.Precision.HIGHEST,
                             preferred_element_type=F32)
    else:
        o_ref[...] = _bdot(h_ref[...], w_ref[...])


def _norm_matmul(x, g, shift, scale, w, *, tm, tn, emit_h=False, highest=False, h_dtype=BF16):
    bsz, t, d = x.shape
    n = w.shape[1]
    r = shift.shape[1]
    assert t % tm == 0 and n % tn == 0 and (r == 1 or r == tm == t)
    out_shape = [jax.ShapeDtypeStruct((bsz, t, n), F32)]
    out_specs = [pl.BlockSpec((None, tm, tn), lambda b, i, j: (b, i, j))]
    row_tile = pl.BlockSpec((None, tm, d), lambda b, i, j: (b, i, 0))
    if emit_h:
        out_shape.append(jax.ShapeDtypeStruct((bsz, t, d), h_dtype))
        out_specs.append(row_tile)
    mod_map = (lambda b, i, j: (b, 0, 0)) if r == 1 else (lambda b, i, j: (b, i, 0))
    res = pl.pallas_call(
        functools.partial(_norm_mm_kernel, emit_h=emit_h, highest=highest),
        grid=(bsz, t // tm, n // tn),
        in_specs=[
            row_tile,
            pl.BlockSpec((1, d), lambda b, i, j: (0, 0)),
            pl.BlockSpec((None, r, d), mod_map),
            pl.BlockSpec((None, r, d), mod_map),
            pl.BlockSpec((d, tn), lambda b, i, j: (0, j)),
        ],
        out_specs=out_specs,
        out_shape=out_shape,
        scratch_shapes=[pltpu.VMEM((tm, d), F32 if highest else BF16)],
        compiler_params=_params("arbitrary", "arbitrary", "arbitrary"),
        name="norm_matmul",
    )(x, g.reshape(1, d), shift, scale, w)
    return res if emit_h else res[0]


def _mm_res_kernel(a_ref, w_ref, r_ref, g_ref, o_ref):
    o_ref[...] = r_ref[...] + g_ref[...] * _bdot(a_ref[...], w_ref[...])


def _matmul_residual(a, w, res, gate, *, tm, tn):
    bsz, t, k = a.shape
    n = w.shape[1]
    r = gate.shape[1]
    assert t % tm == 0 and n % tn == 0 and (r == 1 or r == tm == t)
    gate_map = (lambda b, i, j: (b, 0, j)) if r == 1 else (lambda b, i, j: (b, i, j))
    return pl.pallas_call(
        _mm_res_kernel,
        grid=(bsz, t // tm, n // tn),
        in_specs=[
            pl.BlockSpec((None, tm, k), lambda b, i, j: (b, i, 0)),
            pl.BlockSpec((k, tn), lambda b, i, j: (0, j)),
            pl.BlockSpec((None, tm, tn), lambda b, i, j: (b, i, j)),
            pl.BlockSpec((None, r, tn), gate_map),
        ],
        out_specs=pl.BlockSpec((None, tm, tn), lambda b, i, j: (b, i, j)),
        out_shape=jax.ShapeDtypeStruct((bsz, t, n), F32),
        compiler_params=_params("arbitrary", "arbitrary", "arbitrary"),
        name="matmul_residual",
    )(a, w, res, gate)


def _rmsnorm_kernel(x_ref, g_ref, *refs, residual):
    x = x_ref[...]
    if residual:
        y_ref, gy_ref, o_ref = refs
        x = x + gy_ref[...] * y_ref[...]
    else:
        o_ref, = refs
    ms = jnp.mean(x * x, axis=-1, keepdims=True)
    o_ref[...] = x * lax.rsqrt(ms + EPS) * g_ref[...]


def _rmsnorm(x, g, *, tm, residual=None):
    bsz, t, d = x.shape
    per_b = t // tm
    extra_in, extra_specs = [], []
    if residual is not None:
        extra_in = list(residual)
        extra_specs = [pl.BlockSpec((tm, d), lambda b, i: (b * per_b + i, 0)),
                       pl.BlockSpec((None, 1, d), lambda b, i: (b, 0, 0))]
    return pl.pallas_call(
        functools.partial(_rmsnorm_kernel, residual=residual is not None),
        grid=(bsz, t // tm),
        in_specs=[pl.BlockSpec((None, tm, d), lambda b, i: (b, i, 0)),
                  pl.BlockSpec((1, d), lambda b, i: (0, 0)), *extra_specs],
        out_specs=pl.BlockSpec((None, tm, d), lambda b, i: (b, i, 0)),
        out_shape=jax.ShapeDtypeStruct((bsz, t, d), F32),
        compiler_params=_params("arbitrary", "arbitrary"),
        name="final_rmsnorm",
    )(x, g.reshape(1, d), *extra_in)


GATE_CHUNK = 640


def _log_sigmoid(x):
    return jnp.minimum(x, 0.0) - jnp.log1p(jnp.exp(-jnp.abs(x)))


def _lru_kernel(gb_ref, xb_ref, cs_ref, h0_ref, cw_ref, cb_ref, wr_ref, br_ref, wi_ref, bi_ref, lam_ref,
                y_ref, hl_ref, nc_ref, ext_ref, a_ref, b_ref, hs_ref, carry_ref, *, tt):
    c_all = xb_ref.shape[-1]

    @pl.when(pl.program_id(1) == 0)
    def _():
        ext_ref[0:SUBLANES, :] = cs_ref[...]
        carry_ref[...] = h0_ref[...]

    xb = xb_ref[...]
    ext_ref[SUBLANES:SUBLANES + tt, :] = xb
    cw = cw_ref[...]
    xc = cb_ref[...] + xb * cw[CONV_W - 1:CONV_W, :]
    for j in range(1, CONV_W):
        xc = xc + ext_ref[pl.ds(SUBLANES - j, tt), :] * cw[CONV_W - 1 - j:CONV_W - j, :]
    tail = ext_ref[tt:tt + SUBLANES, :]
    ext_ref[0:SUBLANES, :] = tail
    nc_ref[...] = tail

    xcb = xc.astype(BF16)
    r_parts, i_parts = [], []
    for c in range(c_all // GATE_CHUNK):
        sl = slice(c * GATE_CHUNK, (c + 1) * GATE_CHUNK)
        r_parts.append(jnp.dot(xcb[:, sl], wr_ref[c], preferred_element_type=F32))
        i_parts.append(jnp.dot(xcb[:, sl], wi_ref[c], preferred_element_type=F32))
    r = jax.nn.sigmoid(jnp.concatenate(r_parts, axis=1) + br_ref[...])
    ig = jax.nn.sigmoid(jnp.concatenate(i_parts, axis=1) + bi_ref[...])
    log_a = LRU_C * r * _log_sigmoid(lam_ref[...])
    th = jnp.tanh(log_a)
    one_minus_a2 = -2.0 * th / (1.0 - th)
    a_ref[...] = jnp.exp(log_a)
    b_ref[...] = jnp.sqrt(jnp.maximum(one_minus_a2, 0.0)) * (ig * xc)

    row = lax.broadcasted_iota(jnp.int32, (SUBLANES, GATE_CHUNK), 0)
    for c in range(c_all // GATE_CHUNK):
        sl = slice(c * GATE_CHUNK, (c + 1) * GATE_CHUNK)

        def body(grp, carry, sl=sl):
            r8 = pl.multiple_of(grp * SUBLANES, SUBLANES)
            av = a_ref[pl.ds(r8, SUBLANES), sl]
            bv = b_ref[pl.ds(r8, SUBLANES), sl]
            for s in (1, 2, 4):
                m = row >= s
                a_sh = jnp.where(m, pltpu.roll(av, s, axis=0), 1.0)
                b_sh = jnp.where(m, pltpu.roll(bv, s, axis=0), 0.0)
                bv = av * b_sh + bv
                av = av * a_sh
            h = av * carry + bv
            hs_ref[pl.ds(r8, SUBLANES), sl] = h
            return h[SUBLANES - 1:SUBLANES, :]

        carry_ref[:, sl] = lax.fori_loop(0, tt // SUBLANES, body, carry_ref[:, sl])

    hl_ref[...] = carry_ref[...]
    y_ref[...] = (jax.nn.gelu(gb_ref[...]) * hs_ref[...]).astype(y_ref.dtype)


def _lru_core(u, conv_state, h0, conv_w, conv_b, w_r, b_r, w_i, b_i, lam, *, tt):
    bsz, t, c2 = u.shape
    c = c2 // 2
    blk = c // N_RNN_BLOCKS
    per = GATE_CHUNK // blk
    nchunk = c // GATE_CHUNK

    def block_diag(w):
        wc = w.reshape(nchunk, per, blk, blk).astype(BF16)
        eye = jnp.eye(per, dtype=BF16)
        return jnp.einsum("cpij,pq->cpiqj", wc, eye).reshape(nchunk, GATE_CHUNK, GATE_CHUNK)

    cs_pad = jnp.pad(conv_state, ((0, 0), (SUBLANES - (CONV_W - 1), 0), (0, 0)))
    row = lambda v: v.reshape(1, c)
    full = lambda shp: pl.BlockSpec(shp, lambda b, i: (0,) * len(shp))
    y, hl, nc = pl.pallas_call(
        functools.partial(_lru_kernel, tt=tt),
        grid=(bsz, t // tt),
        in_specs=[
            pl.BlockSpec((None, tt, c), lambda b, i: (b, i, 0)),
            pl.BlockSpec((None, tt, c), lambda b, i: (b, i, 1)),
            pl.BlockSpec((None, SUBLANES, c), lambda b, i: (b, 0, 0)),
            pl.BlockSpec((None, 1, c), lambda b, i: (b, 0, 0)),
            full((CONV_W, c)), full((1, c)),
            full((nchunk, GATE_CHUNK, GATE_CHUNK)), full((1, c)),
            full((nchunk, GATE_CHUNK, GATE_CHUNK)), full((1, c)),
            full((1, c)),
        ],
        out_specs=[
            pl.BlockSpec((None, tt, c), lambda b, i: (b, i, 0)),
            pl.BlockSpec((None, 1, c), lambda b, i: (b, 0, 0)),
            pl.BlockSpec((None, SUBLANES, c), lambda b, i: (b, 0, 0)),
        ],
        out_shape=[
            jax.ShapeDtypeStruct((bsz, t, c), BF16),
            jax.ShapeDtypeStruct((bsz, 1, c), F32),
            jax.ShapeDtypeStruct((bsz, SUBLANES, c), F32),
        ],
        scratch_shapes=[
            pltpu.VMEM((tt + SUBLANES, c), F32),
            pltpu.VMEM((tt, c), F32),
            pltpu.VMEM((tt, c), F32),
            pltpu.VMEM((tt, c), F32),
            pltpu.VMEM((1, c), F32),
        ],
        compiler_params=_params("arbitrary", "arbitrary"),
        name="lru_core",
    )(u, u, cs_pad, h0.reshape(bsz, 1, c), conv_w, row(conv_b), block_diag(w_r), row(b_r),
      block_diag(w_i), row(b_i), row(lam))
    return y, hl.reshape(bsz, c), nc[:, SUBLANES - (CONV_W - 1):, :]


def _route_gates(lg):
    return _route(lg)[0]


def _route(lg):
    col = lax.broadcasted_iota(jnp.int32, lg.shape, 1)
    big = jnp.int32(LANES)
    is_g = col < N_GROUPS
    gl = jnp.where(is_g, lg, NEG_INF)
    gmax = jnp.max(gl, axis=1, keepdims=True)
    g_sel = jnp.min(jnp.where(gl == gmax, col, big), axis=1, keepdims=True)
    g_w = 1.0 / jnp.sum(jnp.exp(gl - gmax), axis=1, keepdims=True)
    eid = col - N_GROUPS
    in_grp = (eid >= 0) & (eid < N_EXPERTS) & ((eid // EXPERTS_PER_GROUP) == g_sel)
    e_in = jnp.where(in_grp, lg, NEG_INF)
    top1 = jnp.max(e_in, axis=1, keepdims=True)
    idx1 = jnp.min(jnp.where((e_in == top1) & in_grp, col, big), axis=1, keepdims=True)
    rest = in_grp & (col != idx1)
    e2 = jnp.where(rest, lg, NEG_INF)
    top2 = jnp.max(e2, axis=1, keepdims=True)
    idx2 = jnp.min(jnp.where((e2 == top2) & rest, col, big), axis=1, keepdims=True)
    z = jnp.exp(top2 - top1)
    w_first = g_w / (1.0 + z)
    w_second = g_w * z / (1.0 + z)
    return jnp.where(col == idx1, w_first, 0.0) + jnp.where(col == idx2, w_second, 0.0), g_sel


def _moe_dense_kernel(h_ref, lg_ref, w1_ref, w3_ref, w2_ref, x_ref, g_ref, o_ref, gates_ref, acc_ref):
    e = pl.program_id(2)

    @pl.when(e == 0)
    def _():
        gates_ref[...] = _route_gates(lg_ref[...])
        acc_ref[...] = jnp.zeros_like(acc_ref)

    gates = gates_ref[...]
    col = lax.broadcasted_iota(jnp.int32, gates.shape, 1)
    ge = jnp.sum(jnp.where(col == e + N_GROUPS, gates, 0.0), axis=1, keepdims=True)
    h = h_ref[...]
    a1 = jnp.dot(h, w1_ref[...], preferred_element_type=F32)
    a3 = jnp.dot(h, w3_ref[...], preferred_element_type=F32)
    hidden = (a1 * jax.nn.sigmoid(a1)) * a3 * ge
    acc_ref[...] += jnp.dot(hidden.astype(BF16), w2_ref[...], preferred_element_type=F32)

    @pl.when(e == pl.num_programs(2) - 1)
    def _():
        o_ref[...] = x_ref[...] + g_ref[...] * acc_ref[...]


def _moe_dense(h, logits, w1, w3, w2, x, gate, *, tm):
    bsz, t, d = x.shape
    ne, _, f = w1.shape
    r = gate.shape[1]
    gate_map = (lambda b, i, e: (b, 0, 0)) if r == 1 else (lambda b, i, e: (b, i, 0))
    tok = lambda b, i, e: (b, i, 0)
    return pl.pallas_call(
        _moe_dense_kernel,
        grid=(bsz, t // tm, ne),
        in_specs=[
            pl.BlockSpec((None, tm, d), tok),
            pl.BlockSpec((None, tm, LANES), tok),
            pl.BlockSpec((None, d, f), lambda b, i, e: (e, 0, 0)),
            pl.BlockSpec((None, d, f), lambda b, i, e: (e, 0, 0)),
            pl.BlockSpec((None, f, d), lambda b, i, e: (e, 0, 0)),
            pl.BlockSpec((None, tm, d), tok),
            pl.BlockSpec((None, r, d), gate_map),
        ],
        out_specs=pl.BlockSpec((None, tm, d), tok),
        out_shape=jax.ShapeDtypeStruct((bsz, t, d), F32),
        scratch_shapes=[pltpu.VMEM((tm, LANES), F32), pltpu.VMEM((tm, d), F32)],
        compiler_params=_params("arbitrary", "arbitrary", "arbitrary"),
        name="moe_dense",
    )(h, logits, w1, w3, w2, x, gate)


INFO_GROUP_LANE = 0
INFO_RANK_LANE = 1


def _route_kernel(lg_ref, info_ref, cnt_ref, carry_ref):
    @pl.when(pl.program_id(0) == 0)
    def _():
        carry_ref[...] = jnp.zeros_like(carry_ref)

    gates, g_sel = _route(lg_ref[...])
    tm = gates.shape[0]
    col = lax.broadcasted_iota(jnp.int32, gates.shape, 1)
    onehot = (col == g_sel).astype(F32)
    before = (lax.broadcasted_iota(jnp.int32, (tm, tm), 0) > lax.broadcasted_iota(jnp.int32, (tm, tm), 1))
    earlier = jnp.dot(before.astype(BF16), onehot.astype(BF16), preferred_element_type=F32) + carry_ref[...]
    rank = jnp.sum(jnp.where(col == g_sel, earlier, 0.0), axis=1, keepdims=True)
    carry_ref[...] += jnp.sum(onehot, axis=0, keepdims=True)
    info_ref[...] = (gates + jnp.where(col == INFO_GROUP_LANE, g_sel.astype(F32), 0.0)
                     + jnp.where(col == INFO_RANK_LANE, rank, 0.0))
    cnt_ref[...] = carry_ref[...]


def _route_tokens(logits, *, tm):
    n = logits.shape[0]
    assert n % tm == 0
    return pl.pallas_call(
        _route_kernel,
        grid=(n // tm,),
        in_specs=[pl.BlockSpec((tm, LANES), lambda i: (i, 0))],
        out_specs=[pl.BlockSpec((tm, LANES), lambda i: (i, 0)), pl.BlockSpec((1, LANES), lambda i: (0, 0))],
        out_shape=[jax.ShapeDtypeStruct((n, LANES), F32), jax.ShapeDtypeStruct((1, LANES), F32)],
        scratch_shapes=[pltpu.VMEM((1, LANES), F32)],
        compiler_params=_params("arbitrary"),
        name="moe_route",
    )(logits)


def _group_layout(info, counts, tm):
    n = info.shape[0]
    g_sel = info[:, INFO_GROUP_LANE].astype(jnp.int32)
    rank = info[:, INFO_RANK_LANE].astype(jnp.int32)
    cnt = counts[0, :N_GROUPS].astype(jnp.int32)
    padded = -(-cnt // tm) * tm
    ends = jnp.cumsum(padded)
    starts = ends - padded
    n_slots = n + N_GROUPS * tm
    src = _slot_tokens(starts[g_sel] + rank, n_slots)
    tile_start = jnp.arange(n_slots // tm, dtype=jnp.int32) * tm
    tile_group = jnp.minimum(jnp.sum((tile_start[:, None] >= ends[None, :]).astype(jnp.int32), axis=1), N_GROUPS - 1)
    slot_group = jnp.repeat(tile_group, tm)
    is_pad = jnp.arange(n_slots, dtype=jnp.int32) - starts[slot_group] >= cnt[slot_group]
    pad_row = n + tm + jnp.cumsum(is_pad.astype(jnp.int32)) - 1
    dst = jnp.concatenate([n + jnp.arange(tm, dtype=jnp.int32), jnp.where(is_pad, pad_row, src)])
    return src, dst, tile_group, (ends[-1] // tm).reshape(1)


SLOT_UNROLL = 8


def _slot_tokens_kernel(slot_ref, src_ref):
    n, n_slots = slot_ref.shape[0], src_ref.shape[0]

    def clear(c, carry):
        for k in range(SLOT_UNROLL):
            src_ref[c * SLOT_UNROLL + k] = 0
        return carry

    def place(c, carry):
        for k in range(SLOT_UNROLL):
            t = c * SLOT_UNROLL + k
            src_ref[slot_ref[t]] = t
        return carry

    lax.fori_loop(0, n_slots // SLOT_UNROLL, clear, 0)
    lax.fori_loop(0, n // SLOT_UNROLL, place, 0)


def _slot_tokens(slot, n_slots):
    assert slot.shape[0] % SLOT_UNROLL == 0 and n_slots % SLOT_UNROLL == 0
    return pl.pallas_call(
        _slot_tokens_kernel,
        in_specs=[pl.BlockSpec(memory_space=pltpu.SMEM)],
        out_specs=pl.BlockSpec(memory_space=pltpu.SMEM),
        out_shape=jax.ShapeDtypeStruct((n_slots,), jnp.int32),
        name="moe_slot_tokens",
    )(slot)


def _moe_group_kernel(src_ref, dst_ref, tg_ref, used_ref, h_hbm, info_hbm, w1_ref, w3_ref, w2_ref, zeros_hbm,
                      out_hbm, xbuf, gbuf, xb, gates, acc, obuf, sem_x, sem_g, sem_o, *, tm):
    del zeros_hbm
    i = pl.program_id(0)
    e = pl.program_id(1)
    used = used_ref[0]
    live = i < used
    chunk = tm // EXPERTS_PER_GROUP
    first = e * chunk
    last_step = e == EXPERTS_PER_GROUP - 1

    def gather_row(tile, r):
        tok = src_ref[tile * tm + r]
        pltpu.make_async_copy(h_hbm.at[pl.ds(tok, 1), :], xbuf.at[pl.ds(r, 1), :], sem_x).start()
        pltpu.make_async_copy(info_hbm.at[pl.ds(tok, 1), :], gbuf.at[pl.ds(r, 1), :], sem_g).start()

    def scatter_chunk(tile):
        slot = (tile + 2) % 2
        for k in range(chunk):
            row = dst_ref[(tile + 1) * tm + first + k]
            pltpu.make_async_copy(obuf.at[slot, pl.ds(first + k, 1), :], out_hbm.at[pl.ds(row, 1), :],
                                  sem_o.at[slot]).start()

    def wait_scatter(slot):
        pltpu.make_async_copy(obuf.at[slot], out_hbm.at[pl.ds(0, tm), :], sem_o.at[slot]).wait()

    @pl.when((i == 0) & (e == 0))
    def _():
        obuf[...] = jnp.zeros_like(obuf)

        def body(r, carry):
            gather_row(0, r)
            return carry
        lax.fori_loop(0, tm, body, 0)

    @pl.when((e == 0) & (i <= used))
    def _():
        pltpu.make_async_copy(h_hbm.at[pl.ds(0, tm), :], xbuf, sem_x).wait()
        pltpu.make_async_copy(info_hbm.at[pl.ds(0, tm), :], gbuf, sem_g).wait()

    @pl.when((e == 0) & live)
    def _():
        xb[...] = xbuf[...].astype(BF16)
        gates[...] = gbuf[...]
        acc[...] = jnp.zeros_like(acc)

    @pl.when(live)
    def _():
        lane = lax.broadcasted_iota(jnp.int32, (tm, LANES), 1)
        gate_lane = N_GROUPS + EXPERTS_PER_GROUP * tg_ref[i] + e
        ge = jnp.sum(jnp.where(lane == gate_lane, gates[...], 0.0), axis=1, keepdims=True)
        x = xb[...]
        a1 = jnp.dot(x, w1_ref[...], preferred_element_type=F32)
        a3 = jnp.dot(x, w3_ref[...], preferred_element_type=F32)
        hidden = (a1 * jax.nn.sigmoid(a1)) * a3 * ge
        acc[...] += jnp.dot(hidden.astype(BF16), w2_ref[...], preferred_element_type=F32)
        for k in range(chunk):
            gather_row(i + 1, first + k)
        scatter_chunk(i - 1)

    @pl.when(i == used)
    def _():
        scatter_chunk(i - 1)

    @pl.when(last_step & live & (i >= 1))
    def _():
        wait_scatter(i % 2)

    @pl.when(last_step & live)
    def _():
        obuf[i % 2] = acc[...]

    @pl.when(last_step & (i == used))
    def _():
        wait_scatter(0)
        wait_scatter(1)


def _moe_grouped(h, info, src, dst, tile_group, n_used, w1, w3, w2, *, tm):
    n, d = h.shape
    f = w1.shape[2]
    n_tiles = src.shape[0] // tm
    n_rows = n + (N_GROUPS + 1) * tm
    assert tm % EXPERTS_PER_GROUP == 0 and dst.shape[0] == (n_tiles + 1) * tm

    def expert(i, e, src_, dst_, tg, used):
        e_eff = jnp.where(i < used[0], e, EXPERTS_PER_GROUP - 1)
        return (tg[i] * EXPERTS_PER_GROUP + e_eff, 0, 0)

    return pl.pallas_call(
        functools.partial(_moe_group_kernel, tm=tm),
        grid_spec=pltpu.PrefetchScalarGridSpec(
            num_scalar_prefetch=4,
            grid=(n_tiles, EXPERTS_PER_GROUP),
            in_specs=[
                pl.BlockSpec(memory_space=pl.ANY),
                pl.BlockSpec(memory_space=pl.ANY),
                pl.BlockSpec((None, d, f), expert),
                pl.BlockSpec((None, d, f), expert),
                pl.BlockSpec((None, f, d), expert),
                pl.BlockSpec(memory_space=pl.ANY),
            ],
            out_specs=pl.BlockSpec(memory_space=pl.ANY),
            scratch_shapes=[
                pltpu.VMEM((tm, d), F32), pltpu.VMEM((tm, LANES), F32), pltpu.VMEM((tm, d), BF16),
                pltpu.VMEM((tm, LANES), F32), pltpu.VMEM((tm, d), F32), pltpu.VMEM((2, tm, d), F32),
                pltpu.SemaphoreType.DMA(()), pltpu.SemaphoreType.DMA(()), pltpu.SemaphoreType.DMA((2,)),
            ],
        ),
        out_shape=jax.ShapeDtypeStruct((n_rows, d), F32),
        input_output_aliases={9: 0},
        compiler_params=_params("arbitrary", "arbitrary"),
        name="moe_grouped",
    )(src, dst, tile_group, n_used, h, info, w1, w3, w2, jnp.zeros((n_rows, d), F32))


def _gated_add_kernel(x_ref, y_ref, g_ref, o_ref):
    o_ref[...] = x_ref[...] + g_ref[...] * y_ref[...]


def _gated_add(x, y, gate, *, tm):
    bsz, t, d = x.shape
    per_b = t // tm
    return pl.pallas_call(
        _gated_add_kernel,
        grid=(bsz, per_b),
        in_specs=[
            pl.BlockSpec((None, tm, d), lambda b, i: (b, i, 0)),
            pl.BlockSpec((tm, d), lambda b, i: (b * per_b + i, 0)),
            pl.BlockSpec((None, 1, d), lambda b, i: (b, 0, 0)),
        ],
        out_specs=pl.BlockSpec((None, tm, d), lambda b, i: (b, i, 0)),
        out_shape=jax.ShapeDtypeStruct((bsz, t, d), F32),
        compiler_params=_params("arbitrary", "arbitrary"),
        name="gated_add",
    )(x, y, gate)


MASKED = -1e30


def _t5_bucket_table(tq):
    qi = np.arange(tq, dtype=np.int32)[:, None]
    col = np.arange(2 * tq, dtype=np.int32)[None, :]
    n = np.maximum(qi + tq - col, 0)
    max_exact = N_BUCKETS // 2
    nf = np.maximum(n, 1).astype(np.float32)
    large = max_exact + (np.log(nf / np.float32(max_exact)) / np.float32(math.log(MAX_DISTANCE / max_exact))
                         * np.float32(N_BUCKETS - max_exact)).astype(np.int32)
    table = np.where(n < max_exact, n, np.minimum(large, N_BUCKETS - 1)).astype(np.int32)
    assert table[0, 0] == N_BUCKETS - 1
    return table


def _bias_kernel(rb_ref, bk_ref, o_ref):
    h = pl.program_id(0)
    bk = bk_ref[...]
    acc = jnp.zeros(bk.shape, F32)
    for k in range(N_BUCKETS):
        acc = jnp.where(bk == k, rb_ref[k, h], acc)
    o_ref[...] = (acc - rb_ref[N_BUCKETS - 1, h]) * LOG2E


def _near_bias(rel_bias, tq):
    return pl.pallas_call(
        _bias_kernel,
        grid=(N_HEADS,),
        in_specs=[pl.BlockSpec(memory_space=pltpu.SMEM),
                  pl.BlockSpec((tq, 2 * tq), lambda h: (0, 0))],
        out_specs=pl.BlockSpec((None, tq, 2 * tq), lambda h: (h, 0, 0)),
        out_shape=jax.ShapeDtypeStruct((N_HEADS, tq, 2 * tq), F32),
        name="near_bias",
    )(rel_bias, jnp.asarray(_t5_bucket_table(tq)))


_NEG_INF_KEY = int(np.int32(np.array(-np.inf, np.float32).view(np.int32)) ^ np.int32(0x7FFFFFFF))


def _sort_key(x):
    k = pltpu.bitcast(x, jnp.int32)
    return k ^ ((k >> 31) & jnp.int32(0x7FFFFFFF))


def _kth_largest_key(count_ge, kth):
    zero = jnp.int32(0)
    first = count_ge(zero)
    ans = jnp.where(first >= kth, zero, jnp.int32(-2 ** 31))
    settled = (first == kth).astype(jnp.int32)

    def cond(state):
        it, _, settled = state
        return (it < 31) & (jnp.min(settled) == 0)

    def body(state):
        it, ans, settled = state
        cand = ans | (jnp.int32(1) << (30 - it))
        count = count_ge(cand)
        return it + 1, jnp.where(count >= kth, cand, ans), settled | (count == kth).astype(jnp.int32)

    return lax.while_loop(cond, body, (jnp.int32(0), ans, settled))[1]


def _dsa_select_kernel(qi_ref, kw_ref, kall_ref, m_ref, key_ref, *, tq, tk, topk):
    i = pl.program_id(1)
    n_kt = (i * tq + tq + tk - 1) // tk
    qpos = i * tq + lax.broadcasted_iota(jnp.int32, (tq, tk), 0)
    wi = kw_ref[:, IDX_DIM:IDX_DIM + N_IDX_HEADS]
    qi = qi_ref[...].astype(BF16)
    key_ref[...] = jnp.full(key_ref.shape, _NEG_INF_KEY, jnp.int32)

    def score_tile(kt, carry):
        k0 = pl.multiple_of(kt * tk, tk)
        kid = kall_ref[pl.ds(k0, tk), 0:IDX_DIM].astype(BF16)
        acc = jnp.zeros((tq, tk), F32)
        for h in range(N_IDX_HEADS):
            s = _bdot_nt(qi[:, h * IDX_DIM:(h + 1) * IDX_DIM], kid)
            acc = acc + wi[:, h:h + 1] * jnp.maximum(s, 0.0)
        kpos = k0 + lax.broadcasted_iota(jnp.int32, (tq, tk), 1)
        sc = jnp.where(kpos <= qpos, acc * IDX_SCALE, NEG_INF)
        key_ref[:, pl.ds(k0, tk)] = _sort_key(sc)
        return carry

    lax.fori_loop(0, n_kt, score_tile, 0)

    def count_ge(cand):
        def body(kt, acc):
            k0 = pl.multiple_of(kt * tk, tk)
            ge = (key_ref[:, pl.ds(k0, tk)] >= cand).astype(jnp.int32)
            for j in range(tk // LANES):
                acc = acc + ge[:, j * LANES:(j + 1) * LANES]
            return acc
        acc = lax.fori_loop(0, n_kt, body, jnp.zeros((tq, LANES), jnp.int32))
        return jnp.sum(acc, axis=1, keepdims=True)

    thr = _kth_largest_key(count_ge, topk)
    keys = key_ref[...]
    m_ref[...] = ((keys >= thr) & (keys > _NEG_INF_KEY)).astype(m_ref.dtype)


def _dsa_select(proj, *, tq, tk, topk):
    bsz, t, _ = proj.shape
    qi_w = N_IDX_HEADS * IDX_DIM
    qi_blk = (N_HEADS * HEAD_DIM + 2 * N_KV_B * HEAD_DIM) // qi_w
    kw_blk = (N_HEADS * HEAD_DIM + 2 * N_KV_B * HEAD_DIM + qi_w) // LANES
    return pl.pallas_call(
        functools.partial(_dsa_select_kernel, tq=tq, tk=tk, topk=topk),
        grid=(bsz, t // tq),
        in_specs=[
            pl.BlockSpec((None, tq, qi_w), lambda b, i: (b, i, qi_blk)),
            pl.BlockSpec((None, tq, LANES), lambda b, i: (b, i, kw_blk)),
            pl.BlockSpec((None, t, LANES), lambda b, i: (b, 0, kw_blk)),
        ],
        out_specs=pl.BlockSpec((None, tq, t), lambda b, i: (b, i, 0)),
        out_shape=jax.ShapeDtypeStruct((bsz, t, t), BF16),
        scratch_shapes=[pltpu.VMEM((tq, t), jnp.int32)],
        compiler_params=_params("arbitrary", "arbitrary"),
        name="dsa_select",
    )(proj, proj, proj)


def _attn_kernel(q_ref, k_ref, v_ref, m_ref, tb_ref, o_ref, *state, tq, rep, shared_kv, block_mask):
    m_sc, l_sc, acc_sc = state[:rep], state[rep:2 * rep], state[2 * rep:]
    i = pl.program_id(2)
    for r in range(rep):
        m_sc[r][...] = jnp.full((tq, LANES), MASKED, F32)
        l_sc[r][...] = jnp.zeros((tq, LANES), F32)
        acc_sc[r][...] = jnp.zeros((tq, HEAD_DIM), F32)
    lane_tiles = tq // LANES

    def row_reduce(x, op, reduce):
        part = x[:, :LANES]
        for c in range(1, lane_tiles):
            part = op(part, x[:, c * LANES:(c + 1) * LANES])
        return reduce(part, axis=1, keepdims=True)

    def key_mask(j, r, causal):
        if block_mask:
            lane = lax.broadcasted_iota(jnp.int32, (tq, LANES), 1)
            vis = jnp.sum(jnp.where(lane == j, m_ref[r].astype(F32), 0.0), axis=1, keepdims=True)
            ok = jnp.broadcast_to(vis > 0.0, (tq, tq))
        else:
            ok = m_ref[:, pl.ds(pl.multiple_of(j * tq, tq), tq)] > 0
        if causal:
            ok = ok & (lax.broadcasted_iota(jnp.int32, (tq, tq), 0) >= lax.broadcasted_iota(jnp.int32, (tq, tq), 1))
        return ok

    def step(j, bias_lo, causal):
        k0 = pl.multiple_of(j * tq, tq)
        kv = lambda r: slice(0, HEAD_DIM) if shared_kv else slice(r * HEAD_DIM, (r + 1) * HEAD_DIM)
        ok_shared = None if block_mask else key_mask(j, 0, causal)
        scores = []
        for r in range(rep):
            q = (q_ref[:, r * HEAD_DIM:(r + 1) * HEAD_DIM] * QK_SCALE_LOG2).astype(BF16)
            s = _bdot_nt(q, k_ref[pl.ds(k0, tq), kv(r)])
            if bias_lo is not None:
                s = s + tb_ref[r, :, bias_lo:bias_lo + tq]
            scores.append(jnp.where(key_mask(j, r, causal) if block_mask else ok_shared, s, MASKED))
        probs, alphas = [], []
        for r in range(rep):
            m_old = m_sc[r][...]
            m_new = jnp.maximum(m_old, row_reduce(scores[r], jnp.maximum, jnp.max))
            alpha = jnp.exp2(m_old - m_new)
            p = jnp.exp2(scores[r] - jnp.concatenate([m_new] * lane_tiles, axis=1))
            l_sc[r][...] = alpha * l_sc[r][...] + row_reduce(p, jnp.add, jnp.sum)
            m_sc[r][...] = m_new
            probs.append(p.astype(BF16))
            alphas.append(alpha)
        for r in range(rep):
            pv = jnp.dot(probs[r], v_ref[pl.ds(k0, tq), kv(r)].astype(BF16), preferred_element_type=F32)
            acc_sc[r][...] = alphas[r] * acc_sc[r][...] + pv

    def far(j, carry):
        step(j, None, False)
        return carry

    lax.fori_loop(0, jnp.maximum(i - 1, 0), far, 0)

    @pl.when(i >= 1)
    def _():
        step(i - 1, 0, False)

    step(i, tq, True)
    o_ref[...] = jnp.concatenate([acc_sc[r][...] / l_sc[r][...] for r in range(rep)], axis=1).astype(o_ref.dtype)


def _attention(proj, k_col, v_col, mask, near_bias, *, tq, rep, shared_kv, block_mask):
    bsz, t, _ = proj.shape
    g = N_HEADS // rep
    kv_w = HEAD_DIM if shared_kv else rep * HEAD_DIM
    assert HEAD_DIM == LANES and k_col % kv_w == 0 and v_col % kv_w == 0
    k_blk, v_blk = k_col // kv_w, v_col // kv_w
    if block_mask:
        assert tq == MOBA_BLOCK
        mask_spec = pl.BlockSpec((None, rep, tq, LANES), lambda b, h, i: (b, h, i, 0))
    else:
        mask_spec = pl.BlockSpec((None, tq, t), lambda b, h, i: (b, i, 0))
    stat = pltpu.VMEM((tq, LANES), F32)
    return pl.pallas_call(
        functools.partial(_attn_kernel, tq=tq, rep=rep, shared_kv=shared_kv, block_mask=block_mask),
        grid=(bsz, g, t // tq),
        in_specs=[
            pl.BlockSpec((None, tq, rep * HEAD_DIM), lambda b, h, i: (b, i, h)),
            pl.BlockSpec((None, t, kv_w), lambda b, h, i: (b, 0, k_blk + h)),
            pl.BlockSpec((None, t, kv_w), lambda b, h, i: (b, 0, v_blk + h)),
            mask_spec,
            pl.BlockSpec((rep, tq, 2 * tq), lambda b, h, i: (h, 0, 0)),
        ],
        out_specs=pl.BlockSpec((None, tq, rep * HEAD_DIM), lambda b, h, i: (b, i, h)),
        out_shape=jax.ShapeDtypeStruct((bsz, t, N_HEADS * HEAD_DIM), BF16),
        scratch_shapes=[stat] * (3 * rep),
        compiler_params=_params("arbitrary", "arbitrary", "arbitrary"),
        name="masked_attention",
    )(proj, proj, proj, mask, near_bias)


PAGES_PER_BLOCK = MOBA_BLOCK // PAGE_SIZE
MEANS_BLOCKS_PER_STEP = 2


def _block_means_kernel(tbl_ref, *refs):
    del tbl_ref
    page_refs, o_ref = refs[:-1], refs[-1]
    for blk in range(MEANS_BLOCKS_PER_STEP):
        pages = page_refs[blk * PAGES_PER_BLOCK:(blk + 1) * PAGES_PER_BLOCK]
        total = jnp.sum(pages[0][...], axis=0)
        for ref in pages[1:]:
            total = total + jnp.sum(ref[...], axis=0)
        o_ref[blk] = total / MOBA_BLOCK


def _block_means(pool, layer, table):
    _, _, _, h, dh = pool.shape
    bsz, npg = table.shape
    nb = npg // PAGES_PER_BLOCK
    per_step = MEANS_BLOCKS_PER_STEP * PAGES_PER_BLOCK
    assert npg % per_step == 0
    page = lambda k: pl.BlockSpec((None, None, PAGE_SIZE, h, dh),
                                  lambda b, n, tbl: (layer, tbl[b, per_step * n + k], 0, 0, 0))
    out = pl.pallas_call(
        _block_means_kernel,
        grid_spec=pltpu.PrefetchScalarGridSpec(
            num_scalar_prefetch=1,
            grid=(bsz, nb // MEANS_BLOCKS_PER_STEP),
            in_specs=[page(k) for k in range(per_step)],
            out_specs=pl.BlockSpec((None, MEANS_BLOCKS_PER_STEP, h, dh), lambda b, n, tbl: (b, n, 0, 0)),
        ),
        out_shape=jax.ShapeDtypeStruct((bsz, nb, h, dh), F32),
        compiler_params=_params("arbitrary", "arbitrary"),
        name="block_means",
    )(table, *([pool] * per_step))
    return out.reshape(bsz, nb, h * dh)


def _moba_select_kernel(q_ref, mean_ref, o_ref, *, tq, past_len):
    i = pl.program_id(1)
    own = (past_len + i * tq + lax.broadcasted_iota(jnp.int32, (tq, LANES), 0)) // MOBA_BLOCK
    blk = lax.broadcasted_iota(jnp.int32, (tq, LANES), 1)
    past = blk < own
    for h in range(N_HEADS):
        sl = slice(h * HEAD_DIM, (h + 1) * HEAD_DIM)
        gate = lax.dot_general(q_ref[:, sl], mean_ref[:, sl], (((1,), (1,)), ((), ())),
                               precision=lax.Precision.HIGHEST, preferred_element_type=F32)
        g = jnp.where(past, gate, NEG_INF)
        kth = jnp.max(g, axis=1, keepdims=True)
        for _ in range(MOBA_TOPK - 1):
            kth = jnp.max(jnp.where(g < kth, g, NEG_INF), axis=1, keepdims=True)
        o_ref[h] = ((past & (g >= kth)) | (blk == own)).astype(o_ref.dtype)


def _moba_select(q_src, means, *, tq, past_len):
    bsz, t, _ = q_src.shape
    w = N_HEADS * HEAD_DIM
    return pl.pallas_call(
        functools.partial(_moba_select_kernel, tq=tq, past_len=past_len),
        grid=(bsz, t // tq),
        in_specs=[
            pl.BlockSpec((None, tq, w), lambda b, i: (b, i, 0)),
            pl.BlockSpec((None, LANES, w), lambda b, i: (b, 0, 0)),
        ],
        out_specs=pl.BlockSpec((None, N_HEADS, tq, LANES), lambda b, i: (b, 0, i, 0)),
        out_shape=jax.ShapeDtypeStruct((bsz, N_HEADS, t, LANES), BF16),
        compiler_params=_params("arbitrary", "arbitrary"),
        name="moba_select",
    )(q_src, means)


def _dsa_select_paged_kernel(tbl_ref, qi_ref, wi_ref, *rest, pp, topk, past_len, t_new):
    del tbl_ref
    page_refs = rest[:pp]
    knew_ref, m_ref, key_ref = rest[pp:]
    p = pl.program_id(1)
    qi = qi_ref[...].astype(BF16)
    wi = wi_ref[...]

    def score(kpage):
        s = _bdot_nt(qi, kpage)
        w = wi * jnp.maximum(s, 0.0)
        return jnp.sum(w.reshape(N_IDX_HEADS, t_new, PAGE_SIZE), axis=0) * IDX_SCALE

    for k in range(pp):
        off = pl.multiple_of((p * pp + k) * PAGE_SIZE, PAGE_SIZE)
        key_ref[:, pl.ds(off, PAGE_SIZE)] = _sort_key(score(page_refs[k][...]))

    @pl.when(p == pl.num_programs(1) - 1)
    def _():
        row = lax.broadcasted_iota(jnp.int32, (t_new, PAGE_SIZE), 0)
        col = lax.broadcasted_iota(jnp.int32, (t_new, PAGE_SIZE), 1)
        sc = jnp.where(col <= row, score(knew_ref[...]), NEG_INF)
        key_ref[:, past_len:past_len + PAGE_SIZE] = _sort_key(sc)

        def count_ge(cand):
            return jnp.sum((key_ref[...] >= cand).astype(jnp.int32), axis=1, keepdims=True)

        thr = _kth_largest_key(count_ge, topk)
        keys = key_ref[...]
        m_ref[...] = ((keys >= thr) & (keys > _NEG_INF_KEY)).astype(m_ref.dtype)


def _dsa_select_paged(qi_rows, wi_rows, kidx_pool, layer, page_table, kidx_new, *, pp, topk):
    bsz, rows, _ = qi_rows.shape
    t_new = rows // N_IDX_HEADS
    n_pages = page_table.shape[1]
    assert n_pages % pp == 0
    past_len = n_pages * PAGE_SIZE
    lpad = past_len + PAGE_SIZE
    page_spec = lambda k: pl.BlockSpec((None, None, PAGE_SIZE, IDX_DIM),
                                       lambda b, p, tbl: (layer, tbl[b, p * pp + k], 0, 0))
    return pl.pallas_call(
        functools.partial(_dsa_select_paged_kernel, pp=pp, topk=topk, past_len=past_len, t_new=t_new),
        grid_spec=pltpu.PrefetchScalarGridSpec(
            num_scalar_prefetch=1,
            grid=(bsz, n_pages // pp),
            in_specs=[
                pl.BlockSpec((None, rows, IDX_DIM), lambda b, p, tbl: (b, 0, 0)),
                pl.BlockSpec((None, rows, 1), lambda b, p, tbl: (b, 0, 0)),
                *[page_spec(k) for k in range(pp)],
                pl.BlockSpec((None, PAGE_SIZE, IDX_DIM), lambda b, p, tbl: (b, 0, 0)),
            ],
            out_specs=pl.BlockSpec((None, t_new, lpad), lambda b, p, tbl: (b, 0, 0)),
            scratch_shapes=[pltpu.VMEM((t_new, lpad), jnp.int32)],
        ),
        out_shape=jax.ShapeDtypeStruct((bsz, t_new, lpad), BF16),
        compiler_params=_params("arbitrary", "arbitrary"),
        name="dsa_select_paged",
    )(page_table, qi_rows, wi_rows, *([kidx_pool] * pp), kidx_new)


def _col_vector(row_vec):
    n = row_vec.shape[1]
    eye = lax.broadcasted_iota(jnp.int32, (n, n), 0) == lax.broadcasted_iota(jnp.int32, (n, n), 1)
    return jnp.sum(jnp.where(eye, row_vec, 0.0), axis=1, keepdims=True)


def _paged_attn_kernel(tbl_ref, qbd_ref, *rest, pp):
    del tbl_ref
    k_refs, v_refs = rest[:pp], rest[pp:2 * pp]
    mt_ref, mnew_ref, knew_ref, vnew_ref, bias_ref, o_ref, m_sc, l_sc, acc_sc = rest[2 * pp:]
    p = pl.program_id(1)
    last = pl.num_programs(1) - 1

    @pl.when(p == 0)
    def _():
        m_sc[...] = jnp.full(m_sc.shape, MASKED, F32)
        l_sc[...] = jnp.zeros_like(l_sc)
        acc_sc[...] = jnp.zeros_like(acc_sc)

    n_kv = qbd_ref.shape[0] // HEAD_DIM

    def heads_on_lanes(ref):
        return jnp.concatenate([ref[pl.ds(g, PAGE_SIZE, stride=n_kv), :] for g in range(n_kv)], axis=1)

    def pages(kps, vps, masks, biases):
        scores = []
        for kp, mask, bias in zip(kps, masks, biases):
            s = jnp.dot(kp().astype(BF16), qbd_ref[...], preferred_element_type=F32)
            if bias is not None:
                s = s + bias
            scores.append(jnp.where(mask > 0, s, MASKED))
        m_old = m_sc[...]
        m_new = m_old
        for s in scores:
            m_new = jnp.maximum(m_new, jnp.max(s, axis=0, keepdims=True))
        alpha = jnp.exp2(m_old - m_new)
        l_new = alpha * l_sc[...]
        pv = None
        for s, vp in zip(scores, vps):
            pt = jnp.exp2(s - m_new)
            l_new = l_new + jnp.sum(pt, axis=0, keepdims=True)
            part = lax.dot_general(pt.astype(BF16), vp().astype(BF16), (((0,), (0,)), ((), ())),
                                   preferred_element_type=F32)
            pv = part if pv is None else pv + part
        l_sc[...] = l_new
        acc_sc[...] = acc_sc[...] * _col_vector(alpha) + pv
        m_sc[...] = m_new

    is_last = (p == last).astype(F32)
    pages([functools.partial(heads_on_lanes, r) for r in k_refs],
          [functools.partial(heads_on_lanes, r) for r in v_refs],
          [mt_ref[k * PAGE_SIZE:(k + 1) * PAGE_SIZE, :] for k in range(pp)],
          [None] * (pp - 1) + [bias_ref[0] * is_last])

    @pl.when(p == last)
    def _():
        pages([lambda: knew_ref[...]], [lambda: vnew_ref[...]], [mnew_ref[...]], [bias_ref[1]])
        o_ref[...] = acc_sc[...] / _col_vector(l_sc[...])


def _paged_attention(qbd, k_pool, v_pool, layer, page_table, mask_t, k_new, v_new, bias_t, *, pp):
    bsz, w, _ = qbd.shape
    n_pages = page_table.shape[1]
    assert n_pages % pp == 0
    n_layers, n_phys, _, n_kv, _ = k_pool.shape
    assert n_kv * HEAD_DIM == w
    as_rows = lambda pool: pool.reshape(n_layers, n_phys, PAGE_SIZE * n_kv, HEAD_DIM)
    k_pool, v_pool = as_rows(k_pool), as_rows(v_pool)
    page_spec = lambda k: pl.BlockSpec((None, None, PAGE_SIZE * n_kv, HEAD_DIM),
                                       lambda b, p, tbl: (layer, tbl[b, p * pp + k], 0, 0))
    per_b = lambda shp: pl.BlockSpec((None,) + shp, lambda b, p, tbl: (b, 0, 0))
    return pl.pallas_call(
        functools.partial(_paged_attn_kernel, pp=pp),
        grid_spec=pltpu.PrefetchScalarGridSpec(
            num_scalar_prefetch=1,
            grid=(bsz, n_pages // pp),
            in_specs=[
                per_b((w, LANES)),
                *[page_spec(k) for k in range(pp)],
                *[page_spec(k) for k in range(pp)],
                pl.BlockSpec((None, pp * PAGE_SIZE, LANES), lambda b, p, tbl: (b, p, 0)),
                pl.BlockSpec((None, PAGE_SIZE, LANES), lambda b, p, tbl: (b, n_pages, 0)),
                per_b((PAGE_SIZE, w)),
                per_b((PAGE_SIZE, w)),
                pl.BlockSpec((2, PAGE_SIZE, LANES), lambda b, p, tbl: (0, 0, 0)),
            ],
            out_specs=per_b((LANES, w)),
            scratch_shapes=[pltpu.VMEM((1, LANES), F32), pltpu.VMEM((1, LANES), F32), pltpu.VMEM((LANES, w), F32)],
        ),
        out_shape=jax.ShapeDtypeStruct((bsz, LANES, w), F32),
        compiler_params=_params("arbitrary", "arbitrary"),
        name="paged_attention",
    )(page_table, qbd, *([k_pool] * pp), *([v_pool] * pp), mask_t, mask_t, k_new, v_new, bias_t)


def _block_diag_queries(q, n_kv):
    bsz, t, _ = q.shape
    q4 = q.reshape(bsz, t, N_HEADS, HEAD_DIM) * QK_SCALE_LOG2
    onehot = (jnp.arange(N_HEADS)[:, None] // (N_HEADS // n_kv) == jnp.arange(n_kv)[None, :]).astype(q.dtype)
    qbd = q4[:, :, :, None, :] * onehot[None, None, :, :, None]
    return qbd.transpose(0, 3, 4, 2, 1).reshape(bsz, n_kv * HEAD_DIM, N_HEADS * t).astype(BF16)


def _own_head_lanes(out, n_kv, t):
    bsz = out.shape[0]
    out5 = out.reshape(bsz, N_HEADS, t, n_kv, HEAD_DIM)
    heads = jnp.arange(N_HEADS)
    picked = out5[:, heads, :, heads // (N_HEADS // n_kv), :]
    return picked.transpose(1, 2, 0, 3).reshape(bsz, t, N_HEADS * HEAD_DIM)


def _decode_bias(near_bias, tq, t):
    tail = near_bias[:, :t, tq - PAGE_SIZE:tq + PAGE_SIZE]
    return tail.transpose(2, 0, 1).reshape(2, PAGE_SIZE, N_HEADS * t)


def _pad_rows(a, rows):
    return jnp.pad(a, ((0, 0), (0, rows - a.shape[1]), (0, 0)))


def _new_token_mask(t):
    key = jnp.arange(PAGE_SIZE)[:, None]
    tok = jnp.tile(jnp.arange(t), N_HEADS)[None, :]
    return (key <= tok).astype(BF16)


def _split_heads_kernel(x_ref, o_ref, *, tm, n_heads):
    for h in range(n_heads):
        o_ref[pl.ds(h, tm, stride=n_heads), :] = x_ref[:, h * HEAD_DIM:(h + 1) * HEAD_DIM]


def _split_heads(proj, col, n_heads, *, tm):
    bsz, t, _ = proj.shape
    w = n_heads * HEAD_DIM
    assert col % w == 0 and t % tm == 0
    out = pl.pallas_call(
        functools.partial(_split_heads_kernel, tm=tm, n_heads=n_heads),
        grid=(bsz, t // tm),
        in_specs=[pl.BlockSpec((None, tm, w), lambda b, i: (b, i, col // w))],
        out_specs=pl.BlockSpec((None, tm * n_heads, HEAD_DIM), lambda b, i: (b, i, 0)),
        out_shape=jax.ShapeDtypeStruct((bsz, t * n_heads, HEAD_DIM), F32),
        compiler_params=_params("arbitrary", "arbitrary"),
        name="split_heads",
    )(proj)
    return out.reshape(bsz, t, n_heads, HEAD_DIM)


ATTN_TQ = 256
SELECT_TQ = 256
SELECT_TK = 512
MOBA_HEADS_PER_STEP = 4
SPLIT_TM = 256
Q_W = N_HEADS * HEAD_DIM
KV_W = N_KV_B * HEAD_DIM
DSA_PROJ = Q_W + 2 * KV_W + N_IDX_HEADS * IDX_DIM + IDX_DIM + N_IDX_HEADS
DSA_PROJ_PAD = -(-DSA_PROJ // 512) * 512


def _dsa_split(proj):
    k = proj[..., Q_W:Q_W + KV_W]
    v = proj[..., Q_W + KV_W:Q_W + 2 * KV_W]
    o = Q_W + 2 * KV_W + N_IDX_HEADS * IDX_DIM
    return k, v, proj[..., o:o + IDX_DIM]


def _dsa_prompt(proj, near_bias):
    bsz, t, _ = proj.shape
    ki = _dsa_split(proj)[2]
    mask = _dsa_select(proj, tq=min(SELECT_TQ, t), tk=min(SELECT_TK, t), topk=min(IDX_TOPK, t // 4))
    o = _attention(proj, Q_W, Q_W + KV_W, mask, near_bias, tq=ATTN_TQ, rep=N_HEADS // N_KV_B,
                   shared_kv=True, block_mask=False)
    k = _split_heads(proj, Q_W, N_KV_B, tm=min(SPLIT_TM, t))
    v = _split_heads(proj, Q_W + KV_W, N_KV_B, tm=min(SPLIT_TM, t))
    return o, k, v, ki


def _dsa_sample(proj, k_pool, v_pool, kidx_pool, layer, page_table, near_bias):
    bsz, t, _ = proj.shape
    past_len = page_table.shape[1] * PAGE_SIZE
    k, v, ki = _dsa_split(proj)
    o_qi = Q_W + 2 * KV_W
    qi = proj[..., o_qi:o_qi + N_IDX_HEADS * IDX_DIM].reshape(bsz, t, N_IDX_HEADS, IDX_DIM)
    wi = proj[..., o_qi + N_IDX_HEADS * IDX_DIM + IDX_DIM:DSA_PROJ]
    qi_rows = qi.transpose(0, 2, 1, 3).reshape(bsz, N_IDX_HEADS * t, IDX_DIM)
    wi_rows = wi.transpose(0, 2, 1).reshape(bsz, N_IDX_HEADS * t, 1)
    sel = _dsa_select_paged(qi_rows, wi_rows, kidx_pool, layer, page_table, _pad_rows(ki, PAGE_SIZE),
                            pp=8, topk=min(IDX_TOPK, (past_len + t) // 4))
    mask_t = jnp.tile(sel.transpose(0, 2, 1), (1, 1, N_HEADS))
    out = _paged_attention(
        _block_diag_queries(proj[..., :Q_W], N_KV_B), k_pool, v_pool, layer, page_table,
        mask_t, _pad_rows(k, PAGE_SIZE), _pad_rows(v, PAGE_SIZE), _decode_bias(near_bias, ATTN_TQ, t), pp=8)
    o = _own_head_lanes(out, N_KV_B, t).astype(BF16)
    return o, k.reshape(bsz, t, N_KV_B, HEAD_DIM), v.reshape(bsz, t, N_KV_B, HEAD_DIM), ki


def _moba_prompt(proj, near_bias):
    bsz, t, _ = proj.shape
    k_heads = _split_heads(proj, Q_W, N_HEADS, tm=min(SPLIT_TM, t))
    v_heads = _split_heads(proj, 2 * Q_W, N_HEADS, tm=min(SPLIT_TM, t))
    n_pages = t // PAGE_SIZE
    pages = jnp.arange(bsz * n_pages, dtype=jnp.int32).reshape(bsz, n_pages)
    means = _block_means(k_heads.reshape(1, bsz * n_pages, PAGE_SIZE, N_HEADS, HEAD_DIM), 0, pages)
    sel = _moba_select(proj, _pad_rows(means, LANES), tq=ATTN_TQ, past_len=0)
    o = _attention(proj, Q_W, 2 * Q_W, sel, near_bias, tq=ATTN_TQ, rep=MOBA_HEADS_PER_STEP,
                   shared_kv=False, block_mask=True)
    return o, k_heads, v_heads


def _moba_sample(proj, k_pool, v_pool, layer, page_table, near_bias):
    bsz, t, _ = proj.shape
    n_pages = page_table.shape[1]
    past_len = n_pages * PAGE_SIZE
    k = proj[..., Q_W:2 * Q_W]
    v = proj[..., 2 * Q_W:3 * Q_W]
    means = _block_means(k_pool, layer, page_table)
    sel = _moba_select(proj, _pad_rows(means, LANES), tq=t, past_len=past_len)
    past = jnp.repeat(sel[..., :past_len // MOBA_BLOCK], MOBA_BLOCK, axis=-1)
    past_t = past.transpose(0, 3, 1, 2).reshape(bsz, past_len, N_HEADS * t)
    new_t = jnp.broadcast_to(_new_token_mask(t), (bsz, PAGE_SIZE, N_HEADS * t))
    out = _paged_attention(
        _block_diag_queries(proj[..., :Q_W], N_HEADS), k_pool, v_pool, layer, page_table,
        jnp.concatenate([past_t, new_t], axis=1), _pad_rows(k, PAGE_SIZE), _pad_rows(v, PAGE_SIZE),
        _decode_bias(near_bias, ATTN_TQ, t), pp=4)
    o = _own_head_lanes(out, N_HEADS, t).astype(BF16)
    return o, k.reshape(bsz, t, N_HEADS, HEAD_DIM), v.reshape(bsz, t, N_HEADS, HEAD_DIM)


N_MIXERS = 3
PROMPT_TM = 1024
MATMUL_TN = 512
LRU_TT = 256
MOE_TM = 512
ROUTE_W = LANES


def kernel(x_prompt, x_sample, state_lru_h, state_lru_conv, cache_dsa_k, cache_dsa_v, cache_dsa_kidx,
           cache_moba_k, cache_moba_v, page_table, c_prompt, c_sample, rel_bias, norm_mix_g, norm_ffn_g,
           final_norm_g, ada_w, ada_b, lru_w_in, lru_conv_w, lru_conv_b, lru_w_rgate, lru_b_rgate,
           lru_w_igate, lru_b_igate, lru_lambda, lru_w_out, dsa_w_in, dsa_w_out, moba_w_in, moba_w_out,
           moe_w_group, moe_w_expert, moe_w1, moe_w3, moe_w2):
    bp, tp, d = x_prompt.shape
    bs, ts, _ = x_sample.shape
    depth = ada_w.shape[0]
    ns = bs * ts
    c_rnn = lru_w_out.shape[1]

    c_all = jnp.concatenate([c_prompt, c_sample], axis=0)
    c_rows = -(-c_all.shape[0] // SUBLANES) * SUBLANES
    mods = _ada(jnp.pad(c_all, ((0, c_rows - c_all.shape[0]), (0, 0))), ada_w, ada_b)
    mods = mods.reshape(depth, c_rows, 6, d)
    near_bias = _near_bias(rel_bias, ATTN_TQ)

    xp = x_prompt
    pending = None
    xs = x_sample.reshape(1, ns, d)
    outs = {n: [] for n in ("lru_h_p", "lru_c_p", "lru_h_s", "lru_c_s", "dsa_k_p", "dsa_v_p", "dsa_i_p",
                            "dsa_k_s", "dsa_v_s", "dsa_i_s", "moba_k_p", "moba_v_p", "moba_k_s", "moba_v_s")}
    for i in range(depth):
        kind, j = i % N_MIXERS, i // N_MIXERS
        mod_p = [mods[i, :bp, m][:, None, :] for m in range(6)]
        mod_s = [jnp.repeat(mods[i, bp:bp + bs, m], ts, axis=0)[None] for m in range(6)]
        shp1, scp1, gtp1, shp2, scp2, gtp2 = mod_p
        shs1, scs1, gts1, shs2, scs2, gts2 = mod_s
        mix_in = lambda w, tn=MATMUL_TN: (
            _norm_matmul(xp, norm_mix_g[i], shp1, scp1, w, tm=min(PROMPT_TM, tp), tn=tn),
            _norm_matmul(xs, norm_mix_g[i], shs1, scs1, w, tm=ns, tn=tn))
        if kind == 0:
            up, us = mix_in(lru_w_in[j])
            lru_params = (lru_conv_w[j], lru_conv_b[j], lru_w_rgate[j], lru_b_rgate[j], lru_w_igate[j],
                          lru_b_igate[j], lru_lambda[j])
            yp, hp_state, cp = _lru_core(up, jnp.zeros((bp, CONV_W - 1, c_rnn), F32), jnp.zeros((bp, c_rnn), F32),
                                         *lru_params, tt=LRU_TT)
            ys, hs_state, cs = _lru_core(us.reshape(bs, ts, 2 * c_rnn), state_lru_conv[j], state_lru_h[j],
                                         *lru_params, tt=ts)
            ys = ys.reshape(1, ns, c_rnn)
            w_out = lru_w_out[j]
            outs["lru_h_p"].append(hp_state)
            outs["lru_c_p"].append(cp)
            outs["lru_h_s"].append(hs_state)
            outs["lru_c_s"].append(cs)
        elif kind == 1:
            w_in = jnp.pad(dsa_w_in[j], ((0, 0), (0, DSA_PROJ_PAD - DSA_PROJ)))
            pp_, ps_ = mix_in(w_in)
            yp, kp, vp, ip = _dsa_prompt(pp_, near_bias)
            ys, k_s, v_s, i_s = _dsa_sample(ps_.reshape(bs, ts, DSA_PROJ_PAD), cache_dsa_k, cache_dsa_v,
                                            cache_dsa_kidx, j, page_table, near_bias)
            ys = ys.reshape(1, ns, Q_W)
            w_out = dsa_w_out[j]
            for n, val in zip(("dsa_k_p", "dsa_v_p", "dsa_i_p", "dsa_k_s", "dsa_v_s", "dsa_i_s"),
                              (kp, vp, ip, k_s, v_s, i_s)):
                outs[n].append(val)
        else:
            pp_, ps_ = mix_in(moba_w_in[j])
            yp, kp, vp = _moba_prompt(pp_, near_bias)
            ys, k_s, v_s = _moba_sample(ps_.reshape(bs, ts, 3 * Q_W), cache_moba_k, cache_moba_v, j,
                                        page_table, near_bias)
            ys = ys.reshape(1, ns, Q_W)
            w_out = moba_w_out[j]
            for n, val in zip(("moba_k_p", "moba_v_p", "moba_k_s", "moba_v_s"), (kp, vp, k_s, v_s)):
                outs[n].append(val)
        xp = _matmul_residual(yp, w_out, xp, gtp1, tm=min(PROMPT_TM, tp), tn=min(MATMUL_TN, d))
        xs = _matmul_residual(ys, w_out, xs, gts1, tm=ns, tn=min(MATMUL_TN, d))

        w_route = jnp.pad(jnp.concatenate([moe_w_group[i], moe_w_expert[i]], axis=1),
                          ((0, 0), (0, ROUTE_W - N_GROUPS - N_EXPERTS)))
        w1, w3, w2 = moe_w1[i].astype(BF16), moe_w3[i].astype(BF16), moe_w2[i].astype(BF16)
        lg_p, h_p = _norm_matmul(xp, norm_ffn_g[i], shp2, scp2, w_route, tm=min(PROMPT_TM, tp), tn=ROUTE_W,
                                 emit_h=True, highest=True, h_dtype=F32)
        moe_tm = min(MOE_TM, tp)
        info, counts = _route_tokens(lg_p.reshape(bp * tp, ROUTE_W), tm=moe_tm)
        moe_p = _moe_grouped(h_p.reshape(bp * tp, d), info, *_group_layout(info, counts, moe_tm), w1, w3, w2,
                             tm=moe_tm)
        if i + 1 < depth:
            xp = _gated_add(xp, moe_p, gtp2, tm=moe_tm)
        else:
            pending = (moe_p, gtp2)
        lg_s, h_s = _norm_matmul(xs, norm_ffn_g[i], shs2, scs2, w_route, tm=ns, tn=ROUTE_W,
                                 emit_h=True, highest=True)
        xs = _moe_dense(h_s, lg_s, w1, w3, w2, xs, gts2, tm=ns)

    y_prompt = _rmsnorm(xp, final_norm_g, tm=min(MOE_TM, tp), residual=pending)
    y_sample = _rmsnorm(xs, final_norm_g, tm=ns).reshape(bs, ts, d)
    stack = lambda n: jnp.stack(outs[n])
    return (y_prompt, y_sample, stack("lru_h_p"), stack("lru_c_p"), stack("lru_h_s"), stack("lru_c_s"),
            stack("dsa_k_p"), stack("dsa_v_p"), stack("dsa_i_p"), stack("dsa_k_s"), stack("dsa_v_s"),
            stack("dsa_i_s"), stack("moba_k_p"), stack("moba_v_p"), stack("moba_k_s"), stack("moba_v_s"))
```

```python
import functools
import math

import numpy as np
import jax
import jax.numpy as jnp
from jax import lax
from jax.experimental import pallas as pl
from jax.experimental.pallas import tpu as pltpu

F32 = jnp.float32
BF16 = jnp.bfloat16

LANES = 128
SUBLANES = 8
VMEM_LIMIT_BYTES = 56 * 1024 * 1024

HEAD_DIM = 128
N_HEADS = 16
N_KV_B = 4
N_IDX_HEADS = 16
IDX_DIM = 64
IDX_TOPK = 256
IDX_SCALE = (IDX_DIM * N_IDX_HEADS) ** -0.5
PAGE_SIZE = 128
MOBA_BLOCK = 256
MOBA_TOPK = 3
N_BUCKETS = 32
MAX_DISTANCE = 128
N_RNN_BLOCKS = 16
CONV_W = 4
LRU_C = 8.0
N_GROUPS = 4
EXPERTS_PER_GROUP = 4
N_EXPERTS = N_GROUPS * EXPERTS_PER_GROUP
EPS = 1e-6
ATTN_SCALE = HEAD_DIM ** -0.5
LOG2E = math.log2(math.e)
QK_SCALE_LOG2 = ATTN_SCALE * LOG2E
NEG_INF = float("-inf")


def _params(*sem):
    return pltpu.CompilerParams(dimension_semantics=sem, vmem_limit_bytes=VMEM_LIMIT_BYTES)


def _bdot(a, b):
    return jnp.dot(a.astype(BF16), b.astype(BF16), preferred_element_type=F32)


def _bdot_nt(a, b):
    return lax.dot_general(a.astype(BF16), b.astype(BF16), (((1,), (1,)), ((), ())),
                           preferred_element_type=F32)


def _ada_kernel(c_ref, w_ref, b_ref, o_ref):
    c = c_ref[...]
    o_ref[...] = _bdot(c * jax.nn.sigmoid(c), w_ref[...]) + b_ref[...]


def _ada(c, ada_w, ada_b):
    depth, d, n = ada_w.shape
    rows = c.shape[0]
    tn = next(w for w in (1024, 512, 256, LANES) if n % w == 0)
    return pl.pallas_call(
        _ada_kernel,
        grid=(depth, n // tn),
        in_specs=[
            pl.BlockSpec((rows, d), lambda l, j: (0, 0)),
            pl.BlockSpec((None, d, tn), lambda l, j: (l, 0, j)),
            pl.BlockSpec((None, 1, tn), lambda l, j: (l, 0, j)),
        ],
        out_specs=pl.BlockSpec((None, rows, tn), lambda l, j: (l, 0, j)),
        out_shape=jax.ShapeDtypeStruct((depth, rows, n), F32),
        compiler_params=_params("arbitrary", "arbitrary"),
        name="ada_mod",
    )(c, ada_w, ada_b.reshape(depth, 1, n))


def _norm_mod(x, g, shift, scale):
    ms = jnp.mean(x * x, axis=-1, keepdims=True)
    h = x * lax.rsqrt(ms + EPS) * g
    return h * (1.0 + scale) + shift


def _norm_mm_kernel(x_ref, g_ref, sh_ref, sc_ref, w_ref, *refs, emit_h, highest):
    if emit_h:
        o_ref, ho_ref, h_ref = refs
    else:
        o_ref, h_ref = refs
        ho_ref = None

    @pl.when(pl.program_id(2) == 0)
    def _():
        h = _norm_mod(x_ref[...], g_ref[...], sh_ref[...], sc_ref[...])
        h_ref[...] = h.astype(h_ref.dtype)
        if ho_ref is not None:
            ho_ref[...] = h.astype(ho_ref.dtype)

    if highest:
        o_ref[...] = jnp.dot(h_ref[...], w_ref[...], precision=lax.Precision.HIGHEST,
                             preferred_element_type=F32)
    else:
        o_ref[...] = _bdot(h_ref[...], w_ref[...])


def _weight_spec(w, layer, tn):
    if w.ndim == 2:
        return pl.BlockSpec((w.shape[0], tn), lambda b, i, j: (0, j))
    return pl.BlockSpec((None, w.shape[1], tn), lambda b, i, j: (layer, 0, j))


def _norm_matmul(x, g, shift, scale, w, *, tm, tn, emit_h=False, highest=False, h_dtype=BF16, layer=None):
    bsz, t, d = x.shape
    n = w.shape[-1]
    r = shift.shape[1]
    assert t % tm == 0 and n % tn == 0 and (r == 1 or r == tm == t)
    out_shape = [jax.ShapeDtypeStruct((bsz, t, n), F32)]
    out_specs = [pl.BlockSpec((None, tm, tn), lambda b, i, j: (b, i, j))]
    row_tile = pl.BlockSpec((None, tm, d), lambda b, i, j: (b, i, 0))
    if emit_h:
        out_shape.append(jax.ShapeDtypeStruct((bsz, t, d), h_dtype))
        out_specs.append(row_tile)
    mod_map = (lambda b, i, j: (b, 0, 0)) if r == 1 else (lambda b, i, j: (b, i, 0))
    res = pl.pallas_call(
        functools.partial(_norm_mm_kernel, emit_h=emit_h, highest=highest),
        grid=(bsz, t // tm, n // tn),
        in_specs=[
            row_tile,
            pl.BlockSpec((1, d), lambda b, i, j: (0, 0)),
            pl.BlockSpec((None, r, d), mod_map),
            pl.BlockSpec((None, r, d), mod_map),
            _weight_spec(w, layer, tn),
        ],
        out_specs=out_specs,
        out_shape=out_shape,
        scratch_shapes=[pltpu.VMEM((tm, d), F32 if highest else BF16)],
        compiler_params=_params("arbitrary", "arbitrary", "arbitrary"),
        name="norm_matmul",
    )(x, g.reshape(1, d), shift, scale, w)
    return res if emit_h else res[0]


def _mm_res_kernel(a_ref, w_ref, r_ref, g_ref, o_ref):
    o_ref[...] = r_ref[...] + g_ref[...] * _bdot(a_ref[...], w_ref[...])


def _matmul_residual(a, w, res, gate, *, tm, tn, layer=None):
    bsz, t, k = a.shape
    n = w.shape[-1]
    r = gate.shape[1]
    assert t % tm == 0 and n % tn == 0 and (r == 1 or r == tm == t)
    gate_map = (lambda b, i, j: (b, 0, j)) if r == 1 else (lambda b, i, j: (b, i, j))
    return pl.pallas_call(
        _mm_res_kernel,
        grid=(bsz, t // tm, n // tn),
        in_specs=[
            pl.BlockSpec((None, tm, k), lambda b, i, j: (b, i, 0)),
            _weight_spec(w, layer, tn),
            pl.BlockSpec((None, tm, tn), lambda b, i, j: (b, i, j)),
            pl.BlockSpec((None, r, tn), gate_map),
        ],
        out_specs=pl.BlockSpec((None, tm, tn), lambda b, i, j: (b, i, j)),
        out_shape=jax.ShapeDtypeStruct((bsz, t, n), F32),
        compiler_params=_params("arbitrary", "arbitrary", "arbitrary"),
        name="matmul_residual",
    )(a, w, res, gate)


def _rmsnorm_kernel(x_ref, g_ref, *refs, residual):
    x = x_ref[...]
    if residual:
        y_ref, gy_ref, o_ref = refs
        x = x + gy_ref[...] * y_ref[...]
    else:
        o_ref, = refs
    ms = jnp.mean(x * x, axis=-1, keepdims=True)
    o_ref[...] = x * lax.rsqrt(ms + EPS) * g_ref[...]


def _rmsnorm(x, g, *, tm, residual=None):
    bsz, t, d = x.shape
    per_b = t // tm
    extra_in, extra_specs = [], []
    if residual is not None:
        extra_in = list(residual)
        extra_specs = [pl.BlockSpec((tm, d), lambda b, i: (b * per_b + i, 0)),
                       pl.BlockSpec((None, 1, d), lambda b, i: (b, 0, 0))]
    return pl.pallas_call(
        functools.partial(_rmsnorm_kernel, residual=residual is not None),
        grid=(bsz, t // tm),
        in_specs=[pl.BlockSpec((None, tm, d), lambda b, i: (b, i, 0)),
                  pl.BlockSpec((1, d), lambda b, i: (0, 0)), *extra_specs],
        out_specs=pl.BlockSpec((None, tm, d), lambda b, i: (b, i, 0)),
        out_shape=jax.ShapeDtypeStruct((bsz, t, d), F32),
        compiler_params=_params("arbitrary", "arbitrary"),
        name="final_rmsnorm",
    )(x, g.reshape(1, d), *extra_in)


GATE_CHUNK = 640


def _log_sigmoid(x):
    return jnp.minimum(x, 0.0) - jnp.log1p(jnp.exp(-jnp.abs(x)))


def _lru_kernel(gb_ref, xb_ref, cs_ref, h0_ref, cw_ref, cb_ref, wr_ref, br_ref, wi_ref, bi_ref, lam_ref,
                y_ref, hl_ref, nc_ref, ext_ref, a_ref, b_ref, hs_ref, carry_ref, *, tt):
    c_all = xb_ref.shape[-1]

    @pl.when(pl.program_id(1) == 0)
    def _():
        ext_ref[0:SUBLANES, :] = cs_ref[...]
        carry_ref[...] = h0_ref[...]

    xb = xb_ref[...]
    ext_ref[SUBLANES:SUBLANES + tt, :] = xb
    cw = cw_ref[...]
    xc = cb_ref[...] + xb * cw[CONV_W - 1:CONV_W, :]
    for j in range(1, CONV_W):
        xc = xc + ext_ref[pl.ds(SUBLANES - j, tt), :] * cw[CONV_W - 1 - j:CONV_W - j, :]
    tail = ext_ref[tt:tt + SUBLANES, :]
    ext_ref[0:SUBLANES, :] = tail
    nc_ref[...] = tail

    xcb = xc.astype(BF16)
    r_parts, i_parts = [], []
    for c in range(c_all // GATE_CHUNK):
        sl = slice(c * GATE_CHUNK, (c + 1) * GATE_CHUNK)
        r_parts.append(jnp.dot(xcb[:, sl], wr_ref[c], preferred_element_type=F32))
        i_parts.append(jnp.dot(xcb[:, sl], wi_ref[c], preferred_element_type=F32))
    r = jax.nn.sigmoid(jnp.concatenate(r_parts, axis=1) + br_ref[...])
    ig = jax.nn.sigmoid(jnp.concatenate(i_parts, axis=1) + bi_ref[...])
    log_a = LRU_C * r * _log_sigmoid(lam_ref[...])
    th = jnp.tanh(log_a)
    one_minus_a2 = -2.0 * th / (1.0 - th)
    a_ref[...] = jnp.exp(log_a)
    b_ref[...] = jnp.sqrt(jnp.maximum(one_minus_a2, 0.0)) * (ig * xc)

    row = lax.broadcasted_iota(jnp.int32, (SUBLANES, GATE_CHUNK), 0)
    for c in range(c_all // GATE_CHUNK):
        sl = slice(c * GATE_CHUNK, (c + 1) * GATE_CHUNK)

        def body(grp, carry, sl=sl):
            r8 = pl.multiple_of(grp * SUBLANES, SUBLANES)
            av = a_ref[pl.ds(r8, SUBLANES), sl]
            bv = b_ref[pl.ds(r8, SUBLANES), sl]
            for s in (1, 2, 4):
                m = row >= s
                a_sh = jnp.where(m, pltpu.roll(av, s, axis=0), 1.0)
                b_sh = jnp.where(m, pltpu.roll(bv, s, axis=0), 0.0)
                bv = av * b_sh + bv
                av = av * a_sh
            h = av * carry + bv
            hs_ref[pl.ds(r8, SUBLANES), sl] = h
            return h[SUBLANES - 1:SUBLANES, :]

        carry_ref[:, sl] = lax.fori_loop(0, tt // SUBLANES, body, carry_ref[:, sl])

    hl_ref[...] = carry_ref[...]
    y_ref[...] = (jax.nn.gelu(gb_ref[...]) * hs_ref[...]).astype(y_ref.dtype)


def _lru_core(u, conv_state, h0, conv_w, conv_b, w_r, b_r, w_i, b_i, lam, *, tt):
    bsz, t, c2 = u.shape
    c = c2 // 2
    blk = c // N_RNN_BLOCKS
    per = GATE_CHUNK // blk
    nchunk = c // GATE_CHUNK

    def block_diag(w):
        wc = w.reshape(nchunk, per, blk, blk).astype(BF16)
        eye = jnp.eye(per, dtype=BF16)
        return jnp.einsum("cpij,pq->cpiqj", wc, eye).reshape(nchunk, GATE_CHUNK, GATE_CHUNK)

    cs_pad = jnp.pad(conv_state, ((0, 0), (SUBLANES - (CONV_W - 1), 0), (0, 0)))
    row = lambda v: v.reshape(1, c)
    full = lambda shp: pl.BlockSpec(shp, lambda b, i: (0,) * len(shp))
    y, hl, nc = pl.pallas_call(
        functools.partial(_lru_kernel, tt=tt),
        grid=(bsz, t // tt),
        in_specs=[
            pl.BlockSpec((None, tt, c), lambda b, i: (b, i, 0)),
            pl.BlockSpec((None, tt, c), lambda b, i: (b, i, 1)),
            pl.BlockSpec((None, SUBLANES, c), lambda b, i: (b, 0, 0)),
            pl.BlockSpec((None, 1, c), lambda b, i: (b, 0, 0)),
            full((CONV_W, c)), full((1, c)),
            full((nchunk, GATE_CHUNK, GATE_CHUNK)), full((1, c)),
            full((nchunk, GATE_CHUNK, GATE_CHUNK)), full((1, c)),
            full((1, c)),
        ],
        out_specs=[
            pl.BlockSpec((None, tt, c), lambda b, i: (b, i, 0)),
            pl.BlockSpec((None, 1, c), lambda b, i: (b, 0, 0)),
            pl.BlockSpec((None, SUBLANES, c), lambda b, i: (b, 0, 0)),
        ],
        out_shape=[
            jax.ShapeDtypeStruct((bsz, t, c), BF16),
            jax.ShapeDtypeStruct((bsz, 1, c), F32),
            jax.ShapeDtypeStruct((bsz, SUBLANES, c), F32),
        ],
        scratch_shapes=[
            pltpu.VMEM((tt + SUBLANES, c), F32),
            pltpu.VMEM((tt, c), F32),
            pltpu.VMEM((tt, c), F32),
            pltpu.VMEM((tt, c), F32),
            pltpu.VMEM((1, c), F32),
        ],
        compiler_params=_params("arbitrary", "arbitrary"),
        name="lru_core",
    )(u, u, cs_pad, h0.reshape(bsz, 1, c), conv_w, row(conv_b), block_diag(w_r), row(b_r),
      block_diag(w_i), row(b_i), row(lam))
    return y, hl.reshape(bsz, c), nc[:, SUBLANES - (CONV_W - 1):, :]


def _route_gates(lg):
    return _route(lg)[0]


def _route(lg):
    col = lax.broadcasted_iota(jnp.int32, lg.shape, 1)
    big = jnp.int32(LANES)
    is_g = col < N_GROUPS
    gl = jnp.where(is_g, lg, NEG_INF)
    gmax = jnp.max(gl, axis=1, keepdims=True)
    g_sel = jnp.min(jnp.where(gl == gmax, col, big), axis=1, keepdims=True)
    g_w = 1.0 / jnp.sum(jnp.exp(gl - gmax), axis=1, keepdims=True)
    eid = col - N_GROUPS
    in_grp = (eid >= 0) & (eid < N_EXPERTS) & ((eid // EXPERTS_PER_GROUP) == g_sel)
    e_in = jnp.where(in_grp, lg, NEG_INF)
    top1 = jnp.max(e_in, axis=1, keepdims=True)
    idx1 = jnp.min(jnp.where((e_in == top1) & in_grp, col, big), axis=1, keepdims=True)
    rest = in_grp & (col != idx1)
    e2 = jnp.where(rest, lg, NEG_INF)
    top2 = jnp.max(e2, axis=1, keepdims=True)
    idx2 = jnp.min(jnp.where((e2 == top2) & rest, col, big), axis=1, keepdims=True)
    z = jnp.exp(top2 - top1)
    w_first = g_w / (1.0 + z)
    w_second = g_w * z / (1.0 + z)
    return jnp.where(col == idx1, w_first, 0.0) + jnp.where(col == idx2, w_second, 0.0), g_sel


def _moe_dense_kernel(h_ref, lg_ref, w1_ref, w3_ref, w2_ref, x_ref, g_ref, o_ref, gates_ref, acc_ref):
    e = pl.program_id(2)

    @pl.when(e == 0)
    def _():
        gates_ref[...] = _route_gates(lg_ref[...])
        acc_ref[...] = jnp.zeros_like(acc_ref)

    gates = gates_ref[...]
    col = lax.broadcasted_iota(jnp.int32, gates.shape, 1)
    ge = jnp.sum(jnp.where(col == e + N_GROUPS, gates, 0.0), axis=1, keepdims=True)
    h = h_ref[...]
    a1 = jnp.dot(h, w1_ref[...], preferred_element_type=F32)
    a3 = jnp.dot(h, w3_ref[...], preferred_element_type=F32)
    hidden = (a1 * jax.nn.sigmoid(a1)) * a3 * ge
    acc_ref[...] += jnp.dot(hidden.astype(BF16), w2_ref[...], preferred_element_type=F32)

    @pl.when(e == pl.num_programs(2) - 1)
    def _():
        o_ref[...] = x_ref[...] + g_ref[...] * acc_ref[...]


def _moe_dense(h, logits, w1, w3, w2, layer, x, gate, *, tm):
    bsz, t, d = x.shape
    _, ne, _, f = w1.shape
    r = gate.shape[1]
    gate_map = (lambda b, i, e: (b, 0, 0)) if r == 1 else (lambda b, i, e: (b, i, 0))
    tok = lambda b, i, e: (b, i, 0)
    expert = lambda b, i, e: (layer, e, 0, 0)
    return pl.pallas_call(
        _moe_dense_kernel,
        grid=(bsz, t // tm, ne),
        in_specs=[
            pl.BlockSpec((None, tm, d), tok),
            pl.BlockSpec((None, tm, LANES), tok),
            pl.BlockSpec((None, None, d, f), expert),
            pl.BlockSpec((None, None, d, f), expert),
            pl.BlockSpec((None, None, f, d), expert),
            pl.BlockSpec((None, tm, d), tok),
            pl.BlockSpec((None, r, d), gate_map),
        ],
        out_specs=pl.BlockSpec((None, tm, d), tok),
        out_shape=jax.ShapeDtypeStruct((bsz, t, d), F32),
        scratch_shapes=[pltpu.VMEM((tm, LANES), F32), pltpu.VMEM((tm, d), F32)],
        compiler_params=_params("arbitrary", "arbitrary", "arbitrary"),
        name="moe_dense",
    )(h, logits, w1, w3, w2, x, gate)


INFO_GROUP_LANE = 0
INFO_RANK_LANE = 1


def _route_kernel(lg_ref, info_ref, cnt_ref, carry_ref):
    @pl.when(pl.program_id(0) == 0)
    def _():
        carry_ref[...] = jnp.zeros_like(carry_ref)

    gates, g_sel = _route(lg_ref[...])
    tm = gates.shape[0]
    col = lax.broadcasted_iota(jnp.int32, gates.shape, 1)
    onehot = (col == g_sel).astype(F32)
    before = (lax.broadcasted_iota(jnp.int32, (tm, tm), 0) > lax.broadcasted_iota(jnp.int32, (tm, tm), 1))
    earlier = jnp.dot(before.astype(BF16), onehot.astype(BF16), preferred_element_type=F32) + carry_ref[...]
    rank = jnp.sum(jnp.where(col == g_sel, earlier, 0.0), axis=1, keepdims=True)
    carry_ref[...] += jnp.sum(onehot, axis=0, keepdims=True)
    info_ref[...] = (gates + jnp.where(col == INFO_GROUP_LANE, g_sel.astype(F32), 0.0)
                     + jnp.where(col == INFO_RANK_LANE, rank, 0.0))
    cnt_ref[...] = carry_ref[...]


def _route_tokens(logits, *, tm):
    n = logits.shape[0]
    assert n % tm == 0
    return pl.pallas_call(
        _route_kernel,
        grid=(n // tm,),
        in_specs=[pl.BlockSpec((tm, LANES), lambda i: (i, 0))],
        out_specs=[pl.BlockSpec((tm, LANES), lambda i: (i, 0)), pl.BlockSpec((1, LANES), lambda i: (0, 0))],
        out_shape=[jax.ShapeDtypeStruct((n, LANES), F32), jax.ShapeDtypeStruct((1, LANES), F32)],
        scratch_shapes=[pltpu.VMEM((1, LANES), F32)],
        compiler_params=_params("arbitrary"),
        name="moe_route",
    )(logits)


def _group_layout(info, counts, tm):
    n = info.shape[0]
    g_sel = info[:, INFO_GROUP_LANE].astype(jnp.int32)
    rank = info[:, INFO_RANK_LANE].astype(jnp.int32)
    cnt = counts[0, :N_GROUPS].astype(jnp.int32)
    padded = -(-cnt // tm) * tm
    ends = jnp.cumsum(padded)
    starts = ends - padded
    n_slots = n + N_GROUPS * tm
    src = _slot_tokens(starts[g_sel] + rank, n_slots)
    tile_start = jnp.arange(n_slots // tm, dtype=jnp.int32) * tm
    tile_group = jnp.minimum(jnp.sum((tile_start[:, None] >= ends[None, :]).astype(jnp.int32), axis=1), N_GROUPS - 1)
    slot_group = jnp.repeat(tile_group, tm)
    is_pad = jnp.arange(n_slots, dtype=jnp.int32) - starts[slot_group] >= cnt[slot_group]
    pad_row = n + tm + jnp.cumsum(is_pad.astype(jnp.int32)) - 1
    dst = jnp.concatenate([n + jnp.arange(tm, dtype=jnp.int32), jnp.where(is_pad, pad_row, src)])
    return src, dst, tile_group, (ends[-1] // tm).reshape(1)


SLOT_UNROLL = 8


def _slot_tokens_kernel(slot_ref, src_ref):
    n, n_slots = slot_ref.shape[0], src_ref.shape[0]

    def clear(c, carry):
        for k in range(SLOT_UNROLL):
            src_ref[c * SLOT_UNROLL + k] = 0
        return carry

    def place(c, carry):
        for k in range(SLOT_UNROLL):
            t = c * SLOT_UNROLL + k
            src_ref[slot_ref[t]] = t
        return carry

    lax.fori_loop(0, n_slots // SLOT_UNROLL, clear, 0)
    lax.fori_loop(0, n // SLOT_UNROLL, place, 0)


def _slot_tokens(slot, n_slots):
    assert slot.shape[0] % SLOT_UNROLL == 0 and n_slots % SLOT_UNROLL == 0
    return pl.pallas_call(
        _slot_tokens_kernel,
        in_specs=[pl.BlockSpec(memory_space=pltpu.SMEM)],
        out_specs=pl.BlockSpec(memory_space=pltpu.SMEM),
        out_shape=jax.ShapeDtypeStruct((n_slots,), jnp.int32),
        name="moe_slot_tokens",
    )(slot)


def _moe_group_kernel(src_ref, dst_ref, tg_ref, used_ref, h_hbm, info_hbm, w1_ref, w3_ref, w2_ref, zeros_hbm,
                      out_hbm, xbuf, gbuf, xb, gates, acc, obuf, sem_x, sem_g, sem_o, *, tm):
    del zeros_hbm
    i = pl.program_id(0)
    e = pl.program_id(1)
    used = used_ref[0]
    live = i < used
    chunk = tm // EXPERTS_PER_GROUP
    first = e * chunk
    last_step = e == EXPERTS_PER_GROUP - 1

    def gather_row(tile, r):
        tok = src_ref[tile * tm + r]
        pltpu.make_async_copy(h_hbm.at[pl.ds(tok, 1), :], xbuf.at[pl.ds(r, 1), :], sem_x).start()
        pltpu.make_async_copy(info_hbm.at[pl.ds(tok, 1), :], gbuf.at[pl.ds(r, 1), :], sem_g).start()

    def scatter_chunk(tile):
        slot = (tile + 2) % 2
        for k in range(chunk):
            row = dst_ref[(tile + 1) * tm + first + k]
            pltpu.make_async_copy(obuf.at[slot, pl.ds(first + k, 1), :], out_hbm.at[pl.ds(row, 1), :],
                                  sem_o.at[slot]).start()

    def wait_scatter(slot):
        pltpu.make_async_copy(obuf.at[slot], out_hbm.at[pl.ds(0, tm), :], sem_o.at[slot]).wait()

    @pl.when((i == 0) & (e == 0))
    def _():
        obuf[...] = jnp.zeros_like(obuf)

        def body(r, carry):
            gather_row(0, r)
            return carry
        lax.fori_loop(0, tm, body, 0)

    @pl.when((e == 0) & (i <= used))
    def _():
        pltpu.make_async_copy(h_hbm.at[pl.ds(0, tm), :], xbuf, sem_x).wait()
        pltpu.make_async_copy(info_hbm.at[pl.ds(0, tm), :], gbuf, sem_g).wait()

    @pl.when((e == 0) & live)
    def _():
        xb[...] = xbuf[...].astype(BF16)
        gates[...] = gbuf[...]
        acc[...] = jnp.zeros_like(acc)

    @pl.when(live)
    def _():
        lane = lax.broadcasted_iota(jnp.int32, (tm, LANES), 1)
        gate_lane = N_GROUPS + EXPERTS_PER_GROUP * tg_ref[i] + e
        ge = jnp.sum(jnp.where(lane == gate_lane, gates[...], 0.0), axis=1, keepdims=True)
        x = xb[...]
        a1 = jnp.dot(x, w1_ref[...], preferred_element_type=F32)
        a3 = jnp.dot(x, w3_ref[...], preferred_element_type=F32)
        hidden = (a1 * jax.nn.sigmoid(a1)) * a3 * ge
        acc[...] += jnp.dot(hidden.astype(BF16), w2_ref[...], preferred_element_type=F32)
        for k in range(chunk):
            gather_row(i + 1, first + k)
        scatter_chunk(i - 1)

    @pl.when(i == used)
    def _():
        scatter_chunk(i - 1)

    @pl.when(last_step & live & (i >= 1))
    def _():
        wait_scatter(i % 2)

    @pl.when(last_step & live)
    def _():
        obuf[i % 2] = acc[...]

    @pl.when(last_step & (i == used))
    def _():
        wait_scatter(0)
        wait_scatter(1)


def _moe_grouped(h, info, src, dst, tile_group, n_used, w1, w3, w2, layer, *, tm):
    n, d = h.shape
    f = w1.shape[3]
    n_tiles = src.shape[0] // tm
    n_rows = n + (N_GROUPS + 1) * tm
    assert tm % EXPERTS_PER_GROUP == 0 and dst.shape[0] == (n_tiles + 1) * tm

    def expert(i, e, src_, dst_, tg, used):
        e_eff = jnp.where(i < used[0], e, EXPERTS_PER_GROUP - 1)
        return (layer, tg[i] * EXPERTS_PER_GROUP + e_eff, 0, 0)

    return pl.pallas_call(
        functools.partial(_moe_group_kernel, tm=tm),
        grid_spec=pltpu.PrefetchScalarGridSpec(
            num_scalar_prefetch=4,
            grid=(n_tiles, EXPERTS_PER_GROUP),
            in_specs=[
                pl.BlockSpec(memory_space=pl.ANY),
                pl.BlockSpec(memory_space=pl.ANY),
                pl.BlockSpec((None, None, d, f), expert),
                pl.BlockSpec((None, None, d, f), expert),
                pl.BlockSpec((None, None, f, d), expert),
                pl.BlockSpec(memory_space=pl.ANY),
            ],
            out_specs=pl.BlockSpec(memory_space=pl.ANY),
            scratch_shapes=[
                pltpu.VMEM((tm, d), F32), pltpu.VMEM((tm, LANES), F32), pltpu.VMEM((tm, d), BF16),
                pltpu.VMEM((tm, LANES), F32), pltpu.VMEM((tm, d), F32), pltpu.VMEM((2, tm, d), F32),
                pltpu.SemaphoreType.DMA(()), pltpu.SemaphoreType.DMA(()), pltpu.SemaphoreType.DMA((2,)),
            ],
        ),
        out_shape=jax.ShapeDtypeStruct((n_rows, d), F32),
        input_output_aliases={9: 0},
        compiler_params=_params("arbitrary", "arbitrary"),
        name="moe_grouped",
    )(src, dst, tile_group, n_used, h, info, w1, w3, w2, jnp.zeros((n_rows, d), F32))


def _gated_add_kernel(x_ref, y_ref, g_ref, o_ref):
    o_ref[...] = x_ref[...] + g_ref[...] * y_ref[...]


def _gated_add(x, y, gate, *, tm):
    bsz, t, d = x.shape
    per_b = t // tm
    return pl.pallas_call(
        _gated_add_kernel,
        grid=(bsz, per_b),
        in_specs=[
            pl.BlockSpec((None, tm, d), lambda b, i: (b, i, 0)),
            pl.BlockSpec((tm, d), lambda b, i: (b * per_b + i, 0)),
            pl.BlockSpec((None, 1, d), lambda b, i: (b, 0, 0)),
        ],
        out_specs=pl.BlockSpec((None, tm, d), lambda b, i: (b, i, 0)),
        out_shape=jax.ShapeDtypeStruct((bsz, t, d), F32),
        compiler_params=_params("arbitrary", "arbitrary"),
        name="gated_add",
    )(x, y, gate)


MASKED = -1e30


def _t5_bucket_table(tq):
    qi = np.arange(tq, dtype=np.int32)[:, None]
    col = np.arange(2 * tq, dtype=np.int32)[None, :]
    n = np.maximum(qi + tq - col, 0)
    max_exact = N_BUCKETS // 2
    nf = np.maximum(n, 1).astype(np.float32)
    large = max_exact + (np.log(nf / np.float32(max_exact)) / np.float32(math.log(MAX_DISTANCE / max_exact))
                         * np.float32(N_BUCKETS - max_exact)).astype(np.int32)
    table = np.where(n < max_exact, n, np.minimum(large, N_BUCKETS - 1)).astype(np.int32)
    assert table[0, 0] == N_BUCKETS - 1
    return table


def _bias_kernel(rb_ref, bk_ref, o_ref):
    h = pl.program_id(0)
    bk = bk_ref[...]
    acc = jnp.zeros(bk.shape, F32)
    for k in range(N_BUCKETS):
        acc = jnp.where(bk == k, rb_ref[k, h], acc)
    o_ref[...] = (acc - rb_ref[N_BUCKETS - 1, h]) * LOG2E


def _near_bias(rel_bias, tq):
    return pl.pallas_call(
        _bias_kernel,
        grid=(N_HEADS,),
        in_specs=[pl.BlockSpec(memory_space=pltpu.SMEM),
                  pl.BlockSpec((tq, 2 * tq), lambda h: (0, 0))],
        out_specs=pl.BlockSpec((None, tq, 2 * tq), lambda h: (h, 0, 0)),
        out_shape=jax.ShapeDtypeStruct((N_HEADS, tq, 2 * tq), F32),
        name="near_bias",
    )(rel_bias, jnp.asarray(_t5_bucket_table(tq)))


_NEG_INF_KEY = int(np.int32(np.array(-np.inf, np.float32).view(np.int32)) ^ np.int32(0x7FFFFFFF))


def _sort_key(x):
    k = pltpu.bitcast(x, jnp.int32)
    return k ^ ((k >> 31) & jnp.int32(0x7FFFFFFF))


def _kth_largest_key(count_ge, kth):
    zero = jnp.int32(0)
    first = count_ge(zero)
    ans = jnp.where(first >= kth, zero, jnp.int32(-2 ** 31))
    settled = (first == kth).astype(jnp.int32)

    def cond(state):
        it, _, settled = state
        return (it < 31) & (jnp.min(settled) == 0)

    def body(state):
        it, ans, settled = state
        cand = ans | (jnp.int32(1) << (30 - it))
        count = count_ge(cand)
        return it + 1, jnp.where(count >= kth, cand, ans), settled | (count == kth).astype(jnp.int32)

    return lax.while_loop(cond, body, (jnp.int32(0), ans, settled))[1]


def _dsa_select_kernel(qi_ref, kw_ref, kall_ref, m_ref, key_ref, *, tq, tk, topk):
    i = pl.program_id(1)
    n_kt = (i * tq + tq + tk - 1) // tk
    qpos = i * tq + lax.broadcasted_iota(jnp.int32, (tq, tk), 0)
    wi = kw_ref[:, IDX_DIM:IDX_DIM + N_IDX_HEADS]
    qi = qi_ref[...].astype(BF16)
    key_ref[...] = jnp.full(key_ref.shape, _NEG_INF_KEY, jnp.int32)

    def score_tile(kt, carry):
        k0 = pl.multiple_of(kt * tk, tk)
        kid = kall_ref[pl.ds(k0, tk), 0:IDX_DIM].astype(BF16)
        acc = jnp.zeros((tq, tk), F32)
        for h in range(N_IDX_HEADS):
            s = _bdot_nt(qi[:, h * IDX_DIM:(h + 1) * IDX_DIM], kid)
            acc = acc + wi[:, h:h + 1] * jnp.maximum(s, 0.0)
        kpos = k0 + lax.broadcasted_iota(jnp.int32, (tq, tk), 1)
        sc = jnp.where(kpos <= qpos, acc * IDX_SCALE, NEG_INF)
        key_ref[:, pl.ds(k0, tk)] = _sort_key(sc)
        return carry

    lax.fori_loop(0, n_kt, score_tile, 0)

    def count_ge(cand):
        def body(kt, acc):
            k0 = pl.multiple_of(kt * tk, tk)
            ge = (key_ref[:, pl.ds(k0, tk)] >= cand).astype(jnp.int32)
            for j in range(tk // LANES):
                acc = acc + ge[:, j * LANES:(j + 1) * LANES]
            return acc
        acc = lax.fori_loop(0, n_kt, body, jnp.zeros((tq, LANES), jnp.int32))
        return jnp.sum(acc, axis=1, keepdims=True)

    thr = _kth_largest_key(count_ge, topk)
    keys = key_ref[...]
    m_ref[...] = ((keys >= thr) & (keys > _NEG_INF_KEY)).astype(m_ref.dtype)


def _dsa_select(proj, *, tq, tk, topk):
    bsz, t, _ = proj.shape
    qi_w = N_IDX_HEADS * IDX_DIM
    qi_blk = (N_HEADS * HEAD_DIM + 2 * N_KV_B * HEAD_DIM) // qi_w
    kw_blk = (N_HEADS * HEAD_DIM + 2 * N_KV_B * HEAD_DIM + qi_w) // LANES
    return pl.pallas_call(
        functools.partial(_dsa_select_kernel, tq=tq, tk=tk, topk=topk),
        grid=(bsz, t // tq),
        in_specs=[
            pl.BlockSpec((None, tq, qi_w), lambda b, i: (b, i, qi_blk)),
            pl.BlockSpec((None, tq, LANES), lambda b, i: (b, i, kw_blk)),
            pl.BlockSpec((None, t, LANES), lambda b, i: (b, 0, kw_blk)),
        ],
        out_specs=pl.BlockSpec((None, tq, t), lambda b, i: (b, i, 0)),
        out_shape=jax.ShapeDtypeStruct((bsz, t, t), BF16),
        scratch_shapes=[pltpu.VMEM((tq, t), jnp.int32)],
        compiler_params=_params("arbitrary", "arbitrary"),
        name="dsa_select",
    )(proj, proj, proj)


def _attn_kernel(q_ref, k_ref, v_ref, m_ref, tb_ref, o_ref, *state, tq, rep, shared_kv, block_mask):
    m_sc, l_sc, acc_sc = state[:rep], state[rep:2 * rep], state[2 * rep:]
    i = pl.program_id(2)
    for r in range(rep):
        m_sc[r][...] = jnp.full((tq, LANES), MASKED, F32)
        l_sc[r][...] = jnp.zeros((tq, LANES), F32)
        acc_sc[r][...] = jnp.zeros((tq, HEAD_DIM), F32)
    lane_tiles = tq // LANES

    def row_reduce(x, op, reduce):
        part = x[:, :LANES]
        for c in range(1, lane_tiles):
            part = op(part, x[:, c * LANES:(c + 1) * LANES])
        return reduce(part, axis=1, keepdims=True)

    def key_mask(j, r, causal):
        if block_mask:
            lane = lax.broadcasted_iota(jnp.int32, (tq, LANES), 1)
            vis = jnp.sum(jnp.where(lane == j, m_ref[r].astype(F32), 0.0), axis=1, keepdims=True)
            ok = jnp.broadcast_to(vis > 0.0, (tq, tq))
        else:
            ok = m_ref[:, pl.ds(pl.multiple_of(j * tq, tq), tq)] > 0
        if causal:
            ok = ok & (lax.broadcasted_iota(jnp.int32, (tq, tq), 0) >= lax.broadcasted_iota(jnp.int32, (tq, tq), 1))
        return ok

    def step(j, bias_lo, causal):
        k0 = pl.multiple_of(j * tq, tq)
        kv = lambda r: slice(0, HEAD_DIM) if shared_kv else slice(r * HEAD_DIM, (r + 1) * HEAD_DIM)
        ok_shared = None if block_mask else key_mask(j, 0, causal)
        scores = []
        for r in range(rep):
            q = (q_ref[:, r * HEAD_DIM:(r + 1) * HEAD_DIM] * QK_SCALE_LOG2).astype(BF16)
            s = _bdot_nt(q, k_ref[pl.ds(k0, tq), kv(r)])
            if bias_lo is not None:
                s = s + tb_ref[r, :, bias_lo:bias_lo + tq]
            scores.append(jnp.where(key_mask(j, r, causal) if block_mask else ok_shared, s, MASKED))
        probs, alphas = [], []
        for r in range(rep):
            m_old = m_sc[r][...]
            m_new = jnp.maximum(m_old, row_reduce(scores[r], jnp.maximum, jnp.max))
            alpha = jnp.exp2(m_old - m_new)
            p = jnp.exp2(scores[r] - jnp.concatenate([m_new] * lane_tiles, axis=1))
            l_sc[r][...] = alpha * l_sc[r][...] + row_reduce(p, jnp.add, jnp.sum)
            m_sc[r][...] = m_new
            probs.append(p.astype(BF16))
            alphas.append(alpha)
        for r in range(rep):
            pv = jnp.dot(probs[r], v_ref[pl.ds(k0, tq), kv(r)].astype(BF16), preferred_element_type=F32)
            acc_sc[r][...] = alphas[r] * acc_sc[r][...] + pv

    def far(j, carry):
        step(j, None, False)
        return carry

    lax.fori_loop(0, jnp.maximum(i - 1, 0), far, 0)

    @pl.when(i >= 1)
    def _():
        step(i - 1, 0, False)

    step(i, tq, True)
    o_ref[...] = jnp.concatenate([acc_sc[r][...] / l_sc[r][...] for r in range(rep)], axis=1).astype(o_ref.dtype)


def _attention(proj, k_col, v_col, mask, near_bias, *, tq, rep, shared_kv, block_mask):
    bsz, t, _ = proj.shape
    g = N_HEADS // rep
    kv_w = HEAD_DIM if shared_kv else rep * HEAD_DIM
    assert HEAD_DIM == LANES and k_col % kv_w == 0 and v_col % kv_w == 0
    k_blk, v_blk = k_col // kv_w, v_col // kv_w
    if block_mask:
        assert tq == MOBA_BLOCK
        mask_spec = pl.BlockSpec((None, rep, tq, LANES), lambda b, h, i: (b, h, i, 0))
    else:
        mask_spec = pl.BlockSpec((None, tq, t), lambda b, h, i: (b, i, 0))
    stat = pltpu.VMEM((tq, LANES), F32)
    return pl.pallas_call(
        functools.partial(_attn_kernel, tq=tq, rep=rep, shared_kv=shared_kv, block_mask=block_mask),
        grid=(bsz, g, t // tq),
        in_specs=[
            pl.BlockSpec((None, tq, rep * HEAD_DIM), lambda b, h, i: (b, i, h)),
            pl.BlockSpec((None, t, kv_w), lambda b, h, i: (b, 0, k_blk + h)),
            pl.BlockSpec((None, t, kv_w), lambda b, h, i: (b, 0, v_blk + h)),
            mask_spec,
            pl.BlockSpec((rep, tq, 2 * tq), lambda b, h, i: (h, 0, 0)),
        ],
        out_specs=pl.BlockSpec((None, tq, rep * HEAD_DIM), lambda b, h, i: (b, i, h)),
        out_shape=jax.ShapeDtypeStruct((bsz, t, N_HEADS * HEAD_DIM), BF16),
        scratch_shapes=[stat] * (3 * rep),
        compiler_params=_params("arbitrary", "arbitrary", "arbitrary"),
        name="masked_attention",
    )(proj, proj, proj, mask, near_bias)


PAGES_PER_BLOCK = MOBA_BLOCK // PAGE_SIZE
MEANS_BLOCKS_PER_STEP = 2


def _block_means_kernel(tbl_ref, *refs):
    del tbl_ref
    page_refs, o_ref = refs[:-1], refs[-1]
    for blk in range(MEANS_BLOCKS_PER_STEP):
        pages = page_refs[blk * PAGES_PER_BLOCK:(blk + 1) * PAGES_PER_BLOCK]
        total = jnp.sum(pages[0][...], axis=0)
        for ref in pages[1:]:
            total = total + jnp.sum(ref[...], axis=0)
        o_ref[blk] = total / MOBA_BLOCK


def _block_means(pool, layer, table):
    _, _, _, h, dh = pool.shape
    bsz, npg = table.shape
    nb = npg // PAGES_PER_BLOCK
    per_step = MEANS_BLOCKS_PER_STEP * PAGES_PER_BLOCK
    assert npg % per_step == 0
    page = lambda k: pl.BlockSpec((None, None, PAGE_SIZE, h, dh),
                                  lambda b, n, tbl: (layer, tbl[b, per_step * n + k], 0, 0, 0))
    out = pl.pallas_call(
        _block_means_kernel,
        grid_spec=pltpu.PrefetchScalarGridSpec(
            num_scalar_prefetch=1,
            grid=(bsz, nb // MEANS_BLOCKS_PER_STEP),
            in_specs=[page(k) for k in range(per_step)],
            out_specs=pl.BlockSpec((None, MEANS_BLOCKS_PER_STEP, h, dh), lambda b, n, tbl: (b, n, 0, 0)),
        ),
        out_shape=jax.ShapeDtypeStruct((bsz, nb, h, dh), F32),
        compiler_params=_params("arbitrary", "arbitrary"),
        name="block_means",
    )(table, *([pool] * per_step))
    return out.reshape(bsz, nb, h * dh)


def _moba_select_kernel(q_ref, mean_ref, o_ref, *, tq, past_len):
    i = pl.program_id(1)
    own = (past_len + i * tq + lax.broadcasted_iota(jnp.int32, (tq, LANES), 0)) // MOBA_BLOCK
    blk = lax.broadcasted_iota(jnp.int32, (tq, LANES), 1)
    past = blk < own
    for h in range(N_HEADS):
        sl = slice(h * HEAD_DIM, (h + 1) * HEAD_DIM)
        gate = lax.dot_general(q_ref[:, sl], mean_ref[:, sl], (((1,), (1,)), ((), ())),
                               precision=lax.Precision.HIGHEST, preferred_element_type=F32)
        g = jnp.where(past, gate, NEG_INF)
        kth = jnp.max(g, axis=1, keepdims=True)
        for _ in range(MOBA_TOPK - 1):
            kth = jnp.max(jnp.where(g < kth, g, NEG_INF), axis=1, keepdims=True)
        o_ref[h] = ((past & (g >= kth)) | (blk == own)).astype(o_ref.dtype)


def _moba_select(q_src, means, *, tq, past_len):
    bsz, t, _ = q_src.shape
    w = N_HEADS * HEAD_DIM
    return pl.pallas_call(
        functools.partial(_moba_select_kernel, tq=tq, past_len=past_len),
        grid=(bsz, t // tq),
        in_specs=[
            pl.BlockSpec((None, tq, w), lambda b, i: (b, i, 0)),
            pl.BlockSpec((None, LANES, w), lambda b, i: (b, 0, 0)),
        ],
        out_specs=pl.BlockSpec((None, N_HEADS, tq, LANES), lambda b, i: (b, 0, i, 0)),
        out_shape=jax.ShapeDtypeStruct((bsz, N_HEADS, t, LANES), BF16),
        compiler_params=_params("arbitrary", "arbitrary"),
        name="moba_select",
    )(q_src, means)


def _dsa_select_paged_kernel(tbl_ref, qi_ref, wi_ref, *rest, pp, topk, past_len, t_new):
    del tbl_ref
    page_refs = rest[:pp]
    knew_ref, m_ref, key_ref = rest[pp:]
    p = pl.program_id(1)
    qi = qi_ref[...].astype(BF16)
    wi = wi_ref[...]

    def score(kpage):
        s = _bdot_nt(qi, kpage)
        w = wi * jnp.maximum(s, 0.0)
        return jnp.sum(w.reshape(N_IDX_HEADS, t_new, PAGE_SIZE), axis=0) * IDX_SCALE

    for k in range(pp):
        off = pl.multiple_of((p * pp + k) * PAGE_SIZE, PAGE_SIZE)
        key_ref[:, pl.ds(off, PAGE_SIZE)] = _sort_key(score(page_refs[k][...]))

    @pl.when(p == pl.num_programs(1) - 1)
    def _():
        row = lax.broadcasted_iota(jnp.int32, (t_new, PAGE_SIZE), 0)
        col = lax.broadcasted_iota(jnp.int32, (t_new, PAGE_SIZE), 1)
        sc = jnp.where(col <= row, score(knew_ref[...]), NEG_INF)
        key_ref[:, past_len:past_len + PAGE_SIZE] = _sort_key(sc)

        def count_ge(cand):
            return jnp.sum((key_ref[...] >= cand).astype(jnp.int32), axis=1, keepdims=True)

        thr = _kth_largest_key(count_ge, topk)
        keys = key_ref[...]
        m_ref[...] = ((keys >= thr) & (keys > _NEG_INF_KEY)).astype(m_ref.dtype)


def _dsa_select_paged(qi_rows, wi_rows, kidx_pool, layer, page_table, kidx_new, *, pp, topk):
    bsz, rows, _ = qi_rows.shape
    t_new = rows // N_IDX_HEADS
    n_pages = page_table.shape[1]
    assert n_pages % pp == 0
    past_len = n_pages * PAGE_SIZE
    lpad = past_len + PAGE_SIZE
    page_spec = lambda k: pl.BlockSpec((None, None, PAGE_SIZE, IDX_DIM),
                                       lambda b, p, tbl: (layer, tbl[b, p * pp + k], 0, 0))
    return pl.pallas_call(
        functools.partial(_dsa_select_paged_kernel, pp=pp, topk=topk, past_len=past_len, t_new=t_new),
        grid_spec=pltpu.PrefetchScalarGridSpec(
            num_scalar_prefetch=1,
            grid=(bsz, n_pages // pp),
            in_specs=[
                pl.BlockSpec((None, rows, IDX_DIM), lambda b, p, tbl: (b, 0, 0)),
                pl.BlockSpec((None, rows, 1), lambda b, p, tbl: (b, 0, 0)),
                *[page_spec(k) for k in range(pp)],
                pl.BlockSpec((None, PAGE_SIZE, IDX_DIM), lambda b, p, tbl: (b, 0, 0)),
            ],
            out_specs=pl.BlockSpec((None, t_new, lpad), lambda b, p, tbl: (b, 0, 0)),
            scratch_shapes=[pltpu.VMEM((t_new, lpad), jnp.int32)],
        ),
        out_shape=jax.ShapeDtypeStruct((bsz, t_new, lpad), BF16),
        compiler_params=_params("arbitrary", "arbitrary"),
        name="dsa_select_paged",
    )(page_table, qi_rows, wi_rows, *([kidx_pool] * pp), kidx_new)


def _col_vector(row_vec):
    n = row_vec.shape[1]
    eye = lax.broadcasted_iota(jnp.int32, (n, n), 0) == lax.broadcasted_iota(jnp.int32, (n, n), 1)
    return jnp.sum(jnp.where(eye, row_vec, 0.0), axis=1, keepdims=True)


def _paged_attn_kernel(tbl_ref, qbd_ref, *rest, pp):
    del tbl_ref
    k_refs, v_refs = rest[:pp], rest[pp:2 * pp]
    mt_ref, mnew_ref, knew_ref, vnew_ref, bias_ref, o_ref, m_sc, l_sc, acc_sc = rest[2 * pp:]
    p = pl.program_id(1)
    last = pl.num_programs(1) - 1

    @pl.when(p == 0)
    def _():
        m_sc[...] = jnp.full(m_sc.shape, MASKED, F32)
        l_sc[...] = jnp.zeros_like(l_sc)
        acc_sc[...] = jnp.zeros_like(acc_sc)

    n_kv = qbd_ref.shape[0] // HEAD_DIM

    def heads_on_lanes(ref):
        return jnp.concatenate([ref[pl.ds(g, PAGE_SIZE, stride=n_kv), :] for g in range(n_kv)], axis=1)

    def pages(kps, vps, masks, biases):
        scores = []
        for kp, mask, bias in zip(kps, masks, biases):
            s = jnp.dot(kp().astype(BF16), qbd_ref[...], preferred_element_type=F32)
            if bias is not None:
                s = s + bias
            scores.append(jnp.where(mask > 0, s, MASKED))
        m_old = m_sc[...]
        m_new = m_old
        for s in scores:
            m_new = jnp.maximum(m_new, jnp.max(s, axis=0, keepdims=True))
        alpha = jnp.exp2(m_old - m_new)
        l_new = alpha * l_sc[...]
        pv = None
        for s, vp in zip(scores, vps):
            pt = jnp.exp2(s - m_new)
            l_new = l_new + jnp.sum(pt, axis=0, keepdims=True)
            part = lax.dot_general(pt.astype(BF16), vp().astype(BF16), (((0,), (0,)), ((), ())),
                                   preferred_element_type=F32)
            pv = part if pv is None else pv + part
        l_sc[...] = l_new
        acc_sc[...] = acc_sc[...] * _col_vector(alpha) + pv
        m_sc[...] = m_new

    is_last = (p == last).astype(F32)
    pages([functools.partial(heads_on_lanes, r) for r in k_refs],
          [functools.partial(heads_on_lanes, r) for r in v_refs],
          [mt_ref[k * PAGE_SIZE:(k + 1) * PAGE_SIZE, :] for k in range(pp)],
          [None] * (pp - 1) + [bias_ref[0] * is_last])

    @pl.when(p == last)
    def _():
        pages([lambda: knew_ref[...]], [lambda: vnew_ref[...]], [mnew_ref[...]], [bias_ref[1]])
        o_ref[...] = acc_sc[...] / _col_vector(l_sc[...])


def _paged_attention(qbd, k_pool, v_pool, layer, page_table, mask_t, k_new, v_new, bias_t, *, pp):
    bsz, w, _ = qbd.shape
    n_pages = page_table.shape[1]
    assert n_pages % pp == 0
    n_layers, n_phys, _, n_kv, _ = k_pool.shape
    assert n_kv * HEAD_DIM == w
    as_rows = lambda pool: pool.reshape(n_layers, n_phys, PAGE_SIZE * n_kv, HEAD_DIM)
    k_pool, v_pool = as_rows(k_pool), as_rows(v_pool)
    page_spec = lambda k: pl.BlockSpec((None, None, PAGE_SIZE * n_kv, HEAD_DIM),
                                       lambda b, p, tbl: (layer, tbl[b, p * pp + k], 0, 0))
    per_b = lambda shp: pl.BlockSpec((None,) + shp, lambda b, p, tbl: (b, 0, 0))
    return pl.pallas_call(
        functools.partial(_paged_attn_kernel, pp=pp),
        grid_spec=pltpu.PrefetchScalarGridSpec(
            num_scalar_prefetch=1,
            grid=(bsz, n_pages // pp),
            in_specs=[
                per_b((w, LANES)),
                *[page_spec(k) for k in range(pp)],
                *[page_spec(k) for k in range(pp)],
                pl.BlockSpec((None, pp * PAGE_SIZE, LANES), lambda b, p, tbl: (b, p, 0)),
                pl.BlockSpec((None, PAGE_SIZE, LANES), lambda b, p, tbl: (b, n_pages, 0)),
                per_b((PAGE_SIZE, w)),
                per_b((PAGE_SIZE, w)),
                pl.BlockSpec((2, PAGE_SIZE, LANES), lambda b, p, tbl: (0, 0, 0)),
            ],
            out_specs=per_b((LANES, w)),
            scratch_shapes=[pltpu.VMEM((1, LANES), F32), pltpu.VMEM((1, LANES), F32), pltpu.VMEM((LANES, w), F32)],
        ),
        out_shape=jax.ShapeDtypeStruct((bsz, LANES, w), F32),
        compiler_params=_params("arbitrary", "arbitrary"),
        name="paged_attention",
    )(page_table, qbd, *([k_pool] * pp), *([v_pool] * pp), mask_t, mask_t, k_new, v_new, bias_t)


def _block_diag_queries(q, n_kv):
    bsz, t, _ = q.shape
    q4 = q.reshape(bsz, t, N_HEADS, HEAD_DIM) * QK_SCALE_LOG2
    onehot = (jnp.arange(N_HEADS)[:, None] // (N_HEADS // n_kv) == jnp.arange(n_kv)[None, :]).astype(q.dtype)
    qbd = q4[:, :, :, None, :] * onehot[None, None, :, :, None]
    return qbd.transpose(0, 3, 4, 2, 1).reshape(bsz, n_kv * HEAD_DIM, N_HEADS * t).astype(BF16)


def _own_head_lanes(out, n_kv, t):
    bsz = out.shape[0]
    out5 = out.reshape(bsz, N_HEADS, t, n_kv, HEAD_DIM)
    heads = jnp.arange(N_HEADS)
    picked = out5[:, heads, :, heads // (N_HEADS // n_kv), :]
    return picked.transpose(1, 2, 0, 3).reshape(bsz, t, N_HEADS * HEAD_DIM)


def _decode_bias(near_bias, tq, t):
    tail = near_bias[:, :t, tq - PAGE_SIZE:tq + PAGE_SIZE]
    return tail.transpose(2, 0, 1).reshape(2, PAGE_SIZE, N_HEADS * t)


def _pad_rows(a, rows):
    return jnp.pad(a, ((0, 0), (0, rows - a.shape[1]), (0, 0)))


def _new_token_mask(t):
    key = jnp.arange(PAGE_SIZE)[:, None]
    tok = jnp.tile(jnp.arange(t), N_HEADS)[None, :]
    return (key <= tok).astype(BF16)


def _split_heads_kernel(x_ref, o_ref, *, tm, n_heads):
    for h in range(n_heads):
        o_ref[pl.ds(h, tm, stride=n_heads), :] = x_ref[:, h * HEAD_DIM:(h + 1) * HEAD_DIM]


def _split_heads(proj, col, n_heads, *, tm):
    bsz, t, _ = proj.shape
    w = n_heads * HEAD_DIM
    assert col % w == 0 and t % tm == 0
    out = pl.pallas_call(
        functools.partial(_split_heads_kernel, tm=tm, n_heads=n_heads),
        grid=(bsz, t // tm),
        in_specs=[pl.BlockSpec((None, tm, w), lambda b, i: (b, i, col // w))],
        out_specs=pl.BlockSpec((None, tm * n_heads, HEAD_DIM), lambda b, i: (b, i, 0)),
        out_shape=jax.ShapeDtypeStruct((bsz, t * n_heads, HEAD_DIM), F32),
        compiler_params=_params("arbitrary", "arbitrary"),
        name="split_heads",
    )(proj)
    return out.reshape(bsz, t, n_heads, HEAD_DIM)


ATTN_TQ = 256
SELECT_TQ = 256
SELECT_TK = 512
MOBA_HEADS_PER_STEP = 4
SPLIT_TM = 256
Q_W = N_HEADS * HEAD_DIM
KV_W = N_KV_B * HEAD_DIM
DSA_PROJ = Q_W + 2 * KV_W + N_IDX_HEADS * IDX_DIM + IDX_DIM + N_IDX_HEADS
DSA_PROJ_PAD = -(-DSA_PROJ // 512) * 512


def _dsa_split(proj):
    k = proj[..., Q_W:Q_W + KV_W]
    v = proj[..., Q_W + KV_W:Q_W + 2 * KV_W]
    o = Q_W + 2 * KV_W + N_IDX_HEADS * IDX_DIM
    return k, v, proj[..., o:o + IDX_DIM]


def _dsa_prompt(proj, near_bias):
    bsz, t, _ = proj.shape
    ki = _dsa_split(proj)[2]
    mask = _dsa_select(proj, tq=min(SELECT_TQ, t), tk=min(SELECT_TK, t), topk=min(IDX_TOPK, t // 4))
    o = _attention(proj, Q_W, Q_W + KV_W, mask, near_bias, tq=ATTN_TQ, rep=N_HEADS // N_KV_B,
                   shared_kv=True, block_mask=False)
    k = _split_heads(proj, Q_W, N_KV_B, tm=min(SPLIT_TM, t))
    v = _split_heads(proj, Q_W + KV_W, N_KV_B, tm=min(SPLIT_TM, t))
    return o, k, v, ki


def _dsa_sample(proj, k_pool, v_pool, kidx_pool, layer, page_table, near_bias):
    bsz, t, _ = proj.shape
    past_len = page_table.shape[1] * PAGE_SIZE
    k, v, ki = _dsa_split(proj)
    o_qi = Q_W + 2 * KV_W
    qi = proj[..., o_qi:o_qi + N_IDX_HEADS * IDX_DIM].reshape(bsz, t, N_IDX_HEADS, IDX_DIM)
    wi = proj[..., o_qi + N_IDX_HEADS * IDX_DIM + IDX_DIM:DSA_PROJ]
    qi_rows = qi.transpose(0, 2, 1, 3).reshape(bsz, N_IDX_HEADS * t, IDX_DIM)
    wi_rows = wi.transpose(0, 2, 1).reshape(bsz, N_IDX_HEADS * t, 1)
    sel = _dsa_select_paged(qi_rows, wi_rows, kidx_pool, layer, page_table, _pad_rows(ki, PAGE_SIZE),
                            pp=8, topk=min(IDX_TOPK, (past_len + t) // 4))
    mask_t = jnp.tile(sel.transpose(0, 2, 1), (1, 1, N_HEADS))
    out = _paged_attention(
        _block_diag_queries(proj[..., :Q_W], N_KV_B), k_pool, v_pool, layer, page_table,
        mask_t, _pad_rows(k, PAGE_SIZE), _pad_rows(v, PAGE_SIZE), _decode_bias(near_bias, ATTN_TQ, t), pp=8)
    o = _own_head_lanes(out, N_KV_B, t).astype(BF16)
    return o, k.reshape(bsz, t, N_KV_B, HEAD_DIM), v.reshape(bsz, t, N_KV_B, HEAD_DIM), ki


def _moba_prompt(proj, near_bias):
    bsz, t, _ = proj.shape
    k_heads = _split_heads(proj, Q_W, N_HEADS, tm=min(SPLIT_TM, t))
    v_heads = _split_heads(proj, 2 * Q_W, N_HEADS, tm=min(SPLIT_TM, t))
    n_pages = t // PAGE_SIZE
    pages = jnp.arange(bsz * n_pages, dtype=jnp.int32).reshape(bsz, n_pages)
    means = _block_means(k_heads.reshape(1, bsz * n_pages, PAGE_SIZE, N_HEADS, HEAD_DIM), 0, pages)
    sel = _moba_select(proj, _pad_rows(means, LANES), tq=ATTN_TQ, past_len=0)
    o = _attention(proj, Q_W, 2 * Q_W, sel, near_bias, tq=ATTN_TQ, rep=MOBA_HEADS_PER_STEP,
                   shared_kv=False, block_mask=True)
    return o, k_heads, v_heads


def _moba_sample(proj, k_pool, v_pool, layer, page_table, near_bias):
    bsz, t, _ = proj.shape
    n_pages = page_table.shape[1]
    past_len = n_pages * PAGE_SIZE
    k = proj[..., Q_W:2 * Q_W]
    v = proj[..., 2 * Q_W:3 * Q_W]
    means = _block_means(k_pool, layer, page_table)
    sel = _moba_select(proj, _pad_rows(means, LANES), tq=t, past_len=past_len)
    past = jnp.repeat(sel[..., :past_len // MOBA_BLOCK], MOBA_BLOCK, axis=-1)
    past_t = past.transpose(0, 3, 1, 2).reshape(bsz, past_len, N_HEADS * t)
    new_t = jnp.broadcast_to(_new_token_mask(t), (bsz, PAGE_SIZE, N_HEADS * t))
    out = _paged_attention(
        _block_diag_queries(proj[..., :Q_W], N_HEADS), k_pool, v_pool, layer, page_table,
        jnp.concatenate([past_t, new_t], axis=1), _pad_rows(k, PAGE_SIZE), _pad_rows(v, PAGE_SIZE),
        _decode_bias(near_bias, ATTN_TQ, t), pp=4)
    o = _own_head_lanes(out, N_HEADS, t).astype(BF16)
    return o, k.reshape(bsz, t, N_HEADS, HEAD_DIM), v.reshape(bsz, t, N_HEADS, HEAD_DIM)


N_MIXERS = 3
PROMPT_TM = 1024
MATMUL_TN = 512
LRU_TT = 256
MOE_TM = 512
ROUTE_W = LANES


def kernel(x_prompt, x_sample, state_lru_h, state_lru_conv, cache_dsa_k, cache_dsa_v, cache_dsa_kidx,
           cache_moba_k, cache_moba_v, page_table, c_prompt, c_sample, rel_bias, norm_mix_g, norm_ffn_g,
           final_norm_g, ada_w, ada_b, lru_w_in, lru_conv_w, lru_conv_b, lru_w_rgate, lru_b_rgate,
           lru_w_igate, lru_b_igate, lru_lambda, lru_w_out, dsa_w_in, dsa_w_out, moba_w_in, moba_w_out,
           moe_w_group, moe_w_expert, moe_w1, moe_w3, moe_w2):
    bp, tp, d = x_prompt.shape
    bs, ts, _ = x_sample.shape
    depth = ada_w.shape[0]
    ns = bs * ts
    c_rnn = lru_w_out.shape[1]

    c_all = jnp.concatenate([c_prompt, c_sample], axis=0)
    c_rows = -(-c_all.shape[0] // SUBLANES) * SUBLANES
    mods = _ada(jnp.pad(c_all, ((0, c_rows - c_all.shape[0]), (0, 0))), ada_w, ada_b)
    mods = mods.reshape(depth, c_rows, 6, d)
    near_bias = _near_bias(rel_bias, ATTN_TQ)

    w1, w3, w2 = moe_w1.astype(BF16), moe_w3.astype(BF16), moe_w2.astype(BF16)
    xp = x_prompt
    pending = None
    xs = x_sample.reshape(1, ns, d)
    outs = {n: [] for n in ("lru_h_p", "lru_c_p", "lru_h_s", "lru_c_s", "dsa_k_p", "dsa_v_p", "dsa_i_p",
                            "dsa_k_s", "dsa_v_s", "dsa_i_s", "moba_k_p", "moba_v_p", "moba_k_s", "moba_v_s")}
    for i in range(depth):
        kind, j = i % N_MIXERS, i // N_MIXERS
        mod_p = [mods[i, :bp, m][:, None, :] for m in range(6)]
        mod_s = [jnp.repeat(mods[i, bp:bp + bs, m], ts, axis=0)[None] for m in range(6)]
        shp1, scp1, gtp1, shp2, scp2, gtp2 = mod_p
        shs1, scs1, gts1, shs2, scs2, gts2 = mod_s
        mix_in = lambda w, tn=MATMUL_TN: (
            _norm_matmul(xp, norm_mix_g[i], shp1, scp1, w, tm=min(PROMPT_TM, tp), tn=tn, layer=j),
            _norm_matmul(xs, norm_mix_g[i], shs1, scs1, w, tm=ns, tn=tn, layer=j))
        if kind == 0:
            up, us = mix_in(lru_w_in)
            lru_params = (lru_conv_w[j], lru_conv_b[j], lru_w_rgate[j], lru_b_rgate[j], lru_w_igate[j],
                          lru_b_igate[j], lru_lambda[j])
            yp, hp_state, cp = _lru_core(up, jnp.zeros((bp, CONV_W - 1, c_rnn), F32), jnp.zeros((bp, c_rnn), F32),
                                         *lru_params, tt=LRU_TT)
            ys, hs_state, cs = _lru_core(us.reshape(bs, ts, 2 * c_rnn), state_lru_conv[j], state_lru_h[j],
                                         *lru_params, tt=ts)
            ys = ys.reshape(1, ns, c_rnn)
            w_out = lru_w_out
            outs["lru_h_p"].append(hp_state)
            outs["lru_c_p"].append(cp)
            outs["lru_h_s"].append(hs_state)
            outs["lru_c_s"].append(cs)
        elif kind == 1:
            pp_, ps_ = mix_in(jnp.pad(dsa_w_in, ((0, 0), (0, 0), (0, DSA_PROJ_PAD - DSA_PROJ))))
            yp, kp, vp, ip = _dsa_prompt(pp_, near_bias)
            ys, k_s, v_s, i_s = _dsa_sample(ps_.reshape(bs, ts, DSA_PROJ_PAD), cache_dsa_k, cache_dsa_v,
                                            cache_dsa_kidx, j, page_table, near_bias)
            ys = ys.reshape(1, ns, Q_W)
            w_out = dsa_w_out
            for n, val in zip(("dsa_k_p", "dsa_v_p", "dsa_i_p", "dsa_k_s", "dsa_v_s", "dsa_i_s"),
                              (kp, vp, ip, k_s, v_s, i_s)):
                outs[n].append(val)
        else:
            pp_, ps_ = mix_in(moba_w_in)
            yp, kp, vp = _moba_prompt(pp_, near_bias)
            ys, k_s, v_s = _moba_sample(ps_.reshape(bs, ts, 3 * Q_W), cache_moba_k, cache_moba_v, j,
                                        page_table, near_bias)
            ys = ys.reshape(1, ns, Q_W)
            w_out = moba_w_out
            for n, val in zip(("moba_k_p", "moba_v_p", "moba_k_s", "moba_v_s"), (kp, vp, k_s, v_s)):
                outs[n].append(val)
        xp = _matmul_residual(yp, w_out, xp, gtp1, tm=min(PROMPT_TM, tp), tn=min(MATMUL_TN, d), layer=j)
        xs = _matmul_residual(ys, w_out, xs, gts1, tm=ns, tn=min(MATMUL_TN, d), layer=j)

        w_route = jnp.pad(jnp.concatenate([moe_w_group[i], moe_w_expert[i]], axis=1),
                          ((0, 0), (0, ROUTE_W - N_GROUPS - N_EXPERTS)))
        lg_p, h_p = _norm_matmul(xp, norm_ffn_g[i], shp2, scp2, w_route, tm=min(PROMPT_TM, tp), tn=ROUTE_W,
                                 emit_h=True, highest=True, h_dtype=F32)
        moe_tm = min(MOE_TM, tp)
        info, counts = _route_tokens(lg_p.reshape(bp * tp, ROUTE_W), tm=moe_tm)
        moe_p = _moe_grouped(h_p.reshape(bp * tp, d), info, *_group_layout(info, counts, moe_tm), w1, w3, w2, i,
                             tm=moe_tm)
        if i + 1 < depth:
            xp = _gated_add(xp, moe_p, gtp2, tm=moe_tm)
        else:
            pending = (moe_p, gtp2)
        lg_s, h_s = _norm_matmul(xs, norm_ffn_g[i], shs2, scs2, w_route, tm=ns, tn=ROUTE_W,
                                 emit_h=True, highest=True)
        xs = _moe_dense(h_s, lg_s, w1, w3, w2, i, xs, gts2, tm=ns)

    y_prompt = _rmsnorm(xp, final_norm_g, tm=min(MOE_TM, tp), residual=pending)
    y_sample = _rmsnorm(xs, final_norm_g, tm=ns).reshape(bs, ts, d)
    stack = lambda n: jnp.stack(outs[n])
    return (y_prompt, y_sample, stack("lru_h_p"), stack("lru_c_p"), stack("lru_h_s"), stack("lru_c_s"),
            stack("dsa_k_p"), stack("dsa_v_p"), stack("dsa_i_p"), stack("dsa_k_s"), stack("dsa_v_s"),
            stack("dsa_i_s"), stack("moba_k_p"), stack("moba_v_p"), stack("moba_k_s"), stack("moba_v_s"))
```

```python
import functools
import math

import numpy as np
import jax
import jax.numpy as jnp
from jax import lax
from jax.experimental import pallas as pl
from jax.experimental.pallas import tpu as pltpu

F32 = jnp.float32
BF16 = jnp.bfloat16

LANES = 128
SUBLANES = 8
VMEM_LIMIT_BYTES = 56 * 1024 * 1024

HEAD_DIM = 128
N_HEADS = 16
N_KV_B = 4
N_IDX_HEADS = 16
IDX_DIM = 64
IDX_TOPK = 256
IDX_SCALE = (IDX_DIM * N_IDX_HEADS) ** -0.5
PAGE_SIZE = 128
MOBA_BLOCK = 256
MOBA_TOPK = 3
N_BUCKETS = 32
MAX_DISTANCE = 128
N_RNN_BLOCKS = 16
CONV_W = 4
LRU_C = 8.0
N_GROUPS = 4
EXPERTS_PER_GROUP = 4
N_EXPERTS = N_GROUPS * EXPERTS_PER_GROUP
EPS = 1e-6
ATTN_SCALE = HEAD_DIM ** -0.5
LOG2E = math.log2(math.e)
QK_SCALE_LOG2 = ATTN_SCALE * LOG2E
NEG_INF = float("-inf")


def _params(*sem):
    return pltpu.CompilerParams(dimension_semantics=sem, vmem_limit_bytes=VMEM_LIMIT_BYTES)


def _bdot(a, b):
    return jnp.dot(a.astype(BF16), b.astype(BF16), preferred_element_type=F32)


def _bdot_nt(a, b):
    return lax.dot_general(a.astype(BF16), b.astype(BF16), (((1,), (1,)), ((), ())),
                           preferred_element_type=F32)


def _ada_kernel(c_ref, w_ref, b_ref, o_ref):
    c = c_ref[...]
    o_ref[...] = _bdot(c * jax.nn.sigmoid(c), w_ref[...]) + b_ref[...]


def _ada(c, ada_w, ada_b):
    depth, d, n = ada_w.shape
    rows = c.shape[0]
    tn = next(w for w in (1024, 512, 256, LANES) if n % w == 0)
    return pl.pallas_call(
        _ada_kernel,
        grid=(depth, n // tn),
        in_specs=[
            pl.BlockSpec((rows, d), lambda l, j: (0, 0)),
            pl.BlockSpec((None, d, tn), lambda l, j: (l, 0, j)),
            pl.BlockSpec((None, 1, tn), lambda l, j: (l, 0, j)),
        ],
        out_specs=pl.BlockSpec((None, rows, tn), lambda l, j: (l, 0, j)),
        out_shape=jax.ShapeDtypeStruct((depth, rows, n), F32),
        compiler_params=_params("arbitrary", "arbitrary"),
        name="ada_mod",
    )(c, ada_w, ada_b.reshape(depth, 1, n))


def _norm_mod(x, g, shift, scale):
    ms = jnp.mean(x * x, axis=-1, keepdims=True)
    h = x * lax.rsqrt(ms + EPS) * g
    return h * (1.0 + scale) + shift


def _norm_mm_kernel(x_ref, g_ref, sh_ref, sc_ref, w_ref, *refs, emit_h, highest):
    if emit_h:
        o_ref, ho_ref, h_ref = refs
    else:
        o_ref, h_ref = refs
        ho_ref = None

    @pl.when(pl.program_id(2) == 0)
    def _():
        h = _norm_mod(x_ref[...], g_ref[...], sh_ref[...], sc_ref[...])
        h_ref[...] = h.astype(h_ref.dtype)
        if ho_ref is not None:
            ho_ref[...] = h.astype(ho_ref.dtype)

    if highest:
        o_ref[...] = jnp.dot(h_ref[...], w_ref[...], precision=lax.Precision.HIGHEST,
                             preferred_element_type=F32)
    else:
        o_ref[...] = _bdot(h_ref[...], w_ref[...])


def _weight_spec(w, layer, tn):
    if w.ndim == 2:
        return pl.BlockSpec((w.shape[0], tn), lambda b, i, j: (0, j))
    return pl.BlockSpec((None, w.shape[1], tn), lambda b, i, j: (layer, 0, j))


def _norm_matmul(x, g, shift, scale, w, *, tm, tn, emit_h=False, highest=False, h_dtype=BF16, layer=None):
    bsz, t, d = x.shape
    n = w.shape[-1]
    r = shift.shape[1]
    assert t % tm == 0 and n % tn == 0 and (r == 1 or r == tm == t)
    out_shape = [jax.ShapeDtypeStruct((bsz, t, n), F32)]
    out_specs = [pl.BlockSpec((None, tm, tn), lambda b, i, j: (b, i, j))]
    row_tile = pl.BlockSpec((None, tm, d), lambda b, i, j: (b, i, 0))
    if emit_h:
        out_shape.append(jax.ShapeDtypeStruct((bsz, t, d), h_dtype))
        out_specs.append(row_tile)
    mod_map = (lambda b, i, j: (b, 0, 0)) if r == 1 else (lambda b, i, j: (b, i, 0))
    res = pl.pallas_call(
        functools.partial(_norm_mm_kernel, emit_h=emit_h, highest=highest),
        grid=(bsz, t // tm, n // tn),
        in_specs=[
            row_tile,
            pl.BlockSpec((1, d), lambda b, i, j: (0, 0)),
            pl.BlockSpec((None, r, d), mod_map),
            pl.BlockSpec((None, r, d), mod_map),
            _weight_spec(w, layer, tn),
        ],
        out_specs=out_specs,
        out_shape=out_shape,
        scratch_shapes=[pltpu.VMEM((tm, d), F32 if highest else BF16)],
        compiler_params=_params("arbitrary", "arbitrary", "arbitrary"),
        name="norm_matmul",
    )(x, g.reshape(1, d), shift, scale, w)
    return res if emit_h else res[0]


def _mm_res_kernel(a_ref, w_ref, r_ref, g_ref, o_ref):
    o_ref[...] = r_ref[...] + g_ref[...] * _bdot(a_ref[...], w_ref[...])


def _matmul_residual(a, w, res, gate, *, tm, tn, layer=None):
    bsz, t, k = a.shape
    n = w.shape[-1]
    r = gate.shape[1]
    assert t % tm == 0 and n % tn == 0 and (r == 1 or r == tm == t)
    gate_map = (lambda b, i, j: (b, 0, j)) if r == 1 else (lambda b, i, j: (b, i, j))
    return pl.pallas_call(
        _mm_res_kernel,
        grid=(bsz, t // tm, n // tn),
        in_specs=[
            pl.BlockSpec((None, tm, k), lambda b, i, j: (b, i, 0)),
            _weight_spec(w, layer, tn),
            pl.BlockSpec((None, tm, tn), lambda b, i, j: (b, i, j)),
            pl.BlockSpec((None, r, tn), gate_map),
        ],
        out_specs=pl.BlockSpec((None, tm, tn), lambda b, i, j: (b, i, j)),
        out_shape=jax.ShapeDtypeStruct((bsz, t, n), F32),
        compiler_params=_params("arbitrary", "arbitrary", "arbitrary"),
        name="matmul_residual",
    )(a, w, res, gate)


def _rmsnorm_kernel(x_ref, g_ref, *refs, residual):
    x = x_ref[...]
    if residual:
        y_ref, gy_ref, o_ref = refs
        x = x + gy_ref[...] * y_ref[...]
    else:
        o_ref, = refs
    ms = jnp.mean(x * x, axis=-1, keepdims=True)
    o_ref[...] = x * lax.rsqrt(ms + EPS) * g_ref[...]


def _rmsnorm(x, g, *, tm, residual=None):
    bsz, t, d = x.shape
    per_b = t // tm
    extra_in, extra_specs = [], []
    if residual is not None:
        extra_in = list(residual)
        extra_specs = [pl.BlockSpec((tm, d), lambda b, i: (b * per_b + i, 0)),
                       pl.BlockSpec((None, 1, d), lambda b, i: (b, 0, 0))]
    return pl.pallas_call(
        functools.partial(_rmsnorm_kernel, residual=residual is not None),
        grid=(bsz, t // tm),
        in_specs=[pl.BlockSpec((None, tm, d), lambda b, i: (b, i, 0)),
                  pl.BlockSpec((1, d), lambda b, i: (0, 0)), *extra_specs],
        out_specs=pl.BlockSpec((None, tm, d), lambda b, i: (b, i, 0)),
        out_shape=jax.ShapeDtypeStruct((bsz, t, d), F32),
        compiler_params=_params("arbitrary", "arbitrary"),
        name="final_rmsnorm",
    )(x, g.reshape(1, d), *extra_in)


GATE_CHUNK = 640


def _log_sigmoid(x):
    return jnp.minimum(x, 0.0) - jnp.log1p(jnp.exp(-jnp.abs(x)))


def _lru_kernel(gb_ref, xb_ref, cs_ref, h0_ref, cw_ref, cb_ref, wr_ref, br_ref, wi_ref, bi_ref, lam_ref,
                y_ref, hl_ref, nc_ref, ext_ref, a_ref, b_ref, hs_ref, carry_ref, *, tt):
    c_all = xb_ref.shape[-1]

    @pl.when(pl.program_id(1) == 0)
    def _():
        ext_ref[0:SUBLANES, :] = cs_ref[...]
        carry_ref[...] = h0_ref[...]

    xb = xb_ref[...]
    ext_ref[SUBLANES:SUBLANES + tt, :] = xb
    cw = cw_ref[...]
    xc = cb_ref[...] + xb * cw[CONV_W - 1:CONV_W, :]
    for j in range(1, CONV_W):
        xc = xc + ext_ref[pl.ds(SUBLANES - j, tt), :] * cw[CONV_W - 1 - j:CONV_W - j, :]
    tail = ext_ref[tt:tt + SUBLANES, :]
    ext_ref[0:SUBLANES, :] = tail
    nc_ref[...] = tail

    xcb = xc.astype(BF16)
    r_parts, i_parts = [], []
    for c in range(c_all // GATE_CHUNK):
        sl = slice(c * GATE_CHUNK, (c + 1) * GATE_CHUNK)
        r_parts.append(jnp.dot(xcb[:, sl], wr_ref[c], preferred_element_type=F32))
        i_parts.append(jnp.dot(xcb[:, sl], wi_ref[c], preferred_element_type=F32))
    r = jax.nn.sigmoid(jnp.concatenate(r_parts, axis=1) + br_ref[...])
    ig = jax.nn.sigmoid(jnp.concatenate(i_parts, axis=1) + bi_ref[...])
    log_a = LRU_C * r * _log_sigmoid(lam_ref[...])
    th = jnp.tanh(log_a)
    one_minus_a2 = -2.0 * th / (1.0 - th)
    a_ref[...] = jnp.exp(log_a)
    b_ref[...] = jnp.sqrt(jnp.maximum(one_minus_a2, 0.0)) * (ig * xc)

    row = lax.broadcasted_iota(jnp.int32, (SUBLANES, GATE_CHUNK), 0)
    for c in range(c_all // GATE_CHUNK):
        sl = slice(c * GATE_CHUNK, (c + 1) * GATE_CHUNK)

        def body(grp, carry, sl=sl):
            r8 = pl.multiple_of(grp * SUBLANES, SUBLANES)
            av = a_ref[pl.ds(r8, SUBLANES), sl]
            bv = b_ref[pl.ds(r8, SUBLANES), sl]
            for s in (1, 2, 4):
                m = row >= s
                a_sh = jnp.where(m, pltpu.roll(av, s, axis=0), 1.0)
                b_sh = jnp.where(m, pltpu.roll(bv, s, axis=0), 0.0)
                bv = av * b_sh + bv
                av = av * a_sh
            h = av * carry + bv
            hs_ref[pl.ds(r8, SUBLANES), sl] = h
            return h[SUBLANES - 1:SUBLANES, :]

        carry_ref[:, sl] = lax.fori_loop(0, tt // SUBLANES, body, carry_ref[:, sl])

    hl_ref[...] = carry_ref[...]
    y_ref[...] = (jax.nn.gelu(gb_ref[...]) * hs_ref[...]).astype(y_ref.dtype)


def _lru_core(u, conv_state, h0, conv_w, conv_b, w_r, b_r, w_i, b_i, lam, *, tt):
    bsz, t, c2 = u.shape
    c = c2 // 2
    blk = c // N_RNN_BLOCKS
    per = GATE_CHUNK // blk
    nchunk = c // GATE_CHUNK

    def block_diag(w):
        wc = w.reshape(nchunk, per, blk, blk).astype(BF16)
        eye = jnp.eye(per, dtype=BF16)
        return jnp.einsum("cpij,pq->cpiqj", wc, eye).reshape(nchunk, GATE_CHUNK, GATE_CHUNK)

    cs_pad = jnp.pad(conv_state, ((0, 0), (SUBLANES - (CONV_W - 1), 0), (0, 0)))
    row = lambda v: v.reshape(1, c)
    full = lambda shp: pl.BlockSpec(shp, lambda b, i: (0,) * len(shp))
    y, hl, nc = pl.pallas_call(
        functools.partial(_lru_kernel, tt=tt),
        grid=(bsz, t // tt),
        in_specs=[
            pl.BlockSpec((None, tt, c), lambda b, i: (b, i, 0)),
            pl.BlockSpec((None, tt, c), lambda b, i: (b, i, 1)),
            pl.BlockSpec((None, SUBLANES, c), lambda b, i: (b, 0, 0)),
            pl.BlockSpec((None, 1, c), lambda b, i: (b, 0, 0)),
            full((CONV_W, c)), full((1, c)),
            full((nchunk, GATE_CHUNK, GATE_CHUNK)), full((1, c)),
            full((nchunk, GATE_CHUNK, GATE_CHUNK)), full((1, c)),
            full((1, c)),
        ],
        out_specs=[
            pl.BlockSpec((None, tt, c), lambda b, i: (b, i, 0)),
            pl.BlockSpec((None, 1, c), lambda b, i: (b, 0, 0)),
            pl.BlockSpec((None, SUBLANES, c), lambda b, i: (b, 0, 0)),
        ],
        out_shape=[
            jax.ShapeDtypeStruct((bsz, t, c), BF16),
            jax.ShapeDtypeStruct((bsz, 1, c), F32),
            jax.ShapeDtypeStruct((bsz, SUBLANES, c), F32),
        ],
        scratch_shapes=[
            pltpu.VMEM((tt + SUBLANES, c), F32),
            pltpu.VMEM((tt, c), F32),
            pltpu.VMEM((tt, c), F32),
            pltpu.VMEM((tt, c), F32),
            pltpu.VMEM((1, c), F32),
        ],
        compiler_params=_params("arbitrary", "arbitrary"),
        name="lru_core",
    )(u, u, cs_pad, h0.reshape(bsz, 1, c), conv_w, row(conv_b), block_diag(w_r), row(b_r),
      block_diag(w_i), row(b_i), row(lam))
    return y, hl.reshape(bsz, c), nc[:, SUBLANES - (CONV_W - 1):, :]


def _route_gates(lg):
    return _route(lg)[0]


def _route(lg):
    col = lax.broadcasted_iota(jnp.int32, lg.shape, 1)
    big = jnp.int32(LANES)
    is_g = col < N_GROUPS
    gl = jnp.where(is_g, lg, NEG_INF)
    gmax = jnp.max(gl, axis=1, keepdims=True)
    g_sel = jnp.min(jnp.where(gl == gmax, col, big), axis=1, keepdims=True)
    g_w = 1.0 / jnp.sum(jnp.exp(gl - gmax), axis=1, keepdims=True)
    eid = col - N_GROUPS
    in_grp = (eid >= 0) & (eid < N_EXPERTS) & ((eid // EXPERTS_PER_GROUP) == g_sel)
    e_in = jnp.where(in_grp, lg, NEG_INF)
    top1 = jnp.max(e_in, axis=1, keepdims=True)
    idx1 = jnp.min(jnp.where((e_in == top1) & in_grp, col, big), axis=1, keepdims=True)
    rest = in_grp & (col != idx1)
    e2 = jnp.where(rest, lg, NEG_INF)
    top2 = jnp.max(e2, axis=1, keepdims=True)
    idx2 = jnp.min(jnp.where((e2 == top2) & rest, col, big), axis=1, keepdims=True)
    z = jnp.exp(top2 - top1)
    w_first = g_w / (1.0 + z)
    w_second = g_w * z / (1.0 + z)
    return jnp.where(col == idx1, w_first, 0.0) + jnp.where(col == idx2, w_second, 0.0), g_sel


def _moe_dense_kernel(h_ref, lg_ref, w1_ref, w3_ref, w2_ref, x_ref, g_ref, o_ref, gates_ref, acc_ref):
    e = pl.program_id(2)

    @pl.when(e == 0)
    def _():
        gates_ref[...] = _route_gates(lg_ref[...])
        acc_ref[...] = jnp.zeros_like(acc_ref)

    gates = gates_ref[...]
    col = lax.broadcasted_iota(jnp.int32, gates.shape, 1)
    ge = jnp.sum(jnp.where(col == e + N_GROUPS, gates, 0.0), axis=1, keepdims=True)
    h = h_ref[...]
    a1 = jnp.dot(h, w1_ref[...], preferred_element_type=F32)
    a3 = jnp.dot(h, w3_ref[...], preferred_element_type=F32)
    hidden = (a1 * jax.nn.sigmoid(a1)) * a3 * ge
    acc_ref[...] += jnp.dot(hidden.astype(BF16), w2_ref[...], preferred_element_type=F32)

    @pl.when(e == pl.num_programs(2) - 1)
    def _():
        o_ref[...] = x_ref[...] + g_ref[...] * acc_ref[...]


def _moe_dense(h, logits, w1, w3, w2, layer, x, gate, *, tm):
    bsz, t, d = x.shape
    _, ne, _, f = w1.shape
    r = gate.shape[1]
    gate_map = (lambda b, i, e: (b, 0, 0)) if r == 1 else (lambda b, i, e: (b, i, 0))
    tok = lambda b, i, e: (b, i, 0)
    expert = lambda b, i, e: (layer, e, 0, 0)
    return pl.pallas_call(
        _moe_dense_kernel,
        grid=(bsz, t // tm, ne),
        in_specs=[
            pl.BlockSpec((None, tm, d), tok),
            pl.BlockSpec((None, tm, LANES), tok),
            pl.BlockSpec((None, None, d, f), expert),
            pl.BlockSpec((None, None, d, f), expert),
            pl.BlockSpec((None, None, f, d), expert),
            pl.BlockSpec((None, tm, d), tok),
            pl.BlockSpec((None, r, d), gate_map),
        ],
        out_specs=pl.BlockSpec((None, tm, d), tok),
        out_shape=jax.ShapeDtypeStruct((bsz, t, d), F32),
        scratch_shapes=[pltpu.VMEM((tm, LANES), F32), pltpu.VMEM((tm, d), F32)],
        compiler_params=_params("arbitrary", "arbitrary", "arbitrary"),
        name="moe_dense",
    )(h, logits, w1, w3, w2, x, gate)


INFO_GROUP_LANE = 0
INFO_RANK_LANE = 1


def _route_kernel(lg_ref, info_ref, cnt_ref, carry_ref):
    @pl.when(pl.program_id(0) == 0)
    def _():
        carry_ref[...] = jnp.zeros_like(carry_ref)

    gates, g_sel = _route(lg_ref[...])
    tm = gates.shape[0]
    col = lax.broadcasted_iota(jnp.int32, gates.shape, 1)
    onehot = (col == g_sel).astype(F32)
    before = (lax.broadcasted_iota(jnp.int32, (tm, tm), 0) > lax.broadcasted_iota(jnp.int32, (tm, tm), 1))
    earlier = jnp.dot(before.astype(BF16), onehot.astype(BF16), preferred_element_type=F32) + carry_ref[...]
    rank = jnp.sum(jnp.where(col == g_sel, earlier, 0.0), axis=1, keepdims=True)
    carry_ref[...] += jnp.sum(onehot, axis=0, keepdims=True)
    info_ref[...] = (gates + jnp.where(col == INFO_GROUP_LANE, g_sel.astype(F32), 0.0)
                     + jnp.where(col == INFO_RANK_LANE, rank, 0.0))
    cnt_ref[...] = carry_ref[...]


def _route_tokens(logits, *, tm):
    n = logits.shape[0]
    assert n % tm == 0
    return pl.pallas_call(
        _route_kernel,
        grid=(n // tm,),
        in_specs=[pl.BlockSpec((tm, LANES), lambda i: (i, 0))],
        out_specs=[pl.BlockSpec((tm, LANES), lambda i: (i, 0)), pl.BlockSpec((1, LANES), lambda i: (0, 0))],
        out_shape=[jax.ShapeDtypeStruct((n, LANES), F32), jax.ShapeDtypeStruct((1, LANES), F32)],
        scratch_shapes=[pltpu.VMEM((1, LANES), F32)],
        compiler_params=_params("arbitrary"),
        name="moe_route",
    )(logits)


def _group_layout(info, counts, tm):
    n = info.shape[0]
    g_sel = info[:, INFO_GROUP_LANE].astype(jnp.int32)
    rank = info[:, INFO_RANK_LANE].astype(jnp.int32)
    cnt = counts[0, :N_GROUPS].astype(jnp.int32)
    padded = -(-cnt // tm) * tm
    ends = jnp.cumsum(padded)
    starts = ends - padded
    n_slots = n + N_GROUPS * tm
    src = _slot_tokens(starts[g_sel] + rank, n_slots)
    tile_start = jnp.arange(n_slots // tm, dtype=jnp.int32) * tm
    tile_group = jnp.minimum(jnp.sum((tile_start[:, None] >= ends[None, :]).astype(jnp.int32), axis=1), N_GROUPS - 1)
    slot_group = jnp.repeat(tile_group, tm)
    is_pad = jnp.arange(n_slots, dtype=jnp.int32) - starts[slot_group] >= cnt[slot_group]
    pad_row = n + tm + jnp.cumsum(is_pad.astype(jnp.int32)) - 1
    dst = jnp.concatenate([n + jnp.arange(tm, dtype=jnp.int32), jnp.where(is_pad, pad_row, src)])
    return src, dst, tile_group, (ends[-1] // tm).reshape(1)


SLOT_UNROLL = 8


def _slot_tokens_kernel(slot_ref, src_ref):
    n, n_slots = slot_ref.shape[0], src_ref.shape[0]

    def clear(c, carry):
        for k in range(SLOT_UNROLL):
            src_ref[c * SLOT_UNROLL + k] = 0
        return carry

    def place(c, carry):
        for k in range(SLOT_UNROLL):
            t = c * SLOT_UNROLL + k
            src_ref[slot_ref[t]] = t
        return carry

    lax.fori_loop(0, n_slots // SLOT_UNROLL, clear, 0)
    lax.fori_loop(0, n // SLOT_UNROLL, place, 0)


def _slot_tokens(slot, n_slots):
    assert slot.shape[0] % SLOT_UNROLL == 0 and n_slots % SLOT_UNROLL == 0
    return pl.pallas_call(
        _slot_tokens_kernel,
        in_specs=[pl.BlockSpec(memory_space=pltpu.SMEM)],
        out_specs=pl.BlockSpec(memory_space=pltpu.SMEM),
        out_shape=jax.ShapeDtypeStruct((n_slots,), jnp.int32),
        name="moe_slot_tokens",
    )(slot)


def _moe_group_kernel(src_ref, dst_ref, tg_ref, used_ref, h_hbm, info_hbm, w1_ref, w3_ref, w2_ref, zeros_hbm,
                      out_hbm, xbuf, gbuf, xb, gates, acc, obuf, sem_x, sem_g, sem_o, *, tm):
    del zeros_hbm
    i = pl.program_id(0)
    e = pl.program_id(1)
    used = used_ref[0]
    live = i < used
    chunk = tm // EXPERTS_PER_GROUP
    first = e * chunk
    last_step = e == EXPERTS_PER_GROUP - 1

    def gather_row(tile, r):
        tok = src_ref[tile * tm + r]
        pltpu.make_async_copy(h_hbm.at[pl.ds(tok, 1), :], xbuf.at[pl.ds(r, 1), :], sem_x).start()
        pltpu.make_async_copy(info_hbm.at[pl.ds(tok, 1), :], gbuf.at[pl.ds(r, 1), :], sem_g).start()

    def scatter_chunk(tile):
        slot = (tile + 2) % 2
        for k in range(chunk):
            row = dst_ref[(tile + 1) * tm + first + k]
            pltpu.make_async_copy(obuf.at[slot, pl.ds(first + k, 1), :], out_hbm.at[pl.ds(row, 1), :],
                                  sem_o.at[slot]).start()

    def wait_scatter(slot):
        pltpu.make_async_copy(obuf.at[slot], out_hbm.at[pl.ds(0, tm), :], sem_o.at[slot]).wait()

    @pl.when((i == 0) & (e == 0))
    def _():
        obuf[...] = jnp.zeros_like(obuf)

        def body(r, carry):
            gather_row(0, r)
            return carry
        lax.fori_loop(0, tm, body, 0)

    @pl.when((e == 0) & (i <= used))
    def _():
        pltpu.make_async_copy(h_hbm.at[pl.ds(0, tm), :], xbuf, sem_x).wait()
        pltpu.make_async_copy(info_hbm.at[pl.ds(0, tm), :], gbuf, sem_g).wait()

    @pl.when((e == 0) & live)
    def _():
        xb[...] = xbuf[...].astype(BF16)
        gates[...] = gbuf[...]
        acc[...] = jnp.zeros_like(acc)

    @pl.when(live)
    def _():
        lane = lax.broadcasted_iota(jnp.int32, (tm, LANES), 1)
        gate_lane = N_GROUPS + EXPERTS_PER_GROUP * tg_ref[i] + e
        ge = jnp.sum(jnp.where(lane == gate_lane, gates[...], 0.0), axis=1, keepdims=True)
        x = xb[...]
        a1 = jnp.dot(x, w1_ref[...], preferred_element_type=F32)
        a3 = jnp.dot(x, w3_ref[...], preferred_element_type=F32)
        hidden = (a1 * jax.nn.sigmoid(a1)) * a3 * ge
        acc[...] += jnp.dot(hidden.astype(BF16), w2_ref[...], preferred_element_type=F32)
        for k in range(chunk):
            gather_row(i + 1, first + k)
        scatter_chunk(i - 1)

    @pl.when(i == used)
    def _():
        scatter_chunk(i - 1)

    @pl.when(last_step & live & (i >= 1))
    def _():
        wait_scatter(i % 2)

    @pl.when(last_step & live)
    def _():
        obuf[i % 2] = acc[...]

    @pl.when(last_step & (i == used))
    def _():
        wait_scatter(0)
        wait_scatter(1)


def _moe_grouped(h, info, src, dst, tile_group, n_used, w1, w3, w2, layer, *, tm):
    n, d = h.shape
    f = w1.shape[3]
    n_tiles = src.shape[0] // tm
    n_rows = n + (N_GROUPS + 1) * tm
    assert tm % EXPERTS_PER_GROUP == 0 and dst.shape[0] == (n_tiles + 1) * tm

    def expert(i, e, src_, dst_, tg, used):
        e_eff = jnp.where(i < used[0], e, EXPERTS_PER_GROUP - 1)
        return (layer, tg[i] * EXPERTS_PER_GROUP + e_eff, 0, 0)

    return pl.pallas_call(
        functools.partial(_moe_group_kernel, tm=tm),
        grid_spec=pltpu.PrefetchScalarGridSpec(
            num_scalar_prefetch=4,
            grid=(n_tiles, EXPERTS_PER_GROUP),
            in_specs=[
                pl.BlockSpec(memory_space=pl.ANY),
                pl.BlockSpec(memory_space=pl.ANY),
                pl.BlockSpec((None, None, d, f), expert),
                pl.BlockSpec((None, None, d, f), expert),
                pl.BlockSpec((None, None, f, d), expert),
                pl.BlockSpec(memory_space=pl.ANY),
            ],
            out_specs=pl.BlockSpec(memory_space=pl.ANY),
            scratch_shapes=[
                pltpu.VMEM((tm, d), F32), pltpu.VMEM((tm, LANES), F32), pltpu.VMEM((tm, d), BF16),
                pltpu.VMEM((tm, LANES), F32), pltpu.VMEM((tm, d), F32), pltpu.VMEM((2, tm, d), F32),
                pltpu.SemaphoreType.DMA(()), pltpu.SemaphoreType.DMA(()), pltpu.SemaphoreType.DMA((2,)),
            ],
        ),
        out_shape=jax.ShapeDtypeStruct((n_rows, d), F32),
        input_output_aliases={9: 0},
        compiler_params=_params("arbitrary", "arbitrary"),
        name="moe_grouped",
    )(src, dst, tile_group, n_used, h, info, w1, w3, w2, jnp.zeros((n_rows, d), F32))


def _gated_add_kernel(x_ref, y_ref, g_ref, o_ref):
    o_ref[...] = x_ref[...] + g_ref[...] * y_ref[...]


def _gated_add(x, y, gate, *, tm):
    bsz, t, d = x.shape
    per_b = t // tm
    return pl.pallas_call(
        _gated_add_kernel,
        grid=(bsz, per_b),
        in_specs=[
            pl.BlockSpec((None, tm, d), lambda b, i: (b, i, 0)),
            pl.BlockSpec((tm, d), lambda b, i: (b * per_b + i, 0)),
            pl.BlockSpec((None, 1, d), lambda b, i: (b, 0, 0)),
        ],
        out_specs=pl.BlockSpec((None, tm, d), lambda b, i: (b, i, 0)),
        out_shape=jax.ShapeDtypeStruct((bsz, t, d), F32),
        compiler_params=_params("arbitrary", "arbitrary"),
        name="gated_add",
    )(x, y, gate)


MASKED = -1e30


def _t5_bucket_table(tq):
    qi = np.arange(tq, dtype=np.int32)[:, None]
    col = np.arange(2 * tq, dtype=np.int32)[None, :]
    n = np.maximum(qi + tq - col, 0)
    max_exact = N_BUCKETS // 2
    nf = np.maximum(n, 1).astype(np.float32)
    large = max_exact + (np.log(nf / np.float32(max_exact)) / np.float32(math.log(MAX_DISTANCE / max_exact))
                         * np.float32(N_BUCKETS - max_exact)).astype(np.int32)
    table = np.where(n < max_exact, n, np.minimum(large, N_BUCKETS - 1)).astype(np.int32)
    assert table[0, 0] == N_BUCKETS - 1
    return table


def _bias_kernel(rb_ref, bk_ref, o_ref):
    h = pl.program_id(0)
    bk = bk_ref[...]
    acc = jnp.zeros(bk.shape, F32)
    for k in range(N_BUCKETS):
        acc = jnp.where(bk == k, rb_ref[k, h], acc)
    o_ref[...] = (acc - rb_ref[N_BUCKETS - 1, h]) * LOG2E


def _near_bias(rel_bias, tq):
    return pl.pallas_call(
        _bias_kernel,
        grid=(N_HEADS,),
        in_specs=[pl.BlockSpec(memory_space=pltpu.SMEM),
                  pl.BlockSpec((tq, 2 * tq), lambda h: (0, 0))],
        out_specs=pl.BlockSpec((None, tq, 2 * tq), lambda h: (h, 0, 0)),
        out_shape=jax.ShapeDtypeStruct((N_HEADS, tq, 2 * tq), F32),
        name="near_bias",
    )(rel_bias, jnp.asarray(_t5_bucket_table(tq)))


_NEG_INF_KEY = int(np.int32(np.array(-np.inf, np.float32).view(np.int32)) ^ np.int32(0x7FFFFFFF))


def _sort_key(x):
    k = pltpu.bitcast(x, jnp.int32)
    return k ^ ((k >> 31) & jnp.int32(0x7FFFFFFF))


def _kth_largest_key(count_ge, kth):
    zero = jnp.int32(0)
    first = count_ge(zero)
    ans = jnp.where(first >= kth, zero, jnp.int32(-2 ** 31))
    settled = (first == kth).astype(jnp.int32)

    def cond(state):
        it, _, settled = state
        return (it < 31) & (jnp.min(settled) == 0)

    def body(state):
        it, ans, settled = state
        cand = ans | (jnp.int32(1) << (30 - it))
        count = count_ge(cand)
        return it + 1, jnp.where(count >= kth, cand, ans), settled | (count == kth).astype(jnp.int32)

    return lax.while_loop(cond, body, (jnp.int32(0), ans, settled))[1]


def _dsa_select_kernel(qi_ref, kw_ref, kall_ref, m_ref, key_ref, *, tq, tk, topk):
    i = pl.program_id(1)
    n_kt = (i * tq + tq + tk - 1) // tk
    qpos = i * tq + lax.broadcasted_iota(jnp.int32, (tq, tk), 0)
    wi = kw_ref[:, IDX_DIM:IDX_DIM + N_IDX_HEADS]
    qi = qi_ref[...].astype(BF16)
    key_ref[...] = jnp.full(key_ref.shape, _NEG_INF_KEY, jnp.int32)

    def score_tile(kt, carry):
        k0 = pl.multiple_of(kt * tk, tk)
        kid = kall_ref[pl.ds(k0, tk), 0:IDX_DIM].astype(BF16)
        acc = jnp.zeros((tq, tk), F32)
        for h in range(N_IDX_HEADS):
            s = _bdot_nt(qi[:, h * IDX_DIM:(h + 1) * IDX_DIM], kid)
            acc = acc + wi[:, h:h + 1] * jnp.maximum(s, 0.0)
        kpos = k0 + lax.broadcasted_iota(jnp.int32, (tq, tk), 1)
        sc = jnp.where(kpos <= qpos, acc * IDX_SCALE, NEG_INF)
        key_ref[:, pl.ds(k0, tk)] = _sort_key(sc)
        return carry

    lax.fori_loop(0, n_kt, score_tile, 0)

    def count_ge(cand):
        def body(kt, acc):
            k0 = pl.multiple_of(kt * tk, tk)
            ge = (key_ref[:, pl.ds(k0, tk)] >= cand).astype(jnp.int32)
            for j in range(tk // LANES):
                acc = acc + ge[:, j * LANES:(j + 1) * LANES]
            return acc
        acc = lax.fori_loop(0, n_kt, body, jnp.zeros((tq, LANES), jnp.int32))
        return jnp.sum(acc, axis=1, keepdims=True)

    thr = _kth_largest_key(count_ge, topk)
    keys = key_ref[...]
    m_ref[...] = ((keys >= thr) & (keys > _NEG_INF_KEY)).astype(m_ref.dtype)


def _dsa_select(proj, *, tq, tk, topk):
    bsz, t, _ = proj.shape
    qi_w = N_IDX_HEADS * IDX_DIM
    qi_blk = (N_HEADS * HEAD_DIM + 2 * N_KV_B * HEAD_DIM) // qi_w
    kw_blk = (N_HEADS * HEAD_DIM + 2 * N_KV_B * HEAD_DIM + qi_w) // LANES
    return pl.pallas_call(
        functools.partial(_dsa_select_kernel, tq=tq, tk=tk, topk=topk),
        grid=(bsz, t // tq),
        in_specs=[
            pl.BlockSpec((None, tq, qi_w), lambda b, i: (b, i, qi_blk)),
            pl.BlockSpec((None, tq, LANES), lambda b, i: (b, i, kw_blk)),
            pl.BlockSpec((None, t, LANES), lambda b, i: (b, 0, kw_blk)),
        ],
        out_specs=pl.BlockSpec((None, tq, t), lambda b, i: (b, i, 0)),
        out_shape=jax.ShapeDtypeStruct((bsz, t, t), BF16),
        scratch_shapes=[pltpu.VMEM((tq, t), jnp.int32)],
        compiler_params=_params("arbitrary", "arbitrary"),
        name="dsa_select",
    )(proj, proj, proj)


def _attn_kernel(q_ref, k_ref, v_ref, m_ref, tb_ref, o_ref, *state, tq, rep, shared_kv, block_mask):
    m_sc, l_sc, acc_sc = state[:rep], state[rep:2 * rep], state[2 * rep:]
    i = pl.program_id(2)
    for r in range(rep):
        m_sc[r][...] = jnp.full((tq, LANES), MASKED, F32)
        l_sc[r][...] = jnp.zeros((tq, LANES), F32)
        acc_sc[r][...] = jnp.zeros((tq, HEAD_DIM), F32)
    lane_tiles = tq // LANES

    def row_reduce(x, op, reduce):
        part = x[:, :LANES]
        for c in range(1, lane_tiles):
            part = op(part, x[:, c * LANES:(c + 1) * LANES])
        return reduce(part, axis=1, keepdims=True)

    def key_mask(j, r, causal):
        if block_mask:
            lane = lax.broadcasted_iota(jnp.int32, (tq, LANES), 1)
            vis = jnp.sum(jnp.where(lane == j, m_ref[r].astype(F32), 0.0), axis=1, keepdims=True)
            ok = jnp.broadcast_to(vis > 0.0, (tq, tq))
        else:
            ok = m_ref[:, pl.ds(pl.multiple_of(j * tq, tq), tq)] > 0
        if causal:
            ok = ok & (lax.broadcasted_iota(jnp.int32, (tq, tq), 0) >= lax.broadcasted_iota(jnp.int32, (tq, tq), 1))
        return ok

    def step(j, bias_lo, causal):
        k0 = pl.multiple_of(j * tq, tq)
        kv = lambda r: slice(0, HEAD_DIM) if shared_kv else slice(r * HEAD_DIM, (r + 1) * HEAD_DIM)
        ok_shared = None if block_mask else key_mask(j, 0, causal)
        scores = []
        for r in range(rep):
            q = (q_ref[:, r * HEAD_DIM:(r + 1) * HEAD_DIM] * QK_SCALE_LOG2).astype(BF16)
            s = _bdot_nt(q, k_ref[pl.ds(k0, tq), kv(r)])
            if bias_lo is not None:
                s = s + tb_ref[r, :, bias_lo:bias_lo + tq]
            scores.append(jnp.where(key_mask(j, r, causal) if block_mask else ok_shared, s, MASKED))
        probs, alphas = [], []
        for r in range(rep):
            m_old = m_sc[r][...]
            m_new = jnp.maximum(m_old, row_reduce(scores[r], jnp.maximum, jnp.max))
            alpha = jnp.exp2(m_old - m_new)
            p = jnp.exp2(scores[r] - jnp.concatenate([m_new] * lane_tiles, axis=1))
            l_sc[r][...] = alpha * l_sc[r][...] + row_reduce(p, jnp.add, jnp.sum)
            m_sc[r][...] = m_new
            probs.append(p.astype(BF16))
            alphas.append(alpha)
        for r in range(rep):
            pv = jnp.dot(probs[r], v_ref[pl.ds(k0, tq), kv(r)].astype(BF16), preferred_element_type=F32)
            acc_sc[r][...] = alphas[r] * acc_sc[r][...] + pv

    def far(j, carry):
        step(j, None, False)
        return carry

    lax.fori_loop(0, jnp.maximum(i - 1, 0), far, 0)

    @pl.when(i >= 1)
    def _():
        step(i - 1, 0, False)

    step(i, tq, True)
    o_ref[...] = jnp.concatenate([acc_sc[r][...] / l_sc[r][...] for r in range(rep)], axis=1).astype(o_ref.dtype)


def _attention(proj, k_col, v_col, mask, near_bias, *, tq, rep, shared_kv, block_mask):
    bsz, t, _ = proj.shape
    g = N_HEADS // rep
    kv_w = HEAD_DIM if shared_kv else rep * HEAD_DIM
    assert HEAD_DIM == LANES and k_col % kv_w == 0 and v_col % kv_w == 0
    k_blk, v_blk = k_col // kv_w, v_col // kv_w
    if block_mask:
        assert tq == MOBA_BLOCK
        mask_spec = pl.BlockSpec((None, rep, tq, LANES), lambda b, h, i: (b, h, i, 0))
    else:
        mask_spec = pl.BlockSpec((None, tq, t), lambda b, h, i: (b, i, 0))
    stat = pltpu.VMEM((tq, LANES), F32)
    return pl.pallas_call(
        functools.partial(_attn_kernel, tq=tq, rep=rep, shared_kv=shared_kv, block_mask=block_mask),
        grid=(bsz, g, t // tq),
        in_specs=[
            pl.BlockSpec((None, tq, rep * HEAD_DIM), lambda b, h, i: (b, i, h)),
            pl.BlockSpec((None, t, kv_w), lambda b, h, i: (b, 0, k_blk + h)),
            pl.BlockSpec((None, t, kv_w), lambda b, h, i: (b, 0, v_blk + h)),
            mask_spec,
            pl.BlockSpec((rep, tq, 2 * tq), lambda b, h, i: (h, 0, 0)),
        ],
        out_specs=pl.BlockSpec((None, tq, rep * HEAD_DIM), lambda b, h, i: (b, i, h)),
        out_shape=jax.ShapeDtypeStruct((bsz, t, N_HEADS * HEAD_DIM), BF16),
        scratch_shapes=[stat] * (3 * rep),
        compiler_params=_params("arbitrary", "arbitrary", "arbitrary"),
        name="masked_attention",
    )(proj, proj, proj, mask, near_bias)


PAGES_PER_BLOCK = MOBA_BLOCK // PAGE_SIZE
MEANS_BLOCKS_PER_STEP = 2


def _block_means_kernel(tbl_ref, *refs):
    del tbl_ref
    page_refs, o_ref = refs[:-1], refs[-1]
    for blk in range(MEANS_BLOCKS_PER_STEP):
        pages = page_refs[blk * PAGES_PER_BLOCK:(blk + 1) * PAGES_PER_BLOCK]
        total = jnp.sum(pages[0][...], axis=0)
        for ref in pages[1:]:
            total = total + jnp.sum(ref[...], axis=0)
        o_ref[blk] = total / MOBA_BLOCK


def _block_means(pool, layer, table):
    _, _, _, h, dh = pool.shape
    bsz, npg = table.shape
    nb = npg // PAGES_PER_BLOCK
    per_step = MEANS_BLOCKS_PER_STEP * PAGES_PER_BLOCK
    assert npg % per_step == 0
    page = lambda k: pl.BlockSpec((None, None, PAGE_SIZE, h, dh),
                                  lambda b, n, tbl: (layer, tbl[b, per_step * n + k], 0, 0, 0))
    out = pl.pallas_call(
        _block_means_kernel,
        grid_spec=pltpu.PrefetchScalarGridSpec(
            num_scalar_prefetch=1,
            grid=(bsz, nb // MEANS_BLOCKS_PER_STEP),
            in_specs=[page(k) for k in range(per_step)],
            out_specs=pl.BlockSpec((None, MEANS_BLOCKS_PER_STEP, h, dh), lambda b, n, tbl: (b, n, 0, 0)),
        ),
        out_shape=jax.ShapeDtypeStruct((bsz, nb, h, dh), F32),
        compiler_params=_params("arbitrary", "arbitrary"),
        name="block_means",
    )(table, *([pool] * per_step))
    return out.reshape(bsz, nb, h * dh)


def _moba_select_kernel(q_ref, mean_ref, o_ref, *, tq, past_len):
    i = pl.program_id(1)
    own = (past_len + i * tq + lax.broadcasted_iota(jnp.int32, (tq, LANES), 0)) // MOBA_BLOCK
    blk = lax.broadcasted_iota(jnp.int32, (tq, LANES), 1)
    past = blk < own
    for h in range(N_HEADS):
        sl = slice(h * HEAD_DIM, (h + 1) * HEAD_DIM)
        gate = lax.dot_general(q_ref[:, sl], mean_ref[:, sl], (((1,), (1,)), ((), ())),
                               precision=lax.Precision.HIGHEST, preferred_element_type=F32)
        g = jnp.where(past, gate, NEG_INF)
        kth = jnp.max(g, axis=1, keepdims=True)
        for _ in range(MOBA_TOPK - 1):
            kth = jnp.max(jnp.where(g < kth, g, NEG_INF), axis=1, keepdims=True)
        o_ref[h] = ((past & (g >= kth)) | (blk == own)).astype(o_ref.dtype)


def _moba_select(q_src, means, *, tq, past_len):
    bsz, t, _ = q_src.shape
    w = N_HEADS * HEAD_DIM
    return pl.pallas_call(
        functools.partial(_moba_select_kernel, tq=tq, past_len=past_len),
        grid=(bsz, t // tq),
        in_specs=[
            pl.BlockSpec((None, tq, w), lambda b, i: (b, i, 0)),
            pl.BlockSpec((None, LANES, w), lambda b, i: (b, 0, 0)),
        ],
        out_specs=pl.BlockSpec((None, N_HEADS, tq, LANES), lambda b, i: (b, 0, i, 0)),
        out_shape=jax.ShapeDtypeStruct((bsz, N_HEADS, t, LANES), BF16),
        compiler_params=_params("arbitrary", "arbitrary"),
        name="moba_select",
    )(q_src, means)


def _dsa_select_paged_kernel(tbl_ref, qi_ref, wi_ref, *rest, pp, topk, past_len, t_new):
    del tbl_ref
    page_refs = rest[:pp]
    knew_ref, m_ref, key_ref = rest[pp:]
    p = pl.program_id(1)
    qi = qi_ref[...].astype(BF16)
    wi = wi_ref[...]

    def score(kpage_t):
        s = _bdot(qi, kpage_t)
        w = wi * jnp.maximum(s, 0.0)
        return jnp.sum(w.reshape(N_IDX_HEADS, t_new, PAGE_SIZE), axis=0) * IDX_SCALE

    for k in range(pp):
        off = pl.multiple_of((p * pp + k) * PAGE_SIZE, PAGE_SIZE)
        key_ref[:, pl.ds(off, PAGE_SIZE)] = _sort_key(score(page_refs[k][...]))

    @pl.when(p == pl.num_programs(1) - 1)
    def _():
        row = lax.broadcasted_iota(jnp.int32, (t_new, PAGE_SIZE), 0)
        col = lax.broadcasted_iota(jnp.int32, (t_new, PAGE_SIZE), 1)
        sc = jnp.where(col <= row, score(knew_ref[...]), NEG_INF)
        key_ref[:, past_len:past_len + PAGE_SIZE] = _sort_key(sc)

        def count_ge(cand):
            return jnp.sum((key_ref[...] >= cand).astype(jnp.int32), axis=1, keepdims=True)

        thr = _kth_largest_key(count_ge, topk)
        keys = key_ref[...]
        m_ref[...] = ((keys >= thr) & (keys > _NEG_INF_KEY)).astype(m_ref.dtype)


def _dsa_select_paged(qi_rows, wi_rows, kidx_pool, layer, page_table, kidx_new, *, pp, topk):
    bsz, rows, _ = qi_rows.shape
    t_new = rows // N_IDX_HEADS
    n_pages = page_table.shape[1]
    assert n_pages % pp == 0
    past_len = n_pages * PAGE_SIZE
    lpad = past_len + PAGE_SIZE
    kidx_pool = jnp.swapaxes(kidx_pool, 2, 3)
    kidx_new = jnp.swapaxes(kidx_new, 1, 2)
    page_spec = lambda k: pl.BlockSpec((None, None, IDX_DIM, PAGE_SIZE),
                                       lambda b, p, tbl: (layer, tbl[b, p * pp + k], 0, 0))
    return pl.pallas_call(
        functools.partial(_dsa_select_paged_kernel, pp=pp, topk=topk, past_len=past_len, t_new=t_new),
        grid_spec=pltpu.PrefetchScalarGridSpec(
            num_scalar_prefetch=1,
            grid=(bsz, n_pages // pp),
            in_specs=[
                pl.BlockSpec((None, rows, IDX_DIM), lambda b, p, tbl: (b, 0, 0)),
                pl.BlockSpec((None, rows, 1), lambda b, p, tbl: (b, 0, 0)),
                *[page_spec(k) for k in range(pp)],
                pl.BlockSpec((None, IDX_DIM, PAGE_SIZE), lambda b, p, tbl: (b, 0, 0)),
            ],
            out_specs=pl.BlockSpec((None, t_new, lpad), lambda b, p, tbl: (b, 0, 0)),
            scratch_shapes=[pltpu.VMEM((t_new, lpad), jnp.int32)],
        ),
        out_shape=jax.ShapeDtypeStruct((bsz, t_new, lpad), BF16),
        compiler_params=_params("arbitrary", "arbitrary"),
        name="dsa_select_paged",
    )(page_table, qi_rows, wi_rows, *([kidx_pool] * pp), kidx_new)


def _col_vector(row_vec):
    n = row_vec.shape[1]
    eye = lax.broadcasted_iota(jnp.int32, (n, n), 0) == lax.broadcasted_iota(jnp.int32, (n, n), 1)
    return jnp.sum(jnp.where(eye, row_vec, 0.0), axis=1, keepdims=True)


def _paged_attn_kernel(tbl_ref, qbd_ref, *rest, pp, key_mask):
    del tbl_ref
    k_refs, v_refs = rest[:pp], rest[pp:2 * pp]
    mt_ref, mnew_ref, knew_ref, vnew_ref, bias_ref, o_ref, m_sc, l_sc, acc_sc = rest[2 * pp:]
    p = pl.program_id(1)
    last = pl.num_programs(1) - 1

    def key_rows(flags):
        t_new = flags.shape[0]
        pick = (lax.broadcasted_iota(jnp.int32, (t_new, LANES), 1) % t_new
                == lax.broadcasted_iota(jnp.int32, (t_new, LANES), 0)).astype(BF16)
        neg = ((flags.astype(F32) - 1.0) * -MASKED).astype(BF16)
        return lax.dot_general(neg, pick, (((0,), (0,)), ((), ())), preferred_element_type=F32)

    if key_mask:
        past_masks = key_rows(mt_ref[...])
        page_mask = lambda k: past_masks[k * PAGE_SIZE:(k + 1) * PAGE_SIZE, :]
        new_mask = lambda: key_rows(mnew_ref[...])
    else:
        page_mask = lambda k: mt_ref[pl.ds((p * pp + k) // PAGES_PER_BLOCK, 1), :]
        new_mask = lambda: mnew_ref[...]

    @pl.when(p == 0)
    def _():
        m_sc[...] = jnp.full(m_sc.shape, MASKED, F32)
        l_sc[...] = jnp.zeros_like(l_sc)
        acc_sc[...] = jnp.zeros_like(acc_sc)

    n_kv = qbd_ref.shape[0] // HEAD_DIM

    def heads_on_lanes(ref):
        return jnp.concatenate([ref[pl.ds(g, PAGE_SIZE, stride=n_kv), :] for g in range(n_kv)], axis=1)

    def pages(kps, vps, masks, biases):
        scores = []
        for kp, mask, bias in zip(kps, masks, biases):
            s = jnp.dot(kp().astype(BF16), qbd_ref[...], preferred_element_type=F32)
            if bias is not None:
                s = s + bias
            scores.append(s + mask)
        m_old = m_sc[...]
        m_new = m_old
        for s in scores:
            m_new = jnp.maximum(m_new, jnp.max(s, axis=0, keepdims=True))
        alpha = jnp.exp2(m_old - m_new)
        l_new = alpha * l_sc[...]
        pv = None
        for s, vp in zip(scores, vps):
            pt = jnp.exp2(s - m_new)
            l_new = l_new + jnp.sum(pt, axis=0, keepdims=True)
            part = lax.dot_general(pt.astype(BF16), vp().astype(BF16), (((0,), (0,)), ((), ())),
                                   preferred_element_type=F32)
            pv = part if pv is None else pv + part
        l_sc[...] = l_new
        acc_sc[...] = acc_sc[...] * _col_vector(alpha) + pv
        m_sc[...] = m_new

    is_last = (p == last).astype(F32)
    pages([functools.partial(heads_on_lanes, r) for r in k_refs],
          [functools.partial(heads_on_lanes, r) for r in v_refs],
          [page_mask(k) for k in range(pp)],
          [None] * (pp - 1) + [bias_ref[0] * is_last])

    @pl.when(p == last)
    def _():
        pages([lambda: knew_ref[...]], [lambda: vnew_ref[...]], [new_mask()], [bias_ref[1]])
        o_ref[...] = acc_sc[...] / _col_vector(l_sc[...])


def _paged_attention(qbd, k_pool, v_pool, layer, page_table, mask, new_mask, k_new, v_new, bias_t, *, pp, key_mask):
    bsz, w, _ = qbd.shape
    n_pages = page_table.shape[1]
    assert n_pages % pp == 0
    if key_mask:
        t_new = mask.shape[1]
        mask_specs = [pl.BlockSpec((None, t_new, pp * PAGE_SIZE), lambda b, p, tbl: (b, 0, p)),
                      pl.BlockSpec((None, t_new, PAGE_SIZE), lambda b, p, tbl: (b, 0, n_pages))]
        new_mask = mask
    else:
        mask_specs = [pl.BlockSpec((None,) + mask.shape[1:], lambda b, p, tbl: (b, 0, 0)),
                      pl.BlockSpec((PAGE_SIZE, LANES), lambda b, p, tbl: (0, 0))]
    n_layers, n_phys, _, n_kv, _ = k_pool.shape
    assert n_kv * HEAD_DIM == w
    as_rows = lambda pool: pool.reshape(n_layers, n_phys, PAGE_SIZE * n_kv, HEAD_DIM)
    k_pool, v_pool = as_rows(k_pool), as_rows(v_pool)
    page_spec = lambda k: pl.BlockSpec((None, None, PAGE_SIZE * n_kv, HEAD_DIM),
                                       lambda b, p, tbl: (layer, tbl[b, p * pp + k], 0, 0))
    per_b = lambda shp: pl.BlockSpec((None,) + shp, lambda b, p, tbl: (b, 0, 0))
    return pl.pallas_call(
        functools.partial(_paged_attn_kernel, pp=pp, key_mask=key_mask),
        grid_spec=pltpu.PrefetchScalarGridSpec(
            num_scalar_prefetch=1,
            grid=(bsz, n_pages // pp),
            in_specs=[
                per_b((w, LANES)),
                *[page_spec(k) for k in range(pp)],
                *[page_spec(k) for k in range(pp)],
                *mask_specs,
                per_b((PAGE_SIZE, w)),
                per_b((PAGE_SIZE, w)),
                pl.BlockSpec((2, PAGE_SIZE, LANES), lambda b, p, tbl: (0, 0, 0)),
            ],
            out_specs=per_b((LANES, w)),
            scratch_shapes=[pltpu.VMEM((1, LANES), F32), pltpu.VMEM((1, LANES), F32), pltpu.VMEM((LANES, w), F32)],
        ),
        out_shape=jax.ShapeDtypeStruct((bsz, LANES, w), F32),
        compiler_params=_params("arbitrary", "arbitrary"),
        name="paged_attention",
    )(page_table, qbd, *([k_pool] * pp), *([v_pool] * pp), mask, new_mask, k_new, v_new, bias_t)


def _block_diag_queries(q, n_kv):
    bsz, t, _ = q.shape
    q4 = q.reshape(bsz, t, N_HEADS, HEAD_DIM) * QK_SCALE_LOG2
    onehot = (jnp.arange(N_HEADS)[:, None] // (N_HEADS // n_kv) == jnp.arange(n_kv)[None, :]).astype(q.dtype)
    qbd = q4[:, :, :, None, :] * onehot[None, None, :, :, None]
    return qbd.transpose(0, 3, 4, 2, 1).reshape(bsz, n_kv * HEAD_DIM, N_HEADS * t).astype(BF16)


def _own_head_lanes(out, n_kv, t):
    bsz = out.shape[0]
    out5 = out.reshape(bsz, N_HEADS, t, n_kv, HEAD_DIM)
    heads = jnp.arange(N_HEADS)
    picked = out5[:, heads, :, heads // (N_HEADS // n_kv), :]
    return picked.transpose(1, 2, 0, 3).reshape(bsz, t, N_HEADS * HEAD_DIM)


def _decode_bias(near_bias, tq, t):
    tail = near_bias[:, :t, tq - PAGE_SIZE:tq + PAGE_SIZE]
    return tail.transpose(2, 0, 1).reshape(2, PAGE_SIZE, N_HEADS * t)


def _pad_rows(a, rows):
    return jnp.pad(a, ((0, 0), (0, rows - a.shape[1]), (0, 0)))


def _new_token_mask(t):
    key = jnp.arange(PAGE_SIZE)[:, None]
    tok = jnp.tile(jnp.arange(t), N_HEADS)[None, :]
    return jnp.where(key <= tok, 0.0, MASKED).astype(F32)


def _split_heads_kernel(x_ref, o_ref, *, tm, n_heads):
    for h in range(n_heads):
        o_ref[pl.ds(h, tm, stride=n_heads), :] = x_ref[:, h * HEAD_DIM:(h + 1) * HEAD_DIM]


def _split_heads(proj, col, n_heads, *, tm):
    bsz, t, _ = proj.shape
    w = n_heads * HEAD_DIM
    assert col % w == 0 and t % tm == 0
    out = pl.pallas_call(
        functools.partial(_split_heads_kernel, tm=tm, n_heads=n_heads),
        grid=(bsz, t // tm),
        in_specs=[pl.BlockSpec((None, tm, w), lambda b, i: (b, i, col // w))],
        out_specs=pl.BlockSpec((None, tm * n_heads, HEAD_DIM), lambda b, i: (b, i, 0)),
        out_shape=jax.ShapeDtypeStruct((bsz, t * n_heads, HEAD_DIM), F32),
        compiler_params=_params("arbitrary", "arbitrary"),
        name="split_heads",
    )(proj)
    return out.reshape(bsz, t, n_heads, HEAD_DIM)


ATTN_TQ = 256
SELECT_TQ = 256
SELECT_TK = 512
MOBA_HEADS_PER_STEP = 4
SPLIT_TM = 256
Q_W = N_HEADS * HEAD_DIM
KV_W = N_KV_B * HEAD_DIM
DSA_PROJ = Q_W + 2 * KV_W + N_IDX_HEADS * IDX_DIM + IDX_DIM + N_IDX_HEADS
DSA_PROJ_PAD = -(-DSA_PROJ // 512) * 512


def _dsa_split(proj):
    k = proj[..., Q_W:Q_W + KV_W]
    v = proj[..., Q_W + KV_W:Q_W + 2 * KV_W]
    o = Q_W + 2 * KV_W + N_IDX_HEADS * IDX_DIM
    return k, v, proj[..., o:o + IDX_DIM]


def _dsa_prompt(proj, near_bias):
    bsz, t, _ = proj.shape
    ki = _dsa_split(proj)[2]
    mask = _dsa_select(proj, tq=min(SELECT_TQ, t), tk=min(SELECT_TK, t), topk=min(IDX_TOPK, t // 4))
    o = _attention(proj, Q_W, Q_W + KV_W, mask, near_bias, tq=ATTN_TQ, rep=N_HEADS // N_KV_B,
                   shared_kv=True, block_mask=False)
    k = _split_heads(proj, Q_W, N_KV_B, tm=min(SPLIT_TM, t))
    v = _split_heads(proj, Q_W + KV_W, N_KV_B, tm=min(SPLIT_TM, t))
    return o, k, v, ki


def _dsa_sample(proj, k_pool, v_pool, kidx_pool, layer, page_table, near_bias):
    bsz, t, _ = proj.shape
    past_len = page_table.shape[1] * PAGE_SIZE
    k, v, ki = _dsa_split(proj)
    o_qi = Q_W + 2 * KV_W
    qi = proj[..., o_qi:o_qi + N_IDX_HEADS * IDX_DIM].reshape(bsz, t, N_IDX_HEADS, IDX_DIM)
    wi = proj[..., o_qi + N_IDX_HEADS * IDX_DIM + IDX_DIM:DSA_PROJ]
    qi_rows = qi.transpose(0, 2, 1, 3).reshape(bsz, N_IDX_HEADS * t, IDX_DIM)
    wi_rows = wi.transpose(0, 2, 1).reshape(bsz, N_IDX_HEADS * t, 1)
    sel = _dsa_select_paged(qi_rows, wi_rows, kidx_pool, layer, page_table, _pad_rows(ki, PAGE_SIZE),
                            pp=8, topk=min(IDX_TOPK, (past_len + t) // 4))
    out = _paged_attention(
        _block_diag_queries(proj[..., :Q_W], N_KV_B), k_pool, v_pool, layer, page_table, sel, None,
        _pad_rows(k, PAGE_SIZE), _pad_rows(v, PAGE_SIZE), _decode_bias(near_bias, ATTN_TQ, t), pp=8, key_mask=True)
    o = _own_head_lanes(out, N_KV_B, t).astype(BF16)
    return o, k.reshape(bsz, t, N_KV_B, HEAD_DIM), v.reshape(bsz, t, N_KV_B, HEAD_DIM), ki


def _moba_prompt(proj, near_bias):
    bsz, t, _ = proj.shape
    k_heads = _split_heads(proj, Q_W, N_HEADS, tm=min(SPLIT_TM, t))
    v_heads = _split_heads(proj, 2 * Q_W, N_HEADS, tm=min(SPLIT_TM, t))
    n_pages = t // PAGE_SIZE
    pages = jnp.arange(bsz * n_pages, dtype=jnp.int32).reshape(bsz, n_pages)
    means = _block_means(k_heads.reshape(1, bsz * n_pages, PAGE_SIZE, N_HEADS, HEAD_DIM), 0, pages)
    sel = _moba_select(proj, _pad_rows(means, LANES), tq=ATTN_TQ, past_len=0)
    o = _attention(proj, Q_W, 2 * Q_W, sel, near_bias, tq=ATTN_TQ, rep=MOBA_HEADS_PER_STEP,
                   shared_kv=False, block_mask=True)
    return o, k_heads, v_heads


def _moba_sample(proj, k_pool, v_pool, layer, page_table, near_bias):
    bsz, t, _ = proj.shape
    n_pages = page_table.shape[1]
    past_len = n_pages * PAGE_SIZE
    k = proj[..., Q_W:2 * Q_W]
    v = proj[..., 2 * Q_W:3 * Q_W]
    means = _block_means(k_pool, layer, page_table)
    sel = _moba_select(proj, _pad_rows(means, LANES), tq=t, past_len=past_len)
    block_rows = (sel.astype(F32).transpose(0, 3, 1, 2).reshape(bsz, LANES, N_HEADS * t) - 1.0) * -MASKED
    out = _paged_attention(
        _block_diag_queries(proj[..., :Q_W], N_HEADS), k_pool, v_pool, layer, page_table,
        block_rows, _new_token_mask(t), _pad_rows(k, PAGE_SIZE), _pad_rows(v, PAGE_SIZE),
        _decode_bias(near_bias, ATTN_TQ, t), pp=4, key_mask=False)
    o = _own_head_lanes(out, N_HEADS, t).astype(BF16)
    return o, k.reshape(bsz, t, N_HEADS, HEAD_DIM), v.reshape(bsz, t, N_HEADS, HEAD_DIM)


N_MIXERS = 3
PROMPT_TM = 1024
MATMUL_TN = 512
LRU_TT = 256
MOE_TM = 512
ROUTE_W = LANES


def kernel(x_prompt, x_sample, state_lru_h, state_lru_conv, cache_dsa_k, cache_dsa_v, cache_dsa_kidx,
           cache_moba_k, cache_moba_v, page_table, c_prompt, c_sample, rel_bias, norm_mix_g, norm_ffn_g,
           final_norm_g, ada_w, ada_b, lru_w_in, lru_conv_w, lru_conv_b, lru_w_rgate, lru_b_rgate,
           lru_w_igate, lru_b_igate, lru_lambda, lru_w_out, dsa_w_in, dsa_w_out, moba_w_in, moba_w_out,
           moe_w_group, moe_w_expert, moe_w1, moe_w3, moe_w2):
    bp, tp, d = x_prompt.shape
    bs, ts, _ = x_sample.shape
    depth = ada_w.shape[0]
    ns = bs * ts
    c_rnn = lru_w_out.shape[1]

    c_all = jnp.concatenate([c_prompt, c_sample], axis=0)
    c_rows = -(-c_all.shape[0] // SUBLANES) * SUBLANES
    mods = _ada(jnp.pad(c_all, ((0, c_rows - c_all.shape[0]), (0, 0))), ada_w, ada_b)
    mods = mods.reshape(depth, c_rows, 6, d)
    near_bias = _near_bias(rel_bias, ATTN_TQ)

    w1, w3, w2 = moe_w1.astype(BF16), moe_w3.astype(BF16), moe_w2.astype(BF16)
    xp = x_prompt
    pending = None
    xs = x_sample.reshape(1, ns, d)
    outs = {n: [] for n in ("lru_h_p", "lru_c_p", "lru_h_s", "lru_c_s", "dsa_k_p", "dsa_v_p", "dsa_i_p",
                            "dsa_k_s", "dsa_v_s", "dsa_i_s", "moba_k_p", "moba_v_p", "moba_k_s", "moba_v_s")}
    for i in range(depth):
        kind, j = i % N_MIXERS, i // N_MIXERS
        mod_p = [mods[i, :bp, m][:, None, :] for m in range(6)]
        mod_s = [jnp.repeat(mods[i, bp:bp + bs, m], ts, axis=0)[None] for m in range(6)]
        shp1, scp1, gtp1, shp2, scp2, gtp2 = mod_p
        shs1, scs1, gts1, shs2, scs2, gts2 = mod_s
        mix_in = lambda w, tn=MATMUL_TN: (
            _norm_matmul(xp, norm_mix_g[i], shp1, scp1, w, tm=min(PROMPT_TM, tp), tn=tn, layer=j),
            _norm_matmul(xs, norm_mix_g[i], shs1, scs1, w, tm=ns, tn=tn, layer=j))
        if kind == 0:
            up, us = mix_in(lru_w_in)
            lru_params = (lru_conv_w[j], lru_conv_b[j], lru_w_rgate[j], lru_b_rgate[j], lru_w_igate[j],
                          lru_b_igate[j], lru_lambda[j])
            yp, hp_state, cp = _lru_core(up, jnp.zeros((bp, CONV_W - 1, c_rnn), F32), jnp.zeros((bp, c_rnn), F32),
                                         *lru_params, tt=LRU_TT)
            ys, hs_state, cs = _lru_core(us.reshape(bs, ts, 2 * c_rnn), state_lru_conv[j], state_lru_h[j],
                                         *lru_params, tt=ts)
            ys = ys.reshape(1, ns, c_rnn)
            w_out = lru_w_out
            outs["lru_h_p"].append(hp_state)
            outs["lru_c_p"].append(cp)
            outs["lru_h_s"].append(hs_state)
            outs["lru_c_s"].append(cs)
        elif kind == 1:
            pp_, ps_ = mix_in(jnp.pad(dsa_w_in, ((0, 0), (0, 0), (0, DSA_PROJ_PAD - DSA_PROJ))))
            yp, kp, vp, ip = _dsa_prompt(pp_, near_bias)
            ys, k_s, v_s, i_s = _dsa_sample(ps_.reshape(bs, ts, DSA_PROJ_PAD), cache_dsa_k, cache_dsa_v,
                                            cache_dsa_kidx, j, page_table, near_bias)
            ys = ys.reshape(1, ns, Q_W)
            w_out = dsa_w_out
            for n, val in zip(("dsa_k_p", "dsa_v_p", "dsa_i_p", "dsa_k_s", "dsa_v_s", "dsa_i_s"),
                              (kp, vp, ip, k_s, v_s, i_s)):
                outs[n].append(val)
        else:
            pp_, ps_ = mix_in(moba_w_in)
            yp, kp, vp = _moba_prompt(pp_, near_bias)
            ys, k_s, v_s = _moba_sample(ps_.reshape(bs, ts, 3 * Q_W), cache_moba_k, cache_moba_v, j,
                                        page_table, near_bias)
            ys = ys.reshape(1, ns, Q_W)
            w_out = moba_w_out
            for n, val in zip(("moba_k_p", "moba_v_p", "moba_k_s", "moba_v_s"), (kp, vp, k_s, v_s)):
                outs[n].append(val)
        xp = _matmul_residual(yp, w_out, xp, gtp1, tm=min(PROMPT_TM, tp), tn=min(MATMUL_TN, d), layer=j)
        xs = _matmul_residual(ys, w_out, xs, gts1, tm=ns, tn=min(MATMUL_TN, d), layer=j)

        w_route = jnp.pad(jnp.concatenate([moe_w_group[i], moe_w_expert[i]], axis=1),
                          ((0, 0), (0, ROUTE_W - N_GROUPS - N_EXPERTS)))
        lg_p, h_p = _norm_matmul(xp, norm_ffn_g[i], shp2, scp2, w_route, tm=min(PROMPT_TM, tp), tn=ROUTE_W,
                                 emit_h=True, highest=True, h_dtype=F32)
        moe_tm = min(MOE_TM, tp)
        info, counts = _route_tokens(lg_p.reshape(bp * tp, ROUTE_W), tm=moe_tm)
        moe_p = _moe_grouped(h_p.reshape(bp * tp, d), info, *_group_layout(info, counts, moe_tm), w1, w3, w2, i,
                             tm=moe_tm)
        if i + 1 < depth:
            xp = _gated_add(xp, moe_p, gtp2, tm=moe_tm)
        else:
            pending = (moe_p, gtp2)
        lg_s, h_s = _norm_matmul(xs, norm_ffn_g[i], shs2, scs2, w_route, tm=ns, tn=ROUTE_W,
                                 emit_h=True, highest=True)
        xs = _moe_dense(h_s, lg_s, w1, w3, w2, i, xs, gts2, tm=ns)

    y_prompt = _rmsnorm(xp, final_norm_g, tm=min(MOE_TM, tp), residual=pending)
    y_sample = _rmsnorm(xs, final_norm_g, tm=ns).reshape(bs, ts, d)
    stack = lambda n: jnp.stack(outs[n])
    return (y_prompt, y_sample, stack("lru_h_p"), stack("lru_c_p"), stack("lru_h_s"), stack("lru_c_s"),
            stack("dsa_k_p"), stack("dsa_v_p"), stack("dsa_i_p"), stack("dsa_k_s"), stack("dsa_v_s"),
            stack("dsa_i_s"), stack("moba_k_p"), stack("moba_v_p"), stack("moba_k_s"), stack("moba_v_s"))
```

```python
import functools
import math

import numpy as np
import jax
import jax.numpy as jnp
from jax import lax
from jax.experimental import pallas as pl
from jax.experimental.pallas import tpu as pltpu

F32 = jnp.float32
BF16 = jnp.bfloat16

LANES = 128
SUBLANES = 8
VMEM_LIMIT_BYTES = 56 * 1024 * 1024

HEAD_DIM = 128
N_HEADS = 16
N_KV_B = 4
N_IDX_HEADS = 16
IDX_DIM = 64
IDX_TOPK = 256
IDX_SCALE = (IDX_DIM * N_IDX_HEADS) ** -0.5
PAGE_SIZE = 128
MOBA_BLOCK = 256
MOBA_TOPK = 3
N_BUCKETS = 32
MAX_DISTANCE = 128
N_RNN_BLOCKS = 16
CONV_W = 4
LRU_C = 8.0
N_GROUPS = 4
EXPERTS_PER_GROUP = 4
N_EXPERTS = N_GROUPS * EXPERTS_PER_GROUP
EPS = 1e-6
ATTN_SCALE = HEAD_DIM ** -0.5
LOG2E = math.log2(math.e)
QK_SCALE_LOG2 = ATTN_SCALE * LOG2E
NEG_INF = float("-inf")


def _params(*sem):
    return pltpu.CompilerParams(dimension_semantics=sem, vmem_limit_bytes=VMEM_LIMIT_BYTES)


def _bdot(a, b):
    return jnp.dot(a.astype(BF16), b.astype(BF16), preferred_element_type=F32)


def _bdot_nt(a, b):
    return lax.dot_general(a.astype(BF16), b.astype(BF16), (((1,), (1,)), ((), ())),
                           preferred_element_type=F32)


def _ada_kernel(c_ref, w_ref, b_ref, o_ref):
    c = c_ref[...]
    o_ref[...] = _bdot(c * jax.nn.sigmoid(c), w_ref[...]) + b_ref[...]


def _ada(c, ada_w, ada_b):
    depth, d, n = ada_w.shape
    rows = c.shape[0]
    tn = next(w for w in (1024, 512, 256, LANES) if n % w == 0)
    return pl.pallas_call(
        _ada_kernel,
        grid=(depth, n // tn),
        in_specs=[
            pl.BlockSpec((rows, d), lambda l, j: (0, 0)),
            pl.BlockSpec((None, d, tn), lambda l, j: (l, 0, j)),
            pl.BlockSpec((None, 1, tn), lambda l, j: (l, 0, j)),
        ],
        out_specs=pl.BlockSpec((None, rows, tn), lambda l, j: (l, 0, j)),
        out_shape=jax.ShapeDtypeStruct((depth, rows, n), F32),
        compiler_params=_params("arbitrary", "arbitrary"),
        name="ada_mod",
    )(c, ada_w, ada_b.reshape(depth, 1, n))


def _norm_mod(x, g, shift, scale):
    ms = jnp.mean(x * x, axis=-1, keepdims=True)
    h = x * lax.rsqrt(ms + EPS) * g
    return h * (1.0 + scale) + shift


def _norm_mm_kernel(x_ref, g_ref, sh_ref, sc_ref, w_ref, *refs, emit_h, highest):
    if emit_h:
        o_ref, ho_ref, h_ref = refs
    else:
        o_ref, h_ref = refs
        ho_ref = None

    @pl.when(pl.program_id(2) == 0)
    def _():
        h = _norm_mod(x_ref[...], g_ref[...], sh_ref[...], sc_ref[...])
        h_ref[...] = h.astype(h_ref.dtype)
        if ho_ref is not None:
            ho_ref[...] = h.astype(ho_ref.dtype)

    if highest:
        o_ref[...] = jnp.dot(h_ref[...], w_ref[...], precision=lax.Precision.HIGHEST,
                             preferred_element_type=F32)
    else:
        o_ref[...] = _bdot(h_ref[...], w_ref[...])


def _weight_spec(w, layer, tn):
    if w.ndim == 2:
        return pl.BlockSpec((w.shape[0], tn), lambda b, i, j: (0, j))
    return pl.BlockSpec((None, w.shape[1], tn), lambda b, i, j: (layer, 0, j))


def _norm_matmul(x, g, shift, scale, w, *, tm, tn, emit_h=False, highest=False, h_dtype=BF16, layer=None):
    bsz, t, d = x.shape
    n = w.shape[-1]
    r = shift.shape[1]
    assert t % tm == 0 and n % tn == 0 and (r == 1 or r == tm == t)
    out_shape = [jax.ShapeDtypeStruct((bsz, t, n), F32)]
    out_specs = [pl.BlockSpec((None, tm, tn), lambda b, i, j: (b, i, j))]
    row_tile = pl.BlockSpec((None, tm, d), lambda b, i, j: (b, i, 0))
    if emit_h:
        out_shape.append(jax.ShapeDtypeStruct((bsz, t, d), h_dtype))
        out_specs.append(row_tile)
    mod_map = (lambda b, i, j: (b, 0, 0)) if r == 1 else (lambda b, i, j: (b, i, 0))
    res = pl.pallas_call(
        functools.partial(_norm_mm_kernel, emit_h=emit_h, highest=highest),
        grid=(bsz, t // tm, n // tn),
        in_specs=[
            row_tile,
            pl.BlockSpec((1, d), lambda b, i, j: (0, 0)),
            pl.BlockSpec((None, r, d), mod_map),
            pl.BlockSpec((None, r, d), mod_map),
            _weight_spec(w, layer, tn),
        ],
        out_specs=out_specs,
        out_shape=out_shape,
        scratch_shapes=[pltpu.VMEM((tm, d), F32 if highest else BF16)],
        compiler_params=_params("arbitrary", "arbitrary", "arbitrary"),
        name="norm_matmul",
    )(x, g.reshape(1, d), shift, scale, w)
    return res if emit_h else res[0]


def _mm_res_kernel(a_ref, w_ref, r_ref, g_ref, o_ref):
    o_ref[...] = r_ref[...] + g_ref[...] * _bdot(a_ref[...], w_ref[...])


def _matmul_residual(a, w, res, gate, *, tm, tn, layer=None):
    bsz, t, k = a.shape
    n = w.shape[-1]
    r = gate.shape[1]
    assert t % tm == 0 and n % tn == 0 and (r == 1 or r == tm == t)
    gate_map = (lambda b, i, j: (b, 0, j)) if r == 1 else (lambda b, i, j: (b, i, j))
    return pl.pallas_call(
        _mm_res_kernel,
        grid=(bsz, t // tm, n // tn),
        in_specs=[
            pl.BlockSpec((None, tm, k), lambda b, i, j: (b, i, 0)),
            _weight_spec(w, layer, tn),
            pl.BlockSpec((None, tm, tn), lambda b, i, j: (b, i, j)),
            pl.BlockSpec((None, r, tn), gate_map),
        ],
        out_specs=pl.BlockSpec((None, tm, tn), lambda b, i, j: (b, i, j)),
        out_shape=jax.ShapeDtypeStruct((bsz, t, n), F32),
        compiler_params=_params("arbitrary", "arbitrary", "arbitrary"),
        name="matmul_residual",
    )(a, w, res, gate)


def _rmsnorm_kernel(x_ref, g_ref, *refs, residual):
    x = x_ref[...]
    if residual:
        y_ref, gy_ref, o_ref = refs
        x = x + gy_ref[...] * y_ref[...]
    else:
        o_ref, = refs
    ms = jnp.mean(x * x, axis=-1, keepdims=True)
    o_ref[...] = x * lax.rsqrt(ms + EPS) * g_ref[...]


def _rmsnorm(x, g, *, tm, residual=None):
    bsz, t, d = x.shape
    per_b = t // tm
    extra_in, extra_specs = [], []
    if residual is not None:
        extra_in = list(residual)
        extra_specs = [pl.BlockSpec((tm, d), lambda b, i: (b * per_b + i, 0)),
                       pl.BlockSpec((None, 1, d), lambda b, i: (b, 0, 0))]
    return pl.pallas_call(
        functools.partial(_rmsnorm_kernel, residual=residual is not None),
        grid=(bsz, t // tm),
        in_specs=[pl.BlockSpec((None, tm, d), lambda b, i: (b, i, 0)),
                  pl.BlockSpec((1, d), lambda b, i: (0, 0)), *extra_specs],
        out_specs=pl.BlockSpec((None, tm, d), lambda b, i: (b, i, 0)),
        out_shape=jax.ShapeDtypeStruct((bsz, t, d), F32),
        compiler_params=_params("arbitrary", "arbitrary"),
        name="final_rmsnorm",
    )(x, g.reshape(1, d), *extra_in)


GATE_CHUNK = 640


def _log_sigmoid(x):
    return jnp.minimum(x, 0.0) - jnp.log1p(jnp.exp(-jnp.abs(x)))


def _lru_kernel(gb_ref, xb_ref, cs_ref, h0_ref, cw_ref, cb_ref, wr_ref, br_ref, wi_ref, bi_ref, lam_ref,
                y_ref, hl_ref, nc_ref, ext_ref, a_ref, b_ref, hs_ref, carry_ref, *, tt):
    c_all = xb_ref.shape[-1]

    @pl.when(pl.program_id(1) == 0)
    def _():
        ext_ref[0:SUBLANES, :] = cs_ref[...]
        carry_ref[...] = h0_ref[...]

    xb = xb_ref[...]
    ext_ref[SUBLANES:SUBLANES + tt, :] = xb
    cw = cw_ref[...]
    xc = cb_ref[...] + xb * cw[CONV_W - 1:CONV_W, :]
    for j in range(1, CONV_W):
        xc = xc + ext_ref[pl.ds(SUBLANES - j, tt), :] * cw[CONV_W - 1 - j:CONV_W - j, :]
    tail = ext_ref[tt:tt + SUBLANES, :]
    ext_ref[0:SUBLANES, :] = tail
    nc_ref[...] = tail

    xcb = xc.astype(BF16)
    r_parts, i_parts = [], []
    for c in range(c_all // GATE_CHUNK):
        sl = slice(c * GATE_CHUNK, (c + 1) * GATE_CHUNK)
        r_parts.append(jnp.dot(xcb[:, sl], wr_ref[c], preferred_element_type=F32))
        i_parts.append(jnp.dot(xcb[:, sl], wi_ref[c], preferred_element_type=F32))
    r = jax.nn.sigmoid(jnp.concatenate(r_parts, axis=1) + br_ref[...])
    ig = jax.nn.sigmoid(jnp.concatenate(i_parts, axis=1) + bi_ref[...])
    log_a = LRU_C * r * _log_sigmoid(lam_ref[...])
    th = jnp.tanh(log_a)
    one_minus_a2 = -2.0 * th / (1.0 - th)
    a_ref[...] = jnp.exp(log_a)
    b_ref[...] = jnp.sqrt(jnp.maximum(one_minus_a2, 0.0)) * (ig * xc)

    row = lax.broadcasted_iota(jnp.int32, (SUBLANES, GATE_CHUNK), 0)
    for c in range(c_all // GATE_CHUNK):
        sl = slice(c * GATE_CHUNK, (c + 1) * GATE_CHUNK)

        def body(grp, carry, sl=sl):
            r8 = pl.multiple_of(grp * SUBLANES, SUBLANES)
            av = a_ref[pl.ds(r8, SUBLANES), sl]
            bv = b_ref[pl.ds(r8, SUBLANES), sl]
            for s in (1, 2, 4):
                m = row >= s
                a_sh = jnp.where(m, pltpu.roll(av, s, axis=0), 1.0)
                b_sh = jnp.where(m, pltpu.roll(bv, s, axis=0), 0.0)
                bv = av * b_sh + bv
                av = av * a_sh
            h = av * carry + bv
            hs_ref[pl.ds(r8, SUBLANES), sl] = h
            return h[SUBLANES - 1:SUBLANES, :]

        carry_ref[:, sl] = lax.fori_loop(0, tt // SUBLANES, body, carry_ref[:, sl])

    hl_ref[...] = carry_ref[...]
    y_ref[...] = (jax.nn.gelu(gb_ref[...]) * hs_ref[...]).astype(y_ref.dtype)


def _lru_core(u, conv_state, h0, conv_w, conv_b, w_r, b_r, w_i, b_i, lam, *, tt):
    bsz, t, c2 = u.shape
    c = c2 // 2
    blk = c // N_RNN_BLOCKS
    per = GATE_CHUNK // blk
    nchunk = c // GATE_CHUNK

    def block_diag(w):
        wc = w.reshape(nchunk, per, blk, blk).astype(BF16)
        eye = jnp.eye(per, dtype=BF16)
        return jnp.einsum("cpij,pq->cpiqj", wc, eye).reshape(nchunk, GATE_CHUNK, GATE_CHUNK)

    cs_pad = jnp.pad(conv_state, ((0, 0), (SUBLANES - (CONV_W - 1), 0), (0, 0)))
    row = lambda v: v.reshape(1, c)
    full = lambda shp: pl.BlockSpec(shp, lambda b, i: (0,) * len(shp))
    y, hl, nc = pl.pallas_call(
        functools.partial(_lru_kernel, tt=tt),
        grid=(bsz, t // tt),
        in_specs=[
            pl.BlockSpec((None, tt, c), lambda b, i: (b, i, 0)),
            pl.BlockSpec((None, tt, c), lambda b, i: (b, i, 1)),
            pl.BlockSpec((None, SUBLANES, c), lambda b, i: (b, 0, 0)),
            pl.BlockSpec((None, 1, c), lambda b, i: (b, 0, 0)),
            full((CONV_W, c)), full((1, c)),
            full((nchunk, GATE_CHUNK, GATE_CHUNK)), full((1, c)),
            full((nchunk, GATE_CHUNK, GATE_CHUNK)), full((1, c)),
            full((1, c)),
        ],
        out_specs=[
            pl.BlockSpec((None, tt, c), lambda b, i: (b, i, 0)),
            pl.BlockSpec((None, 1, c), lambda b, i: (b, 0, 0)),
            pl.BlockSpec((None, SUBLANES, c), lambda b, i: (b, 0, 0)),
        ],
        out_shape=[
            jax.ShapeDtypeStruct((bsz, t, c), BF16),
            jax.ShapeDtypeStruct((bsz, 1, c), F32),
            jax.ShapeDtypeStruct((bsz, SUBLANES, c), F32),
        ],
        scratch_shapes=[
            pltpu.VMEM((tt + SUBLANES, c), F32),
            pltpu.VMEM((tt, c), F32),
            pltpu.VMEM((tt, c), F32),
            pltpu.VMEM((tt, c), F32),
            pltpu.VMEM((1, c), F32),
        ],
        compiler_params=_params("arbitrary", "arbitrary"),
        name="lru_core",
    )(u, u, cs_pad, h0.reshape(bsz, 1, c), conv_w, row(conv_b), block_diag(w_r), row(b_r),
      block_diag(w_i), row(b_i), row(lam))
    return y, hl.reshape(bsz, c), nc[:, SUBLANES - (CONV_W - 1):, :]


def _route_gates(lg):
    return _route(lg)[0]


def _route(lg):
    col = lax.broadcasted_iota(jnp.int32, lg.shape, 1)
    big = jnp.int32(LANES)
    is_g = col < N_GROUPS
    gl = jnp.where(is_g, lg, NEG_INF)
    gmax = jnp.max(gl, axis=1, keepdims=True)
    g_sel = jnp.min(jnp.where(gl == gmax, col, big), axis=1, keepdims=True)
    g_w = 1.0 / jnp.sum(jnp.exp(gl - gmax), axis=1, keepdims=True)
    eid = col - N_GROUPS
    in_grp = (eid >= 0) & (eid < N_EXPERTS) & ((eid // EXPERTS_PER_GROUP) == g_sel)
    e_in = jnp.where(in_grp, lg, NEG_INF)
    top1 = jnp.max(e_in, axis=1, keepdims=True)
    idx1 = jnp.min(jnp.where((e_in == top1) & in_grp, col, big), axis=1, keepdims=True)
    rest = in_grp & (col != idx1)
    e2 = jnp.where(rest, lg, NEG_INF)
    top2 = jnp.max(e2, axis=1, keepdims=True)
    idx2 = jnp.min(jnp.where((e2 == top2) & rest, col, big), axis=1, keepdims=True)
    z = jnp.exp(top2 - top1)
    w_first = g_w / (1.0 + z)
    w_second = g_w * z / (1.0 + z)
    return jnp.where(col == idx1, w_first, 0.0) + jnp.where(col == idx2, w_second, 0.0), g_sel


def _moe_dense_kernel(h_ref, lg_ref, w1_ref, w3_ref, w2_ref, x_ref, g_ref, o_ref, gates_ref, acc_ref):
    e = pl.program_id(2)

    @pl.when(e == 0)
    def _():
        gates_ref[...] = _route_gates(lg_ref[...])
        acc_ref[...] = jnp.zeros_like(acc_ref)

    gates = gates_ref[...]
    col = lax.broadcasted_iota(jnp.int32, gates.shape, 1)
    ge = jnp.sum(jnp.where(col == e + N_GROUPS, gates, 0.0), axis=1, keepdims=True)
    h = h_ref[...]
    a1 = jnp.dot(h, w1_ref[...], preferred_element_type=F32)
    a3 = jnp.dot(h, w3_ref[...], preferred_element_type=F32)
    hidden = (a1 * jax.nn.sigmoid(a1)) * a3 * ge
    acc_ref[...] += jnp.dot(hidden.astype(BF16), w2_ref[...], preferred_element_type=F32)

    @pl.when(e == pl.num_programs(2) - 1)
    def _():
        o_ref[...] = x_ref[...] + g_ref[...] * acc_ref[...]


def _moe_dense(h, logits, w1, w3, w2, layer, x, gate, *, tm):
    bsz, t, d = x.shape
    _, ne, _, f = w1.shape
    r = gate.shape[1]
    gate_map = (lambda b, i, e: (b, 0, 0)) if r == 1 else (lambda b, i, e: (b, i, 0))
    tok = lambda b, i, e: (b, i, 0)
    expert = lambda b, i, e: (layer, e, 0, 0)
    return pl.pallas_call(
        _moe_dense_kernel,
        grid=(bsz, t // tm, ne),
        in_specs=[
            pl.BlockSpec((None, tm, d), tok),
            pl.BlockSpec((None, tm, LANES), tok),
            pl.BlockSpec((None, None, d, f), expert),
            pl.BlockSpec((None, None, d, f), expert),
            pl.BlockSpec((None, None, f, d), expert),
            pl.BlockSpec((None, tm, d), tok),
            pl.BlockSpec((None, r, d), gate_map),
        ],
        out_specs=pl.BlockSpec((None, tm, d), tok),
        out_shape=jax.ShapeDtypeStruct((bsz, t, d), F32),
        scratch_shapes=[pltpu.VMEM((tm, LANES), F32), pltpu.VMEM((tm, d), F32)],
        compiler_params=_params("arbitrary", "arbitrary", "arbitrary"),
        name="moe_dense",
    )(h, logits, w1, w3, w2, x, gate)


INFO_GROUP_LANE = 0
INFO_RANK_LANE = 1


def _route_kernel(lg_ref, info_ref, cnt_ref, carry_ref):
    @pl.when(pl.program_id(0) == 0)
    def _():
        carry_ref[...] = jnp.zeros_like(carry_ref)

    gates, g_sel = _route(lg_ref[...])
    tm = gates.shape[0]
    col = lax.broadcasted_iota(jnp.int32, gates.shape, 1)
    onehot = (col == g_sel).astype(F32)
    before = (lax.broadcasted_iota(jnp.int32, (tm, tm), 0) > lax.broadcasted_iota(jnp.int32, (tm, tm), 1))
    earlier = jnp.dot(before.astype(BF16), onehot.astype(BF16), preferred_element_type=F32) + carry_ref[...]
    rank = jnp.sum(jnp.where(col == g_sel, earlier, 0.0), axis=1, keepdims=True)
    carry_ref[...] += jnp.sum(onehot, axis=0, keepdims=True)
    info_ref[...] = (gates + jnp.where(col == INFO_GROUP_LANE, g_sel.astype(F32), 0.0)
                     + jnp.where(col == INFO_RANK_LANE, rank, 0.0))
    cnt_ref[...] = carry_ref[...]


def _route_tokens(logits, *, tm):
    n = logits.shape[0]
    assert n % tm == 0
    return pl.pallas_call(
        _route_kernel,
        grid=(n // tm,),
        in_specs=[pl.BlockSpec((tm, LANES), lambda i: (i, 0))],
        out_specs=[pl.BlockSpec((tm, LANES), lambda i: (i, 0)), pl.BlockSpec((1, LANES), lambda i: (0, 0))],
        out_shape=[jax.ShapeDtypeStruct((n, LANES), F32), jax.ShapeDtypeStruct((1, LANES), F32)],
        scratch_shapes=[pltpu.VMEM((1, LANES), F32)],
        compiler_params=_params("arbitrary"),
        name="moe_route",
    )(logits)


def _group_layout(info, counts, tm):
    n = info.shape[0]
    g_sel = info[:, INFO_GROUP_LANE].astype(jnp.int32)
    rank = info[:, INFO_RANK_LANE].astype(jnp.int32)
    cnt = counts[0, :N_GROUPS].astype(jnp.int32)
    padded = -(-cnt // tm) * tm
    ends = jnp.cumsum(padded)
    starts = ends - padded
    n_slots = n + N_GROUPS * tm
    src = _slot_tokens(starts[g_sel] + rank, n_slots)
    tile_start = jnp.arange(n_slots // tm, dtype=jnp.int32) * tm
    tile_group = jnp.minimum(jnp.sum((tile_start[:, None] >= ends[None, :]).astype(jnp.int32), axis=1), N_GROUPS - 1)
    slot_group = jnp.repeat(tile_group, tm)
    is_pad = jnp.arange(n_slots, dtype=jnp.int32) - starts[slot_group] >= cnt[slot_group]
    pad_row = n + tm + jnp.cumsum(is_pad.astype(jnp.int32)) - 1
    dst = jnp.concatenate([n + jnp.arange(tm, dtype=jnp.int32), jnp.where(is_pad, pad_row, src)])
    return src, dst, tile_group, (ends[-1] // tm).reshape(1)


SLOT_UNROLL = 8


def _slot_tokens_kernel(slot_ref, src_ref):
    n, n_slots = slot_ref.shape[0], src_ref.shape[0]

    def clear(c, carry):
        for k in range(SLOT_UNROLL):
            src_ref[c * SLOT_UNROLL + k] = 0
        return carry

    def place(c, carry):
        for k in range(SLOT_UNROLL):
            t = c * SLOT_UNROLL + k
            src_ref[slot_ref[t]] = t
        return carry

    lax.fori_loop(0, n_slots // SLOT_UNROLL, clear, 0)
    lax.fori_loop(0, n // SLOT_UNROLL, place, 0)


def _slot_tokens(slot, n_slots):
    assert slot.shape[0] % SLOT_UNROLL == 0 and n_slots % SLOT_UNROLL == 0
    return pl.pallas_call(
        _slot_tokens_kernel,
        in_specs=[pl.BlockSpec(memory_space=pltpu.SMEM)],
        out_specs=pl.BlockSpec(memory_space=pltpu.SMEM),
        out_shape=jax.ShapeDtypeStruct((n_slots,), jnp.int32),
        name="moe_slot_tokens",
    )(slot)


def _moe_group_kernel(src_ref, dst_ref, tg_ref, used_ref, h_hbm, info_hbm, w1_ref, w3_ref, w2_ref, zeros_hbm,
                      out_hbm, xbuf, gbuf, xb, gates, acc, obuf, sem_x, sem_g, sem_o, *, tm):
    del zeros_hbm
    i = pl.program_id(0)
    e = pl.program_id(1)
    used = used_ref[0]
    live = i < used
    chunk = tm // EXPERTS_PER_GROUP
    first = e * chunk
    last_step = e == EXPERTS_PER_GROUP - 1

    def gather_row(tile, r):
        tok = src_ref[tile * tm + r]
        pltpu.make_async_copy(h_hbm.at[pl.ds(tok, 1), :], xbuf.at[pl.ds(r, 1), :], sem_x).start()
        pltpu.make_async_copy(info_hbm.at[pl.ds(tok, 1), :], gbuf.at[pl.ds(r, 1), :], sem_g).start()

    def scatter_chunk(tile):
        slot = (tile + 2) % 2
        for k in range(chunk):
            row = dst_ref[(tile + 1) * tm + first + k]
            pltpu.make_async_copy(obuf.at[slot, pl.ds(first + k, 1), :], out_hbm.at[pl.ds(row, 1), :],
                                  sem_o.at[slot]).start()

    def wait_scatter(slot):
        pltpu.make_async_copy(obuf.at[slot], out_hbm.at[pl.ds(0, tm), :], sem_o.at[slot]).wait()

    @pl.when((i == 0) & (e == 0))
    def _():
        obuf[...] = jnp.zeros_like(obuf)

        def body(r, carry):
            gather_row(0, r)
            return carry
        lax.fori_loop(0, tm, body, 0)

    @pl.when((e == 0) & (i <= used))
    def _():
        pltpu.make_async_copy(h_hbm.at[pl.ds(0, tm), :], xbuf, sem_x).wait()
        pltpu.make_async_copy(info_hbm.at[pl.ds(0, tm), :], gbuf, sem_g).wait()

    @pl.when((e == 0) & live)
    def _():
        xb[...] = xbuf[...].astype(BF16)
        gates[...] = gbuf[...]
        acc[...] = jnp.zeros_like(acc)

    @pl.when(live)
    def _():
        lane = lax.broadcasted_iota(jnp.int32, (tm, LANES), 1)
        gate_lane = N_GROUPS + EXPERTS_PER_GROUP * tg_ref[i] + e
        ge = jnp.sum(jnp.where(lane == gate_lane, gates[...], 0.0), axis=1, keepdims=True)
        x = xb[...]
        a1 = jnp.dot(x, w1_ref[...], preferred_element_type=F32)
        a3 = jnp.dot(x, w3_ref[...], preferred_element_type=F32)
        hidden = (a1 * jax.nn.sigmoid(a1)) * a3 * ge
        acc[...] += jnp.dot(hidden.astype(BF16), w2_ref[...], preferred_element_type=F32)
        for k in range(chunk):
            gather_row(i + 1, first + k)
        scatter_chunk(i - 1)

    @pl.when(i == used)
    def _():
        scatter_chunk(i - 1)

    @pl.when(last_step & live & (i >= 1))
    def _():
        wait_scatter(i % 2)

    @pl.when(last_step & live)
    def _():
        obuf[i % 2] = acc[...]

    @pl.when(last_step & (i == used))
    def _():
        wait_scatter(0)
        wait_scatter(1)


def _moe_grouped(h, info, src, dst, tile_group, n_used, w1, w3, w2, layer, *, tm):
    n, d = h.shape
    f = w1.shape[3]
    n_tiles = src.shape[0] // tm
    n_rows = n + (N_GROUPS + 1) * tm
    assert tm % EXPERTS_PER_GROUP == 0 and dst.shape[0] == (n_tiles + 1) * tm

    def expert(i, e, src_, dst_, tg, used):
        e_eff = jnp.where(i < used[0], e, EXPERTS_PER_GROUP - 1)
        return (layer, tg[i] * EXPERTS_PER_GROUP + e_eff, 0, 0)

    return pl.pallas_call(
        functools.partial(_moe_group_kernel, tm=tm),
        grid_spec=pltpu.PrefetchScalarGridSpec(
            num_scalar_prefetch=4,
            grid=(n_tiles, EXPERTS_PER_GROUP),
            in_specs=[
                pl.BlockSpec(memory_space=pl.ANY),
                pl.BlockSpec(memory_space=pl.ANY),
                pl.BlockSpec((None, None, d, f), expert),
                pl.BlockSpec((None, None, d, f), expert),
                pl.BlockSpec((None, None, f, d), expert),
                pl.BlockSpec(memory_space=pl.ANY),
            ],
            out_specs=pl.BlockSpec(memory_space=pl.ANY),
            scratch_shapes=[
                pltpu.VMEM((tm, d), F32), pltpu.VMEM((tm, LANES), F32), pltpu.VMEM((tm, d), BF16),
                pltpu.VMEM((tm, LANES), F32), pltpu.VMEM((tm, d), F32), pltpu.VMEM((2, tm, d), F32),
                pltpu.SemaphoreType.DMA(()), pltpu.SemaphoreType.DMA(()), pltpu.SemaphoreType.DMA((2,)),
            ],
        ),
        out_shape=jax.ShapeDtypeStruct((n_rows, d), F32),
        input_output_aliases={9: 0},
        compiler_params=_params("arbitrary", "arbitrary"),
        name="moe_grouped",
    )(src, dst, tile_group, n_used, h, info, w1, w3, w2, jnp.zeros((n_rows, d), F32))


def _gated_add_kernel(x_ref, y_ref, g_ref, o_ref):
    o_ref[...] = x_ref[...] + g_ref[...] * y_ref[...]


def _gated_add(x, y, gate, *, tm):
    bsz, t, d = x.shape
    per_b = t // tm
    return pl.pallas_call(
        _gated_add_kernel,
        grid=(bsz, per_b),
        in_specs=[
            pl.BlockSpec((None, tm, d), lambda b, i: (b, i, 0)),
            pl.BlockSpec((tm, d), lambda b, i: (b * per_b + i, 0)),
            pl.BlockSpec((None, 1, d), lambda b, i: (b, 0, 0)),
        ],
        out_specs=pl.BlockSpec((None, tm, d), lambda b, i: (b, i, 0)),
        out_shape=jax.ShapeDtypeStruct((bsz, t, d), F32),
        compiler_params=_params("arbitrary", "arbitrary"),
        name="gated_add",
    )(x, y, gate)


MASKED = -1e30


def _t5_bucket_table(tq):
    qi = np.arange(tq, dtype=np.int32)[:, None]
    col = np.arange(2 * tq, dtype=np.int32)[None, :]
    n = np.maximum(qi + tq - col, 0)
    max_exact = N_BUCKETS // 2
    nf = np.maximum(n, 1).astype(np.float32)
    large = max_exact + (np.log(nf / np.float32(max_exact)) / np.float32(math.log(MAX_DISTANCE / max_exact))
                         * np.float32(N_BUCKETS - max_exact)).astype(np.int32)
    table = np.where(n < max_exact, n, np.minimum(large, N_BUCKETS - 1)).astype(np.int32)
    assert table[0, 0] == N_BUCKETS - 1
    return table


def _bias_kernel(rb_ref, bk_ref, o_ref):
    h = pl.program_id(0)
    bk = bk_ref[...]
    acc = jnp.zeros(bk.shape, F32)
    for k in range(N_BUCKETS):
        acc = jnp.where(bk == k, rb_ref[k, h], acc)
    o_ref[...] = (acc - rb_ref[N_BUCKETS - 1, h]) * LOG2E


def _near_bias(rel_bias, tq):
    return pl.pallas_call(
        _bias_kernel,
        grid=(N_HEADS,),
        in_specs=[pl.BlockSpec(memory_space=pltpu.SMEM),
                  pl.BlockSpec((tq, 2 * tq), lambda h: (0, 0))],
        out_specs=pl.BlockSpec((None, tq, 2 * tq), lambda h: (h, 0, 0)),
        out_shape=jax.ShapeDtypeStruct((N_HEADS, tq, 2 * tq), F32),
        name="near_bias",
    )(rel_bias, jnp.asarray(_t5_bucket_table(tq)))


_NEG_INF_KEY = int(np.int32(np.array(-np.inf, np.float32).view(np.int32)) ^ np.int32(0x7FFFFFFF))


def _sort_key(x):
    k = pltpu.bitcast(x, jnp.int32)
    return k ^ ((k >> 31) & jnp.int32(0x7FFFFFFF))


def _kth_largest_key(count_ge, kth):
    zero = jnp.int32(0)
    first = count_ge(zero)
    ans = jnp.where(first >= kth, zero, jnp.int32(-2 ** 31))
    settled = (first == kth).astype(jnp.int32)

    def cond(state):
        it, _, settled = state
        return (it < 31) & (jnp.min(settled) == 0)

    def body(state):
        it, ans, settled = state
        cand = ans | (jnp.int32(1) << (30 - it))
        count = count_ge(cand)
        return it + 1, jnp.where(count >= kth, cand, ans), settled | (count == kth).astype(jnp.int32)

    return lax.while_loop(cond, body, (jnp.int32(0), ans, settled))[1]


def _dsa_select_kernel(qi_ref, kw_ref, kall_ref, m_ref, key_ref, *, tq, tk, topk):
    i = pl.program_id(1)
    n_kt = (i * tq + tq + tk - 1) // tk
    qpos = i * tq + lax.broadcasted_iota(jnp.int32, (tq, tk), 0)
    wi = kw_ref[:, IDX_DIM:IDX_DIM + N_IDX_HEADS]
    qi = qi_ref[...].astype(BF16)
    key_ref[...] = jnp.full(key_ref.shape, _NEG_INF_KEY, jnp.int32)

    def score_tile(kt, carry):
        k0 = pl.multiple_of(kt * tk, tk)
        kid = kall_ref[pl.ds(k0, tk), 0:IDX_DIM].astype(BF16)
        acc = jnp.zeros((tq, tk), F32)
        for h in range(N_IDX_HEADS):
            s = _bdot_nt(qi[:, h * IDX_DIM:(h + 1) * IDX_DIM], kid)
            acc = acc + wi[:, h:h + 1] * jnp.maximum(s, 0.0)
        kpos = k0 + lax.broadcasted_iota(jnp.int32, (tq, tk), 1)
        sc = jnp.where(kpos <= qpos, acc * IDX_SCALE, NEG_INF)
        key_ref[:, pl.ds(k0, tk)] = _sort_key(sc)
        return carry

    lax.fori_loop(0, n_kt, score_tile, 0)

    def count_ge(cand):
        def body(kt, acc):
            k0 = pl.multiple_of(kt * tk, tk)
            ge = (key_ref[:, pl.ds(k0, tk)] >= cand).astype(jnp.int32)
            for j in range(tk // LANES):
                acc = acc + ge[:, j * LANES:(j + 1) * LANES]
            return acc
        acc = lax.fori_loop(0, n_kt, body, jnp.zeros((tq, LANES), jnp.int32))
        return jnp.sum(acc, axis=1, keepdims=True)

    thr = _kth_largest_key(count_ge, topk)
    keys = key_ref[...]
    m_ref[...] = ((keys >= thr) & (keys > _NEG_INF_KEY)).astype(m_ref.dtype)


def _dsa_select(proj, *, tq, tk, topk):
    bsz, t, _ = proj.shape
    qi_w = N_IDX_HEADS * IDX_DIM
    qi_blk = (N_HEADS * HEAD_DIM + 2 * N_KV_B * HEAD_DIM) // qi_w
    kw_blk = (N_HEADS * HEAD_DIM + 2 * N_KV_B * HEAD_DIM + qi_w) // LANES
    return pl.pallas_call(
        functools.partial(_dsa_select_kernel, tq=tq, tk=tk, topk=topk),
        grid=(bsz, t // tq),
        in_specs=[
            pl.BlockSpec((None, tq, qi_w), lambda b, i: (b, i, qi_blk)),
            pl.BlockSpec((None, tq, LANES), lambda b, i: (b, i, kw_blk)),
            pl.BlockSpec((None, t, LANES), lambda b, i: (b, 0, kw_blk)),
        ],
        out_specs=pl.BlockSpec((None, tq, t), lambda b, i: (b, i, 0)),
        out_shape=jax.ShapeDtypeStruct((bsz, t, t), BF16),
        scratch_shapes=[pltpu.VMEM((tq, t), jnp.int32)],
        compiler_params=_params("arbitrary", "arbitrary"),
        name="dsa_select",
    )(proj, proj, proj)


def _attn_kernel(q_ref, k_ref, v_ref, m_ref, tb_ref, o_ref, *state, tq, rep, shared_kv, block_mask):
    m_sc, l_sc, acc_sc = state[:rep], state[rep:2 * rep], state[2 * rep:]
    i = pl.program_id(2)
    for r in range(rep):
        m_sc[r][...] = jnp.full((tq, LANES), MASKED, F32)
        l_sc[r][...] = jnp.zeros((tq, LANES), F32)
        acc_sc[r][...] = jnp.zeros((tq, HEAD_DIM), F32)
    lane_tiles = tq // LANES

    def row_reduce(x, op, reduce):
        part = x[:, :LANES]
        for c in range(1, lane_tiles):
            part = op(part, x[:, c * LANES:(c + 1) * LANES])
        return reduce(part, axis=1, keepdims=True)

    def key_mask(j, r, causal):
        if block_mask:
            lane = lax.broadcasted_iota(jnp.int32, (tq, LANES), 1)
            vis = jnp.sum(jnp.where(lane == j, m_ref[r].astype(F32), 0.0), axis=1, keepdims=True)
            ok = jnp.broadcast_to(vis > 0.0, (tq, tq))
        else:
            ok = m_ref[:, pl.ds(pl.multiple_of(j * tq, tq), tq)] > 0
        if causal:
            ok = ok & (lax.broadcasted_iota(jnp.int32, (tq, tq), 0) >= lax.broadcasted_iota(jnp.int32, (tq, tq), 1))
        return ok

    def step(j, bias_lo, causal):
        k0 = pl.multiple_of(j * tq, tq)
        kv = lambda r: slice(0, HEAD_DIM) if shared_kv else slice(r * HEAD_DIM, (r + 1) * HEAD_DIM)
        ok_shared = None if block_mask else key_mask(j, 0, causal)
        scores = []
        for r in range(rep):
            q = (q_ref[:, r * HEAD_DIM:(r + 1) * HEAD_DIM] * QK_SCALE_LOG2).astype(BF16)
            s = _bdot_nt(q, k_ref[pl.ds(k0, tq), kv(r)])
            if bias_lo is not None:
                s = s + tb_ref[r, :, bias_lo:bias_lo + tq]
            scores.append(jnp.where(key_mask(j, r, causal) if block_mask else ok_shared, s, MASKED))
        probs, alphas = [], []
        for r in range(rep):
            m_old = m_sc[r][...]
            m_new = jnp.maximum(m_old, row_reduce(scores[r], jnp.maximum, jnp.max))
            alpha = jnp.exp2(m_old - m_new)
            p = jnp.exp2(scores[r] - jnp.concatenate([m_new] * lane_tiles, axis=1))
            l_sc[r][...] = alpha * l_sc[r][...] + row_reduce(p, jnp.add, jnp.sum)
            m_sc[r][...] = m_new
            probs.append(p.astype(BF16))
            alphas.append(alpha)
        for r in range(rep):
            pv = jnp.dot(probs[r], v_ref[pl.ds(k0, tq), kv(r)].astype(BF16), preferred_element_type=F32)
            acc_sc[r][...] = alphas[r] * acc_sc[r][...] + pv

    def far(j, carry):
        step(j, None, False)
        return carry

    lax.fori_loop(0, jnp.maximum(i - 1, 0), far, 0)

    @pl.when(i >= 1)
    def _():
        step(i - 1, 0, False)

    step(i, tq, True)
    o_ref[...] = jnp.concatenate([acc_sc[r][...] / l_sc[r][...] for r in range(rep)], axis=1).astype(o_ref.dtype)


def _attention(proj, k_col, v_col, mask, near_bias, *, tq, rep, shared_kv, block_mask):
    bsz, t, _ = proj.shape
    g = N_HEADS // rep
    kv_w = HEAD_DIM if shared_kv else rep * HEAD_DIM
    assert HEAD_DIM == LANES and k_col % kv_w == 0 and v_col % kv_w == 0
    k_blk, v_blk = k_col // kv_w, v_col // kv_w
    if block_mask:
        assert tq == MOBA_BLOCK
        mask_spec = pl.BlockSpec((None, rep, tq, LANES), lambda b, h, i: (b, h, i, 0))
    else:
        mask_spec = pl.BlockSpec((None, tq, t), lambda b, h, i: (b, i, 0))
    stat = pltpu.VMEM((tq, LANES), F32)
    return pl.pallas_call(
        functools.partial(_attn_kernel, tq=tq, rep=rep, shared_kv=shared_kv, block_mask=block_mask),
        grid=(bsz, g, t // tq),
        in_specs=[
            pl.BlockSpec((None, tq, rep * HEAD_DIM), lambda b, h, i: (b, i, h)),
            pl.BlockSpec((None, t, kv_w), lambda b, h, i: (b, 0, k_blk + h)),
            pl.BlockSpec((None, t, kv_w), lambda b, h, i: (b, 0, v_blk + h)),
            mask_spec,
            pl.BlockSpec((rep, tq, 2 * tq), lambda b, h, i: (h, 0, 0)),
        ],
        out_specs=pl.BlockSpec((None, tq, rep * HEAD_DIM), lambda b, h, i: (b, i, h)),
        out_shape=jax.ShapeDtypeStruct((bsz, t, N_HEADS * HEAD_DIM), BF16),
        scratch_shapes=[stat] * (3 * rep),
        compiler_params=_params("arbitrary", "arbitrary", "arbitrary"),
        name="masked_attention",
    )(proj, proj, proj, mask, near_bias)


PAGES_PER_BLOCK = MOBA_BLOCK // PAGE_SIZE
MEANS_BLOCKS_PER_STEP = 4


def _block_means_kernel(tbl_ref, *refs):
    del tbl_ref
    page_refs, o_ref = refs[:-1], refs[-1]
    for blk in range(len(page_refs) // PAGES_PER_BLOCK):
        pages = page_refs[blk * PAGES_PER_BLOCK:(blk + 1) * PAGES_PER_BLOCK]
        total = jnp.sum(pages[0][...], axis=0)
        for ref in pages[1:]:
            total = total + jnp.sum(ref[...], axis=0)
        o_ref[blk] = total / MOBA_BLOCK


def _block_means(pool, layer, table):
    _, _, _, h, dh = pool.shape
    bsz, npg = table.shape
    nb = npg // PAGES_PER_BLOCK
    blocks = math.gcd(nb, MEANS_BLOCKS_PER_STEP)
    per_step = blocks * PAGES_PER_BLOCK
    assert npg % PAGES_PER_BLOCK == 0
    page = lambda k: pl.BlockSpec((None, None, PAGE_SIZE, h, dh),
                                  lambda b, n, tbl: (layer, tbl[b, per_step * n + k], 0, 0, 0))
    out = pl.pallas_call(
        _block_means_kernel,
        grid_spec=pltpu.PrefetchScalarGridSpec(
            num_scalar_prefetch=1,
            grid=(bsz, nb // blocks),
            in_specs=[page(k) for k in range(per_step)],
            out_specs=pl.BlockSpec((None, blocks, h, dh), lambda b, n, tbl: (b, n, 0, 0)),
        ),
        out_shape=jax.ShapeDtypeStruct((bsz, nb, h, dh), F32),
        compiler_params=_params("arbitrary", "arbitrary"),
        name="block_means",
    )(table, *([pool] * per_step))
    return out.reshape(bsz, nb, h * dh)


def _moba_select_kernel(q_ref, mean_ref, o_ref, *, tq, past_len):
    i = pl.program_id(1)
    own = (past_len + i * tq + lax.broadcasted_iota(jnp.int32, (tq, LANES), 0)) // MOBA_BLOCK
    blk = lax.broadcasted_iota(jnp.int32, (tq, LANES), 1)
    past = blk < own
    for h in range(N_HEADS):
        sl = slice(h * HEAD_DIM, (h + 1) * HEAD_DIM)
        gate = lax.dot_general(q_ref[:, sl], mean_ref[:, sl], (((1,), (1,)), ((), ())),
                               precision=lax.Precision.HIGHEST, preferred_element_type=F32)
        g = jnp.where(past, gate, NEG_INF)
        kth = jnp.max(g, axis=1, keepdims=True)
        for _ in range(MOBA_TOPK - 1):
            kth = jnp.max(jnp.where(g < kth, g, NEG_INF), axis=1, keepdims=True)
        o_ref[h] = ((past & (g >= kth)) | (blk == own)).astype(o_ref.dtype)


def _moba_select(q_src, means, *, tq, past_len):
    bsz, t, _ = q_src.shape
    w = N_HEADS * HEAD_DIM
    return pl.pallas_call(
        functools.partial(_moba_select_kernel, tq=tq, past_len=past_len),
        grid=(bsz, t // tq),
        in_specs=[
            pl.BlockSpec((None, tq, w), lambda b, i: (b, i, 0)),
            pl.BlockSpec((None, LANES, w), lambda b, i: (b, 0, 0)),
        ],
        out_specs=pl.BlockSpec((None, N_HEADS, tq, LANES), lambda b, i: (b, 0, i, 0)),
        out_shape=jax.ShapeDtypeStruct((bsz, N_HEADS, t, LANES), BF16),
        compiler_params=_params("arbitrary", "arbitrary"),
        name="moba_select",
    )(q_src, means)


def _dsa_select_paged_kernel(tbl_ref, qi_ref, wi_ref, *rest, pp, topk, past_len, t_new):
    del tbl_ref
    page_refs = rest[:pp]
    knew_ref, m_ref, key_ref = rest[pp:]
    p = pl.program_id(1)
    qi = qi_ref[...].astype(BF16)
    wi = wi_ref[...]

    def score(kpage_t):
        s = _bdot(qi, kpage_t)
        w = wi * jnp.maximum(s, 0.0)
        return jnp.sum(w.reshape(N_IDX_HEADS, t_new, PAGE_SIZE), axis=0) * IDX_SCALE

    for k in range(pp):
        off = pl.multiple_of((p * pp + k) * PAGE_SIZE, PAGE_SIZE)
        key_ref[:, pl.ds(off, PAGE_SIZE)] = _sort_key(score(page_refs[k][...]))

    @pl.when(p == pl.num_programs(1) - 1)
    def _():
        row = lax.broadcasted_iota(jnp.int32, (t_new, PAGE_SIZE), 0)
        col = lax.broadcasted_iota(jnp.int32, (t_new, PAGE_SIZE), 1)
        sc = jnp.where(col <= row, score(knew_ref[...]), NEG_INF)
        key_ref[:, past_len:past_len + PAGE_SIZE] = _sort_key(sc)

        def count_ge(cand):
            return jnp.sum((key_ref[...] >= cand).astype(jnp.int32), axis=1, keepdims=True)

        thr = _kth_largest_key(count_ge, topk)
        keys = key_ref[...]
        m_ref[...] = ((keys >= thr) & (keys > _NEG_INF_KEY)).astype(m_ref.dtype)


def _dsa_select_paged(qi_rows, wi_rows, kidx_pool, layer, page_table, kidx_new, *, pp, topk):
    bsz, rows, _ = qi_rows.shape
    t_new = rows // N_IDX_HEADS
    n_pages = page_table.shape[1]
    assert n_pages % pp == 0
    past_len = n_pages * PAGE_SIZE
    lpad = past_len + PAGE_SIZE
    kidx_pool = jnp.swapaxes(kidx_pool, 2, 3)
    kidx_new = jnp.swapaxes(kidx_new, 1, 2)
    page_spec = lambda k: pl.BlockSpec((None, None, IDX_DIM, PAGE_SIZE),
                                       lambda b, p, tbl: (layer, tbl[b, p * pp + k], 0, 0))
    return pl.pallas_call(
        functools.partial(_dsa_select_paged_kernel, pp=pp, topk=topk, past_len=past_len, t_new=t_new),
        grid_spec=pltpu.PrefetchScalarGridSpec(
            num_scalar_prefetch=1,
            grid=(bsz, n_pages // pp),
            in_specs=[
                pl.BlockSpec((None, rows, IDX_DIM), lambda b, p, tbl: (b, 0, 0)),
                pl.BlockSpec((None, rows, 1), lambda b, p, tbl: (b, 0, 0)),
                *[page_spec(k) for k in range(pp)],
                pl.BlockSpec((None, IDX_DIM, PAGE_SIZE), lambda b, p, tbl: (b, 0, 0)),
            ],
            out_specs=pl.BlockSpec((None, t_new, lpad), lambda b, p, tbl: (b, 0, 0)),
            scratch_shapes=[pltpu.VMEM((t_new, lpad), jnp.int32)],
        ),
        out_shape=jax.ShapeDtypeStruct((bsz, t_new, lpad), BF16),
        compiler_params=_params("arbitrary", "arbitrary"),
        name="dsa_select_paged",
    )(page_table, qi_rows, wi_rows, *([kidx_pool] * pp), kidx_new)


def _col_vector(row_vec):
    n = row_vec.shape[1]
    eye = lax.broadcasted_iota(jnp.int32, (n, n), 0) == lax.broadcasted_iota(jnp.int32, (n, n), 1)
    return jnp.sum(jnp.where(eye, row_vec, 0.0), axis=1, keepdims=True)


def _paged_attn_kernel(tbl_ref, qbd_ref, *rest, pp, key_mask):
    del tbl_ref
    k_refs, v_refs = rest[:pp], rest[pp:2 * pp]
    mt_ref, mnew_ref, knew_ref, vnew_ref, bias_ref, o_ref, m_sc, l_sc, acc_sc = rest[2 * pp:]
    p = pl.program_id(1)
    last = pl.num_programs(1) - 1

    def key_rows(flags):
        t_new = flags.shape[0]
        pick = (lax.broadcasted_iota(jnp.int32, (t_new, LANES), 1) % t_new
                == lax.broadcasted_iota(jnp.int32, (t_new, LANES), 0)).astype(BF16)
        neg = ((flags.astype(F32) - 1.0) * -MASKED).astype(BF16)
        return lax.dot_general(neg, pick, (((0,), (0,)), ((), ())), preferred_element_type=F32)

    if key_mask:
        past_masks = key_rows(mt_ref[...])
        page_mask = lambda k: past_masks[k * PAGE_SIZE:(k + 1) * PAGE_SIZE, :]
        new_mask = lambda: key_rows(mnew_ref[...])
    else:
        page_mask = lambda k: mt_ref[pl.ds((p * pp + k) // PAGES_PER_BLOCK, 1), :]
        new_mask = lambda: mnew_ref[...]

    @pl.when(p == 0)
    def _():
        m_sc[...] = jnp.full(m_sc.shape, MASKED, F32)
        l_sc[...] = jnp.zeros_like(l_sc)
        acc_sc[...] = jnp.zeros_like(acc_sc)

    n_kv = qbd_ref.shape[0] // HEAD_DIM

    def heads_on_lanes(ref):
        return jnp.concatenate([ref[pl.ds(g, PAGE_SIZE, stride=n_kv), :] for g in range(n_kv)], axis=1)

    def pages(kps, vps, masks, biases):
        scores = []
        for kp, mask, bias in zip(kps, masks, biases):
            s = jnp.dot(kp().astype(BF16), qbd_ref[...], preferred_element_type=F32)
            if bias is not None:
                s = s + bias
            scores.append(s + mask)
        m_old = m_sc[...]
        m_new = m_old
        for s in scores:
            m_new = jnp.maximum(m_new, jnp.max(s, axis=0, keepdims=True))
        alpha = jnp.exp2(m_old - m_new)
        l_new = alpha * l_sc[...]
        pv = None
        for s, vp in zip(scores, vps):
            pt = jnp.exp2(s - m_new)
            l_new = l_new + jnp.sum(pt, axis=0, keepdims=True)
            part = lax.dot_general(pt.astype(BF16), vp().astype(BF16), (((0,), (0,)), ((), ())),
                                   preferred_element_type=F32)
            pv = part if pv is None else pv + part
        l_sc[...] = l_new
        acc_sc[...] = acc_sc[...] * _col_vector(alpha) + pv
        m_sc[...] = m_new

    is_last = (p == last).astype(F32)
    pages([functools.partial(heads_on_lanes, r) for r in k_refs],
          [functools.partial(heads_on_lanes, r) for r in v_refs],
          [page_mask(k) for k in range(pp)],
          [None] * (pp - 1) + [bias_ref[0] * is_last])

    @pl.when(p == last)
    def _():
        pages([lambda: knew_ref[...]], [lambda: vnew_ref[...]], [new_mask()], [bias_ref[1]])
        o_ref[...] = acc_sc[...] / _col_vector(l_sc[...])


def _paged_attention(qbd, k_pool, v_pool, layer, page_table, mask, new_mask, k_new, v_new, bias_t, *, pp, key_mask):
    bsz, w, _ = qbd.shape
    n_pages = page_table.shape[1]
    assert n_pages % pp == 0
    if key_mask:
        t_new = mask.shape[1]
        mask_specs = [pl.BlockSpec((None, t_new, pp * PAGE_SIZE), lambda b, p, tbl: (b, 0, p)),
                      pl.BlockSpec((None, t_new, PAGE_SIZE), lambda b, p, tbl: (b, 0, n_pages))]
        new_mask = mask
    else:
        mask_specs = [pl.BlockSpec((None,) + mask.shape[1:], lambda b, p, tbl: (b, 0, 0)),
                      pl.BlockSpec((PAGE_SIZE, LANES), lambda b, p, tbl: (0, 0))]
    n_layers, n_phys, _, n_kv, _ = k_pool.shape
    assert n_kv * HEAD_DIM == w
    as_rows = lambda pool: pool.reshape(n_layers, n_phys, PAGE_SIZE * n_kv, HEAD_DIM)
    k_pool, v_pool = as_rows(k_pool), as_rows(v_pool)
    page_spec = lambda k: pl.BlockSpec((None, None, PAGE_SIZE * n_kv, HEAD_DIM),
                                       lambda b, p, tbl: (layer, tbl[b, p * pp + k], 0, 0))
    per_b = lambda shp: pl.BlockSpec((None,) + shp, lambda b, p, tbl: (b, 0, 0))
    return pl.pallas_call(
        functools.partial(_paged_attn_kernel, pp=pp, key_mask=key_mask),
        grid_spec=pltpu.PrefetchScalarGridSpec(
            num_scalar_prefetch=1,
            grid=(bsz, n_pages // pp),
            in_specs=[
                per_b((w, LANES)),
                *[page_spec(k) for k in range(pp)],
                *[page_spec(k) for k in range(pp)],
                *mask_specs,
                per_b((PAGE_SIZE, w)),
                per_b((PAGE_SIZE, w)),
                pl.BlockSpec((2, PAGE_SIZE, LANES), lambda b, p, tbl: (0, 0, 0)),
            ],
            out_specs=per_b((LANES, w)),
            scratch_shapes=[pltpu.VMEM((1, LANES), F32), pltpu.VMEM((1, LANES), F32), pltpu.VMEM((LANES, w), F32)],
        ),
        out_shape=jax.ShapeDtypeStruct((bsz, LANES, w), F32),
        compiler_params=_params("arbitrary", "arbitrary"),
        name="paged_attention",
    )(page_table, qbd, *([k_pool] * pp), *([v_pool] * pp), mask, new_mask, k_new, v_new, bias_t)


def _block_diag_queries(q, n_kv):
    bsz, t, _ = q.shape
    q4 = q.reshape(bsz, t, N_HEADS, HEAD_DIM) * QK_SCALE_LOG2
    onehot = (jnp.arange(N_HEADS)[:, None] // (N_HEADS // n_kv) == jnp.arange(n_kv)[None, :]).astype(q.dtype)
    qbd = q4[:, :, :, None, :] * onehot[None, None, :, :, None]
    return qbd.transpose(0, 3, 4, 2, 1).reshape(bsz, n_kv * HEAD_DIM, N_HEADS * t).astype(BF16)


def _own_head_lanes(out, n_kv, t):
    bsz = out.shape[0]
    out5 = out.reshape(bsz, N_HEADS, t, n_kv, HEAD_DIM)
    heads = jnp.arange(N_HEADS)
    picked = out5[:, heads, :, heads // (N_HEADS // n_kv), :]
    return picked.transpose(1, 2, 0, 3).reshape(bsz, t, N_HEADS * HEAD_DIM)


def _decode_bias(near_bias, tq, t):
    tail = near_bias[:, :t, tq - PAGE_SIZE:tq + PAGE_SIZE]
    return tail.transpose(2, 0, 1).reshape(2, PAGE_SIZE, N_HEADS * t)


def _pad_rows(a, rows):
    return jnp.pad(a, ((0, 0), (0, rows - a.shape[1]), (0, 0)))


def _new_token_mask(t):
    key = jnp.arange(PAGE_SIZE)[:, None]
    tok = jnp.tile(jnp.arange(t), N_HEADS)[None, :]
    return jnp.where(key <= tok, 0.0, MASKED).astype(F32)


def _split_heads_kernel(x_ref, o_ref, *, tm, n_heads):
    for h in range(n_heads):
        o_ref[pl.ds(h, tm, stride=n_heads), :] = x_ref[:, h * HEAD_DIM:(h + 1) * HEAD_DIM]


def _split_heads(proj, col, n_heads, *, tm):
    bsz, t, _ = proj.shape
    w = n_heads * HEAD_DIM
    assert col % w == 0 and t % tm == 0
    out = pl.pallas_call(
        functools.partial(_split_heads_kernel, tm=tm, n_heads=n_heads),
        grid=(bsz, t // tm),
        in_specs=[pl.BlockSpec((None, tm, w), lambda b, i: (b, i, col // w))],
        out_specs=pl.BlockSpec((None, tm * n_heads, HEAD_DIM), lambda b, i: (b, i, 0)),
        out_shape=jax.ShapeDtypeStruct((bsz, t * n_heads, HEAD_DIM), F32),
        compiler_params=_params("arbitrary", "arbitrary"),
        name="split_heads",
    )(proj)
    return out.reshape(bsz, t, n_heads, HEAD_DIM)


ATTN_TQ = 256
SELECT_TQ = 256
SELECT_TK = 512
MOBA_HEADS_PER_STEP = 4
SPLIT_TM = 256
DSA_PAGES_PER_STEP = 8
MOBA_PAGES_PER_STEP = 4
Q_W = N_HEADS * HEAD_DIM
KV_W = N_KV_B * HEAD_DIM
DSA_PROJ = Q_W + 2 * KV_W + N_IDX_HEADS * IDX_DIM + IDX_DIM + N_IDX_HEADS
DSA_PROJ_PAD = -(-DSA_PROJ // 512) * 512


def _dsa_split(proj):
    k = proj[..., Q_W:Q_W + KV_W]
    v = proj[..., Q_W + KV_W:Q_W + 2 * KV_W]
    o = Q_W + 2 * KV_W + N_IDX_HEADS * IDX_DIM
    return k, v, proj[..., o:o + IDX_DIM]


def _dsa_prompt(proj, near_bias):
    bsz, t, _ = proj.shape
    ki = _dsa_split(proj)[2]
    mask = _dsa_select(proj, tq=min(SELECT_TQ, t), tk=min(SELECT_TK, t), topk=min(IDX_TOPK, t // 4))
    o = _attention(proj, Q_W, Q_W + KV_W, mask, near_bias, tq=ATTN_TQ, rep=N_HEADS // N_KV_B,
                   shared_kv=True, block_mask=False)
    k = _split_heads(proj, Q_W, N_KV_B, tm=min(SPLIT_TM, t))
    v = _split_heads(proj, Q_W + KV_W, N_KV_B, tm=min(SPLIT_TM, t))
    return o, k, v, ki


def _dsa_sample(proj, k_pool, v_pool, kidx_pool, layer, page_table, near_bias):
    bsz, t, _ = proj.shape
    past_len = page_table.shape[1] * PAGE_SIZE
    k, v, ki = _dsa_split(proj)
    o_qi = Q_W + 2 * KV_W
    qi = proj[..., o_qi:o_qi + N_IDX_HEADS * IDX_DIM].reshape(bsz, t, N_IDX_HEADS, IDX_DIM)
    wi = proj[..., o_qi + N_IDX_HEADS * IDX_DIM + IDX_DIM:DSA_PROJ]
    qi_rows = qi.transpose(0, 2, 1, 3).reshape(bsz, N_IDX_HEADS * t, IDX_DIM)
    wi_rows = wi.transpose(0, 2, 1).reshape(bsz, N_IDX_HEADS * t, 1)
    sel = _dsa_select_paged(qi_rows, wi_rows, kidx_pool, layer, page_table, _pad_rows(ki, PAGE_SIZE),
                            pp=DSA_PAGES_PER_STEP, topk=min(IDX_TOPK, (past_len + t) // 4))
    out = _paged_attention(
        _block_diag_queries(proj[..., :Q_W], N_KV_B), k_pool, v_pool, layer, page_table, sel, None,
        _pad_rows(k, PAGE_SIZE), _pad_rows(v, PAGE_SIZE), _decode_bias(near_bias, ATTN_TQ, t),
        pp=DSA_PAGES_PER_STEP, key_mask=True)
    o = _own_head_lanes(out, N_KV_B, t).astype(BF16)
    return o, k.reshape(bsz, t, N_KV_B, HEAD_DIM), v.reshape(bsz, t, N_KV_B, HEAD_DIM), ki


def _moba_prompt(proj, near_bias):
    bsz, t, _ = proj.shape
    k_heads = _split_heads(proj, Q_W, N_HEADS, tm=min(SPLIT_TM, t))
    v_heads = _split_heads(proj, 2 * Q_W, N_HEADS, tm=min(SPLIT_TM, t))
    n_pages = t // PAGE_SIZE
    pages = jnp.arange(bsz * n_pages, dtype=jnp.int32).reshape(bsz, n_pages)
    means = _block_means(k_heads.reshape(1, bsz * n_pages, PAGE_SIZE, N_HEADS, HEAD_DIM), 0, pages)
    sel = _moba_select(proj, _pad_rows(means, LANES), tq=ATTN_TQ, past_len=0)
    o = _attention(proj, Q_W, 2 * Q_W, sel, near_bias, tq=ATTN_TQ, rep=MOBA_HEADS_PER_STEP,
                   shared_kv=False, block_mask=True)
    return o, k_heads, v_heads


def _moba_sample(proj, k_pool, v_pool, layer, page_table, near_bias):
    bsz, t, _ = proj.shape
    n_pages = page_table.shape[1]
    past_len = n_pages * PAGE_SIZE
    k = proj[..., Q_W:2 * Q_W]
    v = proj[..., 2 * Q_W:3 * Q_W]
    means = _block_means(k_pool, layer, page_table)
    sel = _moba_select(proj, _pad_rows(means, LANES), tq=t, past_len=past_len)
    block_rows = (sel.astype(F32).transpose(0, 3, 1, 2).reshape(bsz, LANES, N_HEADS * t) - 1.0) * -MASKED
    out = _paged_attention(
        _block_diag_queries(proj[..., :Q_W], N_HEADS), k_pool, v_pool, layer, page_table,
        block_rows, _new_token_mask(t), _pad_rows(k, PAGE_SIZE), _pad_rows(v, PAGE_SIZE),
        _decode_bias(near_bias, ATTN_TQ, t), pp=MOBA_PAGES_PER_STEP, key_mask=False)
    o = _own_head_lanes(out, N_HEADS, t).astype(BF16)
    return o, k.reshape(bsz, t, N_HEADS, HEAD_DIM), v.reshape(bsz, t, N_HEADS, HEAD_DIM)


N_MIXERS = 3
PROMPT_TM = 1024
MATMUL_TN = 512
LRU_TT = 256
MOE_TM = 512
ROUTE_W = LANES


def kernel(x_prompt, x_sample, state_lru_h, state_lru_conv, cache_dsa_k, cache_dsa_v, cache_dsa_kidx,
           cache_moba_k, cache_moba_v, page_table, c_prompt, c_sample, rel_bias, norm_mix_g, norm_ffn_g,
           final_norm_g, ada_w, ada_b, lru_w_in, lru_conv_w, lru_conv_b, lru_w_rgate, lru_b_rgate,
           lru_w_igate, lru_b_igate, lru_lambda, lru_w_out, dsa_w_in, dsa_w_out, moba_w_in, moba_w_out,
           moe_w_group, moe_w_expert, moe_w1, moe_w3, moe_w2):
    bp, tp, d = x_prompt.shape
    bs, ts, _ = x_sample.shape
    depth = ada_w.shape[0]
    ns = bs * ts
    c_rnn = lru_w_out.shape[1]

    c_all = jnp.concatenate([c_prompt, c_sample], axis=0)
    c_rows = -(-c_all.shape[0] // SUBLANES) * SUBLANES
    mods = _ada(jnp.pad(c_all, ((0, c_rows - c_all.shape[0]), (0, 0))), ada_w, ada_b)
    mods = mods.reshape(depth, c_rows, 6, d)
    near_bias = _near_bias(rel_bias, ATTN_TQ)

    w1, w3, w2 = moe_w1.astype(BF16), moe_w3.astype(BF16), moe_w2.astype(BF16)
    xp = x_prompt
    pending = None
    xs = x_sample.reshape(1, ns, d)
    outs = {n: [] for n in ("lru_h_p", "lru_c_p", "lru_h_s", "lru_c_s", "dsa_k_p", "dsa_v_p", "dsa_i_p",
                            "dsa_k_s", "dsa_v_s", "dsa_i_s", "moba_k_p", "moba_v_p", "moba_k_s", "moba_v_s")}
    for i in range(depth):
        kind, j = i % N_MIXERS, i // N_MIXERS
        mod_p = [mods[i, :bp, m][:, None, :] for m in range(6)]
        mod_s = [jnp.repeat(mods[i, bp:bp + bs, m], ts, axis=0)[None] for m in range(6)]
        shp1, scp1, gtp1, shp2, scp2, gtp2 = mod_p
        shs1, scs1, gts1, shs2, scs2, gts2 = mod_s
        mix_in = lambda w, tn=MATMUL_TN: (
            _norm_matmul(xp, norm_mix_g[i], shp1, scp1, w, tm=min(PROMPT_TM, tp), tn=tn, layer=j),
            _norm_matmul(xs, norm_mix_g[i], shs1, scs1, w, tm=ns, tn=tn, layer=j))
        if kind == 0:
            up, us = mix_in(lru_w_in)
            lru_params = (lru_conv_w[j], lru_conv_b[j], lru_w_rgate[j], lru_b_rgate[j], lru_w_igate[j],
                          lru_b_igate[j], lru_lambda[j])
            yp, hp_state, cp = _lru_core(up, jnp.zeros((bp, CONV_W - 1, c_rnn), F32), jnp.zeros((bp, c_rnn), F32),
                                         *lru_params, tt=LRU_TT)
            ys, hs_state, cs = _lru_core(us.reshape(bs, ts, 2 * c_rnn), state_lru_conv[j], state_lru_h[j],
                                         *lru_params, tt=ts)
            ys = ys.reshape(1, ns, c_rnn)
            w_out = lru_w_out
            outs["lru_h_p"].append(hp_state)
            outs["lru_c_p"].append(cp)
            outs["lru_h_s"].append(hs_state)
            outs["lru_c_s"].append(cs)
        elif kind == 1:
            pp_, ps_ = mix_in(jnp.pad(dsa_w_in, ((0, 0), (0, 0), (0, DSA_PROJ_PAD - DSA_PROJ))))
            yp, kp, vp, ip = _dsa_prompt(pp_, near_bias)
            ys, k_s, v_s, i_s = _dsa_sample(ps_.reshape(bs, ts, DSA_PROJ_PAD), cache_dsa_k, cache_dsa_v,
                                            cache_dsa_kidx, j, page_table, near_bias)
            ys = ys.reshape(1, ns, Q_W)
            w_out = dsa_w_out
            for n, val in zip(("dsa_k_p", "dsa_v_p", "dsa_i_p", "dsa_k_s", "dsa_v_s", "dsa_i_s"),
                              (kp, vp, ip, k_s, v_s, i_s)):
                outs[n].append(val)
        else:
            pp_, ps_ = mix_in(moba_w_in)
            yp, kp, vp = _moba_prompt(pp_, near_bias)
            ys, k_s, v_s = _moba_sample(ps_.reshape(bs, ts, 3 * Q_W), cache_moba_k, cache_moba_v, j,
                                        page_table, near_bias)
            ys = ys.reshape(1, ns, Q_W)
            w_out = moba_w_out
            for n, val in zip(("moba_k_p", "moba_v_p", "moba_k_s", "moba_v_s"), (kp, vp, k_s, v_s)):
                outs[n].append(val)
        xp = _matmul_residual(yp, w_out, xp, gtp1, tm=min(PROMPT_TM, tp), tn=min(MATMUL_TN, d), layer=j)
        xs = _matmul_residual(ys, w_out, xs, gts1, tm=ns, tn=min(MATMUL_TN, d), layer=j)

        w_route = jnp.pad(jnp.concatenate([moe_w_group[i], moe_w_expert[i]], axis=1),
                          ((0, 0), (0, ROUTE_W - N_GROUPS - N_EXPERTS)))
        lg_p, h_p = _norm_matmul(xp, norm_ffn_g[i], shp2, scp2, w_route, tm=min(PROMPT_TM, tp), tn=ROUTE_W,
                                 emit_h=True, highest=True, h_dtype=F32)
        moe_tm = min(MOE_TM, tp)
        info, counts = _route_tokens(lg_p.reshape(bp * tp, ROUTE_W), tm=moe_tm)
        moe_p = _moe_grouped(h_p.reshape(bp * tp, d), info, *_group_layout(info, counts, moe_tm), w1, w3, w2, i,
                             tm=moe_tm)
        if i + 1 < depth:
            xp = _gated_add(xp, moe_p, gtp2, tm=moe_tm)
        else:
            pending = (moe_p, gtp2)
        lg_s, h_s = _norm_matmul(xs, norm_ffn_g[i], shs2, scs2, w_route, tm=ns, tn=ROUTE_W,
                                 emit_h=True, highest=True)
        xs = _moe_dense(h_s, lg_s, w1, w3, w2, i, xs, gts2, tm=ns)

    y_prompt = _rmsnorm(xp, final_norm_g, tm=min(MOE_TM, tp), residual=pending)
    y_sample = _rmsnorm(xs, final_norm_g, tm=ns).reshape(bs, ts, d)
    stack = lambda n: jnp.stack(outs[n])
    return (y_prompt, y_sample, stack("lru_h_p"), stack("lru_c_p"), stack("lru_h_s"), stack("lru_c_s"),
            stack("dsa_k_p"), stack("dsa_v_p"), stack("dsa_i_p"), stack("dsa_k_s"), stack("dsa_v_s"),
            stack("dsa_i_s"), stack("moba_k_p"), stack("moba_v_p"), stack("moba_k_s"), stack("moba_v_s"))
```

```python
import functools
import math

import numpy as np
import jax
import jax.numpy as jnp
from jax import lax
from jax.experimental import pallas as pl
from jax.experimental.pallas import tpu as pltpu

F32 = jnp.float32
BF16 = jnp.bfloat16

LANES = 128
SUBLANES = 8
VMEM_LIMIT_BYTES = 56 * 1024 * 1024

HEAD_DIM = 128
N_HEADS = 16
N_KV_B = 4
N_IDX_HEADS = 16
IDX_DIM = 64
IDX_TOPK = 256
IDX_SCALE = (IDX_DIM * N_IDX_HEADS) ** -0.5
PAGE_SIZE = 128
MOBA_BLOCK = 256
MOBA_TOPK = 3
N_BUCKETS = 32
MAX_DISTANCE = 128
N_RNN_BLOCKS = 16
CONV_W = 4
LRU_C = 8.0
N_GROUPS = 4
EXPERTS_PER_GROUP = 4
N_EXPERTS = N_GROUPS * EXPERTS_PER_GROUP
EPS = 1e-6
ATTN_SCALE = HEAD_DIM ** -0.5
LOG2E = math.log2(math.e)
QK_SCALE_LOG2 = ATTN_SCALE * LOG2E
NEG_INF = float("-inf")


def _params(*sem):
    return pltpu.CompilerParams(dimension_semantics=sem, vmem_limit_bytes=VMEM_LIMIT_BYTES)


def _bdot(a, b):
    return jnp.dot(a.astype(BF16), b.astype(BF16), preferred_element_type=F32)


def _bdot_nt(a, b):
    return lax.dot_general(a.astype(BF16), b.astype(BF16), (((1,), (1,)), ((), ())),
                           preferred_element_type=F32)


def _ada_kernel(c_ref, w_ref, b_ref, o_ref):
    c = c_ref[...]
    o_ref[...] = _bdot(c * jax.nn.sigmoid(c), w_ref[...]) + b_ref[...]


def _ada(c, ada_w, ada_b):
    depth, d, n = ada_w.shape
    rows = c.shape[0]
    tn = next(w for w in (1024, 512, 256, LANES) if n % w == 0)
    return pl.pallas_call(
        _ada_kernel,
        grid=(depth, n // tn),
        in_specs=[
            pl.BlockSpec((rows, d), lambda l, j: (0, 0)),
            pl.BlockSpec((None, d, tn), lambda l, j: (l, 0, j)),
            pl.BlockSpec((None, 1, tn), lambda l, j: (l, 0, j)),
        ],
        out_specs=pl.BlockSpec((None, rows, tn), lambda l, j: (l, 0, j)),
        out_shape=jax.ShapeDtypeStruct((depth, rows, n), F32),
        compiler_params=_params("arbitrary", "arbitrary"),
        name="ada_mod",
    )(c, ada_w, ada_b.reshape(depth, 1, n))


def _norm_mod(x, g, shift, scale):
    ms = jnp.mean(x * x, axis=-1, keepdims=True)
    h = x * lax.rsqrt(ms + EPS) * g
    return h * (1.0 + scale) + shift


def _norm_mm_kernel(x_ref, g_ref, sh_ref, sc_ref, w_ref, *refs, emit_h, highest):
    if emit_h:
        o_ref, ho_ref, h_ref = refs
    else:
        o_ref, h_ref = refs
        ho_ref = None

    @pl.when(pl.program_id(2) == 0)
    def _():
        h = _norm_mod(x_ref[...], g_ref[...], sh_ref[...], sc_ref[...])
        h_ref[...] = h.astype(h_ref.dtype)
        if ho_ref is not None:
            ho_ref[...] = h.astype(ho_ref.dtype)

    if highest:
        o_ref[...] = jnp.dot(h_ref[...], w_ref[...], precision=lax.Precision.HIGHEST,
                             preferred_element_type=F32)
    else:
        o_ref[...] = _bdot(h_ref[...], w_ref[...])


def _weight_spec(w, layer, tn):
    if w.ndim == 2:
        return pl.BlockSpec((w.shape[0], tn), lambda b, i, j: (0, j))
    return pl.BlockSpec((None, w.shape[1], tn), lambda b, i, j: (layer, 0, j))


def _norm_matmul(x, g, shift, scale, w, *, tm, tn, emit_h=False, highest=False, h_dtype=BF16, layer=None):
    bsz, t, d = x.shape
    n = w.shape[-1]
    r = shift.shape[1]
    assert t % tm == 0 and n % tn == 0 and (r == 1 or r == tm == t)
    out_shape = [jax.ShapeDtypeStruct((bsz, t, n), F32)]
    out_specs = [pl.BlockSpec((None, tm, tn), lambda b, i, j: (b, i, j))]
    row_tile = pl.BlockSpec((None, tm, d), lambda b, i, j: (b, i, 0))
    if emit_h:
        out_shape.append(jax.ShapeDtypeStruct((bsz, t, d), h_dtype))
        out_specs.append(row_tile)
    mod_map = (lambda b, i, j: (b, 0, 0)) if r == 1 else (lambda b, i, j: (b, i, 0))
    res = pl.pallas_call(
        functools.partial(_norm_mm_kernel, emit_h=emit_h, highest=highest),
        grid=(bsz, t // tm, n // tn),
        in_specs=[
            row_tile,
            pl.BlockSpec((1, d), lambda b, i, j: (0, 0)),
            pl.BlockSpec((None, r, d), mod_map),
            pl.BlockSpec((None, r, d), mod_map),
            _weight_spec(w, layer, tn),
        ],
        out_specs=out_specs,
        out_shape=out_shape,
        scratch_shapes=[pltpu.VMEM((tm, d), F32 if highest else BF16)],
        compiler_params=_params("arbitrary", "arbitrary", "arbitrary"),
        name="norm_matmul",
    )(x, g.reshape(1, d), shift, scale, w)
    return res if emit_h else res[0]


def _mm_res_kernel(a_ref, w_ref, r_ref, g_ref, o_ref):
    o_ref[...] = r_ref[...] + g_ref[...] * _bdot(a_ref[...], w_ref[...])


def _matmul_residual(a, w, res, gate, *, tm, tn, layer=None):
    bsz, t, k = a.shape
    n = w.shape[-1]
    r = gate.shape[1]
    assert t % tm == 0 and n % tn == 0 and (r == 1 or r == tm == t)
    gate_map = (lambda b, i, j: (b, 0, j)) if r == 1 else (lambda b, i, j: (b, i, j))
    return pl.pallas_call(
        _mm_res_kernel,
        grid=(bsz, t // tm, n // tn),
        in_specs=[
            pl.BlockSpec((None, tm, k), lambda b, i, j: (b, i, 0)),
            _weight_spec(w, layer, tn),
            pl.BlockSpec((None, tm, tn), lambda b, i, j: (b, i, j)),
            pl.BlockSpec((None, r, tn), gate_map),
        ],
        out_specs=pl.BlockSpec((None, tm, tn), lambda b, i, j: (b, i, j)),
        out_shape=jax.ShapeDtypeStruct((bsz, t, n), F32),
        compiler_params=_params("arbitrary", "arbitrary", "arbitrary"),
        name="matmul_residual",
    )(a, w, res, gate)


def _rmsnorm_kernel(x_ref, g_ref, *refs, residual):
    x = x_ref[...]
    if residual:
        y_ref, gy_ref, o_ref = refs
        x = x + gy_ref[...] * y_ref[...]
    else:
        o_ref, = refs
    ms = jnp.mean(x * x, axis=-1, keepdims=True)
    o_ref[...] = x * lax.rsqrt(ms + EPS) * g_ref[...]


def _rmsnorm(x, g, *, tm, residual=None):
    bsz, t, d = x.shape
    per_b = t // tm
    extra_in, extra_specs = [], []
    if residual is not None:
        extra_in = list(residual)
        extra_specs = [pl.BlockSpec((tm, d), lambda b, i: (b * per_b + i, 0)),
                       pl.BlockSpec((None, 1, d), lambda b, i: (b, 0, 0))]
    return pl.pallas_call(
        functools.partial(_rmsnorm_kernel, residual=residual is not None),
        grid=(bsz, t // tm),
        in_specs=[pl.BlockSpec((None, tm, d), lambda b, i: (b, i, 0)),
                  pl.BlockSpec((1, d), lambda b, i: (0, 0)), *extra_specs],
        out_specs=pl.BlockSpec((None, tm, d), lambda b, i: (b, i, 0)),
        out_shape=jax.ShapeDtypeStruct((bsz, t, d), F32),
        compiler_params=_params("arbitrary", "arbitrary"),
        name="final_rmsnorm",
    )(x, g.reshape(1, d), *extra_in)


GATE_CHUNK = 640


def _log_sigmoid(x):
    return jnp.minimum(x, 0.0) - jnp.log1p(jnp.exp(-jnp.abs(x)))


def _lru_kernel(gb_ref, xb_ref, cs_ref, h0_ref, cw_ref, cb_ref, wr_ref, br_ref, wi_ref, bi_ref, lam_ref,
                y_ref, hl_ref, nc_ref, ext_ref, a_ref, b_ref, hs_ref, carry_ref, *, tt):
    c_all = xb_ref.shape[-1]

    @pl.when(pl.program_id(1) == 0)
    def _():
        ext_ref[0:SUBLANES, :] = cs_ref[...]
        carry_ref[...] = h0_ref[...]

    xb = xb_ref[...]
    ext_ref[SUBLANES:SUBLANES + tt, :] = xb
    cw = cw_ref[...]
    xc = cb_ref[...] + xb * cw[CONV_W - 1:CONV_W, :]
    for j in range(1, CONV_W):
        xc = xc + ext_ref[pl.ds(SUBLANES - j, tt), :] * cw[CONV_W - 1 - j:CONV_W - j, :]
    tail = ext_ref[tt:tt + SUBLANES, :]
    ext_ref[0:SUBLANES, :] = tail
    nc_ref[...] = tail

    xcb = xc.astype(BF16)
    r_parts, i_parts = [], []
    for c in range(c_all // GATE_CHUNK):
        sl = slice(c * GATE_CHUNK, (c + 1) * GATE_CHUNK)
        r_parts.append(jnp.dot(xcb[:, sl], wr_ref[c], preferred_element_type=F32))
        i_parts.append(jnp.dot(xcb[:, sl], wi_ref[c], preferred_element_type=F32))
    r = jax.nn.sigmoid(jnp.concatenate(r_parts, axis=1) + br_ref[...])
    ig = jax.nn.sigmoid(jnp.concatenate(i_parts, axis=1) + bi_ref[...])
    log_a = LRU_C * r * _log_sigmoid(lam_ref[...])
    th = jnp.tanh(log_a)
    one_minus_a2 = -2.0 * th / (1.0 - th)
    a_ref[...] = jnp.exp(log_a)
    b_ref[...] = jnp.sqrt(jnp.maximum(one_minus_a2, 0.0)) * (ig * xc)

    row = lax.broadcasted_iota(jnp.int32, (SUBLANES, GATE_CHUNK), 0)
    for c in range(c_all // GATE_CHUNK):
        sl = slice(c * GATE_CHUNK, (c + 1) * GATE_CHUNK)

        def body(grp, carry, sl=sl):
            r8 = pl.multiple_of(grp * SUBLANES, SUBLANES)
            av = a_ref[pl.ds(r8, SUBLANES), sl]
            bv = b_ref[pl.ds(r8, SUBLANES), sl]
            for s in (1, 2, 4):
                m = row >= s
                a_sh = jnp.where(m, pltpu.roll(av, s, axis=0), 1.0)
                b_sh = jnp.where(m, pltpu.roll(bv, s, axis=0), 0.0)
                bv = av * b_sh + bv
                av = av * a_sh
            h = av * carry + bv
            hs_ref[pl.ds(r8, SUBLANES), sl] = h
            return h[SUBLANES - 1:SUBLANES, :]

        carry_ref[:, sl] = lax.fori_loop(0, tt // SUBLANES, body, carry_ref[:, sl])

    hl_ref[...] = carry_ref[...]
    y_ref[...] = (jax.nn.gelu(gb_ref[...]) * hs_ref[...]).astype(y_ref.dtype)


def _lru_core(u, conv_state, h0, conv_w, conv_b, w_r, b_r, w_i, b_i, lam, *, tt):
    bsz, t, c2 = u.shape
    c = c2 // 2
    blk = c // N_RNN_BLOCKS
    per = GATE_CHUNK // blk
    nchunk = c // GATE_CHUNK

    def block_diag(w):
        wc = w.reshape(nchunk, per, blk, blk).astype(BF16)
        eye = jnp.eye(per, dtype=BF16)
        return jnp.einsum("cpij,pq->cpiqj", wc, eye).reshape(nchunk, GATE_CHUNK, GATE_CHUNK)

    cs_pad = jnp.pad(conv_state, ((0, 0), (SUBLANES - (CONV_W - 1), 0), (0, 0)))
    row = lambda v: v.reshape(1, c)
    full = lambda shp: pl.BlockSpec(shp, lambda b, i: (0,) * len(shp))
    y, hl, nc = pl.pallas_call(
        functools.partial(_lru_kernel, tt=tt),
        grid=(bsz, t // tt),
        in_specs=[
            pl.BlockSpec((None, tt, c), lambda b, i: (b, i, 0)),
            pl.BlockSpec((None, tt, c), lambda b, i: (b, i, 1)),
            pl.BlockSpec((None, SUBLANES, c), lambda b, i: (b, 0, 0)),
            pl.BlockSpec((None, 1, c), lambda b, i: (b, 0, 0)),
            full((CONV_W, c)), full((1, c)),
            full((nchunk, GATE_CHUNK, GATE_CHUNK)), full((1, c)),
            full((nchunk, GATE_CHUNK, GATE_CHUNK)), full((1, c)),
            full((1, c)),
        ],
        out_specs=[
            pl.BlockSpec((None, tt, c), lambda b, i: (b, i, 0)),
            pl.BlockSpec((None, 1, c), lambda b, i: (b, 0, 0)),
            pl.BlockSpec((None, SUBLANES, c), lambda b, i: (b, 0, 0)),
        ],
        out_shape=[
            jax.ShapeDtypeStruct((bsz, t, c), BF16),
            jax.ShapeDtypeStruct((bsz, 1, c), F32),
            jax.ShapeDtypeStruct((bsz, SUBLANES, c), F32),
        ],
        scratch_shapes=[
            pltpu.VMEM((tt + SUBLANES, c), F32),
            pltpu.VMEM((tt, c), F32),
            pltpu.VMEM((tt, c), F32),
            pltpu.VMEM((tt, c), F32),
            pltpu.VMEM((1, c), F32),
        ],
        compiler_params=_params("arbitrary", "arbitrary"),
        name="lru_core",
    )(u, u, cs_pad, h0.reshape(bsz, 1, c), conv_w, row(conv_b), block_diag(w_r), row(b_r),
      block_diag(w_i), row(b_i), row(lam))
    return y, hl.reshape(bsz, c), nc[:, SUBLANES - (CONV_W - 1):, :]


def _route_gates(lg):
    return _route(lg)[0]


def _route(lg):
    col = lax.broadcasted_iota(jnp.int32, lg.shape, 1)
    big = jnp.int32(LANES)
    is_g = col < N_GROUPS
    gl = jnp.where(is_g, lg, NEG_INF)
    gmax = jnp.max(gl, axis=1, keepdims=True)
    g_sel = jnp.min(jnp.where(gl == gmax, col, big), axis=1, keepdims=True)
    g_w = 1.0 / jnp.sum(jnp.exp(gl - gmax), axis=1, keepdims=True)
    eid = col - N_GROUPS
    in_grp = (eid >= 0) & (eid < N_EXPERTS) & ((eid // EXPERTS_PER_GROUP) == g_sel)
    e_in = jnp.where(in_grp, lg, NEG_INF)
    top1 = jnp.max(e_in, axis=1, keepdims=True)
    idx1 = jnp.min(jnp.where((e_in == top1) & in_grp, col, big), axis=1, keepdims=True)
    rest = in_grp & (col != idx1)
    e2 = jnp.where(rest, lg, NEG_INF)
    top2 = jnp.max(e2, axis=1, keepdims=True)
    idx2 = jnp.min(jnp.where((e2 == top2) & rest, col, big), axis=1, keepdims=True)
    z = jnp.exp(top2 - top1)
    w_first = g_w / (1.0 + z)
    w_second = g_w * z / (1.0 + z)
    return jnp.where(col == idx1, w_first, 0.0) + jnp.where(col == idx2, w_second, 0.0), g_sel


def _moe_dense_kernel(h_ref, lg_ref, w1_ref, w3_ref, w2_ref, x_ref, g_ref, o_ref, gates_ref, acc_ref):
    e = pl.program_id(2)

    @pl.when(e == 0)
    def _():
        gates_ref[...] = _route_gates(lg_ref[...])
        acc_ref[...] = jnp.zeros_like(acc_ref)

    gates = gates_ref[...]
    col = lax.broadcasted_iota(jnp.int32, gates.shape, 1)
    ge = jnp.sum(jnp.where(col == e + N_GROUPS, gates, 0.0), axis=1, keepdims=True)
    h = h_ref[...]
    a1 = jnp.dot(h, w1_ref[...], preferred_element_type=F32)
    a3 = jnp.dot(h, w3_ref[...], preferred_element_type=F32)
    hidden = (a1 * jax.nn.sigmoid(a1)) * a3 * ge
    acc_ref[...] += jnp.dot(hidden.astype(BF16), w2_ref[...], preferred_element_type=F32)

    @pl.when(e == pl.num_programs(2) - 1)
    def _():
        o_ref[...] = x_ref[...] + g_ref[...] * acc_ref[...]


def _moe_dense(h, logits, w1, w3, w2, layer, x, gate, *, tm):
    bsz, t, d = x.shape
    _, ne, _, f = w1.shape
    r = gate.shape[1]
    gate_map = (lambda b, i, e: (b, 0, 0)) if r == 1 else (lambda b, i, e: (b, i, 0))
    tok = lambda b, i, e: (b, i, 0)
    expert = lambda b, i, e: (layer, e, 0, 0)
    return pl.pallas_call(
        _moe_dense_kernel,
        grid=(bsz, t // tm, ne),
        in_specs=[
            pl.BlockSpec((None, tm, d), tok),
            pl.BlockSpec((None, tm, LANES), tok),
            pl.BlockSpec((None, None, d, f), expert),
            pl.BlockSpec((None, None, d, f), expert),
            pl.BlockSpec((None, None, f, d), expert),
            pl.BlockSpec((None, tm, d), tok),
            pl.BlockSpec((None, r, d), gate_map),
        ],
        out_specs=pl.BlockSpec((None, tm, d), tok),
        out_shape=jax.ShapeDtypeStruct((bsz, t, d), F32),
        scratch_shapes=[pltpu.VMEM((tm, LANES), F32), pltpu.VMEM((tm, d), F32)],
        compiler_params=_params("arbitrary", "arbitrary", "arbitrary"),
        name="moe_dense",
    )(h, logits, w1, w3, w2, x, gate)


INFO_GROUP_LANE = 0
INFO_RANK_LANE = 1


def _route_kernel(lg_ref, info_ref, cnt_ref, carry_ref):
    @pl.when(pl.program_id(0) == 0)
    def _():
        carry_ref[...] = jnp.zeros_like(carry_ref)

    gates, g_sel = _route(lg_ref[...])
    tm = gates.shape[0]
    col = lax.broadcasted_iota(jnp.int32, gates.shape, 1)
    onehot = (col == g_sel).astype(F32)
    before = (lax.broadcasted_iota(jnp.int32, (tm, tm), 0) > lax.broadcasted_iota(jnp.int32, (tm, tm), 1))
    earlier = jnp.dot(before.astype(BF16), onehot.astype(BF16), preferred_element_type=F32) + carry_ref[...]
    rank = jnp.sum(jnp.where(col == g_sel, earlier, 0.0), axis=1, keepdims=True)
    carry_ref[...] += jnp.sum(onehot, axis=0, keepdims=True)
    info_ref[...] = (gates + jnp.where(col == INFO_GROUP_LANE, g_sel.astype(F32), 0.0)
                     + jnp.where(col == INFO_RANK_LANE, rank, 0.0))
    cnt_ref[...] = carry_ref[...]


def _route_tokens(logits, *, tm):
    n = logits.shape[0]
    assert n % tm == 0
    return pl.pallas_call(
        _route_kernel,
        grid=(n // tm,),
        in_specs=[pl.BlockSpec((tm, LANES), lambda i: (i, 0))],
        out_specs=[pl.BlockSpec((tm, LANES), lambda i: (i, 0)), pl.BlockSpec((1, LANES), lambda i: (0, 0))],
        out_shape=[jax.ShapeDtypeStruct((n, LANES), F32), jax.ShapeDtypeStruct((1, LANES), F32)],
        scratch_shapes=[pltpu.VMEM((1, LANES), F32)],
        compiler_params=_params("arbitrary"),
        name="moe_route",
    )(logits)


def _group_layout(info, counts, tm):
    n = info.shape[0]
    g_sel = info[:, INFO_GROUP_LANE].astype(jnp.int32)
    rank = info[:, INFO_RANK_LANE].astype(jnp.int32)
    cnt = counts[0, :N_GROUPS].astype(jnp.int32)
    padded = -(-cnt // tm) * tm
    ends = jnp.cumsum(padded)
    starts = ends - padded
    n_slots = n + N_GROUPS * tm
    src = _slot_tokens(starts[g_sel] + rank, n_slots)
    tile_start = jnp.arange(n_slots // tm, dtype=jnp.int32) * tm
    tile_group = jnp.minimum(jnp.sum((tile_start[:, None] >= ends[None, :]).astype(jnp.int32), axis=1), N_GROUPS - 1)
    slot_group = jnp.repeat(tile_group, tm)
    is_pad = jnp.arange(n_slots, dtype=jnp.int32) - starts[slot_group] >= cnt[slot_group]
    pad_row = n + tm + jnp.cumsum(is_pad.astype(jnp.int32)) - 1
    dst = jnp.concatenate([n + jnp.arange(tm, dtype=jnp.int32), jnp.where(is_pad, pad_row, src)])
    return src, dst, tile_group, (ends[-1] // tm).reshape(1)


SLOT_UNROLL = 8


def _slot_tokens_kernel(slot_ref, src_ref):
    n, n_slots = slot_ref.shape[0], src_ref.shape[0]

    def clear(c, carry):
        for k in range(SLOT_UNROLL):
            src_ref[c * SLOT_UNROLL + k] = 0
        return carry

    def place(c, carry):
        for k in range(SLOT_UNROLL):
            t = c * SLOT_UNROLL + k
            src_ref[slot_ref[t]] = t
        return carry

    lax.fori_loop(0, n_slots // SLOT_UNROLL, clear, 0)
    lax.fori_loop(0, n // SLOT_UNROLL, place, 0)


def _slot_tokens(slot, n_slots):
    assert slot.shape[0] % SLOT_UNROLL == 0 and n_slots % SLOT_UNROLL == 0
    return pl.pallas_call(
        _slot_tokens_kernel,
        in_specs=[pl.BlockSpec(memory_space=pltpu.SMEM)],
        out_specs=pl.BlockSpec(memory_space=pltpu.SMEM),
        out_shape=jax.ShapeDtypeStruct((n_slots,), jnp.int32),
        name="moe_slot_tokens",
    )(slot)


def _moe_group_kernel(src_ref, dst_ref, tg_ref, used_ref, h_hbm, info_hbm, w1_ref, w3_ref, w2_ref, zeros_hbm,
                      out_hbm, xbuf, gbuf, xb, gates, acc, obuf, sem_x, sem_g, sem_o, *, tm):
    del zeros_hbm
    i = pl.program_id(0)
    e = pl.program_id(1)
    used = used_ref[0]
    live = i < used
    chunk = tm // EXPERTS_PER_GROUP
    first = e * chunk
    last_step = e == EXPERTS_PER_GROUP - 1

    def gather_row(tile, r):
        tok = src_ref[tile * tm + r]
        pltpu.make_async_copy(h_hbm.at[pl.ds(tok, 1), :], xbuf.at[pl.ds(r, 1), :], sem_x).start()
        pltpu.make_async_copy(info_hbm.at[pl.ds(tok, 1), :], gbuf.at[pl.ds(r, 1), :], sem_g).start()

    def scatter_chunk(tile):
        slot = (tile + 2) % 2
        for k in range(chunk):
            row = dst_ref[(tile + 1) * tm + first + k]
            pltpu.make_async_copy(obuf.at[slot, pl.ds(first + k, 1), :], out_hbm.at[pl.ds(row, 1), :],
                                  sem_o.at[slot]).start()

    def wait_scatter(slot):
        pltpu.make_async_copy(obuf.at[slot], out_hbm.at[pl.ds(0, tm), :], sem_o.at[slot]).wait()

    @pl.when((i == 0) & (e == 0))
    def _():
        obuf[...] = jnp.zeros_like(obuf)

        def body(r, carry):
            gather_row(0, r)
            return carry
        lax.fori_loop(0, tm, body, 0)

    @pl.when((e == 0) & (i <= used))
    def _():
        pltpu.make_async_copy(h_hbm.at[pl.ds(0, tm), :], xbuf, sem_x).wait()
        pltpu.make_async_copy(info_hbm.at[pl.ds(0, tm), :], gbuf, sem_g).wait()

    @pl.when((e == 0) & live)
    def _():
        xb[...] = xbuf[...].astype(BF16)
        gates[...] = gbuf[...]
        acc[...] = jnp.zeros_like(acc)

    @pl.when(live)
    def _():
        lane = lax.broadcasted_iota(jnp.int32, (tm, LANES), 1)
        gate_lane = N_GROUPS + EXPERTS_PER_GROUP * tg_ref[i] + e
        ge = jnp.sum(jnp.where(lane == gate_lane, gates[...], 0.0), axis=1, keepdims=True)
        x = xb[...]
        a1 = jnp.dot(x, w1_ref[...], preferred_element_type=F32)
        a3 = jnp.dot(x, w3_ref[...], preferred_element_type=F32)
        hidden = (a1 * jax.nn.sigmoid(a1)) * a3 * ge
        acc[...] += jnp.dot(hidden.astype(BF16), w2_ref[...], preferred_element_type=F32)
        for k in range(chunk):
            gather_row(i + 1, first + k)
        scatter_chunk(i - 1)

    @pl.when(i == used)
    def _():
        scatter_chunk(i - 1)

    @pl.when(last_step & live & (i >= 1))
    def _():
        wait_scatter(i % 2)

    @pl.when(last_step & live)
    def _():
        obuf[i % 2] = acc[...]

    @pl.when(last_step & (i == used))
    def _():
        wait_scatter(0)
        wait_scatter(1)


def _moe_grouped(h, info, src, dst, tile_group, n_used, w1, w3, w2, layer, *, tm):
    n, d = h.shape
    f = w1.shape[3]
    n_tiles = src.shape[0] // tm
    n_rows = n + (N_GROUPS + 1) * tm
    assert tm % EXPERTS_PER_GROUP == 0 and dst.shape[0] == (n_tiles + 1) * tm

    def expert(i, e, src_, dst_, tg, used):
        e_eff = jnp.where(i < used[0], e, EXPERTS_PER_GROUP - 1)
        return (layer, tg[i] * EXPERTS_PER_GROUP + e_eff, 0, 0)

    return pl.pallas_call(
        functools.partial(_moe_group_kernel, tm=tm),
        grid_spec=pltpu.PrefetchScalarGridSpec(
            num_scalar_prefetch=4,
            grid=(n_tiles, EXPERTS_PER_GROUP),
            in_specs=[
                pl.BlockSpec(memory_space=pl.ANY),
                pl.BlockSpec(memory_space=pl.ANY),
                pl.BlockSpec((None, None, d, f), expert),
                pl.BlockSpec((None, None, d, f), expert),
                pl.BlockSpec((None, None, f, d), expert),
                pl.BlockSpec(memory_space=pl.ANY),
            ],
            out_specs=pl.BlockSpec(memory_space=pl.ANY),
            scratch_shapes=[
                pltpu.VMEM((tm, d), F32), pltpu.VMEM((tm, LANES), F32), pltpu.VMEM((tm, d), BF16),
                pltpu.VMEM((tm, LANES), F32), pltpu.VMEM((tm, d), F32), pltpu.VMEM((2, tm, d), F32),
                pltpu.SemaphoreType.DMA(()), pltpu.SemaphoreType.DMA(()), pltpu.SemaphoreType.DMA((2,)),
            ],
        ),
        out_shape=jax.ShapeDtypeStruct((n_rows, d), F32),
        input_output_aliases={9: 0},
        compiler_params=_params("arbitrary", "arbitrary"),
        name="moe_grouped",
    )(src, dst, tile_group, n_used, h, info, w1, w3, w2, jnp.zeros((n_rows, d), F32))


def _gated_add_kernel(x_ref, y_ref, g_ref, o_ref):
    o_ref[...] = x_ref[...] + g_ref[...] * y_ref[...]


def _gated_add(x, y, gate, *, tm):
    bsz, t, d = x.shape
    per_b = t // tm
    return pl.pallas_call(
        _gated_add_kernel,
        grid=(bsz, per_b),
        in_specs=[
            pl.BlockSpec((None, tm, d), lambda b, i: (b, i, 0)),
            pl.BlockSpec((tm, d), lambda b, i: (b * per_b + i, 0)),
            pl.BlockSpec((None, 1, d), lambda b, i: (b, 0, 0)),
        ],
        out_specs=pl.BlockSpec((None, tm, d), lambda b, i: (b, i, 0)),
        out_shape=jax.ShapeDtypeStruct((bsz, t, d), F32),
        compiler_params=_params("arbitrary", "arbitrary"),
        name="gated_add",
    )(x, y, gate)


MASKED = -1e30


def _t5_bucket_table(tq):
    qi = np.arange(tq, dtype=np.int32)[:, None]
    col = np.arange(2 * tq, dtype=np.int32)[None, :]
    n = np.maximum(qi + tq - col, 0)
    max_exact = N_BUCKETS // 2
    nf = np.maximum(n, 1).astype(np.float32)
    large = max_exact + (np.log(nf / np.float32(max_exact)) / np.float32(math.log(MAX_DISTANCE / max_exact))
                         * np.float32(N_BUCKETS - max_exact)).astype(np.int32)
    table = np.where(n < max_exact, n, np.minimum(large, N_BUCKETS - 1)).astype(np.int32)
    assert table[0, 0] == N_BUCKETS - 1
    return table


def _bias_kernel(rb_ref, bk_ref, o_ref):
    h = pl.program_id(0)
    bk = bk_ref[...]
    acc = jnp.zeros(bk.shape, F32)
    for k in range(N_BUCKETS):
        acc = jnp.where(bk == k, rb_ref[k, h], acc)
    o_ref[...] = (acc - rb_ref[N_BUCKETS - 1, h]) * LOG2E


def _near_bias(rel_bias, tq):
    return pl.pallas_call(
        _bias_kernel,
        grid=(N_HEADS,),
        in_specs=[pl.BlockSpec(memory_space=pltpu.SMEM),
                  pl.BlockSpec((tq, 2 * tq), lambda h: (0, 0))],
        out_specs=pl.BlockSpec((None, tq, 2 * tq), lambda h: (h, 0, 0)),
        out_shape=jax.ShapeDtypeStruct((N_HEADS, tq, 2 * tq), F32),
        name="near_bias",
    )(rel_bias, jnp.asarray(_t5_bucket_table(tq)))


_NEG_INF_KEY = int(np.int32(np.array(-np.inf, np.float32).view(np.int32)) ^ np.int32(0x7FFFFFFF))


def _sort_key(x):
    k = pltpu.bitcast(x, jnp.int32)
    return k ^ ((k >> 31) & jnp.int32(0x7FFFFFFF))


def _kth_largest_key(count_ge, kth):
    zero = jnp.int32(0)
    first = count_ge(zero)
    ans = jnp.where(first >= kth, zero, jnp.int32(-2 ** 31))
    settled = (first == kth).astype(jnp.int32)

    def cond(state):
        it, _, settled = state
        return (it < 31) & (jnp.min(settled) == 0)

    def body(state):
        it, ans, settled = state
        cand = ans | (jnp.int32(1) << (30 - it))
        count = count_ge(cand)
        return it + 1, jnp.where(count >= kth, cand, ans), settled | (count == kth).astype(jnp.int32)

    return lax.while_loop(cond, body, (jnp.int32(0), ans, settled))[1]


def _dsa_select_kernel(qi_ref, kw_ref, kall_ref, m_ref, key_ref, *, tq, tk, topk):
    i = pl.program_id(1)
    n_kt = (i * tq + tq + tk - 1) // tk
    qpos = i * tq + lax.broadcasted_iota(jnp.int32, (tq, tk), 0)
    wi = kw_ref[:, IDX_DIM:IDX_DIM + N_IDX_HEADS]
    qi = qi_ref[...].astype(BF16)
    key_ref[...] = jnp.full(key_ref.shape, _NEG_INF_KEY, jnp.int32)

    def score_tile(kt, carry):
        k0 = pl.multiple_of(kt * tk, tk)
        kid = kall_ref[pl.ds(k0, tk), 0:IDX_DIM].astype(BF16)
        acc = jnp.zeros((tq, tk), F32)
        for h in range(N_IDX_HEADS):
            s = _bdot_nt(qi[:, h * IDX_DIM:(h + 1) * IDX_DIM], kid)
            acc = acc + wi[:, h:h + 1] * jnp.maximum(s, 0.0)
        kpos = k0 + lax.broadcasted_iota(jnp.int32, (tq, tk), 1)
        sc = jnp.where(kpos <= qpos, acc * IDX_SCALE, NEG_INF)
        key_ref[:, pl.ds(k0, tk)] = _sort_key(sc)
        return carry

    lax.fori_loop(0, n_kt, score_tile, 0)

    def count_ge(cand):
        def body(kt, acc):
            k0 = pl.multiple_of(kt * tk, tk)
            ge = (key_ref[:, pl.ds(k0, tk)] >= cand).astype(jnp.int32)
            for j in range(tk // LANES):
                acc = acc + ge[:, j * LANES:(j + 1) * LANES]
            return acc
        acc = lax.fori_loop(0, n_kt, body, jnp.zeros((tq, LANES), jnp.int32))
        return jnp.sum(acc, axis=1, keepdims=True)

    thr = _kth_largest_key(count_ge, topk)
    keys = key_ref[...]
    m_ref[...] = ((keys >= thr) & (keys > _NEG_INF_KEY)).astype(m_ref.dtype)


def _dsa_select(proj, *, tq, tk, topk):
    bsz, t, _ = proj.shape
    qi_w = N_IDX_HEADS * IDX_DIM
    qi_blk = (N_HEADS * HEAD_DIM + 2 * N_KV_B * HEAD_DIM) // qi_w
    kw_blk = (N_HEADS * HEAD_DIM + 2 * N_KV_B * HEAD_DIM + qi_w) // LANES
    return pl.pallas_call(
        functools.partial(_dsa_select_kernel, tq=tq, tk=tk, topk=topk),
        grid=(bsz, t // tq),
        in_specs=[
            pl.BlockSpec((None, tq, qi_w), lambda b, i: (b, i, qi_blk)),
            pl.BlockSpec((None, tq, LANES), lambda b, i: (b, i, kw_blk)),
            pl.BlockSpec((None, t, LANES), lambda b, i: (b, 0, kw_blk)),
        ],
        out_specs=pl.BlockSpec((None, tq, t), lambda b, i: (b, i, 0)),
        out_shape=jax.ShapeDtypeStruct((bsz, t, t), BF16),
        scratch_shapes=[pltpu.VMEM((tq, t), jnp.int32)],
        compiler_params=_params("arbitrary", "arbitrary"),
        name="dsa_select",
    )(proj, proj, proj)


FAR_WIDTH = 2


def _attn_kernel(q_ref, k_ref, v_ref, m_ref, tb_ref, o_ref, *state, tq, rep, shared_kv, block_mask):
    m_sc, l_sc, acc_sc = state[:rep], state[rep:2 * rep], state[2 * rep:]
    i = pl.program_id(2)
    for r in range(rep):
        m_sc[r][...] = jnp.full((tq, LANES), MASKED, F32)
        l_sc[r][...] = jnp.zeros((tq, LANES), F32)
        acc_sc[r][...] = jnp.zeros((tq, HEAD_DIM), F32)
    def row_reduce(x, op, reduce):
        part = x[:, :LANES]
        for c in range(1, x.shape[1] // LANES):
            part = op(part, x[:, c * LANES:(c + 1) * LANES])
        return reduce(part, axis=1, keepdims=True)

    def key_mask(j, r, causal, width):
        if block_mask:
            lane = lax.broadcasted_iota(jnp.int32, (tq, LANES), 1)
            flags = m_ref[r].astype(F32)
            vis = [jnp.sum(jnp.where(lane == j + w, flags, 0.0), axis=1, keepdims=True) for w in range(width)]
            ok = jnp.concatenate([jnp.broadcast_to(v > 0.0, (tq, tq)) for v in vis], axis=1)
        else:
            ok = m_ref[:, pl.ds(pl.multiple_of(j * tq, tq), width * tq)] > 0
        if causal:
            ok = ok & (lax.broadcasted_iota(jnp.int32, (tq, tq), 0) >= lax.broadcasted_iota(jnp.int32, (tq, tq), 1))
        return ok

    def step(j, bias_lo, causal, width=1):
        keys = width * tq
        k0 = pl.multiple_of(j * tq, tq)
        kv = lambda r: slice(0, HEAD_DIM) if shared_kv else slice(r * HEAD_DIM, (r + 1) * HEAD_DIM)
        ok_shared = None if block_mask else key_mask(j, 0, causal, width)
        scores = []
        for r in range(rep):
            q = (q_ref[:, r * HEAD_DIM:(r + 1) * HEAD_DIM] * QK_SCALE_LOG2).astype(BF16)
            s = _bdot_nt(q, k_ref[pl.ds(k0, keys), kv(r)])
            if bias_lo is not None:
                s = s + tb_ref[r, :, bias_lo:bias_lo + tq]
            scores.append(jnp.where(key_mask(j, r, causal, width) if block_mask else ok_shared, s, MASKED))
        probs, alphas = [], []
        for r in range(rep):
            m_old = m_sc[r][...]
            m_new = jnp.maximum(m_old, row_reduce(scores[r], jnp.maximum, jnp.max))
            alpha = jnp.exp2(m_old - m_new)
            p = jnp.exp2(scores[r] - jnp.concatenate([m_new] * (keys // LANES), axis=1))
            l_sc[r][...] = alpha * l_sc[r][...] + row_reduce(p, jnp.add, jnp.sum)
            m_sc[r][...] = m_new
            probs.append(p.astype(BF16))
            alphas.append(alpha)
        for r in range(rep):
            pv = jnp.dot(probs[r], v_ref[pl.ds(k0, keys), kv(r)].astype(BF16), preferred_element_type=F32)
            acc_sc[r][...] = alphas[r] * acc_sc[r][...] + pv

    n_far = jnp.maximum(i - 1, 0)

    def far_pair(c, carry):
        step(FAR_WIDTH * c, None, False, width=FAR_WIDTH)
        return carry

    lax.fori_loop(0, n_far // FAR_WIDTH, far_pair, 0)

    def far_rest(c, carry):
        step((n_far // FAR_WIDTH) * FAR_WIDTH + c, None, False)
        return carry

    lax.fori_loop(0, n_far % FAR_WIDTH, far_rest, 0)

    @pl.when(i >= 1)
    def _():
        step(i - 1, 0, False)

    step(i, tq, True)
    o_ref[...] = jnp.concatenate([acc_sc[r][...] / l_sc[r][...] for r in range(rep)], axis=1).astype(o_ref.dtype)


def _attention(proj, k_col, v_col, mask, near_bias, *, tq, rep, shared_kv, block_mask):
    bsz, t, _ = proj.shape
    g = N_HEADS // rep
    kv_w = HEAD_DIM if shared_kv else rep * HEAD_DIM
    assert HEAD_DIM == LANES and k_col % kv_w == 0 and v_col % kv_w == 0
    k_blk, v_blk = k_col // kv_w, v_col // kv_w
    if block_mask:
        assert tq == MOBA_BLOCK
        mask_spec = pl.BlockSpec((None, rep, tq, LANES), lambda b, h, i: (b, h, i, 0))
    else:
        mask_spec = pl.BlockSpec((None, tq, t), lambda b, h, i: (b, i, 0))
    stat = pltpu.VMEM((tq, LANES), F32)
    return pl.pallas_call(
        functools.partial(_attn_kernel, tq=tq, rep=rep, shared_kv=shared_kv, block_mask=block_mask),
        grid=(bsz, g, t // tq),
        in_specs=[
            pl.BlockSpec((None, tq, rep * HEAD_DIM), lambda b, h, i: (b, i, h)),
            pl.BlockSpec((None, t, kv_w), lambda b, h, i: (b, 0, k_blk + h)),
            pl.BlockSpec((None, t, kv_w), lambda b, h, i: (b, 0, v_blk + h)),
            mask_spec,
            pl.BlockSpec((rep, tq, 2 * tq), lambda b, h, i: (h, 0, 0)),
        ],
        out_specs=pl.BlockSpec((None, tq, rep * HEAD_DIM), lambda b, h, i: (b, i, h)),
        out_shape=jax.ShapeDtypeStruct((bsz, t, N_HEADS * HEAD_DIM), BF16),
        scratch_shapes=[stat] * (3 * rep),
        compiler_params=_params("arbitrary", "arbitrary", "arbitrary"),
        name="masked_attention",
    )(proj, proj, proj, mask, near_bias)


PAGES_PER_BLOCK = MOBA_BLOCK // PAGE_SIZE
MEANS_BLOCKS_PER_STEP = 4


def _block_means_kernel(tbl_ref, *refs):
    del tbl_ref
    page_refs, o_ref = refs[:-1], refs[-1]
    for blk in range(len(page_refs) // PAGES_PER_BLOCK):
        pages = page_refs[blk * PAGES_PER_BLOCK:(blk + 1) * PAGES_PER_BLOCK]
        total = jnp.sum(pages[0][...], axis=0)
        for ref in pages[1:]:
            total = total + jnp.sum(ref[...], axis=0)
        o_ref[blk] = total / MOBA_BLOCK


def _block_means(pool, layer, table):
    _, _, _, h, dh = pool.shape
    bsz, npg = table.shape
    nb = npg // PAGES_PER_BLOCK
    blocks = math.gcd(nb, MEANS_BLOCKS_PER_STEP)
    per_step = blocks * PAGES_PER_BLOCK
    assert npg % PAGES_PER_BLOCK == 0
    page = lambda k: pl.BlockSpec((None, None, PAGE_SIZE, h, dh),
                                  lambda b, n, tbl: (layer, tbl[b, per_step * n + k], 0, 0, 0))
    out = pl.pallas_call(
        _block_means_kernel,
        grid_spec=pltpu.PrefetchScalarGridSpec(
            num_scalar_prefetch=1,
            grid=(bsz, nb // blocks),
            in_specs=[page(k) for k in range(per_step)],
            out_specs=pl.BlockSpec((None, blocks, h, dh), lambda b, n, tbl: (b, n, 0, 0)),
        ),
        out_shape=jax.ShapeDtypeStruct((bsz, nb, h, dh), F32),
        compiler_params=_params("arbitrary", "arbitrary"),
        name="block_means",
    )(table, *([pool] * per_step))
    return out.reshape(bsz, nb, h * dh)


def _moba_select_kernel(q_ref, mean_ref, o_ref, *, tq, past_len):
    i = pl.program_id(1)
    own = (past_len + i * tq + lax.broadcasted_iota(jnp.int32, (tq, LANES), 0)) // MOBA_BLOCK
    blk = lax.broadcasted_iota(jnp.int32, (tq, LANES), 1)
    past = blk < own
    for h in range(N_HEADS):
        sl = slice(h * HEAD_DIM, (h + 1) * HEAD_DIM)
        gate = lax.dot_general(q_ref[:, sl], mean_ref[:, sl], (((1,), (1,)), ((), ())),
                               precision=lax.Precision.HIGHEST, preferred_element_type=F32)
        g = jnp.where(past, gate, NEG_INF)
        kth = jnp.max(g, axis=1, keepdims=True)
        for _ in range(MOBA_TOPK - 1):
            kth = jnp.max(jnp.where(g < kth, g, NEG_INF), axis=1, keepdims=True)
        o_ref[h] = ((past & (g >= kth)) | (blk == own)).astype(o_ref.dtype)


def _moba_select(q_src, means, *, tq, past_len):
    bsz, t, _ = q_src.shape
    w = N_HEADS * HEAD_DIM
    return pl.pallas_call(
        functools.partial(_moba_select_kernel, tq=tq, past_len=past_len),
        grid=(bsz, t // tq),
        in_specs=[
            pl.BlockSpec((None, tq, w), lambda b, i: (b, i, 0)),
            pl.BlockSpec((None, LANES, w), lambda b, i: (b, 0, 0)),
        ],
        out_specs=pl.BlockSpec((None, N_HEADS, tq, LANES), lambda b, i: (b, 0, i, 0)),
        out_shape=jax.ShapeDtypeStruct((bsz, N_HEADS, t, LANES), BF16),
        compiler_params=_params("arbitrary", "arbitrary"),
        name="moba_select",
    )(q_src, means)


def _dsa_select_paged_kernel(tbl_ref, qi_ref, wi_ref, *rest, pp, topk, past_len, t_new):
    del tbl_ref
    page_refs = rest[:pp]
    knew_ref, m_ref, key_ref = rest[pp:]
    p = pl.program_id(1)
    qi = qi_ref[...].astype(BF16)
    wi = wi_ref[...]

    def score(kpage_t):
        s = _bdot(qi, kpage_t)
        w = wi * jnp.maximum(s, 0.0)
        return jnp.sum(w.reshape(N_IDX_HEADS, t_new, PAGE_SIZE), axis=0) * IDX_SCALE

    for k in range(pp):
        off = pl.multiple_of((p * pp + k) * PAGE_SIZE, PAGE_SIZE)
        key_ref[:, pl.ds(off, PAGE_SIZE)] = _sort_key(score(page_refs[k][...]))

    @pl.when(p == pl.num_programs(1) - 1)
    def _():
        row = lax.broadcasted_iota(jnp.int32, (t_new, PAGE_SIZE), 0)
        col = lax.broadcasted_iota(jnp.int32, (t_new, PAGE_SIZE), 1)
        sc = jnp.where(col <= row, score(knew_ref[...]), NEG_INF)
        key_ref[:, past_len:past_len + PAGE_SIZE] = _sort_key(sc)

        def count_ge(cand):
            return jnp.sum((key_ref[...] >= cand).astype(jnp.int32), axis=1, keepdims=True)

        thr = _kth_largest_key(count_ge, topk)
        keys = key_ref[...]
        m_ref[...] = ((keys >= thr) & (keys > _NEG_INF_KEY)).astype(m_ref.dtype)


def _dsa_select_paged(qi_rows, wi_rows, kidx_pool, layer, page_table, kidx_new, *, pp, topk):
    bsz, rows, _ = qi_rows.shape
    t_new = rows // N_IDX_HEADS
    n_pages = page_table.shape[1]
    assert n_pages % pp == 0
    past_len = n_pages * PAGE_SIZE
    lpad = past_len + PAGE_SIZE
    kidx_pool = jnp.swapaxes(kidx_pool, 2, 3)
    kidx_new = jnp.swapaxes(kidx_new, 1, 2)
    page_spec = lambda k: pl.BlockSpec((None, None, IDX_DIM, PAGE_SIZE),
                                       lambda b, p, tbl: (layer, tbl[b, p * pp + k], 0, 0))
    return pl.pallas_call(
        functools.partial(_dsa_select_paged_kernel, pp=pp, topk=topk, past_len=past_len, t_new=t_new),
        grid_spec=pltpu.PrefetchScalarGridSpec(
            num_scalar_prefetch=1,
            grid=(bsz, n_pages // pp),
            in_specs=[
                pl.BlockSpec((None, rows, IDX_DIM), lambda b, p, tbl: (b, 0, 0)),
                pl.BlockSpec((None, rows, 1), lambda b, p, tbl: (b, 0, 0)),
                *[page_spec(k) for k in range(pp)],
                pl.BlockSpec((None, IDX_DIM, PAGE_SIZE), lambda b, p, tbl: (b, 0, 0)),
            ],
            out_specs=pl.BlockSpec((None, t_new, lpad), lambda b, p, tbl: (b, 0, 0)),
            scratch_shapes=[pltpu.VMEM((t_new, lpad), jnp.int32)],
        ),
        out_shape=jax.ShapeDtypeStruct((bsz, t_new, lpad), BF16),
        compiler_params=_params("arbitrary", "arbitrary"),
        name="dsa_select_paged",
    )(page_table, qi_rows, wi_rows, *([kidx_pool] * pp), kidx_new)


def _col_vector(row_vec):
    n = row_vec.shape[1]
    eye = lax.broadcasted_iota(jnp.int32, (n, n), 0) == lax.broadcasted_iota(jnp.int32, (n, n), 1)
    return jnp.sum(jnp.where(eye, row_vec, 0.0), axis=1, keepdims=True)


def _paged_attn_kernel(tbl_ref, qbd_ref, *rest, pp, key_mask):
    del tbl_ref
    k_refs, v_refs = rest[:pp], rest[pp:2 * pp]
    mt_ref, mnew_ref, knew_ref, vnew_ref, bias_ref, o_ref, m_sc, l_sc, acc_sc = rest[2 * pp:]
    p = pl.program_id(1)
    last = pl.num_programs(1) - 1

    def key_rows(flags):
        t_new = flags.shape[0]
        pick = (lax.broadcasted_iota(jnp.int32, (t_new, LANES), 1) % t_new
                == lax.broadcasted_iota(jnp.int32, (t_new, LANES), 0)).astype(BF16)
        neg = ((flags.astype(F32) - 1.0) * -MASKED).astype(BF16)
        return lax.dot_general(neg, pick, (((0,), (0,)), ((), ())), preferred_element_type=F32)

    if key_mask:
        past_masks = key_rows(mt_ref[...])
        page_mask = lambda k: past_masks[k * PAGE_SIZE:(k + 1) * PAGE_SIZE, :]
        new_mask = lambda: key_rows(mnew_ref[...])
    else:
        page_mask = lambda k: mt_ref[pl.ds((p * pp + k) // PAGES_PER_BLOCK, 1), :]
        new_mask = lambda: mnew_ref[...]

    @pl.when(p == 0)
    def _():
        m_sc[...] = jnp.full(m_sc.shape, MASKED, F32)
        l_sc[...] = jnp.zeros_like(l_sc)
        acc_sc[...] = jnp.zeros_like(acc_sc)

    n_kv = qbd_ref.shape[0] // HEAD_DIM

    def heads_on_lanes(ref):
        return jnp.concatenate([ref[pl.ds(g, PAGE_SIZE, stride=n_kv), :] for g in range(n_kv)], axis=1)

    def pages(kps, vps, masks, biases):
        scores = []
        for kp, mask, bias in zip(kps, masks, biases):
            s = jnp.dot(kp().astype(BF16), qbd_ref[...], preferred_element_type=F32)
            if bias is not None:
                s = s + bias
            scores.append(s + mask)
        m_old = m_sc[...]
        m_new = m_old
        for s in scores:
            m_new = jnp.maximum(m_new, jnp.max(s, axis=0, keepdims=True))
        alpha = jnp.exp2(m_old - m_new)
        l_new = alpha * l_sc[...]
        pv = None
        for s, vp in zip(scores, vps):
            pt = jnp.exp2(s - m_new)
            l_new = l_new + jnp.sum(pt, axis=0, keepdims=True)
            part = lax.dot_general(pt.astype(BF16), vp().astype(BF16), (((0,), (0,)), ((), ())),
                                   preferred_element_type=F32)
            pv = part if pv is None else pv + part
        l_sc[...] = l_new
        acc_sc[...] = acc_sc[...] * _col_vector(alpha) + pv
        m_sc[...] = m_new

    is_last = (p == last).astype(F32)
    pages([functools.partial(heads_on_lanes, r) for r in k_refs],
          [functools.partial(heads_on_lanes, r) for r in v_refs],
          [page_mask(k) for k in range(pp)],
          [None] * (pp - 1) + [bias_ref[0] * is_last])

    @pl.when(p == last)
    def _():
        pages([lambda: knew_ref[...]], [lambda: vnew_ref[...]], [new_mask()], [bias_ref[1]])
        o_ref[...] = acc_sc[...] / _col_vector(l_sc[...])


def _paged_attention(qbd, k_pool, v_pool, layer, page_table, mask, new_mask, k_new, v_new, bias_t, *, pp, key_mask):
    bsz, w, _ = qbd.shape
    n_pages = page_table.shape[1]
    assert n_pages % pp == 0
    if key_mask:
        t_new = mask.shape[1]
        mask_specs = [pl.BlockSpec((None, t_new, pp * PAGE_SIZE), lambda b, p, tbl: (b, 0, p)),
                      pl.BlockSpec((None, t_new, PAGE_SIZE), lambda b, p, tbl: (b, 0, n_pages))]
        new_mask = mask
    else:
        mask_specs = [pl.BlockSpec((None,) + mask.shape[1:], lambda b, p, tbl: (b, 0, 0)),
                      pl.BlockSpec((PAGE_SIZE, LANES), lambda b, p, tbl: (0, 0))]
    n_layers, n_phys, _, n_kv, _ = k_pool.shape
    assert n_kv * HEAD_DIM == w
    as_rows = lambda pool: pool.reshape(n_layers, n_phys, PAGE_SIZE * n_kv, HEAD_DIM)
    k_pool, v_pool = as_rows(k_pool), as_rows(v_pool)
    page_spec = lambda k: pl.BlockSpec((None, None, PAGE_SIZE * n_kv, HEAD_DIM),
                                       lambda b, p, tbl: (layer, tbl[b, p * pp + k], 0, 0))
    per_b = lambda shp: pl.BlockSpec((None,) + shp, lambda b, p, tbl: (b, 0, 0))
    return pl.pallas_call(
        functools.partial(_paged_attn_kernel, pp=pp, key_mask=key_mask),
        grid_spec=pltpu.PrefetchScalarGridSpec(
            num_scalar_prefetch=1,
            grid=(bsz, n_pages // pp),
            in_specs=[
                per_b((w, LANES)),
                *[page_spec(k) for k in range(pp)],
                *[page_spec(k) for k in range(pp)],
                *mask_specs,
                per_b((PAGE_SIZE, w)),
                per_b((PAGE_SIZE, w)),
                pl.BlockSpec((2, PAGE_SIZE, LANES), lambda b, p, tbl: (0, 0, 0)),
            ],
            out_specs=per_b((LANES, w)),
            scratch_shapes=[pltpu.VMEM((1, LANES), F32), pltpu.VMEM((1, LANES), F32), pltpu.VMEM((LANES, w), F32)],
        ),
        out_shape=jax.ShapeDtypeStruct((bsz, LANES, w), F32),
        compiler_params=_params("arbitrary", "arbitrary"),
        name="paged_attention",
    )(page_table, qbd, *([k_pool] * pp), *([v_pool] * pp), mask, new_mask, k_new, v_new, bias_t)


def _block_diag_queries(q, n_kv):
    bsz, t, _ = q.shape
    q4 = q.reshape(bsz, t, N_HEADS, HEAD_DIM) * QK_SCALE_LOG2
    onehot = (jnp.arange(N_HEADS)[:, None] // (N_HEADS // n_kv) == jnp.arange(n_kv)[None, :]).astype(q.dtype)
    qbd = q4[:, :, :, None, :] * onehot[None, None, :, :, None]
    return qbd.transpose(0, 3, 4, 2, 1).reshape(bsz, n_kv * HEAD_DIM, N_HEADS * t).astype(BF16)


def _own_head_lanes(out, n_kv, t):
    bsz = out.shape[0]
    out5 = out.reshape(bsz, N_HEADS, t, n_kv, HEAD_DIM)
    heads = jnp.arange(N_HEADS)
    picked = out5[:, heads, :, heads // (N_HEADS // n_kv), :]
    return picked.transpose(1, 2, 0, 3).reshape(bsz, t, N_HEADS * HEAD_DIM)


def _decode_bias(near_bias, tq, t):
    tail = near_bias[:, :t, tq - PAGE_SIZE:tq + PAGE_SIZE]
    return tail.transpose(2, 0, 1).reshape(2, PAGE_SIZE, N_HEADS * t)


def _pad_rows(a, rows):
    return jnp.pad(a, ((0, 0), (0, rows - a.shape[1]), (0, 0)))


def _new_token_mask(t):
    key = jnp.arange(PAGE_SIZE)[:, None]
    tok = jnp.tile(jnp.arange(t), N_HEADS)[None, :]
    return jnp.where(key <= tok, 0.0, MASKED).astype(F32)


def _split_heads_kernel(x_ref, o_ref, *, tm, n_heads):
    for h in range(n_heads):
        o_ref[pl.ds(h, tm, stride=n_heads), :] = x_ref[:, h * HEAD_DIM:(h + 1) * HEAD_DIM]


def _split_heads(proj, col, n_heads, *, tm):
    bsz, t, _ = proj.shape
    w = n_heads * HEAD_DIM
    assert col % w == 0 and t % tm == 0
    out = pl.pallas_call(
        functools.partial(_split_heads_kernel, tm=tm, n_heads=n_heads),
        grid=(bsz, t // tm),
        in_specs=[pl.BlockSpec((None, tm, w), lambda b, i: (b, i, col // w))],
        out_specs=pl.BlockSpec((None, tm * n_heads, HEAD_DIM), lambda b, i: (b, i, 0)),
        out_shape=jax.ShapeDtypeStruct((bsz, t * n_heads, HEAD_DIM), F32),
        compiler_params=_params("arbitrary", "arbitrary"),
        name="split_heads",
    )(proj)
    return out.reshape(bsz, t, n_heads, HEAD_DIM)


ATTN_TQ = 256
SELECT_TQ = 256
SELECT_TK = 512
MOBA_HEADS_PER_STEP = 4
SPLIT_TM = 256
DSA_PAGES_PER_STEP = 8
MOBA_PAGES_PER_STEP = 4
Q_W = N_HEADS * HEAD_DIM
KV_W = N_KV_B * HEAD_DIM
DSA_PROJ = Q_W + 2 * KV_W + N_IDX_HEADS * IDX_DIM + IDX_DIM + N_IDX_HEADS
DSA_PROJ_PAD = -(-DSA_PROJ // 512) * 512


def _dsa_split(proj):
    k = proj[..., Q_W:Q_W + KV_W]
    v = proj[..., Q_W + KV_W:Q_W + 2 * KV_W]
    o = Q_W + 2 * KV_W + N_IDX_HEADS * IDX_DIM
    return k, v, proj[..., o:o + IDX_DIM]


def _dsa_prompt(proj, near_bias):
    bsz, t, _ = proj.shape
    ki = _dsa_split(proj)[2]
    mask = _dsa_select(proj, tq=min(SELECT_TQ, t), tk=min(SELECT_TK, t), topk=min(IDX_TOPK, t // 4))
    o = _attention(proj, Q_W, Q_W + KV_W, mask, near_bias, tq=ATTN_TQ, rep=N_HEADS // N_KV_B,
                   shared_kv=True, block_mask=False)
    k = _split_heads(proj, Q_W, N_KV_B, tm=min(SPLIT_TM, t))
    v = _split_heads(proj, Q_W + KV_W, N_KV_B, tm=min(SPLIT_TM, t))
    return o, k, v, ki


def _dsa_sample(proj, k_pool, v_pool, kidx_pool, layer, page_table, near_bias):
    bsz, t, _ = proj.shape
    past_len = page_table.shape[1] * PAGE_SIZE
    k, v, ki = _dsa_split(proj)
    o_qi = Q_W + 2 * KV_W
    qi = proj[..., o_qi:o_qi + N_IDX_HEADS * IDX_DIM].reshape(bsz, t, N_IDX_HEADS, IDX_DIM)
    wi = proj[..., o_qi + N_IDX_HEADS * IDX_DIM + IDX_DIM:DSA_PROJ]
    qi_rows = qi.transpose(0, 2, 1, 3).reshape(bsz, N_IDX_HEADS * t, IDX_DIM)
    wi_rows = wi.transpose(0, 2, 1).reshape(bsz, N_IDX_HEADS * t, 1)
    sel = _dsa_select_paged(qi_rows, wi_rows, kidx_pool, layer, page_table, _pad_rows(ki, PAGE_SIZE),
                            pp=DSA_PAGES_PER_STEP, topk=min(IDX_TOPK, (past_len + t) // 4))
    out = _paged_attention(
        _block_diag_queries(proj[..., :Q_W], N_KV_B), k_pool, v_pool, layer, page_table, sel, None,
        _pad_rows(k, PAGE_SIZE), _pad_rows(v, PAGE_SIZE), _decode_bias(near_bias, ATTN_TQ, t),
        pp=DSA_PAGES_PER_STEP, key_mask=True)
    o = _own_head_lanes(out, N_KV_B, t).astype(BF16)
    return o, k.reshape(bsz, t, N_KV_B, HEAD_DIM), v.reshape(bsz, t, N_KV_B, HEAD_DIM), ki


def _moba_prompt(proj, near_bias):
    bsz, t, _ = proj.shape
    k_heads = _split_heads(proj, Q_W, N_HEADS, tm=min(SPLIT_TM, t))
    v_heads = _split_heads(proj, 2 * Q_W, N_HEADS, tm=min(SPLIT_TM, t))
    n_pages = t // PAGE_SIZE
    pages = jnp.arange(bsz * n_pages, dtype=jnp.int32).reshape(bsz, n_pages)
    means = _block_means(k_heads.reshape(1, bsz * n_pages, PAGE_SIZE, N_HEADS, HEAD_DIM), 0, pages)
    sel = _moba_select(proj, _pad_rows(means, LANES), tq=ATTN_TQ, past_len=0)
    o = _attention(proj, Q_W, 2 * Q_W, sel, near_bias, tq=ATTN_TQ, rep=MOBA_HEADS_PER_STEP,
                   shared_kv=False, block_mask=True)
    return o, k_heads, v_heads


def _moba_sample(proj, k_pool, v_pool, layer, page_table, near_bias):
    bsz, t, _ = proj.shape
    n_pages = page_table.shape[1]
    past_len = n_pages * PAGE_SIZE
    k = proj[..., Q_W:2 * Q_W]
    v = proj[..., 2 * Q_W:3 * Q_W]
    means = _block_means(k_pool, layer, page_table)
    sel = _moba_select(proj, _pad_rows(means, LANES), tq=t, past_len=past_len)
    block_rows = (sel.astype(F32).transpose(0, 3, 1, 2).reshape(bsz, LANES, N_HEADS * t) - 1.0) * -MASKED
    out = _paged_attention(
        _block_diag_queries(proj[..., :Q_W], N_HEADS), k_pool, v_pool, layer, page_table,
        block_rows, _new_token_mask(t), _pad_rows(k, PAGE_SIZE), _pad_rows(v, PAGE_SIZE),
        _decode_bias(near_bias, ATTN_TQ, t), pp=MOBA_PAGES_PER_STEP, key_mask=False)
    o = _own_head_lanes(out, N_HEADS, t).astype(BF16)
    return o, k.reshape(bsz, t, N_HEADS, HEAD_DIM), v.reshape(bsz, t, N_HEADS, HEAD_DIM)


N_MIXERS = 3
PROMPT_TM = 1024
MATMUL_TN = 512
LRU_TT = 256
MOE_TM = 512
ROUTE_W = LANES


def kernel(x_prompt, x_sample, state_lru_h, state_lru_conv, cache_dsa_k, cache_dsa_v, cache_dsa_kidx,
           cache_moba_k, cache_moba_v, page_table, c_prompt, c_sample, rel_bias, norm_mix_g, norm_ffn_g,
           final_norm_g, ada_w, ada_b, lru_w_in, lru_conv_w, lru_conv_b, lru_w_rgate, lru_b_rgate,
           lru_w_igate, lru_b_igate, lru_lambda, lru_w_out, dsa_w_in, dsa_w_out, moba_w_in, moba_w_out,
           moe_w_group, moe_w_expert, moe_w1, moe_w3, moe_w2):
    bp, tp, d = x_prompt.shape
    bs, ts, _ = x_sample.shape
    depth = ada_w.shape[0]
    ns = bs * ts
    c_rnn = lru_w_out.shape[1]

    c_all = jnp.concatenate([c_prompt, c_sample], axis=0)
    c_rows = -(-c_all.shape[0] // SUBLANES) * SUBLANES
    mods = _ada(jnp.pad(c_all, ((0, c_rows - c_all.shape[0]), (0, 0))), ada_w, ada_b)
    mods = mods.reshape(depth, c_rows, 6, d)
    near_bias = _near_bias(rel_bias, ATTN_TQ)

    w1, w3, w2 = moe_w1.astype(BF16), moe_w3.astype(BF16), moe_w2.astype(BF16)
    xp = x_prompt
    pending = None
    xs = x_sample.reshape(1, ns, d)
    outs = {n: [] for n in ("lru_h_p", "lru_c_p", "lru_h_s", "lru_c_s", "dsa_k_p", "dsa_v_p", "dsa_i_p",
                            "dsa_k_s", "dsa_v_s", "dsa_i_s", "moba_k_p", "moba_v_p", "moba_k_s", "moba_v_s")}
    for i in range(depth):
        kind, j = i % N_MIXERS, i // N_MIXERS
        mod_p = [mods[i, :bp, m][:, None, :] for m in range(6)]
        mod_s = [jnp.repeat(mods[i, bp:bp + bs, m], ts, axis=0)[None] for m in range(6)]
        shp1, scp1, gtp1, shp2, scp2, gtp2 = mod_p
        shs1, scs1, gts1, shs2, scs2, gts2 = mod_s
        mix_in = lambda w, tn=MATMUL_TN: (
            _norm_matmul(xp, norm_mix_g[i], shp1, scp1, w, tm=min(PROMPT_TM, tp), tn=tn, layer=j),
            _norm_matmul(xs, norm_mix_g[i], shs1, scs1, w, tm=ns, tn=tn, layer=j))
        if kind == 0:
            up, us = mix_in(lru_w_in)
            lru_params = (lru_conv_w[j], lru_conv_b[j], lru_w_rgate[j], lru_b_rgate[j], lru_w_igate[j],
                          lru_b_igate[j], lru_lambda[j])
            yp, hp_state, cp = _lru_core(up, jnp.zeros((bp, CONV_W - 1, c_rnn), F32), jnp.zeros((bp, c_rnn), F32),
                                         *lru_params, tt=LRU_TT)
            ys, hs_state, cs = _lru_core(us.reshape(bs, ts, 2 * c_rnn), state_lru_conv[j], state_lru_h[j],
                                         *lru_params, tt=ts)
            ys = ys.reshape(1, ns, c_rnn)
            w_out = lru_w_out
            outs["lru_h_p"].append(hp_state)
            outs["lru_c_p"].append(cp)
            outs["lru_h_s"].append(hs_state)
            outs["lru_c_s"].append(cs)
        elif kind == 1:
            pp_, ps_ = mix_in(jnp.pad(dsa_w_in, ((0, 0), (0, 0), (0, DSA_PROJ_PAD - DSA_PROJ))))
            yp, kp, vp, ip = _dsa_prompt(pp_, near_bias)
            ys, k_s, v_s, i_s = _dsa_sample(ps_.reshape(bs, ts, DSA_PROJ_PAD), cache_dsa_k, cache_dsa_v,
                                            cache_dsa_kidx, j, page_table, near_bias)
            ys = ys.reshape(1, ns, Q_W)
            w_out = dsa_w_out
            for n, val in zip(("dsa_k_p", "dsa_v_p", "dsa_i_p", "dsa_k_s", "dsa_v_s", "dsa_i_s"),
                              (kp, vp, ip, k_s, v_s, i_s)):
                outs[n].append(val)
        else:
            pp_, ps_ = mix_in(moba_w_in)
            yp, kp, vp = _moba_prompt(pp_, near_bias)
            ys, k_s, v_s = _moba_sample(ps_.reshape(bs, ts, 3 * Q_W), cache_moba_k, cache_moba_v, j,
                                        page_table, near_bias)
            ys = ys.reshape(1, ns, Q_W)
            w_out = moba_w_out
            for n, val in zip(("moba_k_p", "moba_v_p", "moba_k_s", "moba_v_s"), (kp, vp, k_s, v_s)):
                outs[n].append(val)
        xp = _matmul_residual(yp, w_out, xp, gtp1, tm=min(PROMPT_TM, tp), tn=min(MATMUL_TN, d), layer=j)
        xs = _matmul_residual(ys, w_out, xs, gts1, tm=ns, tn=min(MATMUL_TN, d), layer=j)

        w_route = jnp.pad(jnp.concatenate([moe_w_group[i], moe_w_expert[i]], axis=1),
                          ((0, 0), (0, ROUTE_W - N_GROUPS - N_EXPERTS)))
        lg_p, h_p = _norm_matmul(xp, norm_ffn_g[i], shp2, scp2, w_route, tm=min(PROMPT_TM, tp), tn=ROUTE_W,
                                 emit_h=True, highest=True, h_dtype=F32)
        moe_tm = min(MOE_TM, tp)
        info, counts = _route_tokens(lg_p.reshape(bp * tp, ROUTE_W), tm=moe_tm)
        moe_p = _moe_grouped(h_p.reshape(bp * tp, d), info, *_group_layout(info, counts, moe_tm), w1, w3, w2, i,
                             tm=moe_tm)
        if i + 1 < depth:
            xp = _gated_add(xp, moe_p, gtp2, tm=moe_tm)
        else:
            pending = (moe_p, gtp2)
        lg_s, h_s = _norm_matmul(xs, norm_ffn_g[i], shs2, scs2, w_route, tm=ns, tn=ROUTE_W,
                                 emit_h=True, highest=True)
        xs = _moe_dense(h_s, lg_s, w1, w3, w2, i, xs, gts2, tm=ns)

    y_prompt = _rmsnorm(xp, final_norm_g, tm=min(MOE_TM, tp), residual=pending)
    y_sample = _rmsnorm(xs, final_norm_g, tm=ns).reshape(bs, ts, d)
    stack = lambda n: jnp.stack(outs[n])
    return (y_prompt, y_sample, stack("lru_h_p"), stack("lru_c_p"), stack("lru_h_s"), stack("lru_c_s"),
            stack("dsa_k_p"), stack("dsa_v_p"), stack("dsa_i_p"), stack("dsa_k_s"), stack("dsa_v_s"),
            stack("dsa_i_s"), stack("moba_k_p"), stack("moba_v_p"), stack("moba_k_s"), stack("moba_v_s"))
```

```python
import functools
import math

import numpy as np
import jax
import jax.numpy as jnp
from jax import lax
from jax.experimental import pallas as pl
from jax.experimental.pallas import tpu as pltpu

F32 = jnp.float32
BF16 = jnp.bfloat16

LANES = 128
SUBLANES = 8
VMEM_LIMIT_BYTES = 56 * 1024 * 1024

HEAD_DIM = 128
N_HEADS = 16
N_KV_B = 4
N_IDX_HEADS = 16
IDX_DIM = 64
IDX_TOPK = 256
IDX_SCALE = (IDX_DIM * N_IDX_HEADS) ** -0.5
PAGE_SIZE = 128
MOBA_BLOCK = 256
MOBA_TOPK = 3
N_BUCKETS = 32
MAX_DISTANCE = 128
N_RNN_BLOCKS = 16
CONV_W = 4
LRU_C = 8.0
N_GROUPS = 4
EXPERTS_PER_GROUP = 4
N_EXPERTS = N_GROUPS * EXPERTS_PER_GROUP
EPS = 1e-6
ATTN_SCALE = HEAD_DIM ** -0.5
LOG2E = math.log2(math.e)
QK_SCALE_LOG2 = ATTN_SCALE * LOG2E
NEG_INF = float("-inf")


def _params(*sem):
    return pltpu.CompilerParams(dimension_semantics=sem, vmem_limit_bytes=VMEM_LIMIT_BYTES)


def _bdot(a, b):
    return jnp.dot(a.astype(BF16), b.astype(BF16), preferred_element_type=F32)


def _bdot_nt(a, b):
    return lax.dot_general(a.astype(BF16), b.astype(BF16), (((1,), (1,)), ((), ())),
                           preferred_element_type=F32)


def _ada_kernel(c_ref, w_ref, b_ref, o_ref):
    c = c_ref[...]
    o_ref[...] = _bdot(c * jax.nn.sigmoid(c), w_ref[...]) + b_ref[...]


def _ada(c, ada_w, ada_b):
    depth, d, n = ada_w.shape
    rows = c.shape[0]
    tn = next(w for w in (1024, 512, 256, LANES) if n % w == 0)
    return pl.pallas_call(
        _ada_kernel,
        grid=(depth, n // tn),
        in_specs=[
            pl.BlockSpec((rows, d), lambda l, j: (0, 0)),
            pl.BlockSpec((None, d, tn), lambda l, j: (l, 0, j)),
            pl.BlockSpec((None, 1, tn), lambda l, j: (l, 0, j)),
        ],
        out_specs=pl.BlockSpec((None, rows, tn), lambda l, j: (l, 0, j)),
        out_shape=jax.ShapeDtypeStruct((depth, rows, n), F32),
        compiler_params=_params("arbitrary", "arbitrary"),
        name="ada_mod",
    )(c, ada_w, ada_b.reshape(depth, 1, n))


def _norm_mod(x, g, shift, scale):
    ms = jnp.mean(x * x, axis=-1, keepdims=True)
    h = x * lax.rsqrt(ms + EPS) * g
    return h * (1.0 + scale) + shift


def _norm_mm_kernel(x_ref, g_ref, sh_ref, sc_ref, w_ref, *refs, emit_h, highest):
    if emit_h:
        o_ref, ho_ref, h_ref = refs
    else:
        o_ref, h_ref = refs
        ho_ref = None

    @pl.when(pl.program_id(2) == 0)
    def _():
        h = _norm_mod(x_ref[...], g_ref[...], sh_ref[...], sc_ref[...])
        h_ref[...] = h.astype(h_ref.dtype)
        if ho_ref is not None:
            ho_ref[...] = h.astype(ho_ref.dtype)

    if highest:
        o_ref[...] = jnp.dot(h_ref[...], w_ref[...], precision=lax.Precision.HIGHEST,
                             preferred_element_type=F32)
    else:
        o_ref[...] = _bdot(h_ref[...], w_ref[...])


def _weight_spec(w, layer, tn):
    if w.ndim == 2:
        return pl.BlockSpec((w.shape[0], tn), lambda b, i, j: (0, j))
    return pl.BlockSpec((None, w.shape[1], tn), lambda b, i, j: (layer, 0, j))


def _norm_matmul(x, g, shift, scale, w, *, tm, tn, emit_h=False, highest=False, h_dtype=BF16, layer=None):
    bsz, t, d = x.shape
    n = w.shape[-1]
    r = shift.shape[1]
    assert t % tm == 0 and n % tn == 0 and (r == 1 or r == tm == t)
    out_shape = [jax.ShapeDtypeStruct((bsz, t, n), F32)]
    out_specs = [pl.BlockSpec((None, tm, tn), lambda b, i, j: (b, i, j))]
    row_tile = pl.BlockSpec((None, tm, d), lambda b, i, j: (b, i, 0))
    if emit_h:
        out_shape.append(jax.ShapeDtypeStruct((bsz, t, d), h_dtype))
        out_specs.append(row_tile)
    mod_map = (lambda b, i, j: (b, 0, 0)) if r == 1 else (lambda b, i, j: (b, i, 0))
    res = pl.pallas_call(
        functools.partial(_norm_mm_kernel, emit_h=emit_h, highest=highest),
        grid=(bsz, t // tm, n // tn),
        in_specs=[
            row_tile,
            pl.BlockSpec((1, d), lambda b, i, j: (0, 0)),
            pl.BlockSpec((None, r, d), mod_map),
            pl.BlockSpec((None, r, d), mod_map),
            _weight_spec(w, layer, tn),
        ],
        out_specs=out_specs,
        out_shape=out_shape,
        scratch_shapes=[pltpu.VMEM((tm, d), F32 if highest else BF16)],
        compiler_params=_params("arbitrary", "arbitrary", "arbitrary"),
        name="norm_matmul",
    )(x, g.reshape(1, d), shift, scale, w)
    return res if emit_h else res[0]


def _mm_res_kernel(a_ref, w_ref, r_ref, g_ref, o_ref):
    o_ref[...] = r_ref[...] + g_ref[...] * _bdot(a_ref[...], w_ref[...])


def _matmul_residual(a, w, res, gate, *, tm, tn, layer=None):
    bsz, t, k = a.shape
    n = w.shape[-1]
    r = gate.shape[1]
    assert t % tm == 0 and n % tn == 0 and (r == 1 or r == tm == t)
    gate_map = (lambda b, i, j: (b, 0, j)) if r == 1 else (lambda b, i, j: (b, i, j))
    return pl.pallas_call(
        _mm_res_kernel,
        grid=(bsz, t // tm, n // tn),
        in_specs=[
            pl.BlockSpec((None, tm, k), lambda b, i, j: (b, i, 0)),
            _weight_spec(w, layer, tn),
            pl.BlockSpec((None, tm, tn), lambda b, i, j: (b, i, j)),
            pl.BlockSpec((None, r, tn), gate_map),
        ],
        out_specs=pl.BlockSpec((None, tm, tn), lambda b, i, j: (b, i, j)),
        out_shape=jax.ShapeDtypeStruct((bsz, t, n), F32),
        compiler_params=_params("arbitrary", "arbitrary", "arbitrary"),
        name="matmul_residual",
    )(a, w, res, gate)


def _rmsnorm_kernel(x_ref, g_ref, *refs, residual):
    x = x_ref[...]
    if residual:
        y_ref, gy_ref, o_ref = refs
        x = x + gy_ref[...] * y_ref[...]
    else:
        o_ref, = refs
    ms = jnp.mean(x * x, axis=-1, keepdims=True)
    o_ref[...] = x * lax.rsqrt(ms + EPS) * g_ref[...]


def _rmsnorm(x, g, *, tm, residual=None):
    bsz, t, d = x.shape
    per_b = t // tm
    extra_in, extra_specs = [], []
    if residual is not None:
        extra_in = list(residual)
        extra_specs = [pl.BlockSpec((tm, d), lambda b, i: (b * per_b + i, 0)),
                       pl.BlockSpec((None, 1, d), lambda b, i: (b, 0, 0))]
    return pl.pallas_call(
        functools.partial(_rmsnorm_kernel, residual=residual is not None),
        grid=(bsz, t // tm),
        in_specs=[pl.BlockSpec((None, tm, d), lambda b, i: (b, i, 0)),
                  pl.BlockSpec((1, d), lambda b, i: (0, 0)), *extra_specs],
        out_specs=pl.BlockSpec((None, tm, d), lambda b, i: (b, i, 0)),
        out_shape=jax.ShapeDtypeStruct((bsz, t, d), F32),
        compiler_params=_params("arbitrary", "arbitrary"),
        name="final_rmsnorm",
    )(x, g.reshape(1, d), *extra_in)


GATE_CHUNK = 640
SCAN_UNROLL = 4


def _log_sigmoid(x):
    return jnp.minimum(x, 0.0) - jnp.log1p(jnp.exp(-jnp.abs(x)))


def _lru_kernel(gb_ref, xb_ref, cs_ref, h0_ref, cw_ref, cb_ref, wr_ref, br_ref, wi_ref, bi_ref, lam_ref,
                y_ref, hl_ref, nc_ref, ext_ref, a_ref, b_ref, hs_ref, carry_ref, *, tt):
    c_all = xb_ref.shape[-1]

    @pl.when(pl.program_id(1) == 0)
    def _():
        ext_ref[0:SUBLANES, :] = cs_ref[...]
        carry_ref[...] = h0_ref[...]

    xb = xb_ref[...]
    ext_ref[SUBLANES:SUBLANES + tt, :] = xb
    cw = cw_ref[...]
    xc = cb_ref[...] + xb * cw[CONV_W - 1:CONV_W, :]
    for j in range(1, CONV_W):
        xc = xc + ext_ref[pl.ds(SUBLANES - j, tt), :] * cw[CONV_W - 1 - j:CONV_W - j, :]
    tail = ext_ref[tt:tt + SUBLANES, :]
    ext_ref[0:SUBLANES, :] = tail
    nc_ref[...] = tail

    xcb = xc.astype(BF16)
    r_parts, i_parts = [], []
    for c in range(c_all // GATE_CHUNK):
        sl = slice(c * GATE_CHUNK, (c + 1) * GATE_CHUNK)
        r_parts.append(jnp.dot(xcb[:, sl], wr_ref[c], preferred_element_type=F32))
        i_parts.append(jnp.dot(xcb[:, sl], wi_ref[c], preferred_element_type=F32))
    r = jax.nn.sigmoid(jnp.concatenate(r_parts, axis=1) + br_ref[...])
    ig = jax.nn.sigmoid(jnp.concatenate(i_parts, axis=1) + bi_ref[...])
    log_a = LRU_C * r * _log_sigmoid(lam_ref[...])
    th = jnp.tanh(log_a)
    one_minus_a2 = -2.0 * th / (1.0 - th)
    a_ref[...] = jnp.exp(log_a)
    b_ref[...] = jnp.sqrt(jnp.maximum(one_minus_a2, 0.0)) * (ig * xc)

    row = lax.broadcasted_iota(jnp.int32, (SUBLANES, GATE_CHUNK), 0)
    for c in range(c_all // GATE_CHUNK):
        sl = slice(c * GATE_CHUNK, (c + 1) * GATE_CHUNK)

        def body(grp, carry, sl=sl):
            r8 = pl.multiple_of(grp * SUBLANES, SUBLANES)
            av = a_ref[pl.ds(r8, SUBLANES), sl]
            bv = b_ref[pl.ds(r8, SUBLANES), sl]
            for s in (1, 2, 4):
                m = row >= s
                a_sh = jnp.where(m, pltpu.roll(av, s, axis=0), 1.0)
                b_sh = jnp.where(m, pltpu.roll(bv, s, axis=0), 0.0)
                bv = av * b_sh + bv
                av = av * a_sh
            h = av * carry + bv
            hs_ref[pl.ds(r8, SUBLANES), sl] = h
            return h[SUBLANES - 1:SUBLANES, :]

        carry_ref[:, sl] = lax.fori_loop(0, tt // SUBLANES, body, carry_ref[:, sl],
                                         unroll=min(SCAN_UNROLL, tt // SUBLANES))

    hl_ref[...] = carry_ref[...]
    y_ref[...] = (jax.nn.gelu(gb_ref[...]) * hs_ref[...]).astype(y_ref.dtype)


def _lru_core(u, conv_state, h0, conv_w, conv_b, w_r, b_r, w_i, b_i, lam, *, tt):
    bsz, t, c2 = u.shape
    c = c2 // 2
    blk = c // N_RNN_BLOCKS
    per = GATE_CHUNK // blk
    nchunk = c // GATE_CHUNK

    def block_diag(w):
        wc = w.reshape(nchunk, per, blk, blk).astype(BF16)
        eye = jnp.eye(per, dtype=BF16)
        return jnp.einsum("cpij,pq->cpiqj", wc, eye).reshape(nchunk, GATE_CHUNK, GATE_CHUNK)

    cs_pad = jnp.pad(conv_state, ((0, 0), (SUBLANES - (CONV_W - 1), 0), (0, 0)))
    row = lambda v: v.reshape(1, c)
    full = lambda shp: pl.BlockSpec(shp, lambda b, i: (0,) * len(shp))
    y, hl, nc = pl.pallas_call(
        functools.partial(_lru_kernel, tt=tt),
        grid=(bsz, t // tt),
        in_specs=[
            pl.BlockSpec((None, tt, c), lambda b, i: (b, i, 0)),
            pl.BlockSpec((None, tt, c), lambda b, i: (b, i, 1)),
            pl.BlockSpec((None, SUBLANES, c), lambda b, i: (b, 0, 0)),
            pl.BlockSpec((None, 1, c), lambda b, i: (b, 0, 0)),
            full((CONV_W, c)), full((1, c)),
            full((nchunk, GATE_CHUNK, GATE_CHUNK)), full((1, c)),
            full((nchunk, GATE_CHUNK, GATE_CHUNK)), full((1, c)),
            full((1, c)),
        ],
        out_specs=[
            pl.BlockSpec((None, tt, c), lambda b, i: (b, i, 0)),
            pl.BlockSpec((None, 1, c), lambda b, i: (b, 0, 0)),
            pl.BlockSpec((None, SUBLANES, c), lambda b, i: (b, 0, 0)),
        ],
        out_shape=[
            jax.ShapeDtypeStruct((bsz, t, c), BF16),
            jax.ShapeDtypeStruct((bsz, 1, c), F32),
            jax.ShapeDtypeStruct((bsz, SUBLANES, c), F32),
        ],
        scratch_shapes=[
            pltpu.VMEM((tt + SUBLANES, c), F32),
            pltpu.VMEM((tt, c), F32),
            pltpu.VMEM((tt, c), F32),
            pltpu.VMEM((tt, c), F32),
            pltpu.VMEM((1, c), F32),
        ],
        compiler_params=_params("arbitrary", "arbitrary"),
        name="lru_core",
    )(u, u, cs_pad, h0.reshape(bsz, 1, c), conv_w, row(conv_b), block_diag(w_r), row(b_r),
      block_diag(w_i), row(b_i), row(lam))
    return y, hl.reshape(bsz, c), nc[:, SUBLANES - (CONV_W - 1):, :]


def _route_gates(lg):
    return _route(lg)[0]


def _route(lg):
    col = lax.broadcasted_iota(jnp.int32, lg.shape, 1)
    big = jnp.int32(LANES)
    is_g = col < N_GROUPS
    gl = jnp.where(is_g, lg, NEG_INF)
    gmax = jnp.max(gl, axis=1, keepdims=True)
    g_sel = jnp.min(jnp.where(gl == gmax, col, big), axis=1, keepdims=True)
    g_w = 1.0 / jnp.sum(jnp.exp(gl - gmax), axis=1, keepdims=True)
    eid = col - N_GROUPS
    in_grp = (eid >= 0) & (eid < N_EXPERTS) & ((eid // EXPERTS_PER_GROUP) == g_sel)
    e_in = jnp.where(in_grp, lg, NEG_INF)
    top1 = jnp.max(e_in, axis=1, keepdims=True)
    idx1 = jnp.min(jnp.where((e_in == top1) & in_grp, col, big), axis=1, keepdims=True)
    rest = in_grp & (col != idx1)
    e2 = jnp.where(rest, lg, NEG_INF)
    top2 = jnp.max(e2, axis=1, keepdims=True)
    idx2 = jnp.min(jnp.where((e2 == top2) & rest, col, big), axis=1, keepdims=True)
    z = jnp.exp(top2 - top1)
    w_first = g_w / (1.0 + z)
    w_second = g_w * z / (1.0 + z)
    return jnp.where(col == idx1, w_first, 0.0) + jnp.where(col == idx2, w_second, 0.0), g_sel


def _moe_dense_kernel(h_ref, lg_ref, w1_ref, w3_ref, w2_ref, x_ref, g_ref, o_ref, gates_ref, acc_ref):
    e = pl.program_id(2)

    @pl.when(e == 0)
    def _():
        gates_ref[...] = _route_gates(lg_ref[...])
        acc_ref[...] = jnp.zeros_like(acc_ref)

    gates = gates_ref[...]
    col = lax.broadcasted_iota(jnp.int32, gates.shape, 1)
    ge = jnp.sum(jnp.where(col == e + N_GROUPS, gates, 0.0), axis=1, keepdims=True)
    h = h_ref[...]
    a1 = jnp.dot(h, w1_ref[...], preferred_element_type=F32)
    a3 = jnp.dot(h, w3_ref[...], preferred_element_type=F32)
    hidden = (a1 * jax.nn.sigmoid(a1)) * a3 * ge
    acc_ref[...] += jnp.dot(hidden.astype(BF16), w2_ref[...], preferred_element_type=F32)

    @pl.when(e == pl.num_programs(2) - 1)
    def _():
        o_ref[...] = x_ref[...] + g_ref[...] * acc_ref[...]


def _moe_dense(h, logits, w1, w3, w2, layer, x, gate, *, tm):
    bsz, t, d = x.shape
    _, ne, _, f = w1.shape
    r = gate.shape[1]
    gate_map = (lambda b, i, e: (b, 0, 0)) if r == 1 else (lambda b, i, e: (b, i, 0))
    tok = lambda b, i, e: (b, i, 0)
    expert = lambda b, i, e: (layer, e, 0, 0)
    return pl.pallas_call(
        _moe_dense_kernel,
        grid=(bsz, t // tm, ne),
        in_specs=[
            pl.BlockSpec((None, tm, d), tok),
            pl.BlockSpec((None, tm, LANES), tok),
            pl.BlockSpec((None, None, d, f), expert),
            pl.BlockSpec((None, None, d, f), expert),
            pl.BlockSpec((None, None, f, d), expert),
            pl.BlockSpec((None, tm, d), tok),
            pl.BlockSpec((None, r, d), gate_map),
        ],
        out_specs=pl.BlockSpec((None, tm, d), tok),
        out_shape=jax.ShapeDtypeStruct((bsz, t, d), F32),
        scratch_shapes=[pltpu.VMEM((tm, LANES), F32), pltpu.VMEM((tm, d), F32)],
        compiler_params=_params("arbitrary", "arbitrary", "arbitrary"),
        name="moe_dense",
    )(h, logits, w1, w3, w2, x, gate)


INFO_GROUP_LANE = 0
INFO_RANK_LANE = 1


def _route_kernel(lg_ref, info_ref, cnt_ref, carry_ref):
    @pl.when(pl.program_id(0) == 0)
    def _():
        carry_ref[...] = jnp.zeros_like(carry_ref)

    gates, g_sel = _route(lg_ref[...])
    tm = gates.shape[0]
    col = lax.broadcasted_iota(jnp.int32, gates.shape, 1)
    onehot = (col == g_sel).astype(F32)
    before = (lax.broadcasted_iota(jnp.int32, (tm, tm), 0) > lax.broadcasted_iota(jnp.int32, (tm, tm), 1))
    earlier = jnp.dot(before.astype(BF16), onehot.astype(BF16), preferred_element_type=F32) + carry_ref[...]
    rank = jnp.sum(jnp.where(col == g_sel, earlier, 0.0), axis=1, keepdims=True)
    carry_ref[...] += jnp.sum(onehot, axis=0, keepdims=True)
    info_ref[...] = (gates + jnp.where(col == INFO_GROUP_LANE, g_sel.astype(F32), 0.0)
                     + jnp.where(col == INFO_RANK_LANE, rank, 0.0))
    cnt_ref[...] = carry_ref[...]


def _route_tokens(logits, *, tm):
    n = logits.shape[0]
    assert n % tm == 0
    return pl.pallas_call(
        _route_kernel,
        grid=(n // tm,),
        in_specs=[pl.BlockSpec((tm, LANES), lambda i: (i, 0))],
        out_specs=[pl.BlockSpec((tm, LANES), lambda i: (i, 0)), pl.BlockSpec((1, LANES), lambda i: (0, 0))],
        out_shape=[jax.ShapeDtypeStruct((n, LANES), F32), jax.ShapeDtypeStruct((1, LANES), F32)],
        scratch_shapes=[pltpu.VMEM((1, LANES), F32)],
        compiler_params=_params("arbitrary"),
        name="moe_route",
    )(logits)


def _group_layout(info, counts, tm):
    n = info.shape[0]
    g_sel = info[:, INFO_GROUP_LANE].astype(jnp.int32)
    rank = info[:, INFO_RANK_LANE].astype(jnp.int32)
    cnt = counts[0, :N_GROUPS].astype(jnp.int32)
    padded = -(-cnt // tm) * tm
    ends = jnp.cumsum(padded)
    starts = ends - padded
    n_slots = n + N_GROUPS * tm
    src = _slot_tokens(starts[g_sel] + rank, n_slots)
    tile_start = jnp.arange(n_slots // tm, dtype=jnp.int32) * tm
    tile_group = jnp.minimum(jnp.sum((tile_start[:, None] >= ends[None, :]).astype(jnp.int32), axis=1), N_GROUPS - 1)
    slot_group = jnp.repeat(tile_group, tm)
    is_pad = jnp.arange(n_slots, dtype=jnp.int32) - starts[slot_group] >= cnt[slot_group]
    pad_row = n + tm + jnp.cumsum(is_pad.astype(jnp.int32)) - 1
    dst = jnp.concatenate([n + jnp.arange(tm, dtype=jnp.int32), jnp.where(is_pad, pad_row, src)])
    return src, dst, tile_group, (ends[-1] // tm).reshape(1)


SLOT_UNROLL = 8


def _slot_tokens_kernel(slot_ref, src_ref):
    n, n_slots = slot_ref.shape[0], src_ref.shape[0]

    def clear(c, carry):
        for k in range(SLOT_UNROLL):
            src_ref[c * SLOT_UNROLL + k] = 0
        return carry

    def place(c, carry):
        for k in range(SLOT_UNROLL):
            t = c * SLOT_UNROLL + k
            src_ref[slot_ref[t]] = t
        return carry

    lax.fori_loop(0, n_slots // SLOT_UNROLL, clear, 0)
    lax.fori_loop(0, n // SLOT_UNROLL, place, 0)


def _slot_tokens(slot, n_slots):
    assert slot.shape[0] % SLOT_UNROLL == 0 and n_slots % SLOT_UNROLL == 0
    return pl.pallas_call(
        _slot_tokens_kernel,
        in_specs=[pl.BlockSpec(memory_space=pltpu.SMEM)],
        out_specs=pl.BlockSpec(memory_space=pltpu.SMEM),
        out_shape=jax.ShapeDtypeStruct((n_slots,), jnp.int32),
        name="moe_slot_tokens",
    )(slot)


def _moe_group_kernel(src_ref, dst_ref, tg_ref, used_ref, h_hbm, info_hbm, w1_ref, w3_ref, w2_ref, zeros_hbm,
                      out_hbm, xbuf, gbuf, xb, gates, acc, obuf, sem_x, sem_g, sem_o, *, tm):
    del zeros_hbm
    i = pl.program_id(0)
    e = pl.program_id(1)
    used = used_ref[0]
    live = i < used
    chunk = tm // EXPERTS_PER_GROUP
    first = e * chunk
    last_step = e == EXPERTS_PER_GROUP - 1

    def gather_row(tile, r):
        tok = src_ref[tile * tm + r]
        pltpu.make_async_copy(h_hbm.at[pl.ds(tok, 1), :], xbuf.at[pl.ds(r, 1), :], sem_x).start()
        pltpu.make_async_copy(info_hbm.at[pl.ds(tok, 1), :], gbuf.at[pl.ds(r, 1), :], sem_g).start()

    def scatter_chunk(tile):
        slot = (tile + 2) % 2
        for k in range(chunk):
            row = dst_ref[(tile + 1) * tm + first + k]
            pltpu.make_async_copy(obuf.at[slot, pl.ds(first + k, 1), :], out_hbm.at[pl.ds(row, 1), :],
                                  sem_o.at[slot]).start()

    def wait_scatter(slot):
        pltpu.make_async_copy(obuf.at[slot], out_hbm.at[pl.ds(0, tm), :], sem_o.at[slot]).wait()

    @pl.when((i == 0) & (e == 0))
    def _():
        obuf[...] = jnp.zeros_like(obuf)

        def body(r, carry):
            gather_row(0, r)
            return carry
        lax.fori_loop(0, tm, body, 0)

    @pl.when((e == 0) & (i <= used))
    def _():
        pltpu.make_async_copy(h_hbm.at[pl.ds(0, tm), :], xbuf, sem_x).wait()
        pltpu.make_async_copy(info_hbm.at[pl.ds(0, tm), :], gbuf, sem_g).wait()

    @pl.when((e == 0) & live)
    def _():
        xb[...] = xbuf[...].astype(BF16)
        gates[...] = gbuf[...]
        acc[...] = jnp.zeros_like(acc)

    @pl.when(live)
    def _():
        lane = lax.broadcasted_iota(jnp.int32, (tm, LANES), 1)
        gate_lane = N_GROUPS + EXPERTS_PER_GROUP * tg_ref[i] + e
        ge = jnp.sum(jnp.where(lane == gate_lane, gates[...], 0.0), axis=1, keepdims=True)
        x = xb[...]
        a1 = jnp.dot(x, w1_ref[...], preferred_element_type=F32)
        a3 = jnp.dot(x, w3_ref[...], preferred_element_type=F32)
        hidden = (a1 * jax.nn.sigmoid(a1)) * a3 * ge
        acc[...] += jnp.dot(hidden.astype(BF16), w2_ref[...], preferred_element_type=F32)
        for k in range(chunk):
            gather_row(i + 1, first + k)
        scatter_chunk(i - 1)

    @pl.when(i == used)
    def _():
        scatter_chunk(i - 1)

    @pl.when(last_step & live & (i >= 1))
    def _():
        wait_scatter(i % 2)

    @pl.when(last_step & live)
    def _():
        obuf[i % 2] = acc[...]

    @pl.when(last_step & (i == used))
    def _():
        wait_scatter(0)
        wait_scatter(1)


def _moe_grouped(h, info, src, dst, tile_group, n_used, w1, w3, w2, layer, *, tm):
    n, d = h.shape
    f = w1.shape[3]
    n_tiles = src.shape[0] // tm
    n_rows = n + (N_GROUPS + 1) * tm
    assert tm % EXPERTS_PER_GROUP == 0 and dst.shape[0] == (n_tiles + 1) * tm

    def expert(i, e, src_, dst_, tg, used):
        e_eff = jnp.where(i < used[0], e, EXPERTS_PER_GROUP - 1)
        return (layer, tg[i] * EXPERTS_PER_GROUP + e_eff, 0, 0)

    return pl.pallas_call(
        functools.partial(_moe_group_kernel, tm=tm),
        grid_spec=pltpu.PrefetchScalarGridSpec(
            num_scalar_prefetch=4,
            grid=(n_tiles, EXPERTS_PER_GROUP),
            in_specs=[
                pl.BlockSpec(memory_space=pl.ANY),
                pl.BlockSpec(memory_space=pl.ANY),
                pl.BlockSpec((None, None, d, f), expert),
                pl.BlockSpec((None, None, d, f), expert),
                pl.BlockSpec((None, None, f, d), expert),
                pl.BlockSpec(memory_space=pl.ANY),
            ],
            out_specs=pl.BlockSpec(memory_space=pl.ANY),
            scratch_shapes=[
                pltpu.VMEM((tm, d), F32), pltpu.VMEM((tm, LANES), F32), pltpu.VMEM((tm, d), BF16),
                pltpu.VMEM((tm, LANES), F32), pltpu.VMEM((tm, d), F32), pltpu.VMEM((2, tm, d), F32),
                pltpu.SemaphoreType.DMA(()), pltpu.SemaphoreType.DMA(()), pltpu.SemaphoreType.DMA((2,)),
            ],
        ),
        out_shape=jax.ShapeDtypeStruct((n_rows, d), F32),
        input_output_aliases={9: 0},
        compiler_params=_params("arbitrary", "arbitrary"),
        name="moe_grouped",
    )(src, dst, tile_group, n_used, h, info, w1, w3, w2, jnp.zeros((n_rows, d), F32))


def _gated_add_kernel(x_ref, y_ref, g_ref, o_ref):
    o_ref[...] = x_ref[...] + g_ref[...] * y_ref[...]


def _gated_add(x, y, gate, *, tm):
    bsz, t, d = x.shape
    per_b = t // tm
    return pl.pallas_call(
        _gated_add_kernel,
        grid=(bsz, per_b),
        in_specs=[
            pl.BlockSpec((None, tm, d), lambda b, i: (b, i, 0)),
            pl.BlockSpec((tm, d), lambda b, i: (b * per_b + i, 0)),
            pl.BlockSpec((None, 1, d), lambda b, i: (b, 0, 0)),
        ],
        out_specs=pl.BlockSpec((None, tm, d), lambda b, i: (b, i, 0)),
        out_shape=jax.ShapeDtypeStruct((bsz, t, d), F32),
        compiler_params=_params("arbitrary", "arbitrary"),
        name="gated_add",
    )(x, y, gate)


MASKED = -1e30


def _t5_bucket_table(tq):
    qi = np.arange(tq, dtype=np.int32)[:, None]
    col = np.arange(2 * tq, dtype=np.int32)[None, :]
    n = np.maximum(qi + tq - col, 0)
    max_exact = N_BUCKETS // 2
    nf = np.maximum(n, 1).astype(np.float32)
    large = max_exact + (np.log(nf / np.float32(max_exact)) / np.float32(math.log(MAX_DISTANCE / max_exact))
                         * np.float32(N_BUCKETS - max_exact)).astype(np.int32)
    table = np.where(n < max_exact, n, np.minimum(large, N_BUCKETS - 1)).astype(np.int32)
    assert table[0, 0] == N_BUCKETS - 1
    return table


def _bias_kernel(rb_ref, bk_ref, o_ref):
    h = pl.program_id(0)
    bk = bk_ref[...]
    acc = jnp.zeros(bk.shape, F32)
    for k in range(N_BUCKETS):
        acc = jnp.where(bk == k, rb_ref[k, h], acc)
    o_ref[...] = (acc - rb_ref[N_BUCKETS - 1, h]) * LOG2E


def _near_bias(rel_bias, tq):
    return pl.pallas_call(
        _bias_kernel,
        grid=(N_HEADS,),
        in_specs=[pl.BlockSpec(memory_space=pltpu.SMEM),
                  pl.BlockSpec((tq, 2 * tq), lambda h: (0, 0))],
        out_specs=pl.BlockSpec((None, tq, 2 * tq), lambda h: (h, 0, 0)),
        out_shape=jax.ShapeDtypeStruct((N_HEADS, tq, 2 * tq), F32),
        name="near_bias",
    )(rel_bias, jnp.asarray(_t5_bucket_table(tq)))


_NEG_INF_KEY = int(np.int32(np.array(-np.inf, np.float32).view(np.int32)) ^ np.int32(0x7FFFFFFF))


def _sort_key(x):
    k = pltpu.bitcast(x, jnp.int32)
    return k ^ ((k >> 31) & jnp.int32(0x7FFFFFFF))


def _kth_largest_key(count_ge, kth):
    zero = jnp.int32(0)
    first = count_ge(zero)
    ans = jnp.where(first >= kth, zero, jnp.int32(-2 ** 31))
    settled = (first == kth).astype(jnp.int32)

    def cond(state):
        it, _, settled = state
        return (it < 31) & (jnp.min(settled) == 0)

    def body(state):
        it, ans, settled = state
        cand = ans | (jnp.int32(1) << (30 - it))
        count = count_ge(cand)
        return it + 1, jnp.where(count >= kth, cand, ans), settled | (count == kth).astype(jnp.int32)

    return lax.while_loop(cond, body, (jnp.int32(0), ans, settled))[1]


def _dsa_select_kernel(qi_ref, kw_ref, kall_ref, m_ref, key_ref, *, tq, tk, topk):
    i = pl.program_id(1)
    n_kt = (i * tq + tq + tk - 1) // tk
    qpos = i * tq + lax.broadcasted_iota(jnp.int32, (tq, tk), 0)
    wi = kw_ref[:, IDX_DIM:IDX_DIM + N_IDX_HEADS]
    qi = qi_ref[...].astype(BF16)
    key_ref[...] = jnp.full(key_ref.shape, _NEG_INF_KEY, jnp.int32)

    def score_tile(kt, carry):
        k0 = pl.multiple_of(kt * tk, tk)
        kid = kall_ref[pl.ds(k0, tk), 0:IDX_DIM].astype(BF16)
        acc = jnp.zeros((tq, tk), F32)
        for h in range(N_IDX_HEADS):
            s = _bdot_nt(qi[:, h * IDX_DIM:(h + 1) * IDX_DIM], kid)
            acc = acc + wi[:, h:h + 1] * jnp.maximum(s, 0.0)
        kpos = k0 + lax.broadcasted_iota(jnp.int32, (tq, tk), 1)
        sc = jnp.where(kpos <= qpos, acc * IDX_SCALE, NEG_INF)
        key_ref[:, pl.ds(k0, tk)] = _sort_key(sc)
        return carry

    lax.fori_loop(0, n_kt, score_tile, 0)

    def count_ge(cand):
        def body(kt, acc):
            k0 = pl.multiple_of(kt * tk, tk)
            ge = (key_ref[:, pl.ds(k0, tk)] >= cand).astype(jnp.int32)
            for j in range(tk // LANES):
                acc = acc + ge[:, j * LANES:(j + 1) * LANES]
            return acc
        acc = lax.fori_loop(0, n_kt, body, jnp.zeros((tq, LANES), jnp.int32))
        return jnp.sum(acc, axis=1, keepdims=True)

    thr = _kth_largest_key(count_ge, topk)
    keys = key_ref[...]
    m_ref[...] = ((keys >= thr) & (keys > _NEG_INF_KEY)).astype(m_ref.dtype)


def _dsa_select(proj, *, tq, tk, topk):
    bsz, t, _ = proj.shape
    qi_w = N_IDX_HEADS * IDX_DIM
    qi_blk = (N_HEADS * HEAD_DIM + 2 * N_KV_B * HEAD_DIM) // qi_w
    kw_blk = (N_HEADS * HEAD_DIM + 2 * N_KV_B * HEAD_DIM + qi_w) // LANES
    return pl.pallas_call(
        functools.partial(_dsa_select_kernel, tq=tq, tk=tk, topk=topk),
        grid=(bsz, t // tq),
        in_specs=[
            pl.BlockSpec((None, tq, qi_w), lambda b, i: (b, i, qi_blk)),
            pl.BlockSpec((None, tq, LANES), lambda b, i: (b, i, kw_blk)),
            pl.BlockSpec((None, t, LANES), lambda b, i: (b, 0, kw_blk)),
        ],
        out_specs=pl.BlockSpec((None, tq, t), lambda b, i: (b, i, 0)),
        out_shape=jax.ShapeDtypeStruct((bsz, t, t), BF16),
        scratch_shapes=[pltpu.VMEM((tq, t), jnp.int32)],
        compiler_params=_params("arbitrary", "arbitrary"),
        name="dsa_select",
    )(proj, proj, proj)


FAR_WIDTH = 2


def _attn_kernel(q_ref, k_ref, v_ref, m_ref, tb_ref, o_ref, *state, tq, rep, shared_kv, block_mask):
    m_sc, l_sc, acc_sc = state[:rep], state[rep:2 * rep], state[2 * rep:]
    i = pl.program_id(2)
    for r in range(rep):
        m_sc[r][...] = jnp.full((tq, LANES), MASKED, F32)
        l_sc[r][...] = jnp.zeros((tq, LANES), F32)
        acc_sc[r][...] = jnp.zeros((tq, HEAD_DIM), F32)
    def row_reduce(x, op, reduce):
        part = x[:, :LANES]
        for c in range(1, x.shape[1] // LANES):
            part = op(part, x[:, c * LANES:(c + 1) * LANES])
        return reduce(part, axis=1, keepdims=True)

    def key_mask(j, r, causal, width):
        if block_mask:
            lane = lax.broadcasted_iota(jnp.int32, (tq, LANES), 1)
            flags = m_ref[r].astype(F32)
            vis = [jnp.sum(jnp.where(lane == j + w, flags, 0.0), axis=1, keepdims=True) for w in range(width)]
            ok = jnp.concatenate([jnp.broadcast_to(v > 0.0, (tq, tq)) for v in vis], axis=1)
        else:
            ok = m_ref[:, pl.ds(pl.multiple_of(j * tq, tq), width * tq)] > 0
        if causal:
            ok = ok & (lax.broadcasted_iota(jnp.int32, (tq, tq), 0) >= lax.broadcasted_iota(jnp.int32, (tq, tq), 1))
        return ok

    def step(j, bias_lo, causal, width=1):
        keys = width * tq
        k0 = pl.multiple_of(j * tq, tq)
        kv = lambda r: slice(0, HEAD_DIM) if shared_kv else slice(r * HEAD_DIM, (r + 1) * HEAD_DIM)
        ok_shared = None if block_mask else key_mask(j, 0, causal, width)
        scores = []
        for r in range(rep):
            q = (q_ref[:, r * HEAD_DIM:(r + 1) * HEAD_DIM] * QK_SCALE_LOG2).astype(BF16)
            s = _bdot_nt(q, k_ref[pl.ds(k0, keys), kv(r)])
            if bias_lo is not None:
                s = s + tb_ref[r, :, bias_lo:bias_lo + tq]
            scores.append(jnp.where(key_mask(j, r, causal, width) if block_mask else ok_shared, s, MASKED))
        probs, alphas = [], []
        for r in range(rep):
            m_old = m_sc[r][...]
            m_new = jnp.maximum(m_old, row_reduce(scores[r], jnp.maximum, jnp.max))
            alpha = jnp.exp2(m_old - m_new)
            p = jnp.exp2(scores[r] - jnp.concatenate([m_new] * (keys // LANES), axis=1))
            l_sc[r][...] = alpha * l_sc[r][...] + row_reduce(p, jnp.add, jnp.sum)
            m_sc[r][...] = m_new
            probs.append(p.astype(BF16))
            alphas.append(alpha)
        for r in range(rep):
            pv = jnp.dot(probs[r], v_ref[pl.ds(k0, keys), kv(r)].astype(BF16), preferred_element_type=F32)
            acc_sc[r][...] = alphas[r] * acc_sc[r][...] + pv

    n_far = jnp.maximum(i - 1, 0)

    def far_pair(c, carry):
        step(FAR_WIDTH * c, None, False, width=FAR_WIDTH)
        return carry

    lax.fori_loop(0, n_far // FAR_WIDTH, far_pair, 0)

    def far_rest(c, carry):
        step((n_far // FAR_WIDTH) * FAR_WIDTH + c, None, False)
        return carry

    lax.fori_loop(0, n_far % FAR_WIDTH, far_rest, 0)

    @pl.when(i >= 1)
    def _():
        step(i - 1, 0, False)

    step(i, tq, True)
    o_ref[...] = jnp.concatenate([acc_sc[r][...] / l_sc[r][...] for r in range(rep)], axis=1).astype(o_ref.dtype)


def _attention(proj, k_col, v_col, mask, near_bias, *, tq, rep, shared_kv, block_mask):
    bsz, t, _ = proj.shape
    g = N_HEADS // rep
    kv_w = HEAD_DIM if shared_kv else rep * HEAD_DIM
    assert HEAD_DIM == LANES and k_col % kv_w == 0 and v_col % kv_w == 0
    k_blk, v_blk = k_col // kv_w, v_col // kv_w
    if block_mask:
        assert tq == MOBA_BLOCK
        mask_spec = pl.BlockSpec((None, rep, tq, LANES), lambda b, h, i: (b, h, i, 0))
    else:
        mask_spec = pl.BlockSpec((None, tq, t), lambda b, h, i: (b, i, 0))
    stat = pltpu.VMEM((tq, LANES), F32)
    return pl.pallas_call(
        functools.partial(_attn_kernel, tq=tq, rep=rep, shared_kv=shared_kv, block_mask=block_mask),
        grid=(bsz, g, t // tq),
        in_specs=[
            pl.BlockSpec((None, tq, rep * HEAD_DIM), lambda b, h, i: (b, i, h)),
            pl.BlockSpec((None, t, kv_w), lambda b, h, i: (b, 0, k_blk + h)),
            pl.BlockSpec((None, t, kv_w), lambda b, h, i: (b, 0, v_blk + h)),
            mask_spec,
            pl.BlockSpec((rep, tq, 2 * tq), lambda b, h, i: (h, 0, 0)),
        ],
        out_specs=pl.BlockSpec((None, tq, rep * HEAD_DIM), lambda b, h, i: (b, i, h)),
        out_shape=jax.ShapeDtypeStruct((bsz, t, N_HEADS * HEAD_DIM), BF16),
        scratch_shapes=[stat] * (3 * rep),
        compiler_params=_params("arbitrary", "arbitrary", "arbitrary"),
        name="masked_attention",
    )(proj, proj, proj, mask, near_bias)


PAGES_PER_BLOCK = MOBA_BLOCK // PAGE_SIZE
MEANS_BLOCKS_PER_STEP = 4


def _block_means_kernel(tbl_ref, *refs):
    del tbl_ref
    page_refs, o_ref = refs[:-1], refs[-1]
    for blk in range(len(page_refs) // PAGES_PER_BLOCK):
        pages = page_refs[blk * PAGES_PER_BLOCK:(blk + 1) * PAGES_PER_BLOCK]
        total = jnp.sum(pages[0][...], axis=0)
        for ref in pages[1:]:
            total = total + jnp.sum(ref[...], axis=0)
        o_ref[blk] = total / MOBA_BLOCK


def _block_means(pool, layer, table):
    _, _, _, h, dh = pool.shape
    bsz, npg = table.shape
    nb = npg // PAGES_PER_BLOCK
    blocks = math.gcd(nb, MEANS_BLOCKS_PER_STEP)
    per_step = blocks * PAGES_PER_BLOCK
    assert npg % PAGES_PER_BLOCK == 0
    page = lambda k: pl.BlockSpec((None, None, PAGE_SIZE, h, dh),
                                  lambda b, n, tbl: (layer, tbl[b, per_step * n + k], 0, 0, 0))
    out = pl.pallas_call(
        _block_means_kernel,
        grid_spec=pltpu.PrefetchScalarGridSpec(
            num_scalar_prefetch=1,
            grid=(bsz, nb // blocks),
            in_specs=[page(k) for k in range(per_step)],
            out_specs=pl.BlockSpec((None, blocks, h, dh), lambda b, n, tbl: (b, n, 0, 0)),
        ),
        out_shape=jax.ShapeDtypeStruct((bsz, nb, h, dh), F32),
        compiler_params=_params("arbitrary", "arbitrary"),
        name="block_means",
    )(table, *([pool] * per_step))
    return out.reshape(bsz, nb, h * dh)


def _moba_select_kernel(q_ref, mean_ref, o_ref, *, tq, past_len):
    i = pl.program_id(1)
    own = (past_len + i * tq + lax.broadcasted_iota(jnp.int32, (tq, LANES), 0)) // MOBA_BLOCK
    blk = lax.broadcasted_iota(jnp.int32, (tq, LANES), 1)
    past = blk < own
    for h in range(N_HEADS):
        sl = slice(h * HEAD_DIM, (h + 1) * HEAD_DIM)
        gate = lax.dot_general(q_ref[:, sl], mean_ref[:, sl], (((1,), (1,)), ((), ())),
                               precision=lax.Precision.HIGHEST, preferred_element_type=F32)
        g = jnp.where(past, gate, NEG_INF)
        kth = jnp.max(g, axis=1, keepdims=True)
        for _ in range(MOBA_TOPK - 1):
            kth = jnp.max(jnp.where(g < kth, g, NEG_INF), axis=1, keepdims=True)
        o_ref[h] = ((past & (g >= kth)) | (blk == own)).astype(o_ref.dtype)


def _moba_select(q_src, means, *, tq, past_len):
    bsz, t, _ = q_src.shape
    w = N_HEADS * HEAD_DIM
    return pl.pallas_call(
        functools.partial(_moba_select_kernel, tq=tq, past_len=past_len),
        grid=(bsz, t // tq),
        in_specs=[
            pl.BlockSpec((None, tq, w), lambda b, i: (b, i, 0)),
            pl.BlockSpec((None, LANES, w), lambda b, i: (b, 0, 0)),
        ],
        out_specs=pl.BlockSpec((None, N_HEADS, tq, LANES), lambda b, i: (b, 0, i, 0)),
        out_shape=jax.ShapeDtypeStruct((bsz, N_HEADS, t, LANES), BF16),
        compiler_params=_params("arbitrary", "arbitrary"),
        name="moba_select",
    )(q_src, means)


def _dsa_select_paged_kernel(tbl_ref, qi_ref, wi_ref, *rest, pp, topk, past_len, t_new):
    del tbl_ref
    page_refs = rest[:pp]
    knew_ref, m_ref, key_ref = rest[pp:]
    p = pl.program_id(1)
    qi = qi_ref[...].astype(BF16)
    wi = wi_ref[...]

    def score(kpage_t):
        s = _bdot(qi, kpage_t)
        w = wi * jnp.maximum(s, 0.0)
        return jnp.sum(w.reshape(N_IDX_HEADS, t_new, PAGE_SIZE), axis=0) * IDX_SCALE

    for k in range(pp):
        off = pl.multiple_of((p * pp + k) * PAGE_SIZE, PAGE_SIZE)
        key_ref[:, pl.ds(off, PAGE_SIZE)] = _sort_key(score(page_refs[k][...]))

    @pl.when(p == pl.num_programs(1) - 1)
    def _():
        row = lax.broadcasted_iota(jnp.int32, (t_new, PAGE_SIZE), 0)
        col = lax.broadcasted_iota(jnp.int32, (t_new, PAGE_SIZE), 1)
        sc = jnp.where(col <= row, score(knew_ref[...]), NEG_INF)
        key_ref[:, past_len:past_len + PAGE_SIZE] = _sort_key(sc)

        def count_ge(cand):
            return jnp.sum((key_ref[...] >= cand).astype(jnp.int32), axis=1, keepdims=True)

        thr = _kth_largest_key(count_ge, topk)
        keys = key_ref[...]
        m_ref[...] = ((keys >= thr) & (keys > _NEG_INF_KEY)).astype(m_ref.dtype)


def _dsa_select_paged(qi_rows, wi_rows, kidx_pool, layer, page_table, kidx_new, *, pp, topk):
    bsz, rows, _ = qi_rows.shape
    t_new = rows // N_IDX_HEADS
    n_pages = page_table.shape[1]
    assert n_pages % pp == 0
    past_len = n_pages * PAGE_SIZE
    lpad = past_len + PAGE_SIZE
    kidx_pool = jnp.swapaxes(kidx_pool, 2, 3)
    kidx_new = jnp.swapaxes(kidx_new, 1, 2)
    page_spec = lambda k: pl.BlockSpec((None, None, IDX_DIM, PAGE_SIZE),
                                       lambda b, p, tbl: (layer, tbl[b, p * pp + k], 0, 0))
    return pl.pallas_call(
        functools.partial(_dsa_select_paged_kernel, pp=pp, topk=topk, past_len=past_len, t_new=t_new),
        grid_spec=pltpu.PrefetchScalarGridSpec(
            num_scalar_prefetch=1,
            grid=(bsz, n_pages // pp),
            in_specs=[
                pl.BlockSpec((None, rows, IDX_DIM), lambda b, p, tbl: (b, 0, 0)),
                pl.BlockSpec((None, rows, 1), lambda b, p, tbl: (b, 0, 0)),
                *[page_spec(k) for k in range(pp)],
                pl.BlockSpec((None, IDX_DIM, PAGE_SIZE), lambda b, p, tbl: (b, 0, 0)),
            ],
            out_specs=pl.BlockSpec((None, t_new, lpad), lambda b, p, tbl: (b, 0, 0)),
            scratch_shapes=[pltpu.VMEM((t_new, lpad), jnp.int32)],
        ),
        out_shape=jax.ShapeDtypeStruct((bsz, t_new, lpad), BF16),
        compiler_params=_params("arbitrary", "arbitrary"),
        name="dsa_select_paged",
    )(page_table, qi_rows, wi_rows, *([kidx_pool] * pp), kidx_new)


def _col_vector(row_vec):
    n = row_vec.shape[1]
    eye = lax.broadcasted_iota(jnp.int32, (n, n), 0) == lax.broadcasted_iota(jnp.int32, (n, n), 1)
    return jnp.sum(jnp.where(eye, row_vec, 0.0), axis=1, keepdims=True)


def _paged_attn_kernel(tbl_ref, qbd_ref, *rest, pp, key_mask):
    del tbl_ref
    k_refs, v_refs = rest[:pp], rest[pp:2 * pp]
    mt_ref, mnew_ref, knew_ref, vnew_ref, bias_ref, o_ref, m_sc, l_sc, acc_sc = rest[2 * pp:]
    p = pl.program_id(1)
    last = pl.num_programs(1) - 1

    def key_rows(flags):
        t_new = flags.shape[0]
        pick = (lax.broadcasted_iota(jnp.int32, (t_new, LANES), 1) % t_new
                == lax.broadcasted_iota(jnp.int32, (t_new, LANES), 0)).astype(BF16)
        neg = ((flags.astype(F32) - 1.0) * -MASKED).astype(BF16)
        return lax.dot_general(neg, pick, (((0,), (0,)), ((), ())), preferred_element_type=F32)

    if key_mask:
        past_masks = key_rows(mt_ref[...])
        page_mask = lambda k: past_masks[k * PAGE_SIZE:(k + 1) * PAGE_SIZE, :]
        new_mask = lambda: key_rows(mnew_ref[...])
    else:
        page_mask = lambda k: mt_ref[pl.ds((p * pp + k) // PAGES_PER_BLOCK, 1), :]
        new_mask = lambda: mnew_ref[...]

    @pl.when(p == 0)
    def _():
        m_sc[...] = jnp.full(m_sc.shape, MASKED, F32)
        l_sc[...] = jnp.zeros_like(l_sc)
        acc_sc[...] = jnp.zeros_like(acc_sc)

    n_kv = qbd_ref.shape[0] // HEAD_DIM

    def heads_on_lanes(ref):
        return jnp.concatenate([ref[pl.ds(g, PAGE_SIZE, stride=n_kv), :] for g in range(n_kv)], axis=1)

    def pages(kps, vps, masks, biases):
        scores = []
        for kp, mask, bias in zip(kps, masks, biases):
            s = jnp.dot(kp().astype(BF16), qbd_ref[...], preferred_element_type=F32)
            if bias is not None:
                s = s + bias
            scores.append(s + mask)
        m_old = m_sc[...]
        m_new = m_old
        for s in scores:
            m_new = jnp.maximum(m_new, jnp.max(s, axis=0, keepdims=True))
        alpha = jnp.exp2(m_old - m_new)
        l_new = alpha * l_sc[...]
        pv = None
        for s, vp in zip(scores, vps):
            pt = jnp.exp2(s - m_new)
            l_new = l_new + jnp.sum(pt, axis=0, keepdims=True)
            part = lax.dot_general(pt.astype(BF16), vp().astype(BF16), (((0,), (0,)), ((), ())),
                                   preferred_element_type=F32)
            pv = part if pv is None else pv + part
        l_sc[...] = l_new
        acc_sc[...] = acc_sc[...] * _col_vector(alpha) + pv
        m_sc[...] = m_new

    is_last = (p == last).astype(F32)
    pages([functools.partial(heads_on_lanes, r) for r in k_refs],
          [functools.partial(heads_on_lanes, r) for r in v_refs],
          [page_mask(k) for k in range(pp)],
          [None] * (pp - 1) + [bias_ref[0] * is_last])

    @pl.when(p == last)
    def _():
        pages([lambda: knew_ref[...]], [lambda: vnew_ref[...]], [new_mask()], [bias_ref[1]])
        o_ref[...] = acc_sc[...] / _col_vector(l_sc[...])


def _paged_attention(qbd, k_pool, v_pool, layer, page_table, mask, new_mask, k_new, v_new, bias_t, *, pp, key_mask):
    bsz, w, _ = qbd.shape
    n_pages = page_table.shape[1]
    assert n_pages % pp == 0
    if key_mask:
        t_new = mask.shape[1]
        mask_specs = [pl.BlockSpec((None, t_new, pp * PAGE_SIZE), lambda b, p, tbl: (b, 0, p)),
                      pl.BlockSpec((None, t_new, PAGE_SIZE), lambda b, p, tbl: (b, 0, n_pages))]
        new_mask = mask
    else:
        mask_specs = [pl.BlockSpec((None,) + mask.shape[1:], lambda b, p, tbl: (b, 0, 0)),
                      pl.BlockSpec((PAGE_SIZE, LANES), lambda b, p, tbl: (0, 0))]
    n_layers, n_phys, _, n_kv, _ = k_pool.shape
    assert n_kv * HEAD_DIM == w
    as_rows = lambda pool: pool.reshape(n_layers, n_phys, PAGE_SIZE * n_kv, HEAD_DIM)
    k_pool, v_pool = as_rows(k_pool), as_rows(v_pool)
    page_spec = lambda k: pl.BlockSpec((None, None, PAGE_SIZE * n_kv, HEAD_DIM),
                                       lambda b, p, tbl: (layer, tbl[b, p * pp + k], 0, 0))
    per_b = lambda shp: pl.BlockSpec((None,) + shp, lambda b, p, tbl: (b, 0, 0))
    return pl.pallas_call(
        functools.partial(_paged_attn_kernel, pp=pp, key_mask=key_mask),
        grid_spec=pltpu.PrefetchScalarGridSpec(
            num_scalar_prefetch=1,
            grid=(bsz, n_pages // pp),
            in_specs=[
                per_b((w, LANES)),
                *[page_spec(k) for k in range(pp)],
                *[page_spec(k) for k in range(pp)],
                *mask_specs,
                per_b((PAGE_SIZE, w)),
                per_b((PAGE_SIZE, w)),
                pl.BlockSpec((2, PAGE_SIZE, LANES), lambda b, p, tbl: (0, 0, 0)),
            ],
            out_specs=per_b((LANES, w)),
            scratch_shapes=[pltpu.VMEM((1, LANES), F32), pltpu.VMEM((1, LANES), F32), pltpu.VMEM((LANES, w), F32)],
        ),
        out_shape=jax.ShapeDtypeStruct((bsz, LANES, w), F32),
        compiler_params=_params("arbitrary", "arbitrary"),
        name="paged_attention",
    )(page_table, qbd, *([k_pool] * pp), *([v_pool] * pp), mask, new_mask, k_new, v_new, bias_t)


def _block_diag_queries(q, n_kv):
    bsz, t, _ = q.shape
    q4 = q.reshape(bsz, t, N_HEADS, HEAD_DIM) * QK_SCALE_LOG2
    onehot = (jnp.arange(N_HEADS)[:, None] // (N_HEADS // n_kv) == jnp.arange(n_kv)[None, :]).astype(q.dtype)
    qbd = q4[:, :, :, None, :] * onehot[None, None, :, :, None]
    return qbd.transpose(0, 3, 4, 2, 1).reshape(bsz, n_kv * HEAD_DIM, N_HEADS * t).astype(BF16)


def _own_head_lanes(out, n_kv, t):
    bsz = out.shape[0]
    out5 = out.reshape(bsz, N_HEADS, t, n_kv, HEAD_DIM)
    heads = jnp.arange(N_HEADS)
    picked = out5[:, heads, :, heads // (N_HEADS // n_kv), :]
    return picked.transpose(1, 2, 0, 3).reshape(bsz, t, N_HEADS * HEAD_DIM)


def _decode_bias(near_bias, tq, t):
    tail = near_bias[:, :t, tq - PAGE_SIZE:tq + PAGE_SIZE]
    return tail.transpose(2, 0, 1).reshape(2, PAGE_SIZE, N_HEADS * t)


def _pad_rows(a, rows):
    return jnp.pad(a, ((0, 0), (0, rows - a.shape[1]), (0, 0)))


def _new_token_mask(t):
    key = jnp.arange(PAGE_SIZE)[:, None]
    tok = jnp.tile(jnp.arange(t), N_HEADS)[None, :]
    return jnp.where(key <= tok, 0.0, MASKED).astype(F32)


def _split_heads_kernel(x_ref, o_ref, *, tm, n_heads):
    for h in range(n_heads):
        o_ref[pl.ds(h, tm, stride=n_heads), :] = x_ref[:, h * HEAD_DIM:(h + 1) * HEAD_DIM]


def _split_heads(proj, col, n_heads, *, tm):
    bsz, t, _ = proj.shape
    w = n_heads * HEAD_DIM
    assert col % w == 0 and t % tm == 0
    out = pl.pallas_call(
        functools.partial(_split_heads_kernel, tm=tm, n_heads=n_heads),
        grid=(bsz, t // tm),
        in_specs=[pl.BlockSpec((None, tm, w), lambda b, i: (b, i, col // w))],
        out_specs=pl.BlockSpec((None, tm * n_heads, HEAD_DIM), lambda b, i: (b, i, 0)),
        out_shape=jax.ShapeDtypeStruct((bsz, t * n_heads, HEAD_DIM), F32),
        compiler_params=_params("arbitrary", "arbitrary"),
        name="split_heads",
    )(proj)
    return out.reshape(bsz, t, n_heads, HEAD_DIM)


ATTN_TQ = 256
SELECT_TQ = 256
SELECT_TK = 512
MOBA_HEADS_PER_STEP = 4
SPLIT_TM = 256
DSA_PAGES_PER_STEP = 8
MOBA_PAGES_PER_STEP = 4
Q_W = N_HEADS * HEAD_DIM
KV_W = N_KV_B * HEAD_DIM
DSA_PROJ = Q_W + 2 * KV_W + N_IDX_HEADS * IDX_DIM + IDX_DIM + N_IDX_HEADS
DSA_PROJ_PAD = -(-DSA_PROJ // 512) * 512


def _dsa_split(proj):
    k = proj[..., Q_W:Q_W + KV_W]
    v = proj[..., Q_W + KV_W:Q_W + 2 * KV_W]
    o = Q_W + 2 * KV_W + N_IDX_HEADS * IDX_DIM
    return k, v, proj[..., o:o + IDX_DIM]


def _dsa_prompt(proj, near_bias):
    bsz, t, _ = proj.shape
    ki = _dsa_split(proj)[2]
    mask = _dsa_select(proj, tq=min(SELECT_TQ, t), tk=min(SELECT_TK, t), topk=min(IDX_TOPK, t // 4))
    o = _attention(proj, Q_W, Q_W + KV_W, mask, near_bias, tq=ATTN_TQ, rep=N_HEADS // N_KV_B,
                   shared_kv=True, block_mask=False)
    k = _split_heads(proj, Q_W, N_KV_B, tm=min(SPLIT_TM, t))
    v = _split_heads(proj, Q_W + KV_W, N_KV_B, tm=min(SPLIT_TM, t))
    return o, k, v, ki


def _dsa_sample(proj, k_pool, v_pool, kidx_pool, layer, page_table, near_bias):
    bsz, t, _ = proj.shape
    past_len = page_table.shape[1] * PAGE_SIZE
    k, v, ki = _dsa_split(proj)
    o_qi = Q_W + 2 * KV_W
    qi = proj[..., o_qi:o_qi + N_IDX_HEADS * IDX_DIM].reshape(bsz, t, N_IDX_HEADS, IDX_DIM)
    wi = proj[..., o_qi + N_IDX_HEADS * IDX_DIM + IDX_DIM:DSA_PROJ]
    qi_rows = qi.transpose(0, 2, 1, 3).reshape(bsz, N_IDX_HEADS * t, IDX_DIM)
    wi_rows = wi.transpose(0, 2, 1).reshape(bsz, N_IDX_HEADS * t, 1)
    sel = _dsa_select_paged(qi_rows, wi_rows, kidx_pool, layer, page_table, _pad_rows(ki, PAGE_SIZE),
                            pp=DSA_PAGES_PER_STEP, topk=min(IDX_TOPK, (past_len + t) // 4))
    out = _paged_attention(
        _block_diag_queries(proj[..., :Q_W], N_KV_B), k_pool, v_pool, layer, page_table, sel, None,
        _pad_rows(k, PAGE_SIZE), _pad_rows(v, PAGE_SIZE), _decode_bias(near_bias, ATTN_TQ, t),
        pp=DSA_PAGES_PER_STEP, key_mask=True)
    o = _own_head_lanes(out, N_KV_B, t).astype(BF16)
    return o, k.reshape(bsz, t, N_KV_B, HEAD_DIM), v.reshape(bsz, t, N_KV_B, HEAD_DIM), ki


def _moba_prompt(proj, near_bias):
    bsz, t, _ = proj.shape
    k_heads = _split_heads(proj, Q_W, N_HEADS, tm=min(SPLIT_TM, t))
    v_heads = _split_heads(proj, 2 * Q_W, N_HEADS, tm=min(SPLIT_TM, t))
    n_pages = t // PAGE_SIZE
    pages = jnp.arange(bsz * n_pages, dtype=jnp.int32).reshape(bsz, n_pages)
    means = _block_means(k_heads.reshape(1, bsz * n_pages, PAGE_SIZE, N_HEADS, HEAD_DIM), 0, pages)
    sel = _moba_select(proj, _pad_rows(means, LANES), tq=ATTN_TQ, past_len=0)
    o = _attention(proj, Q_W, 2 * Q_W, sel, near_bias, tq=ATTN_TQ, rep=MOBA_HEADS_PER_STEP,
                   shared_kv=False, block_mask=True)
    return o, k_heads, v_heads


def _moba_sample(proj, k_pool, v_pool, layer, page_table, near_bias):
    bsz, t, _ = proj.shape
    n_pages = page_table.shape[1]
    past_len = n_pages * PAGE_SIZE
    k = proj[..., Q_W:2 * Q_W]
    v = proj[..., 2 * Q_W:3 * Q_W]
    means = _block_means(k_pool, layer, page_table)
    sel = _moba_select(proj, _pad_rows(means, LANES), tq=t, past_len=past_len)
    block_rows = (sel.astype(F32).transpose(0, 3, 1, 2).reshape(bsz, LANES, N_HEADS * t) - 1.0) * -MASKED
    out = _paged_attention(
        _block_diag_queries(proj[..., :Q_W], N_HEADS), k_pool, v_pool, layer, page_table,
        block_rows, _new_token_mask(t), _pad_rows(k, PAGE_SIZE), _pad_rows(v, PAGE_SIZE),
        _decode_bias(near_bias, ATTN_TQ, t), pp=MOBA_PAGES_PER_STEP, key_mask=False)
    o = _own_head_lanes(out, N_HEADS, t).astype(BF16)
    return o, k.reshape(bsz, t, N_HEADS, HEAD_DIM), v.reshape(bsz, t, N_HEADS, HEAD_DIM)


N_MIXERS = 3
PROMPT_TM = 1024
MATMUL_TN = 512
LRU_TT = 256
MOE_TM = 512
ROUTE_W = LANES


def kernel(x_prompt, x_sample, state_lru_h, state_lru_conv, cache_dsa_k, cache_dsa_v, cache_dsa_kidx,
           cache_moba_k, cache_moba_v, page_table, c_prompt, c_sample, rel_bias, norm_mix_g, norm_ffn_g,
           final_norm_g, ada_w, ada_b, lru_w_in, lru_conv_w, lru_conv_b, lru_w_rgate, lru_b_rgate,
           lru_w_igate, lru_b_igate, lru_lambda, lru_w_out, dsa_w_in, dsa_w_out, moba_w_in, moba_w_out,
           moe_w_group, moe_w_expert, moe_w1, moe_w3, moe_w2):
    bp, tp, d = x_prompt.shape
    bs, ts, _ = x_sample.shape
    depth = ada_w.shape[0]
    ns = bs * ts
    c_rnn = lru_w_out.shape[1]

    c_all = jnp.concatenate([c_prompt, c_sample], axis=0)
    c_rows = -(-c_all.shape[0] // SUBLANES) * SUBLANES
    mods = _ada(jnp.pad(c_all, ((0, c_rows - c_all.shape[0]), (0, 0))), ada_w, ada_b)
    mods = mods.reshape(depth, c_rows, 6, d)
    near_bias = _near_bias(rel_bias, ATTN_TQ)

    w1, w3, w2 = moe_w1.astype(BF16), moe_w3.astype(BF16), moe_w2.astype(BF16)
    xp = x_prompt
    pending = None
    xs = x_sample.reshape(1, ns, d)
    outs = {n: [] for n in ("lru_h_p", "lru_c_p", "lru_h_s", "lru_c_s", "dsa_k_p", "dsa_v_p", "dsa_i_p",
                            "dsa_k_s", "dsa_v_s", "dsa_i_s", "moba_k_p", "moba_v_p", "moba_k_s", "moba_v_s")}
    for i in range(depth):
        kind, j = i % N_MIXERS, i // N_MIXERS
        mod_p = [mods[i, :bp, m][:, None, :] for m in range(6)]
        mod_s = [jnp.repeat(mods[i, bp:bp + bs, m], ts, axis=0)[None] for m in range(6)]
        shp1, scp1, gtp1, shp2, scp2, gtp2 = mod_p
        shs1, scs1, gts1, shs2, scs2, gts2 = mod_s
        mix_in = lambda w, tn=MATMUL_TN: (
            _norm_matmul(xp, norm_mix_g[i], shp1, scp1, w, tm=min(PROMPT_TM, tp), tn=tn, layer=j),
            _norm_matmul(xs, norm_mix_g[i], shs1, scs1, w, tm=ns, tn=tn, layer=j))
        if kind == 0:
            up, us = mix_in(lru_w_in)
            lru_params = (lru_conv_w[j], lru_conv_b[j], lru_w_rgate[j], lru_b_rgate[j], lru_w_igate[j],
                          lru_b_igate[j], lru_lambda[j])
            yp, hp_state, cp = _lru_core(up, jnp.zeros((bp, CONV_W - 1, c_rnn), F32), jnp.zeros((bp, c_rnn), F32),
                                         *lru_params, tt=LRU_TT)
            ys, hs_state, cs = _lru_core(us.reshape(bs, ts, 2 * c_rnn), state_lru_conv[j], state_lru_h[j],
                                         *lru_params, tt=ts)
            ys = ys.reshape(1, ns, c_rnn)
            w_out = lru_w_out
            outs["lru_h_p"].append(hp_state)
            outs["lru_c_p"].append(cp)
            outs["lru_h_s"].append(hs_state)
            outs["lru_c_s"].append(cs)
        elif kind == 1:
            pp_, ps_ = mix_in(jnp.pad(dsa_w_in, ((0, 0), (0, 0), (0, DSA_PROJ_PAD - DSA_PROJ))))
            yp, kp, vp, ip = _dsa_prompt(pp_, near_bias)
            ys, k_s, v_s, i_s = _dsa_sample(ps_.reshape(bs, ts, DSA_PROJ_PAD), cache_dsa_k, cache_dsa_v,
                                            cache_dsa_kidx, j, page_table, near_bias)
            ys = ys.reshape(1, ns, Q_W)
            w_out = dsa_w_out
            for n, val in zip(("dsa_k_p", "dsa_v_p", "dsa_i_p", "dsa_k_s", "dsa_v_s", "dsa_i_s"),
                              (kp, vp, ip, k_s, v_s, i_s)):
                outs[n].append(val)
        else:
            pp_, ps_ = mix_in(moba_w_in)
            yp, kp, vp = _moba_prompt(pp_, near_bias)
            ys, k_s, v_s = _moba_sample(ps_.reshape(bs, ts, 3 * Q_W), cache_moba_k, cache_moba_v, j,
                                        page_table, near_bias)
            ys = ys.reshape(1, ns, Q_W)
            w_out = moba_w_out
            for n, val in zip(("moba_k_p", "moba_v_p", "moba_k_s", "moba_v_s"), (kp, vp, k_s, v_s)):
                outs[n].append(val)
        xp = _matmul_residual(yp, w_out, xp, gtp1, tm=min(PROMPT_TM, tp), tn=min(MATMUL_TN, d), layer=j)
        xs = _matmul_residual(ys, w_out, xs, gts1, tm=ns, tn=min(MATMUL_TN, d), layer=j)

        w_route = jnp.pad(jnp.concatenate([moe_w_group[i], moe_w_expert[i]], axis=1),
                          ((0, 0), (0, ROUTE_W - N_GROUPS - N_EXPERTS)))
        lg_p, h_p = _norm_matmul(xp, norm_ffn_g[i], shp2, scp2, w_route, tm=min(PROMPT_TM, tp), tn=ROUTE_W,
                                 emit_h=True, highest=True, h_dtype=F32)
        moe_tm = min(MOE_TM, tp)
        info, counts = _route_tokens(lg_p.reshape(bp * tp, ROUTE_W), tm=moe_tm)
        moe_p = _moe_grouped(h_p.reshape(bp * tp, d), info, *_group_layout(info, counts, moe_tm), w1, w3, w2, i,
                             tm=moe_tm)
        if i + 1 < depth:
            xp = _gated_add(xp, moe_p, gtp2, tm=moe_tm)
        else:
            pending = (moe_p, gtp2)
        lg_s, h_s = _norm_matmul(xs, norm_ffn_g[i], shs2, scs2, w_route, tm=ns, tn=ROUTE_W,
                                 emit_h=True, highest=True)
        xs = _moe_dense(h_s, lg_s, w1, w3, w2, i, xs, gts2, tm=ns)

    y_prompt = _rmsnorm(xp, final_norm_g, tm=min(MOE_TM, tp), residual=pending)
    y_sample = _rmsnorm(xs, final_norm_g, tm=ns).reshape(bs, ts, d)
    stack = lambda n: jnp.stack(outs[n])
    return (y_prompt, y_sample, stack("lru_h_p"), stack("lru_c_p"), stack("lru_h_s"), stack("lru_c_s"),
            stack("dsa_k_p"), stack("dsa_v_p"), stack("dsa_i_p"), stack("dsa_k_s"), stack("dsa_v_s"),
            stack("dsa_i_s"), stack("moba_k_p"), stack("moba_v_p"), stack("moba_k_s"), stack("moba_v_s"))
```
